```python
import math
import jax
import jax.numpy as jnp
from jax import lax
import numpy as np

D_MODEL = 1024
BATCH = 8
SEQ = 2048
DEPTH = 1
DEC_BATCH = 128
DEC_SEQ = 8
PAST_LEN = 16384
PAGE_SIZE = 128

N_HEADS_A = 4
HEAD_K = 128
HEAD_V = 128
QK_W = N_HEADS_A * HEAD_K
WIDTH_A = N_HEADS_A * HEAD_V
QKV_W = 2 * QK_W + WIDTH_A
CONV_W = 4
CHUNK = 64
POOL_WINDOWS = (2, 4, 8, 16)
N_POOL_GROUPS = 4
POOL_GROUP = 128
WIDTH_B = N_POOL_GROUPS * POOL_GROUP
POOL_HIST = 15
D_FF = 4 * D_MODEL
EPS = 1e-6
IN_SPLITS = (QK_W, QK_W, WIDTH_A, WIDTH_A, N_HEADS_A, N_HEADS_A, WIDTH_B, D_MODEL, D_MODEL)
IN_W = 2 * QK_W + 2 * WIDTH_A + 2 * N_HEADS_A + WIDTH_B + 2 * D_MODEL

kernel_name = 'hybrid_gdn_pool_decoder_step'


def rmsnorm(x, g):
    xf = x.astype(jnp.float32)
    return xf * lax.rsqrt(jnp.mean(xf * xf, axis=-1, keepdims=True) + EPS) * g.astype(jnp.float32)


def l2norm(x):
    return x * lax.rsqrt(jnp.sum(x * x, axis=-1, keepdims=True) + EPS)


def gated_delta_rule(q, k, v, g, beta, s0):
    b, h, l, dk = q.shape
    dv = v.shape[-1]
    c = CHUNK if l % CHUNK == 0 else l
    n = l // c
    rs = lambda t: t.reshape((b, h, n, c) + t.shape[3:])
    q, k, v, g, beta = rs(q), rs(k), rs(v), rs(g), rs(beta)
    g = jnp.cumsum(g, axis=-1)
    idx = jnp.arange(c)
    causal = idx[:, None] >= idx[None, :]
    strict = idx[:, None] > idx[None, :]
    decay = jnp.exp(jnp.where(causal, g[..., :, None] - g[..., None, :], -jnp.inf))
    kk = jnp.einsum('bhnid,bhnjd->bhnij', k, k)
    m = jnp.where(strict, kk * decay * beta[..., :, None], 0.0)
    a = m + jnp.eye(c, dtype=jnp.float32)
    eg = jnp.exp(g)
    rhs = jnp.concatenate([v * beta[..., None], k * (beta * eg)[..., None]], axis=-1)
    sol = lax.linalg.triangular_solve(a, rhs, left_side=True, lower=True, unit_diagonal=True)
    u, w = sol[..., :dv], sol[..., dv:]
    qk = jnp.where(causal, jnp.einsum('bhnid,bhnjd->bhnij', q, k) * decay, 0.0)
    q_dec = q * eg[..., None]
    k_dec = k * jnp.exp(g[..., -1:] - g)[..., None]
    last = jnp.exp(g[..., -1])

    def step(s, inp):
        u_c, w_c, qk_c, qd_c, kd_c, last_c = inp
        v_new = u_c - jnp.einsum('bhcd,bhde->bhce', w_c, s)
        o = jnp.einsum('bhcd,bhde->bhce', qd_c, s) + jnp.einsum('bhij,bhje->bhie', qk_c, v_new)
        s = s * last_c[..., None, None] + jnp.einsum('bhcd,bhce->bhde', kd_c, v_new)
        return s, o

    xs = (jnp.moveaxis(u, 2, 0), jnp.moveaxis(w, 2, 0), jnp.moveaxis(qk, 2, 0),
          jnp.moveaxis(q_dec, 2, 0), jnp.moveaxis(k_dec, 2, 0), jnp.moveaxis(last, 2, 0))
    s, o = lax.scan(step, s0, xs)
    o = jnp.moveaxis(o, 0, 2).reshape(b, h, l, dv)
    return o, s


def gated_delta_branch(q, k, v, z, b_raw, a_raw, conv_prev, s0, w_conv, a_log, dt_bias, w_onorm):
    bsz, l, _ = q.shape
    qkv = jnp.concatenate([q, k, v], axis=-1)
    full = jnp.concatenate([conv_prev.astype(jnp.float32), qkv], axis=1)
    conv = lax.conv_general_dilated(full, w_conv.astype(jnp.float32)[:, None, :], (1,), 'VALID',
                                    dimension_numbers=('NWC', 'WIO', 'NWC'),
                                    feature_group_count=QKV_W)
    conv = jax.nn.silu(conv)
    qc, kc, vc = conv[..., :QK_W], conv[..., QK_W:2 * QK_W], conv[..., 2 * QK_W:]
    qh = jnp.swapaxes(l2norm(qc.reshape(bsz, l, N_HEADS_A, HEAD_K)) * (HEAD_K ** -0.5), 1, 2)
    kh = jnp.swapaxes(l2norm(kc.reshape(bsz, l, N_HEADS_A, HEAD_K)), 1, 2)
    vh = jnp.swapaxes(vc.reshape(bsz, l, N_HEADS_A, HEAD_V), 1, 2)
    beta = jnp.swapaxes(jax.nn.sigmoid(b_raw), 1, 2)
    g = jnp.swapaxes(-jnp.exp(a_log.astype(jnp.float32)) * jax.nn.softplus(a_raw + dt_bias), 1, 2)
    o, s = gated_delta_rule(qh, kh, vh, g, beta, s0.astype(jnp.float32))
    o = jnp.swapaxes(o, 1, 2)
    o = rmsnorm(o, w_onorm) * jax.nn.silu(z.reshape(bsz, l, N_HEADS_A, HEAD_V))
    return o.reshape(bsz, l, WIDTH_A), full[:, -(CONV_W - 1):], s


def pool_branch(p, pool_prev, pos0, w_mix, scale):
    bsz, l, _ = p.shape
    full = jnp.concatenate([pool_prev.astype(jnp.float32), p], axis=1)
    c0 = jnp.concatenate([jnp.zeros((bsz, 1, WIDTH_B), jnp.float32), jnp.cumsum(full, axis=1)], axis=1)
    end = c0[:, POOL_HIST + 1:]
    pos = pos0 + jnp.arange(l)
    outs = []
    for gi, win in enumerate(POOL_WINDOWS):
        lo, hi = gi * POOL_GROUP, (gi + 1) * POOL_GROUP
        start = c0[:, POOL_HIST + 1 - win:POOL_HIST + 1 - win + l, lo:hi]
        cnt = jnp.minimum(pos + 1, win).astype(jnp.float32)
        outs.append((end[..., lo:hi] - start) / cnt[None, :, None])
    pooled = jnp.concatenate(outs, axis=-1) - p
    mixed = jnp.einsum('blgc,gcd->blgd', pooled.reshape(bsz, l, N_POOL_GROUPS, POOL_GROUP), w_mix)
    return mixed.reshape(bsz, l, WIDTH_B) * scale, full[:, -POOL_HIST:]


def trunk(x, conv_prev, pool_prev, ssm_prev, pos0, w_in, w_conv, a_log, dt_bias, w_onorm,
          w_pool_mix, pool_scale, w_a_out, w_b_out, w_o, g_attn, g_mlp, w_up, w_down, g_final):
    out_dtype = x.dtype
    x = x.astype(jnp.float32)
    offs = np.cumsum(IN_SPLITS)[:-1].tolist()
    convs, pools, ssms = [], [], []
    for i in range(DEPTH):
        h = rmsnorm(x, g_attn[i])
        q, k, v, z, b_raw, a_raw, p, ga, gb = jnp.split(h @ w_in[i], offs, axis=-1)
        o_a, conv_new, s_new = gated_delta_branch(q, k, v, z, b_raw, a_raw, conv_prev[i], ssm_prev[i],
                                                  w_conv[i], a_log[i], dt_bias[i], w_onorm[i])
        o_b, pool_new = pool_branch(p, pool_prev[i], pos0, w_pool_mix[i], pool_scale[i])
        merged = jax.nn.sigmoid(ga) * (o_a @ w_a_out[i]) + jax.nn.sigmoid(gb) * (o_b @ w_b_out[i])
        x = x + merged @ w_o[i]
        h2 = rmsnorm(x, g_mlp[i])
        x = x + jnp.square(jax.nn.relu(h2 @ w_up[i])) @ w_down[i]
        convs.append(conv_new)
        pools.append(pool_new)
        ssms.append(s_new)
    y = rmsnorm(x, g_final).astype(out_dtype)
    return y, jnp.stack(convs), jnp.stack(pools), jnp.stack(ssms)


def setup_inputs(seed: int = 0) -> dict:
    key = jax.random.key(seed)
    ks = jax.random.split(key, 24)
    f32 = jnp.float32

    def nrm(k, shape, scale):
        return jax.random.normal(k, shape, f32) * scale

    dt = jnp.exp(jax.random.uniform(ks[8], (DEPTH, N_HEADS_A), f32, math.log(1e-3), math.log(1e-1)))
    return {
        'x_prompt': nrm(ks[0], (BATCH, SEQ, D_MODEL), 1.0),
        'x_sample': nrm(ks[1], (DEC_BATCH, DEC_SEQ, D_MODEL), 1.0),
        'state_conv': nrm(ks[2], (DEPTH, DEC_BATCH, CONV_W - 1, QKV_W), 1.0),
        'state_pool': nrm(ks[3], (DEPTH, DEC_BATCH, POOL_HIST, WIDTH_B), 1.0),
        'state_ssm': nrm(ks[4], (DEPTH, DEC_BATCH, N_HEADS_A, HEAD_K, HEAD_V), 0.5),
        'w_in': nrm(ks[5], (DEPTH, D_MODEL, IN_W), D_MODEL ** -0.5),
        'w_conv': nrm(ks[6], (DEPTH, CONV_W, QKV_W), CONV_W ** -0.5),
        'a_log': jnp.log(jax.random.uniform(ks[7], (DEPTH, N_HEADS_A), f32, 1.0, 16.0)),
        'dt_bias': dt + jnp.log(-jnp.expm1(-dt)),
        'w_onorm': 1.0 + nrm(ks[9], (DEPTH, HEAD_V), 0.02),
        'w_pool_mix': nrm(ks[10], (DEPTH, N_POOL_GROUPS, POOL_GROUP, POOL_GROUP), POOL_GROUP ** -0.5),
        'pool_scale': 1.0 + nrm(ks[11], (DEPTH, WIDTH_B), 0.02),
        'w_a_out': nrm(ks[12], (DEPTH, WIDTH_A, D_MODEL), WIDTH_A ** -0.5),
        'w_b_out': nrm(ks[13], (DEPTH, WIDTH_B, D_MODEL), WIDTH_B ** -0.5),
        'w_o': nrm(ks[14], (DEPTH, D_MODEL, D_MODEL), D_MODEL ** -0.5),
        'g_attn': 1.0 + nrm(ks[15], (DEPTH, D_MODEL), 0.02),
        'g_mlp': 1.0 + nrm(ks[16], (DEPTH, D_MODEL), 0.02),
        'w_up': nrm(ks[17], (DEPTH, D_MODEL, D_FF), D_MODEL ** -0.5),
        'w_down': nrm(ks[18], (DEPTH, D_FF, D_MODEL), D_FF ** -0.5),
        'g_final': 1.0 + nrm(ks[19], (D_MODEL,), 0.02),
    }


def reference(x_prompt, x_sample, state_conv, state_pool, state_ssm, w_in, w_conv, a_log, dt_bias,
              w_onorm, w_pool_mix, pool_scale, w_a_out, w_b_out, w_o, g_attn, g_mlp, w_up, w_down,
              g_final):
    bp = x_prompt.shape[0]
    zero_conv = jnp.zeros((DEPTH, bp, CONV_W - 1, QKV_W), jnp.float32)
    zero_pool = jnp.zeros((DEPTH, bp, POOL_HIST, WIDTH_B), jnp.float32)
    zero_ssm = jnp.zeros((DEPTH, bp, N_HEADS_A, HEAD_K, HEAD_V), jnp.float32)
    y_prompt, conv_p, pool_p, ssm_p = trunk(
        x_prompt, zero_conv, zero_pool, zero_ssm, 0, w_in, w_conv, a_log, dt_bias, w_onorm,
        w_pool_mix, pool_scale, w_a_out, w_b_out, w_o, g_attn, g_mlp, w_up, w_down, g_final)
    y_sample, conv_s, pool_s, ssm_s = trunk(
        x_sample, state_conv, state_pool, state_ssm, PAST_LEN, w_in, w_conv, a_log, dt_bias, w_onorm,
        w_pool_mix, pool_scale, w_a_out, w_b_out, w_o, g_attn, g_mlp, w_up, w_down, g_final)
    return (y_prompt, y_sample,
            conv_p.astype(state_conv.dtype), pool_p.astype(state_pool.dtype), ssm_p.astype(state_ssm.dtype),
            conv_s.astype(state_conv.dtype), pool_s.astype(state_pool.dtype), ssm_s.astype(state_ssm.dtype))
```

```python
import functools
import math

import jax
import jax.numpy as jnp
import numpy as np
from jax import lax
from jax.experimental import pallas as pl
from jax.experimental.pallas import tpu as pltpu

D_MODEL = 1024
N_HEADS = 4
HEAD_DIM = 128
QK_W = N_HEADS * HEAD_DIM
QKV_W = 3 * QK_W
CONV_W = 4
CHUNK = 64
POOL_WINDOWS = (2, 4, 8, 16)
POOL_GROUP = 128
WIDTH_B = len(POOL_WINDOWS) * POOL_GROUP
POOL_HIST = 15
D_FF = 4 * D_MODEL
EPS = 1e-6
PAST_LEN = 16384
LANES = 128
SUBLANES = 8

QKVZ_W = QKV_W + QK_W
BA_OFF = QKVZ_W
P_OFF = BA_OFF + LANES
GATE_OFF = P_OFF + WIDTH_B
PROJ_W = GATE_OFF + 2 * D_MODEL

N_BCAST = 5
CONV_PAD = 8
POOL_PAD = 16

VMEM_LIMIT = 56 * 1024 * 1024

BF16 = jnp.bfloat16
F32 = jnp.float32


def _dot(a, b):
    return jnp.dot(a.astype(BF16), b.astype(BF16), preferred_element_type=F32)


def _dot_nt(a, b):
    return lax.dot_general(a.astype(BF16), b.astype(BF16), (((1,), (1,)), ((), ())),
                           preferred_element_type=F32)


def _dot_tn(a, b):
    return lax.dot_general(a.astype(BF16), b.astype(BF16), (((0,), (0,)), ((), ())),
                           preferred_element_type=F32)


def _split3(x):
    hi = x.astype(BF16)
    r1 = x - hi.astype(F32)
    mid = r1.astype(BF16)
    lo = (r1 - mid.astype(F32)).astype(BF16)
    return hi, mid, lo


def _sigmoid(x):
    return 1.0 / (1.0 + jnp.exp(-x))


def _silu(x):
    return x * _sigmoid(x)


def _rms_scale(x):
    return lax.rsqrt(jnp.mean(x * x, axis=-1, keepdims=True) + EPS)


def _proj_kernel(x_ref, g_ref, w_ref, qkvz_ref, ba_ref, p_ref, gate_ref):
    x = x_ref[...]
    h = (x * _rms_scale(x) * g_ref[...]).astype(BF16)
    qkvz_ref[...] = jnp.dot(h, w_ref[:, 0:BA_OFF], preferred_element_type=F32)
    ba_ref[...] = jnp.dot(h, w_ref[:, BA_OFF:P_OFF], preferred_element_type=F32)
    p_ref[...] = jnp.dot(h, w_ref[:, P_OFF:GATE_OFF], preferred_element_type=F32)
    gate_ref[...] = jnp.dot(h, w_ref[:, GATE_OFF:PROJ_W], preferred_element_type=F32)


def _const_spec(shape):
    zeros = (0,) * len(shape)
    return pl.BlockSpec(shape, lambda *_: zeros, pipeline_mode=pl.Buffered(1))


def _project(x2d, g_attn, w_proj, tm):
    t = x2d.shape[0]
    row = lambda w: pl.BlockSpec((tm, w), lambda i: (i, 0))
    return pl.pallas_call(
        _proj_kernel,
        grid=(t // tm,),
        in_specs=[row(D_MODEL), _const_spec((1, D_MODEL)), _const_spec((D_MODEL, PROJ_W))],
        out_specs=[row(QKVZ_W), row(LANES), row(WIDTH_B), row(2 * D_MODEL)],
        out_shape=[jax.ShapeDtypeStruct((t, QKVZ_W), F32), jax.ShapeDtypeStruct((t, LANES), F32),
                   jax.ShapeDtypeStruct((t, WIDTH_B), F32), jax.ShapeDtypeStruct((t, 2 * D_MODEL), F32)],
        compiler_params=pltpu.CompilerParams(dimension_semantics=("arbitrary",),
                                             vmem_limit_bytes=VMEM_LIMIT),
        name="proj",
    )(x2d, g_attn, w_proj)


def _chunk_cumsum(x, chunk):
    row = lax.broadcasted_iota(jnp.int32, x.shape, 0) % chunk
    shift = 1
    while shift < chunk:
        x = x + jnp.where(row >= shift, pltpu.roll(x, shift, axis=0), 0.0)
        shift *= 2
    return x


def _mix_kernel(qkvz_ref, ba_ref, p_ref, convprev_ref, poolprev_ref, s0_ref, wconv_ref, alog_ref,
                dtb_ref, wonorm_ref, sel_ref, wmix_ref, pscale_ref,
                oa_ref, ob_ref, convnew_ref, poolnew_ref, snew_ref,
                ext_ref, pext_ref, *, bb, tl, chunk, pos0):
    tile = pl.program_id(1)
    nc = tl // chunk

    @pl.when(tile == 0)
    def _():
        ext_ref[:, CONV_PAD - (CONV_W - 1):CONV_PAD, :] = convprev_ref[...]
        pext_ref[:, POOL_PAD - POOL_HIST:POOL_PAD, :] = poolprev_ref[...]
        snew_ref[...] = s0_ref[...]

    lane = lax.broadcasted_iota(jnp.int32, (tl, LANES), 1)
    idx_r = lax.broadcasted_iota(jnp.int32, (chunk, chunk), 0)
    idx_c = lax.broadcasted_iota(jnp.int32, (chunk, chunk), 1)
    causal = idx_r >= idx_c
    strict = idx_r > idx_c
    eye = jnp.where(idx_r == idx_c, 1.0, 0.0).astype(F32)
    n_levels = int(math.log2(chunk)) - 1

    for b in range(bb):
        ext_ref[b, CONV_PAD:CONV_PAD + tl, :] = qkvz_ref[b, :, 0:QKV_W]
        convnew_ref[b] = ext_ref[b, CONV_PAD + tl - (CONV_W - 1):CONV_PAD + tl, :]

        ba = ba_ref[b]
        beta = _sigmoid(ba)
        xs = ba + dtb_ref[...]
        softplus = jnp.maximum(xs, 0.0) + jnp.log1p(jnp.exp(-jnp.abs(xs)))
        g = _chunk_cumsum(-jnp.exp(alog_ref[...]) * softplus, chunk)
        eg = jnp.exp(g)
        g3 = g.reshape(nc, chunk, LANES)
        g_last = jnp.broadcast_to(g3[:, chunk - 1:chunk, :], g3.shape).reshape(tl, LANES)
        kds = jnp.exp(g_last - g)
        be = beta * pltpu.roll(eg, 4, axis=1)
        l16 = lane % 16
        packed = jnp.where(lane >= 16, g,
                           jnp.where(l16 < 4, beta, jnp.where(l16 < 8, eg, jnp.where(l16 < 12, be, kds))))
        hi, mid, lo = _split3(packed)
        sel = sel_ref[...]
        bc = (jnp.dot(hi, sel, preferred_element_type=F32) + jnp.dot(mid, sel, preferred_element_type=F32)
              + jnp.dot(lo, sel, preferred_element_type=F32))

        for h in range(N_HEADS):
            def bcast(q):
                off = (q * N_HEADS + h) * LANES
                return bc[:, off:off + LANES]
            beta_b, eg_b, be_b, kds_b, g_b = (bcast(q) for q in range(N_BCAST))

            def conv_silu(col):
                acc = wconv_ref[0:1, col:col + LANES] * ext_ref[b, CONV_PAD - 3:CONV_PAD - 3 + tl, col:col + LANES]
                for j in range(1, CONV_W):
                    acc = acc + (wconv_ref[j:j + 1, col:col + LANES]
                                 * ext_ref[b, CONV_PAD - 3 + j:CONV_PAD - 3 + j + tl, col:col + LANES])
                return _silu(acc)

            qc = conv_silu(h * HEAD_DIM)
            kc = conv_silu(QK_W + h * HEAD_DIM)
            vc = conv_silu(2 * QK_W + h * HEAD_DIM)
            qn = qc * (lax.rsqrt(jnp.sum(qc * qc, axis=-1, keepdims=True) + EPS) * (HEAD_DIM ** -0.5))
            kn = kc * lax.rsqrt(jnp.sum(kc * kc, axis=-1, keepdims=True) + EPS)

            q_dec = qn * eg_b
            k_dec = kn * kds_b
            rhs = jnp.concatenate([vc * beta_b, kn * be_b], axis=1)
            g_row = g_b.T

            s = snew_ref[b, h]
            outs = []
            for n in range(nc):
                r0 = n * chunk
                rows = slice(r0, r0 + chunk)
                k_n = kn[rows]
                g_diff = g_b[rows, 0:chunk] - g_row[0:chunk, r0:r0 + chunk]
                decay = jnp.exp(jnp.where(causal, g_diff, -jnp.inf))
                kk = _dot_nt(k_n, k_n)
                neg_m = jnp.where(strict, -(kk * decay * beta_b[rows, 0:chunk]), 0.0)
                inv = eye + neg_m
                pw = neg_m
                for _ in range(n_levels):
                    pw = _dot(pw, pw)
                    inv = inv + _dot(inv, pw)
                sol = _dot(inv, rhs[rows])
                u, w = sol[:, 0:HEAD_DIM], sol[:, HEAD_DIM:]
                qk = jnp.where(causal, _dot_nt(qn[rows], k_n) * decay, 0.0)

                ws = _dot(jnp.concatenate([w, q_dec[rows]], axis=0), s)
                v_new = u - ws[0:chunk]
                outs.append(ws[chunk:] + _dot(qk, v_new))
                s = s * eg_b[r0 + chunk - 1:r0 + chunk, :] + _dot_tn(k_dec[rows], v_new)
            snew_ref[b, h] = s

            o = jnp.concatenate(outs, axis=0) if nc > 1 else outs[0]
            z = qkvz_ref[b, :, QKV_W + h * HEAD_DIM:QKV_W + (h + 1) * HEAD_DIM]
            oa_ref[b, :, h * HEAD_DIM:(h + 1) * HEAD_DIM] = o * _rms_scale(o) * wonorm_ref[...] * _silu(z)

        ext_ref[b, CONV_PAD - (CONV_W - 1):CONV_PAD, :] = convnew_ref[b]

        pext_ref[b, POOL_PAD:POOL_PAD + tl, :] = p_ref[b]
        new_hist = pext_ref[b, POOL_PAD + tl - POOL_HIST:POOL_PAD + tl, :]
        poolnew_ref[b] = new_hist
        pos = pos0 + tile * tl + lax.broadcasted_iota(jnp.int32, (tl, POOL_GROUP), 0)
        for gi, win in enumerate(POOL_WINDOWS):
            cols = slice(gi * POOL_GROUP, (gi + 1) * POOL_GROUP)
            acc = pext_ref[b, POOL_PAD:POOL_PAD + tl, cols]
            for sft in range(1, win):
                acc = acc + pext_ref[b, POOL_PAD - sft:POOL_PAD - sft + tl, cols]
            cnt = jnp.minimum(pos + 1, win).astype(F32)
            pooled = acc / cnt - p_ref[b, :, cols]
            ob_ref[b, :, cols] = _dot(pooled, wmix_ref[gi]) * pscale_ref[:, cols]
        pext_ref[b, POOL_PAD - POOL_HIST:POOL_PAD, :] = new_hist


def _selector():
    sel = np.zeros((LANES, N_BCAST * N_HEADS * LANES), np.float32)
    src_base = (0, 4, 8, 12, 20)
    for q in range(N_BCAST):
        for h in range(N_HEADS):
            off = (q * N_HEADS + h) * LANES
            sel[src_base[q] + h, off:off + LANES] = 1.0
    return jnp.asarray(sel, BF16)


def _mix(qkvz, ba, p, conv_prev, pool_prev, s0, w_conv, alog_lane, dtb_lane, w_onorm, w_mix, pool_scale,
         *, bb, tl, chunk, pos0):
    bsz, l = qkvz.shape[0], qkvz.shape[1]
    seq = lambda w: pl.BlockSpec((bb, tl, w), lambda i, j: (i, j, 0))
    state3 = lambda r, w: pl.BlockSpec((bb, r, w), lambda i, j: (i, 0, 0))
    state_s = pl.BlockSpec((bb, N_HEADS, HEAD_DIM, HEAD_DIM), lambda i, j: (i, 0, 0, 0))
    kern = functools.partial(_mix_kernel, bb=bb, tl=tl, chunk=chunk, pos0=pos0)
    return pl.pallas_call(
        kern,
        grid=(bsz // bb, l // tl),
        in_specs=[seq(QKVZ_W), seq(LANES), seq(WIDTH_B),
                  state3(CONV_W - 1, QKV_W), state3(POOL_HIST, WIDTH_B), state_s,
                  _const_spec((CONV_W, QKV_W)), _const_spec((1, LANES)), _const_spec((1, LANES)),
                  _const_spec((1, HEAD_DIM)), _const_spec((LANES, N_BCAST * N_HEADS * LANES)),
                  _const_spec((len(POOL_WINDOWS), POOL_GROUP, POOL_GROUP)), _const_spec((1, WIDTH_B))],
        out_specs=[seq(QK_W), seq(WIDTH_B), state3(CONV_W - 1, QKV_W), state3(POOL_HIST, WIDTH_B), state_s],
        out_shape=[jax.ShapeDtypeStruct((bsz, l, QK_W), F32), jax.ShapeDtypeStruct((bsz, l, WIDTH_B), F32),
                   jax.ShapeDtypeStruct((bsz, CONV_W - 1, QKV_W), F32),
                   jax.ShapeDtypeStruct((bsz, POOL_HIST, WIDTH_B), F32),
                   jax.ShapeDtypeStruct((bsz, N_HEADS, HEAD_DIM, HEAD_DIM), F32)],
        scratch_shapes=[pltpu.VMEM((bb, CONV_PAD + tl, QKV_W), F32), pltpu.VMEM((bb, POOL_PAD + tl, WIDTH_B), F32)],
        compiler_params=pltpu.CompilerParams(dimension_semantics=("arbitrary", "arbitrary"),
                                             vmem_limit_bytes=VMEM_LIMIT),
        name="mix",
    )(qkvz, ba, p, conv_prev, pool_prev, s0, w_conv, alog_lane, dtb_lane, w_onorm, _selector(), w_mix,
      pool_scale)


FF_BLOCK = 1024


def _mlp_kernel(x_ref, oa_ref, ob_ref, gate_ref, wa_ref, wb_ref, wo_ref, gmlp_ref, wup_ref, wdown_ref, gfin_ref,
                y_ref):
    ma = _dot(oa_ref[...], wa_ref[...])
    mb = _dot(ob_ref[...], wb_ref[...])
    merged = _sigmoid(gate_ref[:, 0:D_MODEL]) * ma + _sigmoid(gate_ref[:, D_MODEL:]) * mb
    x1 = x_ref[...] + _dot(merged, wo_ref[...])
    h2 = (x1 * _rms_scale(x1) * gmlp_ref[...]).astype(BF16)
    acc = x1
    for c0 in range(0, D_FF, FF_BLOCK):
        up = jnp.dot(h2, wup_ref[:, c0:c0 + FF_BLOCK], preferred_element_type=F32)
        act = jnp.square(jnp.maximum(up, 0.0))
        acc = acc + _dot(act, wdown_ref[c0:c0 + FF_BLOCK, :])
    y_ref[...] = acc * _rms_scale(acc) * gfin_ref[...]


def _merge_mlp(x2d, oa, ob, gates, wa, wb, wo, g_mlp, w_up, w_down, g_final, tm):
    t = x2d.shape[0]
    row = lambda w: pl.BlockSpec((tm, w), lambda i: (i, 0))
    return pl.pallas_call(
        _mlp_kernel,
        grid=(t // tm,),
        in_specs=[row(D_MODEL), row(QK_W), row(WIDTH_B), row(2 * D_MODEL),
                  _const_spec((QK_W, D_MODEL)), _const_spec((WIDTH_B, D_MODEL)), _const_spec((D_MODEL, D_MODEL)),
                  _const_spec((1, D_MODEL)), _const_spec((D_MODEL, D_FF)), _const_spec((D_FF, D_MODEL)),
                  _const_spec((1, D_MODEL))],
        out_specs=row(D_MODEL),
        out_shape=jax.ShapeDtypeStruct((t, D_MODEL), F32),
        compiler_params=pltpu.CompilerParams(dimension_semantics=("arbitrary",),
                                             vmem_limit_bytes=VMEM_LIMIT),
        name="merge_mlp",
    )(x2d, oa, ob, gates, wa, wb, wo, g_mlp, w_up, w_down, g_final)


def _trunk(x, conv_prev, pool_prev, s_prev, pos0, prm, *, tm, bb, tl, chunk):
    bsz, l, _ = x.shape
    x2d = x.reshape(bsz * l, D_MODEL)
    qkvz, ba, p, gates = _project(x2d, prm["g_attn"], prm["w_proj"], tm)
    oa, ob, conv_new, pool_new, s_new = _mix(
        qkvz.reshape(bsz, l, QKVZ_W), ba.reshape(bsz, l, LANES), p.reshape(bsz, l, WIDTH_B),
        conv_prev, pool_prev, s_prev, prm["w_conv"], prm["alog_lane"], prm["dtb_lane"], prm["w_onorm"],
        prm["w_mix"], prm["pool_scale"], bb=bb, tl=tl, chunk=chunk, pos0=pos0)
    y = _merge_mlp(x2d, oa.reshape(bsz * l, QK_W), ob.reshape(bsz * l, WIDTH_B), gates,
                   prm["w_a_out"], prm["w_b_out"], prm["w_o"], prm["g_mlp"], prm["w_up"], prm["w_down"],
                   prm["g_final"], tm)
    return y.reshape(bsz, l, D_MODEL), conv_new[None], pool_new[None], s_new[None]


def kernel(x_prompt, x_sample, state_conv, state_pool, state_ssm, w_in, w_conv, a_log, dt_bias, w_onorm,
           w_pool_mix, pool_scale, w_a_out, w_b_out, w_o, g_attn, g_mlp, w_up, w_down, g_final):
    assert w_in.shape[0] == 1, "single-layer decoder"
    w = w_in[0]
    ba_cols = jnp.tile(w[:, QKVZ_W:QKVZ_W + 2 * N_HEADS], (1, LANES // (2 * N_HEADS)))
    p_off = QKVZ_W + 2 * N_HEADS
    w_proj = jnp.concatenate([w[:, 0:QKVZ_W], ba_cols, w[:, p_off:p_off + WIDTH_B], w[:, p_off + WIDTH_B:]],
                             axis=1).astype(BF16)
    zeros4 = jnp.zeros((N_HEADS,), F32)
    lane_tile = lambda v: jnp.tile(jnp.concatenate([zeros4, v.astype(F32)]), LANES // (2 * N_HEADS))[None, :]
    prm = {
        "w_proj": w_proj,
        "g_attn": g_attn[0][None, :], "g_mlp": g_mlp[0][None, :], "g_final": g_final[None, :],
        "w_conv": w_conv[0].astype(F32),
        "alog_lane": lane_tile(a_log[0]), "dtb_lane": lane_tile(dt_bias[0]),
        "w_onorm": w_onorm[0][None, :].astype(F32),
        "w_mix": w_pool_mix[0].astype(BF16), "pool_scale": pool_scale[0][None, :].astype(F32),
        "w_a_out": w_a_out[0].astype(BF16), "w_b_out": w_b_out[0].astype(BF16), "w_o": w_o[0].astype(BF16),
        "w_up": w_up[0].astype(BF16), "w_down": w_down[0].astype(BF16),
    }
    bp = x_prompt.shape[0]
    y_p, conv_p, pool_p, ssm_p = _trunk(
        x_prompt, jnp.zeros((bp, CONV_W - 1, QKV_W), F32), jnp.zeros((bp, POOL_HIST, WIDTH_B), F32),
        jnp.zeros((bp, N_HEADS, HEAD_DIM, HEAD_DIM), F32), 0, prm, tm=512, bb=1, tl=256, chunk=CHUNK)
    dec_len = x_sample.shape[1]
    y_s, conv_s, pool_s, ssm_s = _trunk(
        x_sample, state_conv[0].astype(F32), state_pool[0].astype(F32), state_ssm[0].astype(F32),
        PAST_LEN, prm, tm=512, bb=8, tl=dec_len, chunk=dec_len)
    return (y_p, y_s, conv_p.astype(state_conv.dtype), pool_p.astype(state_pool.dtype),
            ssm_p.astype(state_ssm.dtype), conv_s.astype(state_conv.dtype), pool_s.astype(state_pool.dtype),
            ssm_s.astype(state_ssm.dtype))
```

```python
import functools
import math

import jax
import jax.numpy as jnp
from jax import lax
from jax.experimental import pallas as pl
from jax.experimental.pallas import tpu as pltpu

D_MODEL = 1024
N_HEADS = 4
HEAD_DIM = 128
QK_W = N_HEADS * HEAD_DIM
QKV_W = 3 * QK_W
CONV_W = 4
POOL_WINDOWS = (2, 4, 8, 16)
POOL_GROUP = 128
WIDTH_B = len(POOL_WINDOWS) * POOL_GROUP
POOL_HIST = 15
D_FF = 4 * D_MODEL
EPS = 1e-6
PAST_LEN = 16384
LANES = 128
SUBLANES = 8

QKVZ_W = QKV_W + QK_W
BA_OFF = QKVZ_W
P_OFF = BA_OFF + LANES
GATE_OFF = P_OFF + WIDTH_B
PROJ_W = GATE_OFF + 2 * D_MODEL

GROUP = 128
SERIES_BLOCK = 64
CONV_PAD = 8
POOL_PAD = 16

VMEM_LIMIT = 56 * 1024 * 1024

BF16 = jnp.bfloat16
F32 = jnp.float32


def _dot(a, b):
    return jnp.dot(a.astype(BF16), b.astype(BF16), preferred_element_type=F32)


def _sigmoid(x):
    return 1.0 / (1.0 + jnp.exp(-x))


def _silu(x):
    return x * _sigmoid(x)


def _rms_scale(x):
    return lax.rsqrt(jnp.mean(x * x, axis=-1, keepdims=True) + EPS)


def _proj_kernel(x_ref, g_ref, w_ref, qkvz_ref, ba_ref, p_ref, gate_ref):
    x = x_ref[...]
    h = (x * _rms_scale(x) * g_ref[...]).astype(BF16)
    qkvz_ref[...] = jnp.dot(h, w_ref[:, 0:BA_OFF], preferred_element_type=F32)
    ba_ref[...] = jnp.dot(h, w_ref[:, BA_OFF:P_OFF], preferred_element_type=F32)
    p_ref[...] = jnp.dot(h, w_ref[:, P_OFF:GATE_OFF], preferred_element_type=F32)
    gate_ref[...] = jnp.dot(h, w_ref[:, GATE_OFF:PROJ_W], preferred_element_type=F32)


def _const_spec(shape):
    zeros = (0,) * len(shape)
    return pl.BlockSpec(shape, lambda *_: zeros, pipeline_mode=pl.Buffered(1))


def _project(x2d, g_attn, w_proj, tm):
    t = x2d.shape[0]
    row = lambda w: pl.BlockSpec((tm, w), lambda i: (i, 0))
    return pl.pallas_call(
        _proj_kernel,
        grid=(t // tm,),
        in_specs=[row(D_MODEL), _const_spec((1, D_MODEL)), _const_spec((D_MODEL, PROJ_W))],
        out_specs=[row(QKVZ_W), row(LANES), row(WIDTH_B), row(2 * D_MODEL)],
        out_shape=[jax.ShapeDtypeStruct((t, QKVZ_W), F32), jax.ShapeDtypeStruct((t, LANES), F32),
                   jax.ShapeDtypeStruct((t, WIDTH_B), F32), jax.ShapeDtypeStruct((t, 2 * D_MODEL), F32)],
        compiler_params=pltpu.CompilerParams(dimension_semantics=("arbitrary",),
                                             vmem_limit_bytes=VMEM_LIMIT),
        name="proj",
    )(x2d, g_attn, w_proj)


def _lane_prefix_sum(x, chunk, lane):
    shift = 1
    while shift < chunk:
        x = x + jnp.where(lane % chunk >= shift, pltpu.roll(x, shift, axis=1), 0.0)
        shift *= 2
    return x


def _lane_suffix_sum(x, chunk, lane):
    shift = 1
    while shift < chunk:
        x = x + jnp.where(lane % chunk + shift < chunk, pltpu.roll(x, LANES - shift, axis=1), 0.0)
        shift *= 2
    return x


def _unit_lower_inverses(neg_ms, eye, chunk, row, col):
    base = min(chunk, SERIES_BLOCK)
    n_factors = int(math.log2(base))
    if base == chunk:
        nbs = neg_ms
    else:
        in_base = row // base == col // base
        nbs = [jnp.where(in_base, m, 0.0) for m in neg_ms]
    invs = [eye + nb for nb in nbs]
    if n_factors > 1:
        pws = [_dot(nb, nb) for nb in nbs]
        for _ in range(n_factors - 2):
            boths = [_dot(jnp.concatenate([inv, pw], axis=0), pw) for inv, pw in zip(invs, pws)]
            invs = [inv + both[0:GROUP] for inv, both in zip(invs, boths)]
            pws = [both[GROUP:] for both in boths]
        invs = [inv + _dot(inv, pw) for inv, pw in zip(invs, pws)]
    size = base
    while size < chunk:
        off_block = (row // size != col // size) & (row // (2 * size) == col // (2 * size))
        ts = [_dot(jnp.where(off_block, m, 0.0), inv) for m, inv in zip(neg_ms, invs)]
        invs = [inv + _dot(inv, t) for inv, t in zip(invs, ts)]
        size *= 2
    return invs


def _mix_kernel(qkvz_ref, ba_ref, p_ref, convprev_ref, poolprev_ref, s0_ref, wconv_ref, alog_ref, dtb_ref,
                wonorm_ref, wmix_ref, pscale_ref,
                oa_ref, ob_ref, convnew_ref, poolnew_ref, snew_ref,
                ext_ref, pext_ref, *, bb, tl, chunk, pos0):
    tile = pl.program_id(1)
    rows_b = min(tl, GROUP)
    seqs_g = GROUP // rows_b
    groups_b = tl // rows_b
    n_groups = bb * tl // GROUP
    chained = chunk == GROUP
    assert chained or (chunk == tl and tl < GROUP), "chunk must be a whole group or a whole short sequence"

    @pl.when(tile == 0)
    def _():
        ext_ref[:, CONV_PAD - (CONV_W - 1):CONV_PAD, :] = convprev_ref[...]
        pext_ref[:, POOL_PAD - POOL_HIST:POOL_PAD, :] = poolprev_ref[...]
        snew_ref[...] = s0_ref[...]

    ext_ref[:, CONV_PAD:CONV_PAD + tl, :] = qkvz_ref[:, :, 0:QKV_W]
    convnew_ref[...] = ext_ref[:, CONV_PAD + tl - (CONV_W - 1):CONV_PAD + tl, :]
    pext_ref[:, POOL_PAD:POOL_PAD + tl, :] = p_ref[...]
    poolnew_ref[...] = pext_ref[:, POOL_PAD + tl - POOL_HIST:POOL_PAD + tl, :]

    row = lax.broadcasted_iota(jnp.int32, (GROUP, GROUP), 0)
    col = lax.broadcasted_iota(jnp.int32, (GROUP, GROUP), 1)
    causal = row >= col
    strict = row > col
    if not chained:
        same = (row // chunk) == (col // chunk)
        causal = causal & same
        strict = strict & same
    eye = jnp.where(row == col, 1.0, 0.0).astype(F32)
    lane8 = lax.broadcasted_iota(jnp.int32, (SUBLANES, LANES), 1)

    def origin(g):
        if tl >= GROUP:
            return g // groups_b, (g % groups_b) * GROUP
        return g * seqs_g, 0

    def load(ref, g, row_off, cols):
        b0, t0 = origin(g)
        blk = ref[b0:b0 + seqs_g, row_off + t0:row_off + t0 + rows_b, cols]
        return blk.reshape(GROUP, blk.shape[-1])

    def store(ref, g, cols, val):
        b0, t0 = origin(g)
        ref[b0:b0 + seqs_g, t0:t0 + rows_b, cols] = val.reshape(seqs_g, rows_b, val.shape[-1])

    gates = []
    for g in range(n_groups):
        x8 = load(ba_ref, g, 0, slice(None)).T[0:SUBLANES, :]
        xs = x8 + dtb_ref[...]
        softplus = jnp.maximum(xs, 0.0) + jnp.log1p(jnp.exp(-jnp.abs(xs)))
        graw8 = -jnp.exp(alog_ref[...]) * softplus
        g8 = _lane_prefix_sum(graw8, chunk, lane8)
        kds8 = jnp.exp(_lane_suffix_sum(graw8, chunk, lane8) - graw8)
        gates.append((_sigmoid(x8), g8, jnp.exp(g8), kds8))

    probs = [(g, h) for g in range(n_groups) for h in range(N_HEADS)]

    def conv_silu(g, c0):
        cols = slice(c0, c0 + LANES)
        acc = wconv_ref[0:1, cols] * load(ext_ref, g, CONV_PAD - 3, cols)
        for j in range(1, CONV_W):
            acc = acc + wconv_ref[j:j + 1, cols] * load(ext_ref, g, CONV_PAD - 3 + j, cols)
        return _silu(acc)

    st = []
    for g, h in probs:
        beta8, g8, eg8, kds8 = gates[g]
        qc = conv_silu(g, h * HEAD_DIM)
        kc = conv_silu(g, QK_W + h * HEAD_DIM)
        d = {"v": conv_silu(g, 2 * QK_W + h * HEAD_DIM)}
        d["q"] = qc * (lax.rsqrt(jnp.sum(qc * qc, axis=-1, keepdims=True) + EPS) * (HEAD_DIM ** -0.5))
        d["k"] = kc * lax.rsqrt(jnp.sum(kc * kc, axis=-1, keepdims=True) + EPS)
        d["beta_row"] = beta8[h:h + 1, :]
        d["eg_row"] = eg8[N_HEADS + h:N_HEADS + h + 1, :]
        d["g_row"] = g8[N_HEADS + h:N_HEADS + h + 1, :]
        d["kb_row"] = kds8[N_HEADS + h:N_HEADS + h + 1, :] * d["beta_row"]
        d["g_col"] = jnp.broadcast_to(d["g_row"], (GROUP, GROUP)).T
        d["kt"] = d["k"].T
        st.append(d)
    for d in st:
        both = _dot(jnp.concatenate([d["k"], d["q"]], axis=0), d["kt"])
        d["kk"], d["qk"] = both[0:GROUP], both[GROUP:]
    for d in st:
        decay = jnp.exp(jnp.where(causal, d["g_col"] - d["g_row"], -jnp.inf))
        d["neg_m"] = jnp.where(strict, -(d.pop("kk") * decay), 0.0) * d["beta_row"]
        d["qkm"] = d.pop("qk") * decay * d["beta_row"]
        d["q_dec"] = d.pop("q") * jnp.exp(d["g_col"])
        d["kt_dec"] = d.pop("kt") * d["kb_row"]
    invs = _unit_lower_inverses([d.pop("neg_m") for d in st], eye, chunk, row, col)
    for d, inv in zip(st, invs):
        d["uy"] = _dot(inv, d.pop("v"))
        d["wy"] = _dot(inv * d["eg_row"], d.pop("k"))

    outs = {}
    if chained:
        for j in range(groups_b):
            wave = [(i, g, h) for i, (g, h) in enumerate(probs) if g % groups_b == j]
            s_old = {i: snew_ref[origin(g)[0], h] for i, g, h in wave}
            ws = {i: _dot(jnp.concatenate([st[i]["wy"], st[i]["q_dec"]], axis=0), s_old[i]) for i, g, h in wave}
            ys = {i: st[i]["uy"] - ws[i][0:GROUP] for i, g, h in wave}
            for i, g, h in wave:
                outs[i] = ws[i][GROUP:] + _dot(st[i]["qkm"], ys[i])
            for i, g, h in wave:
                last = jnp.exp(st[i]["g_col"][GROUP - 1:GROUP, :])
                snew_ref[origin(g)[0], h] = s_old[i] * last + _dot(st[i]["kt_dec"], ys[i])
    else:
        for i, (g, h) in enumerate(probs):
            d, b0 = st[i], origin(g)[0]
            ws_w, ws_q = [], []
            for s_i in range(seqs_g):
                r = slice(s_i * rows_b, (s_i + 1) * rows_b)
                ws = _dot(jnp.concatenate([d["wy"][r], d["q_dec"][r]], axis=0), snew_ref[b0 + s_i, h])
                ws_w.append(ws[0:rows_b])
                ws_q.append(ws[rows_b:])
            d["y"] = d["uy"] - jnp.concatenate(ws_w, axis=0)
            outs[i] = jnp.concatenate(ws_q, axis=0) + _dot(d["qkm"], d["y"])
        for i, (g, h) in enumerate(probs):
            d, b0 = st[i], origin(g)[0]
            for s_i in range(seqs_g):
                last = jnp.exp(d["g_col"][(s_i + 1) * rows_b - 1:(s_i + 1) * rows_b, :])
                upd = _dot(jnp.where(col // rows_b == s_i, d["kt_dec"], 0.0), d["y"])
                snew_ref[b0 + s_i, h] = snew_ref[b0 + s_i, h] * last + upd

    for i, (g, h) in enumerate(probs):
        o = outs[i]
        z = load(qkvz_ref, g, 0, slice(QKV_W + h * HEAD_DIM, QKV_W + (h + 1) * HEAD_DIM))
        store(oa_ref, g, slice(h * HEAD_DIM, (h + 1) * HEAD_DIM), o * _rms_scale(o) * wonorm_ref[...] * _silu(z))

    for g in range(n_groups):
        pos = pos0 + tile * tl + origin(g)[1] + row % rows_b
        for gi, win in enumerate(POOL_WINDOWS):
            cols = slice(gi * POOL_GROUP, (gi + 1) * POOL_GROUP)
            acc = load(pext_ref, g, POOL_PAD, cols)
            for sft in range(1, win):
                acc = acc + load(pext_ref, g, POOL_PAD - sft, cols)
            cnt = jnp.minimum(pos + 1, win).astype(F32)
            pooled = acc / cnt - load(p_ref, g, 0, cols)
            store(ob_ref, g, cols, _dot(pooled, wmix_ref[gi]) * pscale_ref[:, cols])

    ext_ref[:, CONV_PAD - (CONV_W - 1):CONV_PAD, :] = convnew_ref[...]
    pext_ref[:, POOL_PAD - POOL_HIST:POOL_PAD, :] = poolnew_ref[...]


def _mix(qkvz, ba, p, conv_prev, pool_prev, s0, w_conv, alog8, dtb8, w_onorm, w_mix, pool_scale,
         *, bb, tl, chunk, pos0):
    bsz, l = qkvz.shape[0], qkvz.shape[1]
    seq = lambda w: pl.BlockSpec((bb, tl, w), lambda i, j: (i, j, 0))
    state3 = lambda r, w: pl.BlockSpec((bb, r, w), lambda i, j: (i, 0, 0))
    state_s = pl.BlockSpec((bb, N_HEADS, HEAD_DIM, HEAD_DIM), lambda i, j: (i, 0, 0, 0))
    kern = functools.partial(_mix_kernel, bb=bb, tl=tl, chunk=chunk, pos0=pos0)
    return pl.pallas_call(
        kern,
        grid=(bsz // bb, l // tl),
        in_specs=[seq(QKVZ_W), seq(LANES), seq(WIDTH_B),
                  state3(CONV_W - 1, QKV_W), state3(POOL_HIST, WIDTH_B), state_s,
                  _const_spec((CONV_W, QKV_W)), _const_spec((SUBLANES, LANES)), _const_spec((SUBLANES, LANES)),
                  _const_spec((1, HEAD_DIM)),
                  _const_spec((len(POOL_WINDOWS), POOL_GROUP, POOL_GROUP)), _const_spec((1, WIDTH_B))],
        out_specs=[seq(QK_W), seq(WIDTH_B), state3(CONV_W - 1, QKV_W), state3(POOL_HIST, WIDTH_B), state_s],
        out_shape=[jax.ShapeDtypeStruct((bsz, l, QK_W), F32), jax.ShapeDtypeStruct((bsz, l, WIDTH_B), F32),
                   jax.ShapeDtypeStruct((bsz, CONV_W - 1, QKV_W), F32),
                   jax.ShapeDtypeStruct((bsz, POOL_HIST, WIDTH_B), F32),
                   jax.ShapeDtypeStruct((bsz, N_HEADS, HEAD_DIM, HEAD_DIM), F32)],
        scratch_shapes=[pltpu.VMEM((bb, CONV_PAD + tl, QKV_W), F32), pltpu.VMEM((bb, POOL_PAD + tl, WIDTH_B), F32)],
        compiler_params=pltpu.CompilerParams(dimension_semantics=("arbitrary", "arbitrary"),
                                             vmem_limit_bytes=VMEM_LIMIT),
        name="mix",
    )(qkvz, ba, p, conv_prev, pool_prev, s0, w_conv, alog8, dtb8, w_onorm, w_mix, pool_scale)


FF_BLOCK = 1024


def _mlp_kernel(x_ref, oa_ref, ob_ref, gate_ref, wa_ref, wb_ref, wo_ref, gmlp_ref, wup_ref, wdown_ref, gfin_ref,
                y_ref):
    ma = _dot(oa_ref[...], wa_ref[...])
    mb = _dot(ob_ref[...], wb_ref[...])
    merged = _sigmoid(gate_ref[:, 0:D_MODEL]) * ma + _sigmoid(gate_ref[:, D_MODEL:]) * mb
    x1 = x_ref[...] + _dot(merged, wo_ref[...])
    h2 = (x1 * _rms_scale(x1) * gmlp_ref[...]).astype(BF16)
    acc = x1
    for c0 in range(0, D_FF, FF_BLOCK):
        up = jnp.dot(h2, wup_ref[:, c0:c0 + FF_BLOCK], preferred_element_type=F32)
        act = jnp.square(jnp.maximum(up, 0.0))
        acc = acc + _dot(act, wdown_ref[c0:c0 + FF_BLOCK, :])
    y_ref[...] = acc * _rms_scale(acc) * gfin_ref[...]


def _merge_mlp(x2d, oa, ob, gates, wa, wb, wo, g_mlp, w_up, w_down, g_final, tm):
    t = x2d.shape[0]
    row = lambda w: pl.BlockSpec((tm, w), lambda i: (i, 0))
    return pl.pallas_call(
        _mlp_kernel,
        grid=(t // tm,),
        in_specs=[row(D_MODEL), row(QK_W), row(WIDTH_B), row(2 * D_MODEL),
                  _const_spec((QK_W, D_MODEL)), _const_spec((WIDTH_B, D_MODEL)), _const_spec((D_MODEL, D_MODEL)),
                  _const_spec((1, D_MODEL)), _const_spec((D_MODEL, D_FF)), _const_spec((D_FF, D_MODEL)),
                  _const_spec((1, D_MODEL))],
        out_specs=row(D_MODEL),
        out_shape=jax.ShapeDtypeStruct((t, D_MODEL), F32),
        compiler_params=pltpu.CompilerParams(dimension_semantics=("arbitrary",),
                                             vmem_limit_bytes=VMEM_LIMIT),
        name="merge_mlp",
    )(x2d, oa, ob, gates, wa, wb, wo, g_mlp, w_up, w_down, g_final)


def _trunk(x, conv_prev, pool_prev, s_prev, pos0, prm, *, tm, bb, tl, chunk):
    bsz, l, _ = x.shape
    x2d = x.reshape(bsz * l, D_MODEL)
    qkvz, ba, p, gates = _project(x2d, prm["g_attn"], prm["w_proj"], tm)
    oa, ob, conv_new, pool_new, s_new = _mix(
        qkvz.reshape(bsz, l, QKVZ_W), ba.reshape(bsz, l, LANES), p.reshape(bsz, l, WIDTH_B),
        conv_prev, pool_prev, s_prev, prm["w_conv"], prm["alog8"], prm["dtb8"], prm["w_onorm"],
        prm["w_mix"], prm["pool_scale"], bb=bb, tl=tl, chunk=chunk, pos0=pos0)
    y = _merge_mlp(x2d, oa.reshape(bsz * l, QK_W), ob.reshape(bsz * l, WIDTH_B), gates,
                   prm["w_a_out"], prm["w_b_out"], prm["w_o"], prm["g_mlp"], prm["w_up"], prm["w_down"],
                   prm["g_final"], tm)
    return y.reshape(bsz, l, D_MODEL), conv_new[None], pool_new[None], s_new[None]


def kernel(x_prompt, x_sample, state_conv, state_pool, state_ssm, w_in, w_conv, a_log, dt_bias, w_onorm,
           w_pool_mix, pool_scale, w_a_out, w_b_out, w_o, g_attn, g_mlp, w_up, w_down, g_final):
    assert w_in.shape[0] == 1, "single-layer decoder"
    w = w_in[0]
    ba_cols = jnp.tile(w[:, QKVZ_W:QKVZ_W + 2 * N_HEADS], (1, LANES // (2 * N_HEADS)))
    p_off = QKVZ_W + 2 * N_HEADS
    w_proj = jnp.concatenate([w[:, 0:QKVZ_W], ba_cols, w[:, p_off:p_off + WIDTH_B], w[:, p_off + WIDTH_B:]],
                             axis=1).astype(BF16)
    zeros4 = jnp.zeros((N_HEADS,), F32)
    rows8 = lambda v: jnp.broadcast_to(jnp.concatenate([zeros4, v.astype(F32)])[:, None], (SUBLANES, LANES))
    prm = {
        "w_proj": w_proj,
        "g_attn": g_attn[0][None, :], "g_mlp": g_mlp[0][None, :], "g_final": g_final[None, :],
        "w_conv": w_conv[0].astype(F32),
        "alog8": rows8(a_log[0]), "dtb8": rows8(dt_bias[0]),
        "w_onorm": w_onorm[0][None, :].astype(F32),
        "w_mix": w_pool_mix[0].astype(BF16), "pool_scale": pool_scale[0][None, :].astype(F32),
        "w_a_out": w_a_out[0].astype(BF16), "w_b_out": w_b_out[0].astype(BF16), "w_o": w_o[0].astype(BF16),
        "w_up": w_up[0].astype(BF16), "w_down": w_down[0].astype(BF16),
    }
    bp = x_prompt.shape[0]
    y_p, conv_p, pool_p, ssm_p = _trunk(
        x_prompt, jnp.zeros((bp, CONV_W - 1, QKV_W), F32), jnp.zeros((bp, POOL_HIST, WIDTH_B), F32),
        jnp.zeros((bp, N_HEADS, HEAD_DIM, HEAD_DIM), F32), 0, prm, tm=512, bb=1, tl=256, chunk=GROUP)
    dec_len = x_sample.shape[1]
    y_s, conv_s, pool_s, ssm_s = _trunk(
        x_sample, state_conv[0].astype(F32), state_pool[0].astype(F32), state_ssm[0].astype(F32),
        PAST_LEN, prm, tm=512, bb=GROUP // dec_len, tl=dec_len, chunk=dec_len)
    return (y_p, y_s, conv_p.astype(state_conv.dtype), pool_p.astype(state_pool.dtype),
            ssm_p.astype(state_ssm.dtype), conv_s.astype(state_conv.dtype), pool_s.astype(state_pool.dtype),
            ssm_s.astype(state_ssm.dtype))
```

```python
import functools
import math

import jax
import jax.numpy as jnp
from jax import lax
from jax.experimental import pallas as pl
from jax.experimental.pallas import tpu as pltpu

D_MODEL = 1024
N_HEADS = 4
HEAD_DIM = 128
QK_W = N_HEADS * HEAD_DIM
QKV_W = 3 * QK_W
CONV_W = 4
POOL_WINDOWS = (2, 4, 8, 16)
POOL_GROUP = 128
WIDTH_B = len(POOL_WINDOWS) * POOL_GROUP
POOL_HIST = 15
D_FF = 4 * D_MODEL
EPS = 1e-6
PAST_LEN = 16384
LANES = 128
SUBLANES = 8
MXU_COLS = 256
N_GATE_ROWS = 4

QKVZ_W = QKV_W + QK_W
GATE_SCALARS = 2 * N_HEADS
REST_W = WIDTH_B + 2 * D_MODEL

GROUP = 128
SERIES_BLOCK = 64
CONV_PAD = 8
POOL_PAD = 16
FF_BLOCK = 1024

VMEM_LIMIT = 56 * 1024 * 1024

BF16 = jnp.bfloat16
F32 = jnp.float32


def _dot(a, b):
    return jnp.dot(a.astype(BF16), b.astype(BF16), preferred_element_type=F32)


def _sigmoid(x):
    return 1.0 / (1.0 + jnp.exp(-x))


def _silu(x):
    return x * _sigmoid(x)


def _rms_scale(x):
    return lax.rsqrt(jnp.mean(x * x, axis=-1, keepdims=True) + EPS)


def _const_spec(shape):
    zeros = (0,) * len(shape)
    return pl.BlockSpec(shape, lambda *_: zeros, pipeline_mode=pl.Buffered(1))


def _seq_spec(bb, tl, width):
    return pl.BlockSpec((bb, tl, width), lambda i, j: (i, j, 0))


def _state_spec(bb, n_rows, width):
    return pl.BlockSpec((bb, n_rows, width), lambda i, j: (i, 0, 0))


def _gate_rows_spec(bb, tl, n_tiles):
    return pl.BlockSpec((N_GATE_ROWS, SUBLANES, bb * tl), lambda i, j: (0, 0, i * n_tiles + j))


def _lane_prefix_sum(x, chunk):
    lane = lax.broadcasted_iota(jnp.int32, x.shape, 1)
    shift = 1
    while shift < chunk:
        x = x + jnp.where(lane % chunk >= shift, pltpu.roll(x, shift, axis=1), 0.0)
        shift *= 2
    return x


def _lane_suffix_sum(x, chunk):
    lane = lax.broadcasted_iota(jnp.int32, x.shape, 1)
    shift = 1
    while shift < chunk:
        x = x + jnp.where(lane % chunk + shift < chunk, pltpu.roll(x, x.shape[1] - shift, axis=1), 0.0)
        shift *= 2
    return x


def _front_kernel(x_ref, g_ref, wq_ref, wba_ref, wr_ref, wconv_ref, alog_ref, dtb_ref, keep_ref, convprev_ref,
                  qkv_ref, z_ref, gsc_ref, p_ref, gate_ref, convnew_ref,
                  ext_ref, ba_ref, *, bb, tl, n_tiles, chunk):
    tile = pl.program_id(1)
    rows = bb * tl
    hist = slice(CONV_PAD - (CONV_W - 1), CONV_PAD)

    @pl.when(tile == 0)
    def _():
        ext_ref[:, hist, :] = convprev_ref[...]

    x = x_ref[...].reshape(rows, D_MODEL)
    h = (x * _rms_scale(x) * g_ref[...]).astype(BF16)
    proj = lambda w: jnp.dot(h, w, preferred_element_type=F32)
    ext_ref[:, CONV_PAD:CONV_PAD + tl, :] = proj(wq_ref[:, 0:QKV_W]).reshape(bb, tl, QKV_W)
    convnew_ref[...] = ext_ref[:, CONV_PAD + tl - (CONV_W - 1):CONV_PAD + tl, :]

    ba_ref[...] = proj(wba_ref[...])
    x8 = ba_ref[...].T[0:SUBLANES, :]
    xs = x8 + dtb_ref[...]
    softplus = jnp.maximum(xs, 0.0) + jnp.log1p(jnp.exp(-jnp.abs(xs)))
    graw8 = -jnp.exp(alog_ref[...]) * softplus
    g8 = _lane_prefix_sum(graw8, chunk)
    gsc_ref[0] = _sigmoid(x8)
    gsc_ref[1] = g8
    gsc_ref[2] = jnp.exp(g8)
    gsc_ref[3] = jnp.exp(_lane_suffix_sum(graw8, chunk) - graw8)

    keep = keep_ref[...] != 0

    def conv_block(c0, dep):
        cols = slice(c0, c0 + LANES)
        acc = wconv_ref[0:1, cols] * ext_ref[:, CONV_PAD - 3:CONV_PAD - 3 + tl, cols]
        for j in range(1, CONV_W):
            acc = acc + wconv_ref[j:j + 1, cols] * ext_ref[:, CONV_PAD - 3 + j:CONV_PAD - 3 + j + tl, cols]
        val = _silu(acc)
        if c0 < 2 * QK_W:
            scale = HEAD_DIM ** -0.5 if c0 < QK_W else 1.0
            val = val * (lax.rsqrt(jnp.sum(val * val, axis=-1, keepdims=True) + EPS) * scale)
        qkv_ref[:, :, cols] = jnp.where(keep, val, dep[:, 0:LANES].reshape(bb, tl, LANES))

    def proj_block(out_ref, w_ref, w0, c0):
        val = proj(w_ref[:, w0 + c0:w0 + c0 + MXU_COLS])
        out_ref[:, :, c0:c0 + MXU_COLS] = val.reshape(bb, tl, MXU_COLS)
        return val

    mxu_work = ([functools.partial(proj_block, z_ref, wq_ref, QKV_W, c0) for c0 in range(0, QK_W, MXU_COLS)]
                + [functools.partial(proj_block, p_ref, wr_ref, 0, c0) for c0 in range(0, WIDTH_B, MXU_COLS)]
                + [functools.partial(proj_block, gate_ref, wr_ref, WIDTH_B, c0)
                   for c0 in range(0, 2 * D_MODEL, MXU_COLS)])
    for i, c0 in enumerate(range(0, QKV_W, LANES)):
        conv_block(c0, mxu_work[i]())

    if n_tiles > 1:
        ext_ref[:, hist, :] = convnew_ref[...]


def _front(x, g_attn, wq, wba, wr, w_conv, a_log, dt_bias, conv_prev, *, bb, tl, chunk):
    bsz, l, _ = x.shape
    n_tiles = l // tl
    kern = functools.partial(_front_kernel, bb=bb, tl=tl, n_tiles=n_tiles, chunk=chunk)
    sds = lambda *shape: jax.ShapeDtypeStruct(shape, F32)
    zeros4 = jnp.zeros((N_HEADS,), F32)
    rows8 = lambda v: jnp.broadcast_to(jnp.concatenate([zeros4, v.astype(F32)])[:, None], (SUBLANES, bb * tl))
    return pl.pallas_call(
        kern,
        grid=(bsz // bb, n_tiles),
        in_specs=[_seq_spec(bb, tl, D_MODEL), _const_spec((1, D_MODEL)), _const_spec((D_MODEL, QKVZ_W)),
                  _const_spec((D_MODEL, LANES)), _const_spec((D_MODEL, REST_W)), _const_spec((CONV_W, QKV_W)),
                  _const_spec((SUBLANES, bb * tl)), _const_spec((SUBLANES, bb * tl)), _const_spec((1, LANES)),
                  _state_spec(bb, CONV_W - 1, QKV_W)],
        out_specs=[_seq_spec(bb, tl, QKV_W), _seq_spec(bb, tl, QK_W), _gate_rows_spec(bb, tl, n_tiles),
                   _seq_spec(bb, tl, WIDTH_B), _seq_spec(bb, tl, 2 * D_MODEL), _state_spec(bb, CONV_W - 1, QKV_W)],
        out_shape=[sds(bsz, l, QKV_W), sds(bsz, l, QK_W), sds(N_GATE_ROWS, SUBLANES, bsz * l),
                   sds(bsz, l, WIDTH_B), sds(bsz, l, 2 * D_MODEL), sds(bsz, CONV_W - 1, QKV_W)],
        scratch_shapes=[pltpu.VMEM((bb, CONV_PAD + tl, QKV_W), F32), pltpu.VMEM((bb * tl, LANES), F32)],
        compiler_params=pltpu.CompilerParams(dimension_semantics=("arbitrary", "arbitrary"),
                                             vmem_limit_bytes=VMEM_LIMIT),
        name="front",
    )(x, g_attn, wq, wba, wr, w_conv, rows8(a_log), rows8(dt_bias), jnp.ones((1, LANES), jnp.int32), conv_prev)


def _unit_lower_inverses(neg_ms, eye, chunk, row, col):
    base = min(chunk, SERIES_BLOCK)
    n_factors = int(math.log2(base))
    if base == chunk:
        nbs = neg_ms
    else:
        in_base = row // base == col // base
        nbs = [jnp.where(in_base, m, 0.0) for m in neg_ms]
    invs = [eye + nb for nb in nbs]
    if n_factors > 1:
        pws = [_dot(nb, nb) for nb in nbs]
        for _ in range(n_factors - 2):
            boths = [_dot(jnp.concatenate([inv, pw], axis=0), pw) for inv, pw in zip(invs, pws)]
            invs = [inv + both[0:GROUP] for inv, both in zip(invs, boths)]
            pws = [both[GROUP:] for both in boths]
        invs = [inv + _dot(inv, pw) for inv, pw in zip(invs, pws)]
    size = base
    while size < chunk:
        off_block = (row // size != col // size) & (row // (2 * size) == col // (2 * size))
        ts = [_dot(jnp.where(off_block, m, 0.0), inv) for m, inv in zip(neg_ms, invs)]
        invs = [inv + _dot(inv, t) for inv, t in zip(invs, ts)]
        size *= 2
    return invs


def _delta_kernel(qkv_ref, z_ref, gsc_ref, p_ref, poolprev_ref, s0_ref, wonorm_ref, wmix_ref, pscale_ref,
                  oa_ref, ob_ref, poolnew_ref, snew_ref, pext_ref, *, bb, tl, chunk, pos0, n_tiles):
    tile = pl.program_id(1)
    rows_b = min(tl, GROUP)
    seqs_g = GROUP // rows_b
    groups_b = tl // rows_b
    n_groups = bb * tl // GROUP
    chained = chunk == GROUP
    assert chained or (chunk == tl and tl < GROUP), "chunk must be a whole group or a whole short sequence"
    pool_hist = slice(POOL_PAD - POOL_HIST, POOL_PAD)

    @pl.when(tile == 0)
    def _():
        snew_ref[...] = s0_ref[...]
        pext_ref[:, pool_hist, :] = poolprev_ref[...]

    pext_ref[:, POOL_PAD:POOL_PAD + tl, :] = p_ref[...]
    poolnew_ref[...] = pext_ref[:, POOL_PAD + tl - POOL_HIST:POOL_PAD + tl, :]

    row = lax.broadcasted_iota(jnp.int32, (GROUP, GROUP), 0)
    col = lax.broadcasted_iota(jnp.int32, (GROUP, GROUP), 1)
    causal = row >= col
    strict = row > col
    if not chained:
        same = (row // chunk) == (col // chunk)
        causal = causal & same
        strict = strict & same
    eye = jnp.where(row == col, 1.0, 0.0).astype(F32)

    def origin(g):
        if tl >= GROUP:
            return g // groups_b, (g % groups_b) * GROUP
        return g * seqs_g, 0

    def load(c0, g):
        b0, t0 = origin(g)
        return qkv_ref[b0:b0 + seqs_g, t0:t0 + rows_b, c0:c0 + HEAD_DIM].reshape(GROUP, HEAD_DIM)

    gates = [tuple(gsc_ref[q, :, g * GROUP:(g + 1) * GROUP] for q in range(N_GATE_ROWS)) for g in range(n_groups)]

    probs = [(g, h) for g in range(n_groups) for h in range(N_HEADS)]
    st = []
    for g, h in probs:
        beta8, g8, eg8, kds8 = gates[g]
        d = {"q": load(h * HEAD_DIM, g), "k": load(QK_W + h * HEAD_DIM, g), "v": load(2 * QK_W + h * HEAD_DIM, g)}
        d["beta_row"] = beta8[h:h + 1, :]
        d["eg_row"] = eg8[N_HEADS + h:N_HEADS + h + 1, :]
        d["g_row"] = g8[N_HEADS + h:N_HEADS + h + 1, :]
        d["kb_row"] = kds8[N_HEADS + h:N_HEADS + h + 1, :] * d["beta_row"]
        d["g_col"] = jnp.broadcast_to(d["g_row"], (GROUP, GROUP)).T
        d["kt"] = d["k"].T
        st.append(d)
    for d in st:
        both = _dot(jnp.concatenate([d["k"], d["q"]], axis=0), d["kt"])
        d["kk"], d["qk"] = both[0:GROUP], both[GROUP:]
    for d in st:
        decay = jnp.exp(jnp.where(causal, d["g_col"] - d["g_row"], -jnp.inf))
        d["neg_m"] = jnp.where(strict, -(d.pop("kk") * decay), 0.0) * d["beta_row"]
        d["qkm"] = d.pop("qk") * decay * d["beta_row"]
        d["q_dec"] = d.pop("q") * jnp.exp(d["g_col"])
        d["kt_dec"] = d.pop("kt") * d["kb_row"]
    invs = _unit_lower_inverses([d.pop("neg_m") for d in st], eye, chunk, row, col)
    for d, inv in zip(st, invs):
        d["uy"] = _dot(inv, d.pop("v"))
        d["wy"] = _dot(inv * d["eg_row"], d.pop("k"))

    outs = {}
    if chained:
        for j in range(groups_b):
            wave = [(i, g, h) for i, (g, h) in enumerate(probs) if g % groups_b == j]
            s_old = {i: snew_ref[origin(g)[0], h] for i, g, h in wave}
            ws = {i: _dot(jnp.concatenate([st[i]["wy"], st[i]["q_dec"]], axis=0), s_old[i]) for i, g, h in wave}
            ys = {i: st[i]["uy"] - ws[i][0:GROUP] for i, g, h in wave}
            for i, g, h in wave:
                outs[i] = ws[i][GROUP:] + _dot(st[i]["qkm"], ys[i])
            for i, g, h in wave:
                last = jnp.exp(st[i]["g_col"][GROUP - 1:GROUP, :])
                snew_ref[origin(g)[0], h] = s_old[i] * last + _dot(st[i]["kt_dec"], ys[i])
    else:
        for i, (g, h) in enumerate(probs):
            d, b0 = st[i], origin(g)[0]
            ws_w, ws_q = [], []
            for s_i in range(seqs_g):
                r = slice(s_i * rows_b, (s_i + 1) * rows_b)
                ws = _dot(jnp.concatenate([d["wy"][r], d["q_dec"][r]], axis=0), snew_ref[b0 + s_i, h])
                ws_w.append(ws[0:rows_b])
                ws_q.append(ws[rows_b:])
            d["y"] = d["uy"] - jnp.concatenate(ws_w, axis=0)
            outs[i] = jnp.concatenate(ws_q, axis=0) + _dot(d["qkm"], d["y"])
        for i, (g, h) in enumerate(probs):
            d, b0 = st[i], origin(g)[0]
            for s_i in range(seqs_g):
                last = jnp.exp(d["g_col"][(s_i + 1) * rows_b - 1:(s_i + 1) * rows_b, :])
                upd = _dot(jnp.where(col // rows_b == s_i, d["kt_dec"], 0.0), d["y"])
                snew_ref[b0 + s_i, h] = snew_ref[b0 + s_i, h] * last + upd

    def group_slab(ref, g, cols, row_off=0):
        b0, t0 = origin(g)
        return ref.at[b0:b0 + seqs_g, row_off + t0:row_off + t0 + rows_b, cols]

    for i, (g, h) in enumerate(probs):
        cols = slice(h * HEAD_DIM, (h + 1) * HEAD_DIM)
        o = outs[i]
        z = group_slab(z_ref, g, cols)[...].reshape(GROUP, HEAD_DIM)
        group_slab(oa_ref, g, cols)[...] = (o * _rms_scale(o) * wonorm_ref[...] * _silu(z)).reshape(
            seqs_g, rows_b, HEAD_DIM)

    for g in range(n_groups):
        pos = pos0 + tile * tl + origin(g)[1] + row % rows_b
        for gi, win in enumerate(POOL_WINDOWS):
            cols = slice(gi * POOL_GROUP, (gi + 1) * POOL_GROUP)
            cur = group_slab(pext_ref, g, cols, POOL_PAD)[...]
            acc = cur
            for sft in range(1, win):
                acc = acc + group_slab(pext_ref, g, cols, POOL_PAD - sft)[...]
            pooled = (acc / jnp.minimum(pos + 1, win).astype(F32).reshape(seqs_g, rows_b, POOL_GROUP) - cur)
            mixed = _dot(pooled.reshape(GROUP, POOL_GROUP), wmix_ref[gi]) * pscale_ref[:, cols]
            group_slab(ob_ref, g, cols)[...] = mixed.reshape(seqs_g, rows_b, POOL_GROUP)

    if n_tiles > 1:
        pext_ref[:, pool_hist, :] = poolnew_ref[...]


def _delta(qkv, z, gsc, p, pool_prev, s0, w_onorm, w_mix, pool_scale, *, bb, tl, chunk, pos0):
    bsz, l = qkv.shape[0], qkv.shape[1]
    n_tiles = l // tl
    state_s = pl.BlockSpec((bb, N_HEADS, HEAD_DIM, HEAD_DIM), lambda i, j: (i, 0, 0, 0))
    kern = functools.partial(_delta_kernel, bb=bb, tl=tl, chunk=chunk, pos0=pos0, n_tiles=n_tiles)
    sds = lambda *shape: jax.ShapeDtypeStruct(shape, F32)
    return pl.pallas_call(
        kern,
        grid=(bsz // bb, n_tiles),
        in_specs=[_seq_spec(bb, tl, QKV_W), _seq_spec(bb, tl, QK_W), _gate_rows_spec(bb, tl, n_tiles),
                  _seq_spec(bb, tl, WIDTH_B), _state_spec(bb, POOL_HIST, WIDTH_B), state_s,
                  _const_spec((1, HEAD_DIM)), _const_spec((len(POOL_WINDOWS), POOL_GROUP, POOL_GROUP)),
                  _const_spec((1, WIDTH_B))],
        out_specs=[_seq_spec(bb, tl, QK_W), _seq_spec(bb, tl, WIDTH_B), _state_spec(bb, POOL_HIST, WIDTH_B), state_s],
        out_shape=[sds(bsz, l, QK_W), sds(bsz, l, WIDTH_B), sds(bsz, POOL_HIST, WIDTH_B),
                   sds(bsz, N_HEADS, HEAD_DIM, HEAD_DIM)],
        scratch_shapes=[pltpu.VMEM((bb, POOL_PAD + tl, WIDTH_B), F32)],
        compiler_params=pltpu.CompilerParams(dimension_semantics=("arbitrary", "arbitrary"),
                                             vmem_limit_bytes=VMEM_LIMIT),
        name="delta",
    )(qkv, z, gsc, p, pool_prev, s0, w_onorm, w_mix, pool_scale)


def _mlp_kernel(x_ref, oa_ref, ob_ref, gate_ref, wa_ref, wb_ref, wo_ref, gmlp_ref, wup_ref, wdown_ref, gfin_ref,
                y_ref):
    ma = _dot(oa_ref[...], wa_ref[...])
    mb = _dot(ob_ref[...], wb_ref[...])
    merged = _sigmoid(gate_ref[:, 0:D_MODEL]) * ma + _sigmoid(gate_ref[:, D_MODEL:]) * mb
    x1 = x_ref[...] + _dot(merged, wo_ref[...])
    h2 = (x1 * _rms_scale(x1) * gmlp_ref[...]).astype(BF16)
    acc = x1
    for c0 in range(0, D_FF, FF_BLOCK):
        up = jnp.dot(h2, wup_ref[:, c0:c0 + FF_BLOCK], preferred_element_type=F32)
        act = jnp.square(jnp.maximum(up, 0.0))
        acc = acc + _dot(act, wdown_ref[c0:c0 + FF_BLOCK, :])
    y_ref[...] = acc * _rms_scale(acc) * gfin_ref[...]


def _merge_mlp(x2d, oa, ob, gates, prm, tm):
    t = x2d.shape[0]
    row = lambda w: pl.BlockSpec((tm, w), lambda i: (i, 0))
    return pl.pallas_call(
        _mlp_kernel,
        grid=(t // tm,),
        in_specs=[row(D_MODEL), row(QK_W), row(WIDTH_B), row(2 * D_MODEL),
                  _const_spec((QK_W, D_MODEL)), _const_spec((WIDTH_B, D_MODEL)), _const_spec((D_MODEL, D_MODEL)),
                  _const_spec((1, D_MODEL)), _const_spec((D_MODEL, D_FF)), _const_spec((D_FF, D_MODEL)),
                  _const_spec((1, D_MODEL))],
        out_specs=row(D_MODEL),
        out_shape=jax.ShapeDtypeStruct((t, D_MODEL), F32),
        compiler_params=pltpu.CompilerParams(dimension_semantics=("arbitrary",),
                                             vmem_limit_bytes=VMEM_LIMIT),
        name="merge_mlp",
    )(x2d, oa, ob, gates, prm["w_a_out"], prm["w_b_out"], prm["w_o"], prm["g_mlp"], prm["w_up"], prm["w_down"],
      prm["g_final"])


def _trunk(x, conv_prev, pool_prev, s_prev, pos0, prm, *, front_blk, delta_blk, tm, chunk):
    bsz, l, _ = x.shape
    t = bsz * l
    qkv, z, gsc, p, gates, conv_new = _front(
        x, prm["g_attn"], prm["wq"], prm["wba"], prm["wr"], prm["w_conv"], prm["a_log"], prm["dt_bias"],
        conv_prev, bb=front_blk[0], tl=front_blk[1], chunk=chunk)
    oa, ob, pool_new, s_new = _delta(qkv, z, gsc, p, pool_prev, s_prev, prm["w_onorm"], prm["w_mix"],
                                     prm["pool_scale"], bb=delta_blk[0], tl=delta_blk[1], chunk=chunk, pos0=pos0)
    y = _merge_mlp(x.reshape(t, D_MODEL), oa.reshape(t, QK_W), ob.reshape(t, WIDTH_B),
                   gates.reshape(t, 2 * D_MODEL), prm, tm)
    return y.reshape(bsz, l, D_MODEL), conv_new[None], pool_new[None], s_new[None]


def kernel(x_prompt, x_sample, state_conv, state_pool, state_ssm, w_in, w_conv, a_log, dt_bias, w_onorm,
           w_pool_mix, pool_scale, w_a_out, w_b_out, w_o, g_attn, g_mlp, w_up, w_down, g_final):
    assert w_in.shape[0] == 1, "single-layer decoder"
    w = w_in[0]
    rest_off = QKVZ_W + GATE_SCALARS
    prm = {
        "wq": w[:, 0:QKVZ_W].astype(BF16),
        "wba": jnp.pad(w[:, QKVZ_W:rest_off], ((0, 0), (0, LANES - GATE_SCALARS))).astype(BF16),
        "wr": w[:, rest_off:rest_off + REST_W].astype(BF16),
        "g_attn": g_attn[0][None, :], "g_mlp": g_mlp[0][None, :], "g_final": g_final[None, :],
        "w_conv": w_conv[0].astype(F32),
        "a_log": a_log[0], "dt_bias": dt_bias[0],
        "w_onorm": w_onorm[0][None, :].astype(F32),
        "w_mix": w_pool_mix[0].astype(BF16), "pool_scale": pool_scale[0][None, :].astype(F32),
        "w_a_out": w_a_out[0].astype(BF16), "w_b_out": w_b_out[0].astype(BF16), "w_o": w_o[0].astype(BF16),
        "w_up": w_up[0].astype(BF16), "w_down": w_down[0].astype(BF16),
    }
    bp = x_prompt.shape[0]
    y_p, conv_p, pool_p, ssm_p = _trunk(
        x_prompt, jnp.zeros((bp, CONV_W - 1, QKV_W), F32), jnp.zeros((bp, POOL_HIST, WIDTH_B), F32),
        jnp.zeros((bp, N_HEADS, HEAD_DIM, HEAD_DIM), F32), 0, prm,
        front_blk=(1, 512), delta_blk=(1, 256), tm=512, chunk=GROUP)
    dec_len = x_sample.shape[1]
    y_s, conv_s, pool_s, ssm_s = _trunk(
        x_sample, state_conv[0].astype(F32), state_pool[0].astype(F32), state_ssm[0].astype(F32), PAST_LEN, prm,
        front_blk=(256 // dec_len, dec_len), delta_blk=(GROUP // dec_len, dec_len), tm=512,
        chunk=dec_len)
    return (y_p, y_s, conv_p.astype(state_conv.dtype), pool_p.astype(state_pool.dtype),
            ssm_p.astype(state_ssm.dtype), conv_s.astype(state_conv.dtype), pool_s.astype(state_pool.dtype),
            ssm_s.astype(state_ssm.dtype))
```

```python
import functools
import math

import jax
import jax.numpy as jnp
from jax import lax
from jax.experimental import pallas as pl
from jax.experimental.pallas import tpu as pltpu

D_MODEL = 1024
N_HEADS = 4
HEAD_DIM = 128
QK_W = N_HEADS * HEAD_DIM
QKV_W = 3 * QK_W
CONV_W = 4
POOL_WINDOWS = (2, 4, 8, 16)
POOL_GROUP = 128
WIDTH_B = len(POOL_WINDOWS) * POOL_GROUP
POOL_HIST = 15
D_FF = 4 * D_MODEL
EPS = 1e-6
PAST_LEN = 16384
LANES = 128
SUBLANES = 8
MXU_COLS = 256
N_GATE_ROWS = 4

QKVZ_W = QKV_W + QK_W
GATE_SCALARS = 2 * N_HEADS
REST_W = WIDTH_B + 2 * D_MODEL

GROUP = 128
SERIES_BLOCK = 64
CONV_PAD = 8
POOL_LOOKBACK = 16
POOL_PAD = 24
FF_BLOCK = 1024

VMEM_LIMIT = 56 * 1024 * 1024

BF16 = jnp.bfloat16
F32 = jnp.float32


def _dot(a, b):
    return jnp.dot(a.astype(BF16), b.astype(BF16), preferred_element_type=F32)


def _sigmoid(x):
    return 1.0 / (1.0 + jnp.exp(-x))


def _silu(x):
    half = 0.5 * x
    return half * jnp.tanh(half) + half


def _rms_scale(x):
    return lax.rsqrt(jnp.mean(x * x, axis=-1, keepdims=True) + EPS)


def _const_spec(shape):
    zeros = (0,) * len(shape)
    return pl.BlockSpec(shape, lambda *_: zeros, pipeline_mode=pl.Buffered(1))


def _seq_spec(bb, tl, width):
    return pl.BlockSpec((bb, tl, width), lambda i, j: (i, j, 0))


def _state_spec(bb, n_rows, width):
    return pl.BlockSpec((bb, n_rows, width), lambda i, j: (i, 0, 0))


def _gate_rows_spec(bb, tl, n_tiles):
    return pl.BlockSpec((N_GATE_ROWS, SUBLANES, bb * tl), lambda i, j: (0, 0, i * n_tiles + j))


def _lane_prefix_sum(x, chunk):
    lane = lax.broadcasted_iota(jnp.int32, x.shape, 1)
    shift = 1
    while shift < chunk:
        x = x + jnp.where(lane % chunk >= shift, pltpu.roll(x, shift, axis=1), 0.0)
        shift *= 2
    return x


def _lane_suffix_sum(x, chunk):
    lane = lax.broadcasted_iota(jnp.int32, x.shape, 1)
    shift = 1
    while shift < chunk:
        x = x + jnp.where(lane % chunk + shift < chunk, pltpu.roll(x, x.shape[1] - shift, axis=1), 0.0)
        shift *= 2
    return x


def _front_kernel(x_ref, g_ref, wq_ref, wba_ref, wr_ref, wconv_ref, alog_ref, dtb_ref, keep_ref, convprev_ref,
                  qkv_ref, z_ref, gsc_ref, p_ref, gate_ref, convnew_ref,
                  ext_ref, ba_ref, *, bb, tl, n_tiles, chunk):
    tile = pl.program_id(1)
    rows = bb * tl
    hist = slice(CONV_PAD - (CONV_W - 1), CONV_PAD)

    @pl.when(tile == 0)
    def _():
        ext_ref[:, hist, :] = convprev_ref[...]
        ext_ref[:, 0:CONV_PAD - (CONV_W - 1), :] = jnp.zeros((bb, CONV_PAD - (CONV_W - 1), QKV_W), F32)

    x = x_ref[...].reshape(rows, D_MODEL)
    h = (x * _rms_scale(x) * g_ref[...]).astype(BF16)
    proj = lambda w: jnp.dot(h, w, preferred_element_type=F32)
    ext_ref[:, CONV_PAD:CONV_PAD + tl, :] = proj(wq_ref[:, 0:QKV_W]).reshape(bb, tl, QKV_W)
    convnew_ref[...] = ext_ref[:, CONV_PAD + tl - (CONV_W - 1):CONV_PAD + tl, :]

    ba_ref[...] = proj(wba_ref[...])
    x8 = ba_ref[...].T[0:SUBLANES, :]
    xs = x8 + dtb_ref[...]
    softplus = jnp.maximum(xs, 0.0) + jnp.log1p(jnp.exp(-jnp.abs(xs)))
    graw8 = -jnp.exp(alog_ref[...]) * softplus
    g8 = _lane_prefix_sum(graw8, chunk)
    gsc_ref[0] = _sigmoid(x8)
    gsc_ref[1] = g8
    gsc_ref[2] = jnp.exp(g8)
    gsc_ref[3] = jnp.exp(_lane_suffix_sum(graw8, chunk) - graw8)

    keep = keep_ref[...] != 0

    def conv_block(c0, dep):
        cols = slice(c0, c0 + LANES)
        xe = ext_ref[:, :, cols].reshape(bb * (CONV_PAD + tl), LANES)
        acc = wconv_ref[0:1, cols] * xe
        for j in range(1, CONV_W):
            acc = wconv_ref[j:j + 1, cols] * xe + pltpu.roll(acc, 1, axis=0)
        val = _silu(acc.reshape(bb, CONV_PAD + tl, LANES)[:, CONV_PAD:, :])
        if c0 < 2 * QK_W:
            scale = HEAD_DIM ** -0.5 if c0 < QK_W else 1.0
            val = val * (lax.rsqrt(jnp.sum(val * val, axis=-1, keepdims=True) + EPS) * scale)
        qkv_ref[:, :, cols] = jnp.where(keep, val, dep[:, 0:LANES].reshape(bb, tl, LANES))

    def proj_block(out_ref, w_ref, w0, c0):
        val = proj(w_ref[:, w0 + c0:w0 + c0 + MXU_COLS])
        out_ref[:, :, c0:c0 + MXU_COLS] = val.reshape(bb, tl, MXU_COLS)
        return val

    mxu_work = ([functools.partial(proj_block, z_ref, wq_ref, QKV_W, c0) for c0 in range(0, QK_W, MXU_COLS)]
                + [functools.partial(proj_block, p_ref, wr_ref, 0, c0) for c0 in range(0, WIDTH_B, MXU_COLS)]
                + [functools.partial(proj_block, gate_ref, wr_ref, WIDTH_B, c0)
                   for c0 in range(0, 2 * D_MODEL, MXU_COLS)])
    for i, c0 in enumerate(range(0, QKV_W, LANES)):
        conv_block(c0, mxu_work[i]())

    if n_tiles > 1:
        ext_ref[:, hist, :] = convnew_ref[...]


def _front(x, g_attn, wq, wba, wr, w_conv, a_log, dt_bias, conv_prev, *, bb, tl, chunk):
    bsz, l, _ = x.shape
    n_tiles = l // tl
    kern = functools.partial(_front_kernel, bb=bb, tl=tl, n_tiles=n_tiles, chunk=chunk)
    sds = lambda *shape: jax.ShapeDtypeStruct(shape, F32)
    zeros4 = jnp.zeros((N_HEADS,), F32)
    rows8 = lambda v: jnp.broadcast_to(jnp.concatenate([zeros4, v.astype(F32)])[:, None], (SUBLANES, bb * tl))
    return pl.pallas_call(
        kern,
        grid=(bsz // bb, n_tiles),
        in_specs=[_seq_spec(bb, tl, D_MODEL), _const_spec((1, D_MODEL)), _const_spec((D_MODEL, QKVZ_W)),
                  _const_spec((D_MODEL, LANES)), _const_spec((D_MODEL, REST_W)), _const_spec((CONV_W, QKV_W)),
                  _const_spec((SUBLANES, bb * tl)), _const_spec((SUBLANES, bb * tl)), _const_spec((1, LANES)),
                  _state_spec(bb, CONV_W - 1, QKV_W)],
        out_specs=[_seq_spec(bb, tl, QKV_W), _seq_spec(bb, tl, QK_W), _gate_rows_spec(bb, tl, n_tiles),
                   _seq_spec(bb, tl, WIDTH_B), _seq_spec(bb, tl, 2 * D_MODEL), _state_spec(bb, CONV_W - 1, QKV_W)],
        out_shape=[sds(bsz, l, QKV_W), sds(bsz, l, QK_W), sds(N_GATE_ROWS, SUBLANES, bsz * l),
                   sds(bsz, l, WIDTH_B), sds(bsz, l, 2 * D_MODEL), sds(bsz, CONV_W - 1, QKV_W)],
        scratch_shapes=[pltpu.VMEM((bb, CONV_PAD + tl, QKV_W), F32), pltpu.VMEM((bb * tl, LANES), F32)],
        compiler_params=pltpu.CompilerParams(dimension_semantics=("arbitrary", "arbitrary"),
                                             vmem_limit_bytes=VMEM_LIMIT),
        name="front",
    )(x, g_attn, wq, wba, wr, w_conv, rows8(a_log), rows8(dt_bias), jnp.ones((1, LANES), jnp.int32), conv_prev)


def _unit_lower_inverses(neg_ms, eye, chunk, row, col):
    base = min(chunk, SERIES_BLOCK)
    n_factors = int(math.log2(base))
    if base == chunk:
        nbs = neg_ms
    else:
        in_base = row // base == col // base
        nbs = [jnp.where(in_base, m, 0.0) for m in neg_ms]
    invs = [eye + nb for nb in nbs]
    if n_factors > 1:
        pws = [_dot(nb, nb) for nb in nbs]
        for _ in range(n_factors - 2):
            boths = [_dot(jnp.concatenate([inv, pw], axis=0), pw) for inv, pw in zip(invs, pws)]
            invs = [inv + both[0:GROUP] for inv, both in zip(invs, boths)]
            pws = [both[GROUP:] for both in boths]
        invs = [inv + _dot(inv, pw) for inv, pw in zip(invs, pws)]
    size = base
    while size < chunk:
        off_block = (row // size != col // size) & (row // (2 * size) == col // (2 * size))
        ts = [_dot(jnp.where(off_block, m, 0.0), inv) for m, inv in zip(neg_ms, invs)]
        invs = [inv + _dot(inv, t) for inv, t in zip(invs, ts)]
        size *= 2
    return invs


def _delta_kernel(*refs, bb, tl, chunk, pos0, n_tiles, n_gate_refs):
    gsc_refs, refs = refs[:n_gate_refs], refs[n_gate_refs:]
    (qkv_ref, z_ref, p_ref, poolprev_ref, s0_ref, wonorm_ref, wmix_ref, pscale_ref,
     oa_ref, ob_ref, poolnew_ref, snew_ref, pext_ref) = refs
    tile = pl.program_id(1)
    rows_b = min(tl, GROUP)
    seqs_g = GROUP // rows_b
    groups_b = tl // rows_b
    n_groups = bb * tl // GROUP
    chained = chunk == GROUP
    assert chained or (chunk == tl and tl < GROUP), "chunk must be a whole group or a whole short sequence"
    pool_hist = slice(POOL_PAD - POOL_HIST, POOL_PAD)

    @pl.when(tile == 0)
    def _():
        snew_ref[...] = s0_ref[...]
        pext_ref[:, pool_hist, :] = poolprev_ref[...]
        pext_ref[:, POOL_PAD - POOL_LOOKBACK:POOL_PAD - POOL_HIST, :] = jnp.zeros((bb, 1, WIDTH_B), F32)

    pext_ref[:, POOL_PAD:POOL_PAD + tl, :] = p_ref[...]
    poolnew_ref[...] = pext_ref[:, POOL_PAD + tl - POOL_HIST:POOL_PAD + tl, :]

    row = lax.broadcasted_iota(jnp.int32, (GROUP, GROUP), 0)
    col = lax.broadcasted_iota(jnp.int32, (GROUP, GROUP), 1)
    causal = row >= col
    strict = row > col
    if not chained:
        same = (row // chunk) == (col // chunk)
        causal = causal & same
        strict = strict & same
    eye = jnp.where(row == col, 1.0, 0.0).astype(F32)

    def origin(g):
        if tl >= GROUP:
            return g // groups_b, (g % groups_b) * GROUP
        return g * seqs_g, 0

    def load(c0, g):
        b0, t0 = origin(g)
        return qkv_ref[b0:b0 + seqs_g, t0:t0 + rows_b, c0:c0 + HEAD_DIM].reshape(GROUP, HEAD_DIM)

    def gate_rows(g):
        if n_gate_refs > 1:
            b0, t0 = origin(g)
            return tuple(gsc_refs[b0][q, :, t0:t0 + GROUP] for q in range(N_GATE_ROWS))
        return tuple(gsc_refs[0][q, :, g * GROUP:(g + 1) * GROUP] for q in range(N_GATE_ROWS))

    gates = [gate_rows(g) for g in range(n_groups)]

    probs = [(g, h) for g in range(n_groups) for h in range(N_HEADS)]
    st = []
    for g, h in probs:
        beta8, g8, eg8, kds8 = gates[g]
        d = {"q": load(h * HEAD_DIM, g), "k": load(QK_W + h * HEAD_DIM, g), "v": load(2 * QK_W + h * HEAD_DIM, g)}
        d["beta_row"] = beta8[h:h + 1, :]
        d["eg_row"] = eg8[N_HEADS + h:N_HEADS + h + 1, :]
        d["g_row"] = g8[N_HEADS + h:N_HEADS + h + 1, :]
        d["kb_row"] = kds8[N_HEADS + h:N_HEADS + h + 1, :] * d["beta_row"]
        d["g_col"] = jnp.broadcast_to(d["g_row"], (GROUP, GROUP)).T
        d["kt"] = d["k"].T
        st.append(d)
    for d in st:
        both = _dot(jnp.concatenate([d["k"], d["q"]], axis=0), d["kt"])
        d["kk"], d["qk"] = both[0:GROUP], both[GROUP:]
    for d in st:
        decay = jnp.exp(jnp.where(causal, d["g_col"] - d["g_row"], -jnp.inf))
        d["neg_m"] = jnp.where(strict, -(d.pop("kk") * decay), 0.0) * d["beta_row"]
        d["qkm"] = d.pop("qk") * decay * d["beta_row"]
        d["q_dec"] = d.pop("q") * jnp.exp(d["g_col"])
        d["kt_dec"] = d.pop("kt") * d["kb_row"]
    invs = _unit_lower_inverses([d.pop("neg_m") for d in st], eye, chunk, row, col)
    for d, inv in zip(st, invs):
        d["uy"] = _dot(inv, d.pop("v"))
        d["wy"] = _dot(inv * d["eg_row"], d.pop("k"))

    outs = {}
    if chained:
        for j in range(groups_b):
            wave = [(i, g, h) for i, (g, h) in enumerate(probs) if g % groups_b == j]
            s_old = {i: snew_ref[origin(g)[0], h] for i, g, h in wave}
            ws = {i: _dot(jnp.concatenate([st[i]["wy"], st[i]["q_dec"]], axis=0), s_old[i]) for i, g, h in wave}
            ys = {i: st[i]["uy"] - ws[i][0:GROUP] for i, g, h in wave}
            for i, g, h in wave:
                outs[i] = ws[i][GROUP:] + _dot(st[i]["qkm"], ys[i])
            for i, g, h in wave:
                last = jnp.exp(st[i]["g_col"][GROUP - 1:GROUP, :])
                snew_ref[origin(g)[0], h] = s_old[i] * last + _dot(st[i]["kt_dec"], ys[i])
    else:
        for i, (g, h) in enumerate(probs):
            d, b0 = st[i], origin(g)[0]
            ws_w, ws_q = [], []
            for s_i in range(seqs_g):
                r = slice(s_i * rows_b, (s_i + 1) * rows_b)
                ws = _dot(jnp.concatenate([d["wy"][r], d["q_dec"][r]], axis=0), snew_ref[b0 + s_i, h])
                ws_w.append(ws[0:rows_b])
                ws_q.append(ws[rows_b:])
            d["y"] = d["uy"] - jnp.concatenate(ws_w, axis=0)
            outs[i] = jnp.concatenate(ws_q, axis=0) + _dot(d["qkm"], d["y"])
        for i, (g, h) in enumerate(probs):
            d, b0 = st[i], origin(g)[0]
            for s_i in range(seqs_g):
                last = jnp.exp(d["g_col"][(s_i + 1) * rows_b - 1:(s_i + 1) * rows_b, :])
                upd = _dot(jnp.where(col // rows_b == s_i, d["kt_dec"], 0.0), d["y"])
                snew_ref[b0 + s_i, h] = snew_ref[b0 + s_i, h] * last + upd

    def group_slab(ref, g, cols, row_off=0):
        b0, t0 = origin(g)
        return ref.at[b0:b0 + seqs_g, row_off + t0:row_off + t0 + rows_b, cols]

    for i, (g, h) in enumerate(probs):
        cols = slice(h * HEAD_DIM, (h + 1) * HEAD_DIM)
        o = outs[i]
        z = group_slab(z_ref, g, cols)[...].reshape(GROUP, HEAD_DIM)
        group_slab(oa_ref, g, cols)[...] = (o * _rms_scale(o) * wonorm_ref[...] * _silu(z)).reshape(
            seqs_g, rows_b, HEAD_DIM)

    for g in range(n_groups):
        b0, t0 = origin(g)
        pos = pos0 + tile * tl + t0 + row % rows_b
        for gi, win in enumerate(POOL_WINDOWS):
            cols = slice(gi * POOL_GROUP, (gi + 1) * POOL_GROUP)
            slab = pext_ref[b0:b0 + seqs_g, POOL_PAD - POOL_LOOKBACK + t0:POOL_PAD + t0 + rows_b, cols]
            acc = slab.reshape(seqs_g * (POOL_LOOKBACK + rows_b), POOL_GROUP)
            shift = 1
            while shift < win:
                acc = acc + pltpu.roll(acc, shift, axis=0)
                shift *= 2
            acc = acc.reshape(seqs_g, POOL_LOOKBACK + rows_b, POOL_GROUP)[:, POOL_LOOKBACK:, :]
            cur = slab[:, POOL_LOOKBACK:, :]
            pooled = (acc / jnp.minimum(pos + 1, win).astype(F32).reshape(seqs_g, rows_b, POOL_GROUP) - cur)
            mixed = _dot(pooled.reshape(GROUP, POOL_GROUP), wmix_ref[gi]) * pscale_ref[:, cols]
            ob_ref[b0:b0 + seqs_g, t0:t0 + rows_b, cols] = mixed.reshape(seqs_g, rows_b, POOL_GROUP)

    if n_tiles > 1:
        pext_ref[:, pool_hist, :] = poolnew_ref[...]


def _delta(qkv, z, gsc, p, pool_prev, s0, w_onorm, w_mix, pool_scale, *, bb, tl, chunk, pos0):
    bsz, l = qkv.shape[0], qkv.shape[1]
    n_tiles = l // tl
    state_s = pl.BlockSpec((bb, N_HEADS, HEAD_DIM, HEAD_DIM), lambda i, j: (i, 0, 0, 0))
    if n_tiles == 1:
        gate_specs = [_gate_rows_spec(bb, tl, 1)]
    else:
        gate_specs = [pl.BlockSpec((N_GATE_ROWS, SUBLANES, tl), lambda i, j, k=k: (0, 0, (i * bb + k) * n_tiles + j))
                      for k in range(bb)]
    kern = functools.partial(_delta_kernel, bb=bb, tl=tl, chunk=chunk, pos0=pos0, n_tiles=n_tiles,
                             n_gate_refs=len(gate_specs))
    sds = lambda *shape: jax.ShapeDtypeStruct(shape, F32)
    return pl.pallas_call(
        kern,
        grid=(bsz // bb, n_tiles),
        in_specs=gate_specs + [
                  _seq_spec(bb, tl, QKV_W), _seq_spec(bb, tl, QK_W),
                  _seq_spec(bb, tl, WIDTH_B), _state_spec(bb, POOL_HIST, WIDTH_B), state_s,
                  _const_spec((1, HEAD_DIM)), _const_spec((len(POOL_WINDOWS), POOL_GROUP, POOL_GROUP)),
                  _const_spec((1, WIDTH_B))],
        out_specs=[_seq_spec(bb, tl, QK_W), _seq_spec(bb, tl, WIDTH_B), _state_spec(bb, POOL_HIST, WIDTH_B), state_s],
        out_shape=[sds(bsz, l, QK_W), sds(bsz, l, WIDTH_B), sds(bsz, POOL_HIST, WIDTH_B),
                   sds(bsz, N_HEADS, HEAD_DIM, HEAD_DIM)],
        scratch_shapes=[pltpu.VMEM((bb, POOL_PAD + tl, WIDTH_B), F32)],
        compiler_params=pltpu.CompilerParams(dimension_semantics=("arbitrary", "arbitrary"),
                                             vmem_limit_bytes=VMEM_LIMIT),
        name="delta",
    )(*([gsc] * len(gate_specs)), qkv, z, p, pool_prev, s0, w_onorm, w_mix, pool_scale)


def _mlp_kernel(x_ref, oa_ref, ob_ref, gate_ref, wa_ref, wb_ref, wo_ref, gmlp_ref, wup_ref, wdown_ref, gfin_ref,
                y_ref):
    ma = _dot(oa_ref[...], wa_ref[...])
    mb = _dot(ob_ref[...], wb_ref[...])
    merged = _sigmoid(gate_ref[:, 0:D_MODEL]) * ma + _sigmoid(gate_ref[:, D_MODEL:]) * mb
    x1 = x_ref[...] + _dot(merged, wo_ref[...])
    h2 = (x1 * _rms_scale(x1) * gmlp_ref[...]).astype(BF16)
    acc = x1
    for c0 in range(0, D_FF, FF_BLOCK):
        up = jnp.dot(h2, wup_ref[:, c0:c0 + FF_BLOCK], preferred_element_type=F32)
        act = jnp.square(jnp.maximum(up, 0.0))
        acc = acc + _dot(act, wdown_ref[c0:c0 + FF_BLOCK, :])
    y_ref[...] = acc * _rms_scale(acc) * gfin_ref[...]


def _merge_mlp(x2d, oa, ob, gates, prm, tm):
    t = x2d.shape[0]
    row = lambda w: pl.BlockSpec((tm, w), lambda i: (i, 0))
    return pl.pallas_call(
        _mlp_kernel,
        grid=(t // tm,),
        in_specs=[row(D_MODEL), row(QK_W), row(WIDTH_B), row(2 * D_MODEL),
                  _const_spec((QK_W, D_MODEL)), _const_spec((WIDTH_B, D_MODEL)), _const_spec((D_MODEL, D_MODEL)),
                  _const_spec((1, D_MODEL)), _const_spec((D_MODEL, D_FF)), _const_spec((D_FF, D_MODEL)),
                  _const_spec((1, D_MODEL))],
        out_specs=row(D_MODEL),
        out_shape=jax.ShapeDtypeStruct((t, D_MODEL), F32),
        compiler_params=pltpu.CompilerParams(dimension_semantics=("arbitrary",),
                                             vmem_limit_bytes=VMEM_LIMIT),
        name="merge_mlp",
    )(x2d, oa, ob, gates, prm["w_a_out"], prm["w_b_out"], prm["w_o"], prm["g_mlp"], prm["w_up"], prm["w_down"],
      prm["g_final"])


def _trunk(x, conv_prev, pool_prev, s_prev, pos0, prm, *, front_blk, delta_blk, tm, chunk):
    bsz, l, _ = x.shape
    t = bsz * l
    qkv, z, gsc, p, gates, conv_new = _front(
        x, prm["g_attn"], prm["wq"], prm["wba"], prm["wr"], prm["w_conv"], prm["a_log"], prm["dt_bias"],
        conv_prev, bb=front_blk[0], tl=front_blk[1], chunk=chunk)
    oa, ob, pool_new, s_new = _delta(qkv, z, gsc, p, pool_prev, s_prev, prm["w_onorm"], prm["w_mix"],
                                     prm["pool_scale"], bb=delta_blk[0], tl=delta_blk[1], chunk=chunk, pos0=pos0)
    y = _merge_mlp(x.reshape(t, D_MODEL), oa.reshape(t, QK_W), ob.reshape(t, WIDTH_B),
                   gates.reshape(t, 2 * D_MODEL), prm, tm)
    return y.reshape(bsz, l, D_MODEL), conv_new[None], pool_new[None], s_new[None]


def kernel(x_prompt, x_sample, state_conv, state_pool, state_ssm, w_in, w_conv, a_log, dt_bias, w_onorm,
           w_pool_mix, pool_scale, w_a_out, w_b_out, w_o, g_attn, g_mlp, w_up, w_down, g_final):
    assert w_in.shape[0] == 1, "single-layer decoder"
    w = w_in[0]
    rest_off = QKVZ_W + GATE_SCALARS
    prm = {
        "wq": w[:, 0:QKVZ_W].astype(BF16),
        "wba": jnp.pad(w[:, QKVZ_W:rest_off], ((0, 0), (0, LANES - GATE_SCALARS))).astype(BF16),
        "wr": w[:, rest_off:rest_off + REST_W].astype(BF16),
        "g_attn": g_attn[0][None, :], "g_mlp": g_mlp[0][None, :], "g_final": g_final[None, :],
        "w_conv": w_conv[0].astype(F32),
        "a_log": a_log[0], "dt_bias": dt_bias[0],
        "w_onorm": w_onorm[0][None, :].astype(F32),
        "w_mix": w_pool_mix[0].astype(BF16), "pool_scale": pool_scale[0][None, :].astype(F32),
        "w_a_out": w_a_out[0].astype(BF16), "w_b_out": w_b_out[0].astype(BF16), "w_o": w_o[0].astype(BF16),
        "w_up": w_up[0].astype(BF16), "w_down": w_down[0].astype(BF16),
    }
    bp = x_prompt.shape[0]
    y_p, conv_p, pool_p, ssm_p = _trunk(
        x_prompt, jnp.zeros((bp, CONV_W - 1, QKV_W), F32), jnp.zeros((bp, POOL_HIST, WIDTH_B), F32),
        jnp.zeros((bp, N_HEADS, HEAD_DIM, HEAD_DIM), F32), 0, prm,
        front_blk=(1, 512), delta_blk=(2, 256), tm=512, chunk=GROUP)
    dec_len = x_sample.shape[1]
    y_s, conv_s, pool_s, ssm_s = _trunk(
        x_sample, state_conv[0].astype(F32), state_pool[0].astype(F32), state_ssm[0].astype(F32), PAST_LEN, prm,
        front_blk=(256 // dec_len, dec_len), delta_blk=(2 * GROUP // dec_len, dec_len), tm=512,
        chunk=dec_len)
    return (y_p, y_s, conv_p.astype(state_conv.dtype), pool_p.astype(state_pool.dtype),
            ssm_p.astype(state_ssm.dtype), conv_s.astype(state_conv.dtype), pool_s.astype(state_pool.dtype),
            ssm_s.astype(state_ssm.dtype))
```

```python
import functools
import math

import jax
import jax.numpy as jnp
from jax import lax
from jax.experimental import pallas as pl
from jax.experimental.pallas import tpu as pltpu

D_MODEL = 1024
N_HEADS = 4
HEAD_DIM = 128
QK_W = N_HEADS * HEAD_DIM
QKV_W = 3 * QK_W
CONV_W = 4
POOL_WINDOWS = (2, 4, 8, 16)
POOL_GROUP = 128
WIDTH_B = len(POOL_WINDOWS) * POOL_GROUP
POOL_HIST = 15
D_FF = 4 * D_MODEL
EPS = 1e-6
PAST_LEN = 16384
LANES = 128
SUBLANES = 8
MXU_COLS = 256
N_GATE_ROWS = 4

QKVZ_W = QKV_W + QK_W
GATE_SCALARS = 2 * N_HEADS
REST_W = WIDTH_B + 2 * D_MODEL

GROUP = 128
SERIES_BLOCK = 64
CONV_PAD = 8
POOL_LOOKBACK = 16
POOL_PAD = 24
FF_BLOCK = 1024

VMEM_LIMIT = 56 * 1024 * 1024

BF16 = jnp.bfloat16
F32 = jnp.float32


def _dot(a, b):
    return jnp.dot(a.astype(BF16), b.astype(BF16), preferred_element_type=F32)


def _sigmoid(x):
    return 1.0 / (1.0 + jnp.exp(-x))


def _silu(x):
    half = 0.5 * x
    return half * jnp.tanh(half) + half


def _rms_scale(x):
    return lax.rsqrt(jnp.mean(x * x, axis=-1, keepdims=True) + EPS)


def _const_spec(shape):
    zeros = (0,) * len(shape)
    return pl.BlockSpec(shape, lambda *_: zeros, pipeline_mode=pl.Buffered(1))


def _seq_spec(bb, tl, width):
    return pl.BlockSpec((bb, tl, width), lambda i, j: (i, j, 0))


def _state_spec(bb, n_rows, width):
    return pl.BlockSpec((bb, n_rows, width), lambda i, j: (i, 0, 0))


def _gate_rows_spec(bb, tl, n_tiles):
    return pl.BlockSpec((N_GATE_ROWS, SUBLANES, bb * tl), lambda i, j: (0, 0, i * n_tiles + j))


def _lane_prefix_sum(x, chunk):
    lane = lax.broadcasted_iota(jnp.int32, x.shape, 1)
    shift = 1
    while shift < chunk:
        x = x + jnp.where(lane % chunk >= shift, pltpu.roll(x, shift, axis=1), 0.0)
        shift *= 2
    return x


def _lane_suffix_sum(x, chunk):
    lane = lax.broadcasted_iota(jnp.int32, x.shape, 1)
    shift = 1
    while shift < chunk:
        x = x + jnp.where(lane % chunk + shift < chunk, pltpu.roll(x, x.shape[1] - shift, axis=1), 0.0)
        shift *= 2
    return x


def _front_kernel(x_ref, g_ref, wq_ref, wba_ref, wr_ref, wconv_ref, alog_ref, dtb_ref, keep_ref, convprev_ref,
                  qkv_ref, z_ref, gsc_ref, p_ref, gate_ref, convnew_ref,
                  ext_ref, *, bb, tl, n_tiles, chunk):
    tile = pl.program_id(1)
    rows = bb * tl
    hist = slice(CONV_PAD - (CONV_W - 1), CONV_PAD)

    @pl.when(tile == 0)
    def _():
        ext_ref[:, hist, :] = convprev_ref[...]
        ext_ref[:, 0:CONV_PAD - (CONV_W - 1), :] = jnp.zeros((bb, CONV_PAD - (CONV_W - 1), QKV_W), F32)

    x = x_ref[...].reshape(rows, D_MODEL)
    h = (x * _rms_scale(x) * g_ref[...]).astype(BF16)
    contract_last = (((1,), (1,)), ((), ()))
    proj = lambda w_rows: lax.dot_general(h, w_rows, contract_last, preferred_element_type=F32)
    ext_ref[:, CONV_PAD:CONV_PAD + tl, :] = proj(wq_ref[0:QKV_W, :]).reshape(bb, tl, QKV_W)
    convnew_ref[...] = ext_ref[:, CONV_PAD + tl - (CONV_W - 1):CONV_PAD + tl, :]

    x8 = lax.dot_general(wba_ref[...], h, contract_last, preferred_element_type=F32)[0:SUBLANES, :]
    xs = x8 + dtb_ref[...]
    softplus = jnp.maximum(xs, 0.0) + jnp.log1p(jnp.exp(-jnp.abs(xs)))
    graw8 = -jnp.exp(alog_ref[...]) * softplus
    g8 = _lane_prefix_sum(graw8, chunk)
    gsc_ref[0] = _sigmoid(x8)
    gsc_ref[1] = g8
    gsc_ref[2] = jnp.exp(g8)
    gsc_ref[3] = jnp.exp(_lane_suffix_sum(graw8, chunk) - graw8)

    keep = keep_ref[...] != 0

    def conv_block(c0, dep):
        cols = slice(c0, c0 + LANES)
        xe = ext_ref[:, :, cols].reshape(bb * (CONV_PAD + tl), LANES)
        acc = wconv_ref[0:1, cols] * xe
        for j in range(1, CONV_W):
            acc = wconv_ref[j:j + 1, cols] * xe + pltpu.roll(acc, 1, axis=0)
        val = _silu(acc.reshape(bb, CONV_PAD + tl, LANES)[:, CONV_PAD:, :])
        if c0 < 2 * QK_W:
            scale = HEAD_DIM ** -0.5 if c0 < QK_W else 1.0
            val = val * (lax.rsqrt(jnp.sum(val * val, axis=-1, keepdims=True) + EPS) * scale)
        qkv_ref[:, :, cols] = jnp.where(keep, val, dep[:, 0:LANES].reshape(bb, tl, LANES))

    def proj_block(out_ref, w_ref, w0, c0):
        val = proj(w_ref[w0 + c0:w0 + c0 + MXU_COLS, :])
        out_ref[:, :, c0:c0 + MXU_COLS] = val.reshape(bb, tl, MXU_COLS)
        return val

    mxu_work = ([functools.partial(proj_block, z_ref, wq_ref, QKV_W, c0) for c0 in range(0, QK_W, MXU_COLS)]
                + [functools.partial(proj_block, p_ref, wr_ref, 0, c0) for c0 in range(0, WIDTH_B, MXU_COLS)]
                + [functools.partial(proj_block, gate_ref, wr_ref, WIDTH_B, c0)
                   for c0 in range(0, 2 * D_MODEL, MXU_COLS)])
    for i, c0 in enumerate(range(0, QKV_W, LANES)):
        conv_block(c0, mxu_work[i]())

    if n_tiles > 1:
        ext_ref[:, hist, :] = convnew_ref[...]


def _front(x, g_attn, wq, wba, wr, w_conv, a_log, dt_bias, conv_prev, *, bb, tl, chunk):
    bsz, l, _ = x.shape
    n_tiles = l // tl
    kern = functools.partial(_front_kernel, bb=bb, tl=tl, n_tiles=n_tiles, chunk=chunk)
    sds = lambda *shape: jax.ShapeDtypeStruct(shape, F32)
    zeros4 = jnp.zeros((N_HEADS,), F32)
    rows8 = lambda v: jnp.broadcast_to(jnp.concatenate([zeros4, v.astype(F32)])[:, None], (SUBLANES, bb * tl))
    return pl.pallas_call(
        kern,
        grid=(bsz // bb, n_tiles),
        in_specs=[_seq_spec(bb, tl, D_MODEL), _const_spec((1, D_MODEL)), _const_spec((QKVZ_W, D_MODEL)),
                  _const_spec((2 * SUBLANES, D_MODEL)), _const_spec((REST_W, D_MODEL)), _const_spec((CONV_W, QKV_W)),
                  _const_spec((SUBLANES, bb * tl)), _const_spec((SUBLANES, bb * tl)), _const_spec((1, LANES)),
                  _state_spec(bb, CONV_W - 1, QKV_W)],
        out_specs=[_seq_spec(bb, tl, QKV_W), _seq_spec(bb, tl, QK_W), _gate_rows_spec(bb, tl, n_tiles),
                   _seq_spec(bb, tl, WIDTH_B), _seq_spec(bb, tl, 2 * D_MODEL), _state_spec(bb, CONV_W - 1, QKV_W)],
        out_shape=[sds(bsz, l, QKV_W), sds(bsz, l, QK_W), sds(N_GATE_ROWS, SUBLANES, bsz * l),
                   sds(bsz, l, WIDTH_B), sds(bsz, l, 2 * D_MODEL), sds(bsz, CONV_W - 1, QKV_W)],
        scratch_shapes=[pltpu.VMEM((bb, CONV_PAD + tl, QKV_W), F32)],
        compiler_params=pltpu.CompilerParams(dimension_semantics=("arbitrary", "arbitrary"),
                                             vmem_limit_bytes=VMEM_LIMIT),
        name="front",
    )(x, g_attn, wq, wba, wr, w_conv, rows8(a_log), rows8(dt_bias), jnp.ones((1, LANES), jnp.int32), conv_prev)


def _unit_lower_inverses(neg_ms, chunk):
    nb = min(chunk, SERIES_BLOCK)
    n_blocks = GROUP // nb
    n_factors = int(math.log2(nb))
    assert chunk == nb or (chunk == 2 * nb and n_blocks == 2), "chunks are one or two series blocks"
    lane = lax.broadcasted_iota(jnp.int32, (nb, GROUP), 1)
    lane_block = lane // nb

    def packed(m):
        out = m[0:nb]
        for b in range(1, n_blocks):
            out = jnp.where(lane_block == b, m[b * nb:(b + 1) * nb], out)
        return out

    def block_diag(p):
        return jnp.concatenate([jnp.where(lane_block == b, p, 0.0) for b in range(n_blocks)], axis=0)

    nps = [packed(m) for m in neg_ms]
    eye_p = jnp.where(lane % nb == lax.broadcasted_iota(jnp.int32, (nb, GROUP), 0), 1.0, 0.0).astype(F32)
    invs = [eye_p + n for n in nps]
    if n_factors > 1:
        pws = [_dot(n, block_diag(n)) for n in nps]
        for _ in range(n_factors - 2):
            boths = [_dot(jnp.concatenate([inv, pw], axis=0), block_diag(pw)) for inv, pw in zip(invs, pws)]
            invs = [inv + both[0:nb] for inv, both in zip(invs, boths)]
            pws = [both[nb:] for both in boths]
        invs = [inv + _dot(inv, block_diag(pw)) for inv, pw in zip(invs, pws)]
    if chunk == nb:
        return [block_diag(inv) for inv in invs]
    zeros = jnp.zeros((nb, GROUP), F32)
    a_invs = [jnp.where(lane < nb, inv, 0.0) for inv in invs]
    neg_ls = [jnp.where(lane < nb, m[nb:], 0.0) for m in neg_ms]
    xs = [_dot(neg_l, jnp.concatenate([a_inv, zeros], axis=0)) for neg_l, a_inv in zip(neg_ls, a_invs)]
    ys = [_dot(inv, jnp.concatenate([zeros, x], axis=0)) for inv, x in zip(invs, xs)]
    return [jnp.concatenate([a_inv, y + jnp.where(lane >= nb, inv, 0.0)], axis=0)
            for a_inv, y, inv in zip(a_invs, ys, invs)]


def _delta_kernel(*refs, bb, tl, chunk, pos0, n_tiles, n_gate_refs):
    gsc_refs, refs = refs[:n_gate_refs], refs[n_gate_refs:]
    (qkv_ref, z_ref, p_ref, poolprev_ref, s0_ref, wonorm_ref, wmix_ref, pscale_ref,
     oa_ref, ob_ref, poolnew_ref, snew_ref, pext_ref) = refs
    tile = pl.program_id(1)
    rows_b = min(tl, GROUP)
    seqs_g = GROUP // rows_b
    groups_b = tl // rows_b
    n_groups = bb * tl // GROUP
    chained = chunk == GROUP
    assert chained or (chunk == tl and tl < GROUP), "chunk must be a whole group or a whole short sequence"
    pool_hist = slice(POOL_PAD - POOL_HIST, POOL_PAD)

    @pl.when(tile == 0)
    def _():
        snew_ref[...] = s0_ref[...]
        pext_ref[:, pool_hist, :] = poolprev_ref[...]
        pext_ref[:, POOL_PAD - POOL_LOOKBACK:POOL_PAD - POOL_HIST, :] = jnp.zeros((bb, 1, WIDTH_B), F32)

    pext_ref[:, POOL_PAD:POOL_PAD + tl, :] = p_ref[...]
    poolnew_ref[...] = pext_ref[:, POOL_PAD + tl - POOL_HIST:POOL_PAD + tl, :]

    row = lax.broadcasted_iota(jnp.int32, (GROUP, GROUP), 0)
    col = lax.broadcasted_iota(jnp.int32, (GROUP, GROUP), 1)
    causal = row >= col
    strict = row > col
    if not chained:
        same = (row // chunk) == (col // chunk)
        causal = causal & same
        strict = strict & same

    def origin(g):
        if tl >= GROUP:
            return g // groups_b, (g % groups_b) * GROUP
        return g * seqs_g, 0

    def load(c0, g):
        b0, t0 = origin(g)
        return qkv_ref[b0:b0 + seqs_g, t0:t0 + rows_b, c0:c0 + HEAD_DIM].reshape(GROUP, HEAD_DIM)

    def gate_rows(g):
        if n_gate_refs > 1:
            b0, t0 = origin(g)
            return tuple(gsc_refs[b0][q, :, t0:t0 + GROUP] for q in range(N_GATE_ROWS))
        return tuple(gsc_refs[0][q, :, g * GROUP:(g + 1) * GROUP] for q in range(N_GATE_ROWS))

    gates = [gate_rows(g) for g in range(n_groups)]

    probs = [(g, h) for g in range(n_groups) for h in range(N_HEADS)]
    st = []
    for g, h in probs:
        beta8, g8, eg8, kds8 = gates[g]
        d = {"q": load(h * HEAD_DIM, g), "k": load(QK_W + h * HEAD_DIM, g), "v": load(2 * QK_W + h * HEAD_DIM, g)}
        d["beta_row"] = beta8[h:h + 1, :]
        d["eg_row"] = eg8[N_HEADS + h:N_HEADS + h + 1, :]
        d["g_row"] = g8[N_HEADS + h:N_HEADS + h + 1, :]
        d["kb_row"] = kds8[N_HEADS + h:N_HEADS + h + 1, :] * d["beta_row"]
        d["g_col"] = jnp.broadcast_to(d["g_row"], (GROUP, GROUP)).T
        d["kt"] = d["k"].T
        st.append(d)
    for d in st:
        both = _dot(jnp.concatenate([d["k"], d["q"]], axis=0), d["kt"])
        d["kk"], d["qk"] = both[0:GROUP], both[GROUP:]
    for d in st:
        decay = jnp.exp(jnp.where(causal, d["g_col"] - d["g_row"], -jnp.inf))
        d["neg_m"] = jnp.where(strict, -(d.pop("kk") * decay), 0.0) * d["beta_row"]
        d["qkm"] = d.pop("qk") * decay * d["beta_row"]
        d["q_dec"] = d.pop("q") * jnp.exp(d["g_col"])
        d["kt_dec"] = d.pop("kt") * d["kb_row"]
    invs = _unit_lower_inverses([d.pop("neg_m") for d in st], chunk)
    for d, inv in zip(st, invs):
        d["uy"] = _dot(inv, d.pop("v"))
        d["wy"] = _dot(inv * d["eg_row"], d.pop("k"))

    outs = {}
    if chained:
        for j in range(groups_b):
            wave = [(i, g, h) for i, (g, h) in enumerate(probs) if g % groups_b == j]
            s_old = {i: snew_ref[origin(g)[0], h] for i, g, h in wave}
            ws = {i: _dot(jnp.concatenate([st[i]["wy"], st[i]["q_dec"]], axis=0), s_old[i]) for i, g, h in wave}
            ys = {i: st[i]["uy"] - ws[i][0:GROUP] for i, g, h in wave}
            for i, g, h in wave:
                outs[i] = ws[i][GROUP:] + _dot(st[i]["qkm"], ys[i])
            for i, g, h in wave:
                last = jnp.exp(st[i]["g_col"][GROUP - 1:GROUP, :])
                snew_ref[origin(g)[0], h] = s_old[i] * last + _dot(st[i]["kt_dec"], ys[i])
    else:
        for i, (g, h) in enumerate(probs):
            d, b0 = st[i], origin(g)[0]
            ws_w, ws_q = [], []
            for s_i in range(seqs_g):
                r = slice(s_i * rows_b, (s_i + 1) * rows_b)
                ws = _dot(jnp.concatenate([d["wy"][r], d["q_dec"][r]], axis=0), snew_ref[b0 + s_i, h])
                ws_w.append(ws[0:rows_b])
                ws_q.append(ws[rows_b:])
            d["y"] = d["uy"] - jnp.concatenate(ws_w, axis=0)
            outs[i] = jnp.concatenate(ws_q, axis=0) + _dot(d["qkm"], d["y"])
        for i, (g, h) in enumerate(probs):
            d, b0 = st[i], origin(g)[0]
            for s_i in range(seqs_g):
                last = jnp.exp(d["g_col"][(s_i + 1) * rows_b - 1:(s_i + 1) * rows_b, :])
                upd = _dot(jnp.where(col // rows_b == s_i, d["kt_dec"], 0.0), d["y"])
                snew_ref[b0 + s_i, h] = snew_ref[b0 + s_i, h] * last + upd

    def group_slab(ref, g, cols, row_off=0):
        b0, t0 = origin(g)
        return ref.at[b0:b0 + seqs_g, row_off + t0:row_off + t0 + rows_b, cols]

    for i, (g, h) in enumerate(probs):
        cols = slice(h * HEAD_DIM, (h + 1) * HEAD_DIM)
        o = outs[i]
        z = group_slab(z_ref, g, cols)[...].reshape(GROUP, HEAD_DIM)
        group_slab(oa_ref, g, cols)[...] = (o * _rms_scale(o) * wonorm_ref[...] * _silu(z)).reshape(
            seqs_g, rows_b, HEAD_DIM)

    for g in range(n_groups):
        b0, t0 = origin(g)
        pos = pos0 + tile * tl + t0 + row % rows_b
        for gi, win in enumerate(POOL_WINDOWS):
            cols = slice(gi * POOL_GROUP, (gi + 1) * POOL_GROUP)
            slab = pext_ref[b0:b0 + seqs_g, POOL_PAD - POOL_LOOKBACK + t0:POOL_PAD + t0 + rows_b, cols]
            acc = slab.reshape(seqs_g * (POOL_LOOKBACK + rows_b), POOL_GROUP)
            shift = 1
            while shift < win:
                acc = acc + pltpu.roll(acc, shift, axis=0)
                shift *= 2
            acc = acc.reshape(seqs_g, POOL_LOOKBACK + rows_b, POOL_GROUP)[:, POOL_LOOKBACK:, :]
            cur = slab[:, POOL_LOOKBACK:, :]
            pooled = (acc / jnp.minimum(pos + 1, win).astype(F32).reshape(seqs_g, rows_b, POOL_GROUP) - cur)
            mixed = _dot(pooled.reshape(GROUP, POOL_GROUP), wmix_ref[gi]) * pscale_ref[:, cols]
            ob_ref[b0:b0 + seqs_g, t0:t0 + rows_b, cols] = mixed.reshape(seqs_g, rows_b, POOL_GROUP)

    if n_tiles > 1:
        pext_ref[:, pool_hist, :] = poolnew_ref[...]


def _delta(qkv, z, gsc, p, pool_prev, s0, w_onorm, w_mix, pool_scale, *, bb, tl, chunk, pos0):
    bsz, l = qkv.shape[0], qkv.shape[1]
    n_tiles = l // tl
    state_s = pl.BlockSpec((bb, N_HEADS, HEAD_DIM, HEAD_DIM), lambda i, j: (i, 0, 0, 0))
    if n_tiles == 1:
        gate_specs = [_gate_rows_spec(bb, tl, 1)]
    else:
        gate_specs = [pl.BlockSpec((N_GATE_ROWS, SUBLANES, tl), lambda i, j, k=k: (0, 0, (i * bb + k) * n_tiles + j))
                      for k in range(bb)]
    kern = functools.partial(_delta_kernel, bb=bb, tl=tl, chunk=chunk, pos0=pos0, n_tiles=n_tiles,
                             n_gate_refs=len(gate_specs))
    sds = lambda *shape: jax.ShapeDtypeStruct(shape, F32)
    return pl.pallas_call(
        kern,
        grid=(bsz // bb, n_tiles),
        in_specs=gate_specs + [
                  _seq_spec(bb, tl, QKV_W), _seq_spec(bb, tl, QK_W),
                  _seq_spec(bb, tl, WIDTH_B), _state_spec(bb, POOL_HIST, WIDTH_B), state_s,
                  _const_spec((1, HEAD_DIM)), _const_spec((len(POOL_WINDOWS), POOL_GROUP, POOL_GROUP)),
                  _const_spec((1, WIDTH_B))],
        out_specs=[_seq_spec(bb, tl, QK_W), _seq_spec(bb, tl, WIDTH_B), _state_spec(bb, POOL_HIST, WIDTH_B), state_s],
        out_shape=[sds(bsz, l, QK_W), sds(bsz, l, WIDTH_B), sds(bsz, POOL_HIST, WIDTH_B),
                   sds(bsz, N_HEADS, HEAD_DIM, HEAD_DIM)],
        scratch_shapes=[pltpu.VMEM((bb, POOL_PAD + tl, WIDTH_B), F32)],
        compiler_params=pltpu.CompilerParams(dimension_semantics=("arbitrary", "arbitrary"),
                                             vmem_limit_bytes=VMEM_LIMIT),
        name="delta",
    )(*([gsc] * len(gate_specs)), qkv, z, p, pool_prev, s0, w_onorm, w_mix, pool_scale)


def _mlp_kernel(x_ref, oa_ref, ob_ref, gate_ref, wa_ref, wb_ref, wo_ref, gmlp_ref, wup_ref, wdown_ref, gfin_ref,
                y_ref):
    ma = _dot(oa_ref[...], wa_ref[...])
    mb = _dot(ob_ref[...], wb_ref[...])
    merged = _sigmoid(gate_ref[:, 0:D_MODEL]) * ma + _sigmoid(gate_ref[:, D_MODEL:]) * mb
    x1 = x_ref[...] + _dot(merged, wo_ref[...])
    h2 = (x1 * _rms_scale(x1) * gmlp_ref[...]).astype(BF16)
    acc = x1
    for c0 in range(0, D_FF, FF_BLOCK):
        up = jnp.dot(h2, wup_ref[:, c0:c0 + FF_BLOCK], preferred_element_type=F32)
        act = jnp.square(jnp.maximum(up, 0.0))
        acc = acc + _dot(act, wdown_ref[c0:c0 + FF_BLOCK, :])
    y_ref[...] = acc * _rms_scale(acc) * gfin_ref[...]


def _merge_mlp(x2d, oa, ob, gates, prm, tm):
    t = x2d.shape[0]
    row = lambda w: pl.BlockSpec((tm, w), lambda i: (i, 0))
    return pl.pallas_call(
        _mlp_kernel,
        grid=(t // tm,),
        in_specs=[row(D_MODEL), row(QK_W), row(WIDTH_B), row(2 * D_MODEL),
                  _const_spec((QK_W, D_MODEL)), _const_spec((WIDTH_B, D_MODEL)), _const_spec((D_MODEL, D_MODEL)),
                  _const_spec((1, D_MODEL)), _const_spec((D_MODEL, D_FF)), _const_spec((D_FF, D_MODEL)),
                  _const_spec((1, D_MODEL))],
        out_specs=row(D_MODEL),
        out_shape=jax.ShapeDtypeStruct((t, D_MODEL), F32),
        compiler_params=pltpu.CompilerParams(dimension_semantics=("arbitrary",),
                                             vmem_limit_bytes=VMEM_LIMIT),
        name="merge_mlp",
    )(x2d, oa, ob, gates, prm["w_a_out"], prm["w_b_out"], prm["w_o"], prm["g_mlp"], prm["w_up"], prm["w_down"],
      prm["g_final"])


def _trunk(x, conv_prev, pool_prev, s_prev, pos0, prm, *, front_blk, delta_blk, tm, chunk):
    bsz, l, _ = x.shape
    t = bsz * l
    qkv, z, gsc, p, gates, conv_new = _front(
        x, prm["g_attn"], prm["wq"], prm["wba"], prm["wr"], prm["w_conv"], prm["a_log"], prm["dt_bias"],
        conv_prev, bb=front_blk[0], tl=front_blk[1], chunk=chunk)
    oa, ob, pool_new, s_new = _delta(qkv, z, gsc, p, pool_prev, s_prev, prm["w_onorm"], prm["w_mix"],
                                     prm["pool_scale"], bb=delta_blk[0], tl=delta_blk[1], chunk=chunk, pos0=pos0)
    y = _merge_mlp(x.reshape(t, D_MODEL), oa.reshape(t, QK_W), ob.reshape(t, WIDTH_B),
                   gates.reshape(t, 2 * D_MODEL), prm, tm)
    return y.reshape(bsz, l, D_MODEL), conv_new[None], pool_new[None], s_new[None]


def kernel(x_prompt, x_sample, state_conv, state_pool, state_ssm, w_in, w_conv, a_log, dt_bias, w_onorm,
           w_pool_mix, pool_scale, w_a_out, w_b_out, w_o, g_attn, g_mlp, w_up, w_down, g_final):
    assert w_in.shape[0] == 1, "single-layer decoder"
    wt = jnp.transpose(w_in[0])
    rest_off = QKVZ_W + GATE_SCALARS
    prm = {
        "wq": wt[0:QKVZ_W].astype(BF16),
        "wba": jnp.pad(wt[QKVZ_W:rest_off], ((0, 2 * SUBLANES - GATE_SCALARS), (0, 0))).astype(BF16),
        "wr": wt[rest_off:rest_off + REST_W].astype(BF16),
        "g_attn": g_attn[0][None, :], "g_mlp": g_mlp[0][None, :], "g_final": g_final[None, :],
        "w_conv": w_conv[0].astype(F32),
        "a_log": a_log[0], "dt_bias": dt_bias[0],
        "w_onorm": w_onorm[0][None, :].astype(F32),
        "w_mix": w_pool_mix[0].astype(BF16), "pool_scale": pool_scale[0][None, :].astype(F32),
        "w_a_out": w_a_out[0].astype(BF16), "w_b_out": w_b_out[0].astype(BF16), "w_o": w_o[0].astype(BF16),
        "w_up": w_up[0].astype(BF16), "w_down": w_down[0].astype(BF16),
    }
    bp = x_prompt.shape[0]
    y_p, conv_p, pool_p, ssm_p = _trunk(
        x_prompt, jnp.zeros((bp, CONV_W - 1, QKV_W), F32), jnp.zeros((bp, POOL_HIST, WIDTH_B), F32),
        jnp.zeros((bp, N_HEADS, HEAD_DIM, HEAD_DIM), F32), 0, prm,
        front_blk=(1, 512), delta_blk=(2, 256), tm=512, chunk=GROUP)
    dec_len = x_sample.shape[1]
    y_s, conv_s, pool_s, ssm_s = _trunk(
        x_sample, state_conv[0].astype(F32), state_pool[0].astype(F32), state_ssm[0].astype(F32), PAST_LEN, prm,
        front_blk=(256 // dec_len, dec_len), delta_blk=(2 * GROUP // dec_len, dec_len), tm=512,
        chunk=dec_len)
    return (y_p, y_s, conv_p.astype(state_conv.dtype), pool_p.astype(state_pool.dtype),
            ssm_p.astype(state_ssm.dtype), conv_s.astype(state_conv.dtype), pool_s.astype(state_pool.dtype),
            ssm_s.astype(state_ssm.dtype))
```

```python
import functools
import math

import jax
import jax.numpy as jnp
from jax import lax
from jax.experimental import pallas as pl
from jax.experimental.pallas import tpu as pltpu

D_MODEL = 1024
N_HEADS = 4
HEAD_DIM = 128
QK_W = N_HEADS * HEAD_DIM
QKV_W = 3 * QK_W
CONV_W = 4
POOL_WINDOWS = (2, 4, 8, 16)
POOL_GROUP = 128
WIDTH_B = len(POOL_WINDOWS) * POOL_GROUP
POOL_HIST = 15
D_FF = 4 * D_MODEL
EPS = 1e-6
PAST_LEN = 16384
LANES = 128
SUBLANES = 8
MXU_COLS = 256
N_GATE_ROWS = 4

QKVZ_W = QKV_W + QK_W
GATE_SCALARS = 2 * N_HEADS
REST_W = WIDTH_B + 2 * D_MODEL

GROUP = 128
SERIES_BLOCK = 64
CONV_PAD = 8
POOL_LOOKBACK = 16
POOL_PAD = 24
FF_BLOCK = 1024

VMEM_LIMIT = 56 * 1024 * 1024

BF16 = jnp.bfloat16
F32 = jnp.float32


def _dot(a, b):
    return jnp.dot(a.astype(BF16), b.astype(BF16), preferred_element_type=F32)


def _sigmoid(x):
    return 1.0 / (1.0 + jnp.exp(-x))


def _silu(x):
    half = 0.5 * x
    return half * jnp.tanh(half) + half


def _rms_scale(x):
    return lax.rsqrt(jnp.mean(x * x, axis=-1, keepdims=True) + EPS)


def _const_spec(shape):
    zeros = (0,) * len(shape)
    return pl.BlockSpec(shape, lambda *_: zeros, pipeline_mode=pl.Buffered(1))


def _seq_spec(bb, tl, width):
    return pl.BlockSpec((bb, tl, width), lambda i, j: (i, j, 0))


def _state_spec(bb, n_rows, width):
    return pl.BlockSpec((bb, n_rows, width), lambda i, j: (i, 0, 0))


def _gate_rows_spec(bb, tl, n_tiles):
    return pl.BlockSpec((N_GATE_ROWS, SUBLANES, bb * tl), lambda i, j: (0, 0, i * n_tiles + j))


def _lane_prefix_sum(x, chunk):
    lane = lax.broadcasted_iota(jnp.int32, x.shape, 1)
    shift = 1
    while shift < chunk:
        x = x + jnp.where(lane % chunk >= shift, pltpu.roll(x, shift, axis=1), 0.0)
        shift *= 2
    return x


def _lane_suffix_sum(x, chunk):
    lane = lax.broadcasted_iota(jnp.int32, x.shape, 1)
    shift = 1
    while shift < chunk:
        x = x + jnp.where(lane % chunk + shift < chunk, pltpu.roll(x, x.shape[1] - shift, axis=1), 0.0)
        shift *= 2
    return x


def _front_kernel(x_ref, g_ref, wq_ref, wba_ref, wr_ref, wconv_ref, alog_ref, dtb_ref, keep_ref, convprev_ref,
                  qkv_ref, z_ref, gsc_ref, p_ref, gate_ref, convnew_ref,
                  ext_ref, *, bb, tl, n_tiles, chunk):
    tile = pl.program_id(1)
    rows = bb * tl
    hist = slice(CONV_PAD - (CONV_W - 1), CONV_PAD)

    @pl.when(tile == 0)
    def _():
        ext_ref[:, hist, :] = convprev_ref[...]
        ext_ref[:, 0:CONV_PAD - (CONV_W - 1), :] = jnp.zeros((bb, CONV_PAD - (CONV_W - 1), QKV_W), F32)

    x = x_ref[...].reshape(rows, D_MODEL)
    h = (x * _rms_scale(x) * g_ref[...]).astype(BF16)
    contract_last = (((1,), (1,)), ((), ()))
    proj = lambda w_rows: lax.dot_general(h, w_rows, contract_last, preferred_element_type=F32)
    ext_ref[:, CONV_PAD:CONV_PAD + tl, :] = proj(wq_ref[0:QKV_W, :]).reshape(bb, tl, QKV_W)
    convnew_ref[...] = ext_ref[:, CONV_PAD + tl - (CONV_W - 1):CONV_PAD + tl, :]

    x8 = lax.dot_general(wba_ref[...], h, contract_last, preferred_element_type=F32)[0:SUBLANES, :]
    xs = x8 + dtb_ref[...]
    softplus = jnp.maximum(xs, 0.0) + jnp.log1p(jnp.exp(-jnp.abs(xs)))
    graw8 = -jnp.exp(alog_ref[...]) * softplus
    g8 = _lane_prefix_sum(graw8, chunk)
    gsc_ref[0] = _sigmoid(x8)
    gsc_ref[1] = g8
    gsc_ref[2] = jnp.exp(g8)
    gsc_ref[3] = jnp.exp(_lane_suffix_sum(graw8, chunk) - graw8)

    keep = keep_ref[...] != 0

    def conv_block(c0, dep):
        cols = slice(c0, c0 + LANES)
        xe = ext_ref[:, :, cols].reshape(bb * (CONV_PAD + tl), LANES)
        acc = wconv_ref[0:1, cols] * xe
        for j in range(1, CONV_W):
            acc = wconv_ref[j:j + 1, cols] * xe + pltpu.roll(acc, 1, axis=0)
        val = _silu(acc.reshape(bb, CONV_PAD + tl, LANES)[:, CONV_PAD:, :])
        if c0 < 2 * QK_W:
            scale = HEAD_DIM ** -0.5 if c0 < QK_W else 1.0
            val = val * (lax.rsqrt(jnp.sum(val * val, axis=-1, keepdims=True) + EPS) * scale)
        qkv_ref[:, :, cols] = jnp.where(keep, val, dep[:, 0:LANES].reshape(bb, tl, LANES)).astype(qkv_ref.dtype)

    def proj_block(out_ref, w_ref, w0, c0):
        val = proj(w_ref[w0 + c0:w0 + c0 + MXU_COLS, :])
        out_ref[:, :, c0:c0 + MXU_COLS] = val.reshape(bb, tl, MXU_COLS)
        return val

    mxu_work = ([functools.partial(proj_block, z_ref, wq_ref, QKV_W, c0) for c0 in range(0, QK_W, MXU_COLS)]
                + [functools.partial(proj_block, p_ref, wr_ref, 0, c0) for c0 in range(0, WIDTH_B, MXU_COLS)]
                + [functools.partial(proj_block, gate_ref, wr_ref, WIDTH_B, c0)
                   for c0 in range(0, 2 * D_MODEL, MXU_COLS)])
    for i, c0 in enumerate(range(0, QKV_W, LANES)):
        conv_block(c0, mxu_work[i]())

    if n_tiles > 1:
        ext_ref[:, hist, :] = convnew_ref[...]


def _front(x, g_attn, wq, wba, wr, w_conv, a_log, dt_bias, conv_prev, *, bb, tl, chunk, act_dtype):
    bsz, l, _ = x.shape
    n_tiles = l // tl
    kern = functools.partial(_front_kernel, bb=bb, tl=tl, n_tiles=n_tiles, chunk=chunk)
    sds = lambda *shape: jax.ShapeDtypeStruct(shape, F32)
    zeros4 = jnp.zeros((N_HEADS,), F32)
    rows8 = lambda v: jnp.broadcast_to(jnp.concatenate([zeros4, v.astype(F32)])[:, None], (SUBLANES, bb * tl))
    return pl.pallas_call(
        kern,
        grid=(bsz // bb, n_tiles),
        in_specs=[_seq_spec(bb, tl, D_MODEL), _const_spec((1, D_MODEL)), _const_spec((QKVZ_W, D_MODEL)),
                  _const_spec((2 * SUBLANES, D_MODEL)), _const_spec((REST_W, D_MODEL)), _const_spec((CONV_W, QKV_W)),
                  _const_spec((SUBLANES, bb * tl)), _const_spec((SUBLANES, bb * tl)), _const_spec((1, LANES)),
                  _state_spec(bb, CONV_W - 1, QKV_W)],
        out_specs=[_seq_spec(bb, tl, QKV_W), _seq_spec(bb, tl, QK_W), _gate_rows_spec(bb, tl, n_tiles),
                   _seq_spec(bb, tl, WIDTH_B), _seq_spec(bb, tl, 2 * D_MODEL), _state_spec(bb, CONV_W - 1, QKV_W)],
        out_shape=[jax.ShapeDtypeStruct((bsz, l, QKV_W), act_dtype), sds(bsz, l, QK_W),
                   sds(N_GATE_ROWS, SUBLANES, bsz * l),
                   sds(bsz, l, WIDTH_B), sds(bsz, l, 2 * D_MODEL), sds(bsz, CONV_W - 1, QKV_W)],
        scratch_shapes=[pltpu.VMEM((bb, CONV_PAD + tl, QKV_W), F32)],
        compiler_params=pltpu.CompilerParams(dimension_semantics=("arbitrary", "arbitrary"),
                                             vmem_limit_bytes=VMEM_LIMIT),
        name="front",
    )(x, g_attn, wq, wba, wr, w_conv, rows8(a_log), rows8(dt_bias), jnp.ones((1, LANES), jnp.int32), conv_prev)


def _unit_lower_inverses(neg_ms, chunk):
    nb = min(chunk, SERIES_BLOCK)
    n_blocks = GROUP // nb
    n_factors = int(math.log2(nb))
    assert chunk == nb or (chunk == 2 * nb and n_blocks == 2), "chunks are one or two series blocks"
    lane = lax.broadcasted_iota(jnp.int32, (nb, GROUP), 1)
    lane_block = lane // nb

    def packed(m):
        out = m[0:nb]
        for b in range(1, n_blocks):
            out = jnp.where(lane_block == b, m[b * nb:(b + 1) * nb], out)
        return out

    def block_diag(p):
        return jnp.concatenate([jnp.where(lane_block == b, p, 0.0) for b in range(n_blocks)], axis=0)

    nps = [packed(m) for m in neg_ms]
    eye_p = jnp.where(lane % nb == lax.broadcasted_iota(jnp.int32, (nb, GROUP), 0), 1.0, 0.0).astype(F32)
    invs = [eye_p + n for n in nps]
    if n_factors > 1:
        pws = [_dot(n, block_diag(n)) for n in nps]
        for _ in range(n_factors - 2):
            boths = [_dot(jnp.concatenate([inv, pw], axis=0), block_diag(pw)) for inv, pw in zip(invs, pws)]
            invs = [inv + both[0:nb] for inv, both in zip(invs, boths)]
            pws = [both[nb:] for both in boths]
        invs = [inv + _dot(inv, block_diag(pw)) for inv, pw in zip(invs, pws)]
    if chunk == nb:
        return [block_diag(inv) for inv in invs]
    zeros = jnp.zeros((nb, GROUP), F32)
    a_invs = [jnp.where(lane < nb, inv, 0.0) for inv in invs]
    neg_ls = [jnp.where(lane < nb, m[nb:], 0.0) for m in neg_ms]
    xs = [_dot(neg_l, jnp.concatenate([a_inv, zeros], axis=0)) for neg_l, a_inv in zip(neg_ls, a_invs)]
    ys = [_dot(inv, jnp.concatenate([zeros, x], axis=0)) for inv, x in zip(invs, xs)]
    return [jnp.concatenate([a_inv, y + jnp.where(lane >= nb, inv, 0.0)], axis=0)
            for a_inv, y, inv in zip(a_invs, ys, invs)]


def _delta_kernel(*refs, bb, tl, chunk, pos0, n_tiles, n_gate_refs):
    gsc_refs, refs = refs[:n_gate_refs], refs[n_gate_refs:]
    (qkv_ref, z_ref, p_ref, poolprev_ref, s0_ref, wonorm_ref, wmix_ref, pscale_ref,
     oa_ref, ob_ref, poolnew_ref, snew_ref, pext_ref) = refs
    tile = pl.program_id(1)
    rows_b = min(tl, GROUP)
    seqs_g = GROUP // rows_b
    groups_b = tl // rows_b
    n_groups = bb * tl // GROUP
    chained = chunk == GROUP
    assert chained or (chunk == tl and tl < GROUP), "chunk must be a whole group or a whole short sequence"
    pool_hist = slice(POOL_PAD - POOL_HIST, POOL_PAD)

    @pl.when(tile == 0)
    def _():
        snew_ref[...] = s0_ref[...]
        pext_ref[:, pool_hist, :] = poolprev_ref[...]
        pext_ref[:, POOL_PAD - POOL_LOOKBACK:POOL_PAD - POOL_HIST, :] = jnp.zeros((bb, 1, WIDTH_B), F32)

    pext_ref[:, POOL_PAD:POOL_PAD + tl, :] = p_ref[...]
    poolnew_ref[...] = pext_ref[:, POOL_PAD + tl - POOL_HIST:POOL_PAD + tl, :]

    row = lax.broadcasted_iota(jnp.int32, (GROUP, GROUP), 0)
    col = lax.broadcasted_iota(jnp.int32, (GROUP, GROUP), 1)
    causal = row >= col
    strict = row > col
    if not chained:
        same = (row // chunk) == (col // chunk)
        causal = causal & same
        strict = strict & same

    def origin(g):
        if tl >= GROUP:
            return g // groups_b, (g % groups_b) * GROUP
        return g * seqs_g, 0

    def load(c0, g):
        b0, t0 = origin(g)
        return qkv_ref[b0:b0 + seqs_g, t0:t0 + rows_b, c0:c0 + HEAD_DIM].astype(F32).reshape(GROUP, HEAD_DIM)

    def gate_rows(g):
        if n_gate_refs > 1:
            b0, t0 = origin(g)
            return tuple(gsc_refs[b0][q, :, t0:t0 + GROUP] for q in range(N_GATE_ROWS))
        return tuple(gsc_refs[0][q, :, g * GROUP:(g + 1) * GROUP] for q in range(N_GATE_ROWS))

    gates = [gate_rows(g) for g in range(n_groups)]

    probs = [(g, h) for g in range(n_groups) for h in range(N_HEADS)]
    st = []
    for g, h in probs:
        beta8, g8, eg8, kds8 = gates[g]
        d = {"q": load(h * HEAD_DIM, g), "k": load(QK_W + h * HEAD_DIM, g), "v": load(2 * QK_W + h * HEAD_DIM, g)}
        d["beta_row"] = beta8[h:h + 1, :]
        d["eg_row"] = eg8[N_HEADS + h:N_HEADS + h + 1, :]
        d["g_row"] = g8[N_HEADS + h:N_HEADS + h + 1, :]
        d["kb_row"] = kds8[N_HEADS + h:N_HEADS + h + 1, :] * d["beta_row"]
        d["g_col"] = jnp.broadcast_to(d["g_row"], (GROUP, GROUP)).T
        d["kt"] = d["k"].T
        st.append(d)
    for d in st:
        both = _dot(jnp.concatenate([d["k"], d["q"]], axis=0), d["kt"])
        d["kk"], d["qk"] = both[0:GROUP], both[GROUP:]
    for d in st:
        decay = jnp.exp(jnp.where(causal, d["g_col"] - d["g_row"], -jnp.inf))
        d["neg_m"] = jnp.where(strict, -(d.pop("kk") * decay), 0.0) * d["beta_row"]
        d["qkm"] = d.pop("qk") * decay * d["beta_row"]
        d["q_dec"] = d.pop("q") * jnp.exp(d["g_col"])
        d["kt_dec"] = d.pop("kt") * d["kb_row"]
    invs = _unit_lower_inverses([d.pop("neg_m") for d in st], chunk)
    for d, inv in zip(st, invs):
        d["uy"] = _dot(inv, d.pop("v"))
        d["wy"] = _dot(inv * d["eg_row"], d.pop("k"))

    outs = {}
    if chained:
        for j in range(groups_b):
            wave = [(i, g, h) for i, (g, h) in enumerate(probs) if g % groups_b == j]
            s_old = {i: snew_ref[origin(g)[0], h] for i, g, h in wave}
            ws = {i: _dot(jnp.concatenate([st[i]["wy"], st[i]["q_dec"]], axis=0), s_old[i]) for i, g, h in wave}
            ys = {i: st[i]["uy"] - ws[i][0:GROUP] for i, g, h in wave}
            for i, g, h in wave:
                outs[i] = ws[i][GROUP:] + _dot(st[i]["qkm"], ys[i])
            for i, g, h in wave:
                last = jnp.exp(st[i]["g_col"][GROUP - 1:GROUP, :])
                snew_ref[origin(g)[0], h] = s_old[i] * last + _dot(st[i]["kt_dec"], ys[i])
    else:
        for i, (g, h) in enumerate(probs):
            d, b0 = st[i], origin(g)[0]
            ws_w, ws_q = [], []
            for s_i in range(seqs_g):
                r = slice(s_i * rows_b, (s_i + 1) * rows_b)
                ws = _dot(jnp.concatenate([d["wy"][r], d["q_dec"][r]], axis=0), snew_ref[b0 + s_i, h])
                ws_w.append(ws[0:rows_b])
                ws_q.append(ws[rows_b:])
            d["y"] = d["uy"] - jnp.concatenate(ws_w, axis=0)
            outs[i] = jnp.concatenate(ws_q, axis=0) + _dot(d["qkm"], d["y"])
        for i, (g, h) in enumerate(probs):
            d, b0 = st[i], origin(g)[0]
            for s_i in range(seqs_g):
                last = jnp.exp(d["g_col"][(s_i + 1) * rows_b - 1:(s_i + 1) * rows_b, :])
                upd = _dot(jnp.where(col // rows_b == s_i, d["kt_dec"], 0.0), d["y"])
                snew_ref[b0 + s_i, h] = snew_ref[b0 + s_i, h] * last + upd

    def group_slab(ref, g, cols, row_off=0):
        b0, t0 = origin(g)
        return ref.at[b0:b0 + seqs_g, row_off + t0:row_off + t0 + rows_b, cols]

    for i, (g, h) in enumerate(probs):
        cols = slice(h * HEAD_DIM, (h + 1) * HEAD_DIM)
        o = outs[i]
        z = group_slab(z_ref, g, cols)[...].reshape(GROUP, HEAD_DIM)
        group_slab(oa_ref, g, cols)[...] = (o * _rms_scale(o) * wonorm_ref[...] * _silu(z)).reshape(
            seqs_g, rows_b, HEAD_DIM).astype(oa_ref.dtype)

    for g in range(n_groups):
        b0, t0 = origin(g)
        pos = pos0 + tile * tl + t0 + row % rows_b
        for gi, win in enumerate(POOL_WINDOWS):
            cols = slice(gi * POOL_GROUP, (gi + 1) * POOL_GROUP)
            slab = pext_ref[b0:b0 + seqs_g, POOL_PAD - POOL_LOOKBACK + t0:POOL_PAD + t0 + rows_b, cols]
            acc = slab.reshape(seqs_g * (POOL_LOOKBACK + rows_b), POOL_GROUP)
            shift = 1
            while shift < win:
                acc = acc + pltpu.roll(acc, shift, axis=0)
                shift *= 2
            acc = acc.reshape(seqs_g, POOL_LOOKBACK + rows_b, POOL_GROUP)[:, POOL_LOOKBACK:, :]
            cur = slab[:, POOL_LOOKBACK:, :]
            pooled = (acc / jnp.minimum(pos + 1, win).astype(F32).reshape(seqs_g, rows_b, POOL_GROUP) - cur)
            mixed = _dot(pooled.reshape(GROUP, POOL_GROUP), wmix_ref[gi]) * pscale_ref[:, cols]
            ob_ref[b0:b0 + seqs_g, t0:t0 + rows_b, cols] = mixed.reshape(seqs_g, rows_b, POOL_GROUP).astype(ob_ref.dtype)

    if n_tiles > 1:
        pext_ref[:, pool_hist, :] = poolnew_ref[...]


def _delta(qkv, z, gsc, p, pool_prev, s0, w_onorm, w_mix, pool_scale, *, bb, tl, chunk, pos0, act_dtype):
    bsz, l = qkv.shape[0], qkv.shape[1]
    n_tiles = l // tl
    state_s = pl.BlockSpec((bb, N_HEADS, HEAD_DIM, HEAD_DIM), lambda i, j: (i, 0, 0, 0))
    if n_tiles == 1:
        gate_specs = [_gate_rows_spec(bb, tl, 1)]
    else:
        gate_specs = [pl.BlockSpec((N_GATE_ROWS, SUBLANES, tl), lambda i, j, k=k: (0, 0, (i * bb + k) * n_tiles + j))
                      for k in range(bb)]
    kern = functools.partial(_delta_kernel, bb=bb, tl=tl, chunk=chunk, pos0=pos0, n_tiles=n_tiles,
                             n_gate_refs=len(gate_specs))
    sds = lambda *shape: jax.ShapeDtypeStruct(shape, F32)
    return pl.pallas_call(
        kern,
        grid=(bsz // bb, n_tiles),
        in_specs=gate_specs + [
                  _seq_spec(bb, tl, QKV_W), _seq_spec(bb, tl, QK_W),
                  _seq_spec(bb, tl, WIDTH_B), _state_spec(bb, POOL_HIST, WIDTH_B), state_s,
                  _const_spec((1, HEAD_DIM)), _const_spec((len(POOL_WINDOWS), POOL_GROUP, POOL_GROUP)),
                  _const_spec((1, WIDTH_B))],
        out_specs=[_seq_spec(bb, tl, QK_W), _seq_spec(bb, tl, WIDTH_B), _state_spec(bb, POOL_HIST, WIDTH_B), state_s],
        out_shape=[jax.ShapeDtypeStruct((bsz, l, QK_W), act_dtype), jax.ShapeDtypeStruct((bsz, l, WIDTH_B), act_dtype),
                   sds(bsz, POOL_HIST, WIDTH_B),
                   sds(bsz, N_HEADS, HEAD_DIM, HEAD_DIM)],
        scratch_shapes=[pltpu.VMEM((bb, POOL_PAD + tl, WIDTH_B), F32)],
        compiler_params=pltpu.CompilerParams(dimension_semantics=("arbitrary", "arbitrary"),
                                             vmem_limit_bytes=VMEM_LIMIT),
        name="delta",
    )(*([gsc] * len(gate_specs)), qkv, z, p, pool_prev, s0, w_onorm, w_mix, pool_scale)


def _mlp_kernel(x_ref, oa_ref, ob_ref, gate_ref, wa_ref, wb_ref, wo_ref, gmlp_ref, wup_ref, wdown_ref, gfin_ref,
                y_ref):
    ma = _dot(oa_ref[...], wa_ref[...])
    mb = _dot(ob_ref[...], wb_ref[...])
    merged = _sigmoid(gate_ref[:, 0:D_MODEL]) * ma + _sigmoid(gate_ref[:, D_MODEL:]) * mb
    x1 = x_ref[...] + _dot(merged, wo_ref[...])
    h2 = (x1 * _rms_scale(x1) * gmlp_ref[...]).astype(BF16)
    acc = x1
    for c0 in range(0, D_FF, FF_BLOCK):
        up = jnp.dot(h2, wup_ref[:, c0:c0 + FF_BLOCK], preferred_element_type=F32)
        act = jnp.square(jnp.maximum(up, 0.0))
        acc = acc + _dot(act, wdown_ref[c0:c0 + FF_BLOCK, :])
    y_ref[...] = acc * _rms_scale(acc) * gfin_ref[...]


def _merge_mlp(x2d, oa, ob, gates, prm, tm):
    t = x2d.shape[0]
    row = lambda w: pl.BlockSpec((tm, w), lambda i: (i, 0))
    return pl.pallas_call(
        _mlp_kernel,
        grid=(t // tm,),
        in_specs=[row(D_MODEL), row(QK_W), row(WIDTH_B), row(2 * D_MODEL),
                  _const_spec((QK_W, D_MODEL)), _const_spec((WIDTH_B, D_MODEL)), _const_spec((D_MODEL, D_MODEL)),
                  _const_spec((1, D_MODEL)), _const_spec((D_MODEL, D_FF)), _const_spec((D_FF, D_MODEL)),
                  _const_spec((1, D_MODEL))],
        out_specs=row(D_MODEL),
        out_shape=jax.ShapeDtypeStruct((t, D_MODEL), F32),
        compiler_params=pltpu.CompilerParams(dimension_semantics=("arbitrary",),
                                             vmem_limit_bytes=VMEM_LIMIT),
        name="merge_mlp",
    )(x2d, oa, ob, gates, prm["w_a_out"], prm["w_b_out"], prm["w_o"], prm["g_mlp"], prm["w_up"], prm["w_down"],
      prm["g_final"])


def _trunk(x, conv_prev, pool_prev, s_prev, pos0, prm, *, front_blk, delta_blk, tm, chunk, act_dtype):
    bsz, l, _ = x.shape
    t = bsz * l
    qkv, z, gsc, p, gates, conv_new = _front(
        x, prm["g_attn"], prm["wq"], prm["wba"], prm["wr"], prm["w_conv"], prm["a_log"], prm["dt_bias"],
        conv_prev, bb=front_blk[0], tl=front_blk[1], chunk=chunk, act_dtype=act_dtype)
    oa, ob, pool_new, s_new = _delta(qkv, z, gsc, p, pool_prev, s_prev, prm["w_onorm"], prm["w_mix"],
                                     prm["pool_scale"], bb=delta_blk[0], tl=delta_blk[1], chunk=chunk, pos0=pos0,
                                     act_dtype=act_dtype)
    y = _merge_mlp(x.reshape(t, D_MODEL), oa.reshape(t, QK_W), ob.reshape(t, WIDTH_B),
                   gates.reshape(t, 2 * D_MODEL), prm, tm)
    return y.reshape(bsz, l, D_MODEL), conv_new[None], pool_new[None], s_new[None]


def kernel(x_prompt, x_sample, state_conv, state_pool, state_ssm, w_in, w_conv, a_log, dt_bias, w_onorm,
           w_pool_mix, pool_scale, w_a_out, w_b_out, w_o, g_attn, g_mlp, w_up, w_down, g_final):
    assert w_in.shape[0] == 1, "single-layer decoder"
    wt = jnp.transpose(w_in[0])
    rest_off = QKVZ_W + GATE_SCALARS
    prm = {
        "wq": wt[0:QKVZ_W].astype(BF16),
        "wba": jnp.pad(wt[QKVZ_W:rest_off], ((0, 2 * SUBLANES - GATE_SCALARS), (0, 0))).astype(BF16),
        "wr": wt[rest_off:rest_off + REST_W].astype(BF16),
        "g_attn": g_attn[0][None, :], "g_mlp": g_mlp[0][None, :], "g_final": g_final[None, :],
        "w_conv": w_conv[0].astype(F32),
        "a_log": a_log[0], "dt_bias": dt_bias[0],
        "w_onorm": w_onorm[0][None, :].astype(F32),
        "w_mix": w_pool_mix[0].astype(BF16), "pool_scale": pool_scale[0][None, :].astype(F32),
        "w_a_out": w_a_out[0].astype(BF16), "w_b_out": w_b_out[0].astype(BF16), "w_o": w_o[0].astype(BF16),
        "w_up": w_up[0].astype(BF16), "w_down": w_down[0].astype(BF16),
    }
    bp = x_prompt.shape[0]
    y_p, conv_p, pool_p, ssm_p = _trunk(
        x_prompt, jnp.zeros((bp, CONV_W - 1, QKV_W), F32), jnp.zeros((bp, POOL_HIST, WIDTH_B), F32),
        jnp.zeros((bp, N_HEADS, HEAD_DIM, HEAD_DIM), F32), 0, prm,
        front_blk=(1, 512), delta_blk=(2, 256), tm=512, chunk=GROUP, act_dtype=BF16)
    dec_len = x_sample.shape[1]
    y_s, conv_s, pool_s, ssm_s = _trunk(
        x_sample, state_conv[0].astype(F32), state_pool[0].astype(F32), state_ssm[0].astype(F32), PAST_LEN, prm,
        front_blk=(256 // dec_len, dec_len), delta_blk=(2 * GROUP // dec_len, dec_len), tm=512,
        chunk=dec_len, act_dtype=F32)
    return (y_p, y_s, conv_p.astype(state_conv.dtype), pool_p.astype(state_pool.dtype),
            ssm_p.astype(state_ssm.dtype), conv_s.astype(state_conv.dtype), pool_s.astype(state_pool.dtype),
            ssm_s.astype(state_ssm.dtype))
```

```python
import functools
import math

import jax
import jax.numpy as jnp
from jax import lax
from jax.experimental import pallas as pl
from jax.experimental.pallas import tpu as pltpu

D_MODEL = 1024
N_HEADS = 4
HEAD_DIM = 128
QK_W = N_HEADS * HEAD_DIM
QKV_W = 3 * QK_W
CONV_W = 4
POOL_WINDOWS = (2, 4, 8, 16)
POOL_GROUP = 128
WIDTH_B = len(POOL_WINDOWS) * POOL_GROUP
POOL_HIST = 15
D_FF = 4 * D_MODEL
EPS = 1e-6
PAST_LEN = 16384
LANES = 128
SUBLANES = 8
MXU_COLS = 256
N_GATE_ROWS = 4

QKVZ_W = QKV_W + QK_W
GATE_SCALARS = 2 * N_HEADS
REST_W = WIDTH_B + 2 * D_MODEL

GROUP = 128
SERIES_BLOCK = 64
CONV_PAD = 8
POOL_LOOKBACK = 16
POOL_PAD = 24
FF_BLOCK = 1024

VMEM_LIMIT = 56 * 1024 * 1024

BF16 = jnp.bfloat16
F32 = jnp.float32


def _dot(a, b):
    return jnp.dot(a.astype(BF16), b.astype(BF16), preferred_element_type=F32)


def _sigmoid(x):
    return 1.0 / (1.0 + jnp.exp(-x))


def _silu(x):
    half = 0.5 * x
    return half * jnp.tanh(half) + half


def _rms_scale(x):
    return lax.rsqrt(jnp.mean(x * x, axis=-1, keepdims=True) + EPS)


def _const_spec(shape):
    zeros = (0,) * len(shape)
    return pl.BlockSpec(shape, lambda *_: zeros, pipeline_mode=pl.Buffered(1))


def _seq_spec(bb, tl, width):
    return pl.BlockSpec((bb, tl, width), lambda i, j: (i, j, 0))


def _state_spec(bb, n_rows, width):
    return pl.BlockSpec((bb, n_rows, width), lambda i, j: (i, 0, 0))


def _gate_rows_spec(bb, tl, n_tiles):
    return pl.BlockSpec((N_GATE_ROWS, SUBLANES, bb * tl), lambda i, j: (0, 0, i * n_tiles + j))


def _lane_prefix_sum(x, chunk):
    lane = lax.broadcasted_iota(jnp.int32, x.shape, 1)
    shift = 1
    while shift < chunk:
        x = x + jnp.where(lane % chunk >= shift, pltpu.roll(x, shift, axis=1), 0.0)
        shift *= 2
    return x


def _lane_suffix_sum(x, chunk):
    lane = lax.broadcasted_iota(jnp.int32, x.shape, 1)
    shift = 1
    while shift < chunk:
        x = x + jnp.where(lane % chunk + shift < chunk, pltpu.roll(x, x.shape[1] - shift, axis=1), 0.0)
        shift *= 2
    return x


def _front_kernel(x_ref, g_ref, wq_ref, wba_ref, wr_ref, wconv_ref, alog_ref, dtb_ref, keep_ref, convprev_ref,
                  qkv_ref, z_ref, gsc_ref, p_ref, gate_ref, convnew_ref,
                  ext_ref, *, bb, tl, n_tiles, chunk):
    tile = pl.program_id(1)
    rows = bb * tl
    hist = slice(CONV_PAD - (CONV_W - 1), CONV_PAD)

    @pl.when(tile == 0)
    def _():
        ext_ref[:, hist, :] = convprev_ref[...]
        ext_ref[:, 0:CONV_PAD - (CONV_W - 1), :] = jnp.zeros((bb, CONV_PAD - (CONV_W - 1), QKV_W), F32)

    x = x_ref[...].reshape(rows, D_MODEL)
    normed = (x * _rms_scale(x) * g_ref[...]).astype(BF16)
    contract_last = (((1,), (1,)), ((), ()))
    proj = lambda w_rows: lax.dot_general(normed, w_rows, contract_last, preferred_element_type=F32)
    ext_ref[:, CONV_PAD:CONV_PAD + tl, :] = proj(wq_ref[0:QKV_W, :]).reshape(bb, tl, QKV_W)
    convnew_ref[...] = ext_ref[:, CONV_PAD + tl - (CONV_W - 1):CONV_PAD + tl, :]

    x8 = lax.dot_general(wba_ref[...], normed, contract_last, preferred_element_type=F32)[0:SUBLANES, :]
    xs = x8 + dtb_ref[...]
    softplus = jnp.maximum(xs, 0.0) + jnp.log1p(jnp.exp(-jnp.abs(xs)))
    graw8 = -jnp.exp(alog_ref[...]) * softplus
    g8 = _lane_prefix_sum(graw8, chunk)
    gsc_ref[0] = _sigmoid(x8)
    gsc_ref[1] = g8
    gsc_ref[2] = jnp.exp(g8)
    gsc_ref[3] = jnp.exp(_lane_suffix_sum(graw8, chunk) - graw8)

    keep = keep_ref[...] != 0

    def conv_block(c0, dep):
        cols = slice(c0, c0 + LANES)
        xe = ext_ref[:, :, cols].reshape(bb * (CONV_PAD + tl), LANES)
        acc = wconv_ref[0:1, cols] * xe
        for j in range(1, CONV_W):
            acc = wconv_ref[j:j + 1, cols] * xe + pltpu.roll(acc, 1, axis=0)
        val = _silu(acc.reshape(bb, CONV_PAD + tl, LANES)[:, CONV_PAD:, :])
        if c0 < 2 * QK_W:
            scale = HEAD_DIM ** -0.5 if c0 < QK_W else 1.0
            val = val * (lax.rsqrt(jnp.sum(val * val, axis=-1, keepdims=True) + EPS) * scale)
        qkv_ref[:, :, cols] = jnp.where(keep, val, dep[:, 0:LANES].reshape(bb, tl, LANES)).astype(qkv_ref.dtype)

    def proj_block(out_ref, w_ref, w0, c0):
        val = proj(w_ref[w0 + c0:w0 + c0 + MXU_COLS, :])
        out_ref[:, :, c0:c0 + MXU_COLS] = val.reshape(bb, tl, MXU_COLS)
        return val

    mxu_work = ([functools.partial(proj_block, z_ref, wq_ref, QKV_W, c0) for c0 in range(0, QK_W, MXU_COLS)]
                + [functools.partial(proj_block, p_ref, wr_ref, 0, c0) for c0 in range(0, WIDTH_B, MXU_COLS)]
                + [functools.partial(proj_block, gate_ref, wr_ref, WIDTH_B, c0)
                   for c0 in range(0, 2 * D_MODEL, MXU_COLS)])
    for i, c0 in enumerate(range(0, QKV_W, LANES)):
        conv_block(c0, mxu_work[i]())

    if n_tiles > 1:
        ext_ref[:, hist, :] = convnew_ref[...]


def _front(x, g_attn, wq, wba, wr, w_conv, a_log, dt_bias, conv_prev, *, bb, tl, chunk, act_dtype):
    bsz, l, _ = x.shape
    n_tiles = l // tl
    kern = functools.partial(_front_kernel, bb=bb, tl=tl, n_tiles=n_tiles, chunk=chunk)
    sds = lambda *shape: jax.ShapeDtypeStruct(shape, F32)
    zeros4 = jnp.zeros((N_HEADS,), F32)
    rows8 = lambda v: jnp.broadcast_to(jnp.concatenate([zeros4, v.astype(F32)])[:, None], (SUBLANES, bb * tl))
    return pl.pallas_call(
        kern,
        grid=(bsz // bb, n_tiles),
        in_specs=[_seq_spec(bb, tl, D_MODEL), _const_spec((1, D_MODEL)), _const_spec((QKVZ_W, D_MODEL)),
                  _const_spec((2 * SUBLANES, D_MODEL)), _const_spec((REST_W, D_MODEL)), _const_spec((CONV_W, QKV_W)),
                  _const_spec((SUBLANES, bb * tl)), _const_spec((SUBLANES, bb * tl)), _const_spec((1, LANES)),
                  _state_spec(bb, CONV_W - 1, QKV_W)],
        out_specs=[_seq_spec(bb, tl, QKV_W), _seq_spec(bb, tl, QK_W), _gate_rows_spec(bb, tl, n_tiles),
                   _seq_spec(bb, tl, WIDTH_B), _seq_spec(bb, tl, 2 * D_MODEL), _state_spec(bb, CONV_W - 1, QKV_W)],
        out_shape=[jax.ShapeDtypeStruct((bsz, l, QKV_W), act_dtype), sds(bsz, l, QK_W),
                   sds(N_GATE_ROWS, SUBLANES, bsz * l),
                   sds(bsz, l, WIDTH_B), sds(bsz, l, 2 * D_MODEL), sds(bsz, CONV_W - 1, QKV_W)],
        scratch_shapes=[pltpu.VMEM((bb, CONV_PAD + tl, QKV_W), F32)],
        compiler_params=pltpu.CompilerParams(dimension_semantics=("arbitrary", "arbitrary"),
                                             vmem_limit_bytes=VMEM_LIMIT),
        name="front",
    )(x, g_attn, wq, wba, wr, w_conv, rows8(a_log), rows8(dt_bias), jnp.ones((1, LANES), jnp.int32), conv_prev)


def _unit_lower_inverses(neg_ms, chunk):
    nb = min(chunk, SERIES_BLOCK)
    n_blocks = GROUP // nb
    n_factors = int(math.log2(nb))
    assert chunk == nb or (chunk == 2 * nb and n_blocks == 2), "chunks are one or two series blocks"
    lane = lax.broadcasted_iota(jnp.int32, (nb, GROUP), 1)
    lane_block = lane // nb

    def packed(m):
        out = m[0:nb]
        for b in range(1, n_blocks):
            out = jnp.where(lane_block == b, m[b * nb:(b + 1) * nb], out)
        return out

    def block_diag(p):
        return jnp.concatenate([jnp.where(lane_block == b, p, 0.0) for b in range(n_blocks)], axis=0)

    nps = [packed(m) for m in neg_ms]
    eye_p = jnp.where(lane % nb == lax.broadcasted_iota(jnp.int32, (nb, GROUP), 0), 1.0, 0.0).astype(F32)
    invs = [eye_p + n for n in nps]
    if n_factors > 1:
        pws = [_dot(n, block_diag(n)) for n in nps]
        for _ in range(n_factors - 2):
            boths = [_dot(jnp.concatenate([inv, pw], axis=0), block_diag(pw)) for inv, pw in zip(invs, pws)]
            invs = [inv + both[0:nb] for inv, both in zip(invs, boths)]
            pws = [both[nb:] for both in boths]
        invs = [inv + _dot(inv, block_diag(pw)) for inv, pw in zip(invs, pws)]
    if chunk == nb:
        return [block_diag(inv) for inv in invs]
    zeros = jnp.zeros((nb, GROUP), F32)
    a_invs = [jnp.where(lane < nb, inv, 0.0) for inv in invs]
    neg_ls = [jnp.where(lane < nb, m[nb:], 0.0) for m in neg_ms]
    xs = [_dot(neg_l, jnp.concatenate([a_inv, zeros], axis=0)) for neg_l, a_inv in zip(neg_ls, a_invs)]
    ys = [_dot(inv, jnp.concatenate([zeros, x], axis=0)) for inv, x in zip(invs, xs)]
    return [jnp.concatenate([a_inv, y + jnp.where(lane >= nb, inv, 0.0)], axis=0)
            for a_inv, y, inv in zip(a_invs, ys, invs)]


def _delta_kernel(*refs, bb, tl, chunk, pos0, n_tiles, n_gate_refs):
    gsc_refs, refs = refs[:n_gate_refs], refs[n_gate_refs:]
    (qkv_ref, z_ref, p_ref, poolprev_ref, s0_ref, wonorm_ref, wmix_ref, pscale_ref,
     oa_ref, ob_ref, poolnew_ref, snew_ref, pext_ref) = refs
    tile = pl.program_id(1)
    rows_b = min(tl, GROUP)
    seqs_g = GROUP // rows_b
    groups_b = tl // rows_b
    n_groups = bb * tl // GROUP
    chained = chunk == GROUP
    assert chained or (chunk == tl and tl < GROUP), "chunk must be a whole group or a whole short sequence"
    pool_hist = slice(POOL_PAD - POOL_HIST, POOL_PAD)

    @pl.when(tile == 0)
    def _():
        snew_ref[...] = s0_ref[...]
        pext_ref[:, pool_hist, :] = poolprev_ref[...]
        pext_ref[:, POOL_PAD - POOL_LOOKBACK:POOL_PAD - POOL_HIST, :] = jnp.zeros((bb, 1, WIDTH_B), F32)

    pext_ref[:, POOL_PAD:POOL_PAD + tl, :] = p_ref[...]
    poolnew_ref[...] = pext_ref[:, POOL_PAD + tl - POOL_HIST:POOL_PAD + tl, :]

    row = lax.broadcasted_iota(jnp.int32, (GROUP, GROUP), 0)
    col = lax.broadcasted_iota(jnp.int32, (GROUP, GROUP), 1)
    causal = row >= col
    strict = row > col
    if not chained:
        same = (row // chunk) == (col // chunk)
        causal = causal & same
        strict = strict & same

    def origin(g):
        if tl >= GROUP:
            return g // groups_b, (g % groups_b) * GROUP
        return g * seqs_g, 0

    def load(c0, g):
        b0, t0 = origin(g)
        return qkv_ref[b0:b0 + seqs_g, t0:t0 + rows_b, c0:c0 + HEAD_DIM].astype(F32).reshape(GROUP, HEAD_DIM)

    def gate_rows(g):
        if n_gate_refs > 1:
            b0, t0 = origin(g)
            return tuple(gsc_refs[b0][q, :, t0:t0 + GROUP] for q in range(N_GATE_ROWS))
        return tuple(gsc_refs[0][q, :, g * GROUP:(g + 1) * GROUP] for q in range(N_GATE_ROWS))

    gates = [gate_rows(g) for g in range(n_groups)]

    probs = [(g, h) for g in range(n_groups) for h in range(N_HEADS)]
    st = []
    for g, h in probs:
        beta8, g8, eg8, kds8 = gates[g]
        d = {"q": load(h * HEAD_DIM, g), "k": load(QK_W + h * HEAD_DIM, g), "v": load(2 * QK_W + h * HEAD_DIM, g)}
        d["beta_row"] = beta8[h:h + 1, :]
        d["eg_row"] = eg8[N_HEADS + h:N_HEADS + h + 1, :]
        d["g_row"] = g8[N_HEADS + h:N_HEADS + h + 1, :]
        d["kb_row"] = kds8[N_HEADS + h:N_HEADS + h + 1, :] * d["beta_row"]
        d["g_col"] = jnp.broadcast_to(d["g_row"], (GROUP, GROUP)).T
        d["kt"] = d["k"].T
        st.append(d)
    for d in st:
        both = _dot(jnp.concatenate([d["k"], d["q"]], axis=0), d["kt"])
        d["kk"], d["qk"] = both[0:GROUP], both[GROUP:]
    for d in st:
        decay = jnp.exp(jnp.where(causal, d["g_col"] - d["g_row"], -jnp.inf))
        d["neg_m"] = jnp.where(strict, -(d.pop("kk") * decay), 0.0) * d["beta_row"]
        d["qkm"] = d.pop("qk") * decay * d["beta_row"]
        d["q_dec"] = d.pop("q") * jnp.exp(d["g_col"])
        d["kt_dec"] = d.pop("kt") * d["kb_row"]
    invs = _unit_lower_inverses([d.pop("neg_m") for d in st], chunk)
    for d, inv in zip(st, invs):
        d["uy"] = _dot(inv, d.pop("v"))
        d["wy"] = _dot(inv * d["eg_row"], d.pop("k"))

    outs = {}
    if chained:
        for j in range(groups_b):
            wave = [(i, g, h) for i, (g, h) in enumerate(probs) if g % groups_b == j]
            s_old = {i: snew_ref[origin(g)[0], h] for i, g, h in wave}
            ws = {i: _dot(jnp.concatenate([st[i]["wy"], st[i]["q_dec"]], axis=0), s_old[i]) for i, g, h in wave}
            ys = {i: st[i]["uy"] - ws[i][0:GROUP] for i, g, h in wave}
            for i, g, h in wave:
                outs[i] = ws[i][GROUP:] + _dot(st[i]["qkm"], ys[i])
            for i, g, h in wave:
                last = jnp.exp(st[i]["g_col"][GROUP - 1:GROUP, :])
                snew_ref[origin(g)[0], h] = s_old[i] * last + _dot(st[i]["kt_dec"], ys[i])
    else:
        for i, (g, h) in enumerate(probs):
            d, b0 = st[i], origin(g)[0]
            ws_w, ws_q = [], []
            for s_i in range(seqs_g):
                r = slice(s_i * rows_b, (s_i + 1) * rows_b)
                ws = _dot(jnp.concatenate([d["wy"][r], d["q_dec"][r]], axis=0), snew_ref[b0 + s_i, h])
                ws_w.append(ws[0:rows_b])
                ws_q.append(ws[rows_b:])
            d["y"] = d["uy"] - jnp.concatenate(ws_w, axis=0)
            outs[i] = jnp.concatenate(ws_q, axis=0) + _dot(d["qkm"], d["y"])
        for i, (g, h) in enumerate(probs):
            d, b0 = st[i], origin(g)[0]
            for s_i in range(seqs_g):
                last = jnp.exp(d["g_col"][(s_i + 1) * rows_b - 1:(s_i + 1) * rows_b, :])
                upd = _dot(jnp.where(col // rows_b == s_i, d["kt_dec"], 0.0), d["y"])
                snew_ref[b0 + s_i, h] = snew_ref[b0 + s_i, h] * last + upd

    def group_slab(ref, g, cols, row_off=0):
        b0, t0 = origin(g)
        return ref.at[b0:b0 + seqs_g, row_off + t0:row_off + t0 + rows_b, cols]

    for i, (g, h) in enumerate(probs):
        cols = slice(h * HEAD_DIM, (h + 1) * HEAD_DIM)
        o = outs[i]
        z = group_slab(z_ref, g, cols)[...].reshape(GROUP, HEAD_DIM)
        group_slab(oa_ref, g, cols)[...] = (o * _rms_scale(o) * wonorm_ref[...] * _silu(z)).reshape(
            seqs_g, rows_b, HEAD_DIM).astype(oa_ref.dtype)

    for g in range(n_groups):
        b0, t0 = origin(g)
        pos = pos0 + tile * tl + t0 + row % rows_b
        for gi, win in enumerate(POOL_WINDOWS):
            cols = slice(gi * POOL_GROUP, (gi + 1) * POOL_GROUP)
            slab = pext_ref[b0:b0 + seqs_g, POOL_PAD - POOL_LOOKBACK + t0:POOL_PAD + t0 + rows_b, cols]
            acc = slab.reshape(seqs_g * (POOL_LOOKBACK + rows_b), POOL_GROUP)
            shift = 1
            while shift < win:
                acc = acc + pltpu.roll(acc, shift, axis=0)
                shift *= 2
            acc = acc.reshape(seqs_g, POOL_LOOKBACK + rows_b, POOL_GROUP)[:, POOL_LOOKBACK:, :]
            cur = slab[:, POOL_LOOKBACK:, :]
            pooled = (acc / jnp.minimum(pos + 1, win).astype(F32).reshape(seqs_g, rows_b, POOL_GROUP) - cur)
            mixed = _dot(pooled.reshape(GROUP, POOL_GROUP), wmix_ref[gi]) * pscale_ref[:, cols]
            ob_ref[b0:b0 + seqs_g, t0:t0 + rows_b, cols] = mixed.reshape(seqs_g, rows_b, POOL_GROUP).astype(ob_ref.dtype)

    if n_tiles > 1:
        pext_ref[:, pool_hist, :] = poolnew_ref[...]


def _delta(qkv, z, gsc, p, pool_prev, s0, w_onorm, w_mix, pool_scale, *, bb, tl, chunk, pos0, act_dtype):
    bsz, l = qkv.shape[0], qkv.shape[1]
    n_tiles = l // tl
    state_s = pl.BlockSpec((bb, N_HEADS, HEAD_DIM, HEAD_DIM), lambda i, j: (i, 0, 0, 0))
    if n_tiles == 1:
        gate_specs = [_gate_rows_spec(bb, tl, 1)]
    else:
        gate_specs = [pl.BlockSpec((N_GATE_ROWS, SUBLANES, tl), lambda i, j, k=k: (0, 0, (i * bb + k) * n_tiles + j))
                      for k in range(bb)]
    kern = functools.partial(_delta_kernel, bb=bb, tl=tl, chunk=chunk, pos0=pos0, n_tiles=n_tiles,
                             n_gate_refs=len(gate_specs))
    sds = lambda *shape: jax.ShapeDtypeStruct(shape, F32)
    return pl.pallas_call(
        kern,
        grid=(bsz // bb, n_tiles),
        in_specs=gate_specs + [
                  _seq_spec(bb, tl, QKV_W), _seq_spec(bb, tl, QK_W),
                  _seq_spec(bb, tl, WIDTH_B), _state_spec(bb, POOL_HIST, WIDTH_B), state_s,
                  _const_spec((1, HEAD_DIM)), _const_spec((len(POOL_WINDOWS), POOL_GROUP, POOL_GROUP)),
                  _const_spec((1, WIDTH_B))],
        out_specs=[_seq_spec(bb, tl, QK_W), _seq_spec(bb, tl, WIDTH_B), _state_spec(bb, POOL_HIST, WIDTH_B), state_s],
        out_shape=[jax.ShapeDtypeStruct((bsz, l, QK_W), act_dtype), jax.ShapeDtypeStruct((bsz, l, WIDTH_B), act_dtype),
                   sds(bsz, POOL_HIST, WIDTH_B),
                   sds(bsz, N_HEADS, HEAD_DIM, HEAD_DIM)],
        scratch_shapes=[pltpu.VMEM((bb, POOL_PAD + tl, WIDTH_B), F32)],
        compiler_params=pltpu.CompilerParams(dimension_semantics=("arbitrary", "arbitrary"),
                                             vmem_limit_bytes=VMEM_LIMIT),
        name="delta",
    )(*([gsc] * len(gate_specs)), qkv, z, p, pool_prev, s0, w_onorm, w_mix, pool_scale)


def _mlp_rows(x_ref, oa_ref, ob_ref, gate_ref, wa_ref, wb_ref, wo_ref, gmlp_ref, wup_ref, wdown_ref, gfin_ref, y_ref):
    ma = _dot(oa_ref[...], wa_ref[...])
    mb = _dot(ob_ref[...], wb_ref[...])
    merged = _sigmoid(gate_ref[:, 0:D_MODEL]) * ma + _sigmoid(gate_ref[:, D_MODEL:]) * mb
    x1 = x_ref[...] + _dot(merged, wo_ref[...])
    h2 = (x1 * _rms_scale(x1) * gmlp_ref[...]).astype(BF16)
    acc = x1
    for c0 in range(0, D_FF, FF_BLOCK):
        up = jnp.dot(h2, wup_ref[:, c0:c0 + FF_BLOCK], preferred_element_type=F32)
        act = jnp.square(jnp.maximum(up, 0.0))
        acc = acc + _dot(act, wdown_ref[c0:c0 + FF_BLOCK, :])
    y_ref[...] = acc * _rms_scale(acc) * gfin_ref[...]


def _mlp_kernel(*refs, n_first):
    first, second, weights, (y1_ref, y2_ref) = refs[0:4], refs[4:8], refs[8:15], refs[15:17]
    step = pl.program_id(0)

    @pl.when(step < n_first)
    def _():
        _mlp_rows(*first, *weights, y1_ref)

    @pl.when(step >= n_first)
    def _():
        _mlp_rows(*second, *weights, y2_ref)


def _merge_mlp(first, second, prm, tm, tm_second):
    n_first, n_second = first[0].shape[0] // tm, second[0].shape[0] // tm_second
    row1 = lambda w: pl.BlockSpec((tm, w), lambda i: (jnp.minimum(i, n_first - 1), 0))
    row2 = lambda w, **kw: pl.BlockSpec((tm_second, w), lambda i: (jnp.maximum(i - n_first, 0), 0), **kw)
    widths = (D_MODEL, QK_W, WIDTH_B, 2 * D_MODEL)
    return pl.pallas_call(
        functools.partial(_mlp_kernel, n_first=n_first),
        grid=(n_first + n_second,),
        in_specs=[row1(w) for w in widths] + [row2(w, pipeline_mode=pl.Buffered(1)) for w in widths] + [
            _const_spec((QK_W, D_MODEL)), _const_spec((WIDTH_B, D_MODEL)), _const_spec((D_MODEL, D_MODEL)),
            _const_spec((1, D_MODEL)), _const_spec((D_MODEL, D_FF)), _const_spec((D_FF, D_MODEL)),
            _const_spec((1, D_MODEL))],
        out_specs=[row1(D_MODEL), row2(D_MODEL)],
        out_shape=[jax.ShapeDtypeStruct((first[0].shape[0], D_MODEL), F32),
                   jax.ShapeDtypeStruct((second[0].shape[0], D_MODEL), F32)],
        compiler_params=pltpu.CompilerParams(dimension_semantics=("arbitrary",),
                                             vmem_limit_bytes=VMEM_LIMIT),
        name="merge_mlp",
    )(*first, *second, prm["w_a_out"], prm["w_b_out"], prm["w_o"], prm["g_mlp"], prm["w_up"], prm["w_down"],
      prm["g_final"])


def _mix(x, conv_prev, pool_prev, s_prev, pos0, prm, *, front_blk, delta_blk, chunk, act_dtype):
    bsz, l, _ = x.shape
    t = bsz * l
    qkv, z, gsc, p, gates, conv_new = _front(
        x, prm["g_attn"], prm["wq"], prm["wba"], prm["wr"], prm["w_conv"], prm["a_log"], prm["dt_bias"],
        conv_prev, bb=front_blk[0], tl=front_blk[1], chunk=chunk, act_dtype=act_dtype)
    oa, ob, pool_new, s_new = _delta(qkv, z, gsc, p, pool_prev, s_prev, prm["w_onorm"], prm["w_mix"],
                                     prm["pool_scale"], bb=delta_blk[0], tl=delta_blk[1], chunk=chunk, pos0=pos0,
                                     act_dtype=act_dtype)
    rows = (x.reshape(t, D_MODEL), oa.reshape(t, QK_W), ob.reshape(t, WIDTH_B), gates.reshape(t, 2 * D_MODEL))
    return rows, (conv_new[None], pool_new[None], s_new[None])


def kernel(x_prompt, x_sample, state_conv, state_pool, state_ssm, w_in, w_conv, a_log, dt_bias, w_onorm,
           w_pool_mix, pool_scale, w_a_out, w_b_out, w_o, g_attn, g_mlp, w_up, w_down, g_final):
    assert w_in.shape[0] == 1, "single-layer decoder"
    wt = jnp.transpose(w_in[0])
    rest_off = QKVZ_W + GATE_SCALARS
    prm = {
        "wq": wt[0:QKVZ_W].astype(BF16),
        "wba": jnp.pad(wt[QKVZ_W:rest_off], ((0, 2 * SUBLANES - GATE_SCALARS), (0, 0))).astype(BF16),
        "wr": wt[rest_off:rest_off + REST_W].astype(BF16),
        "g_attn": g_attn[0][None, :], "g_mlp": g_mlp[0][None, :], "g_final": g_final[None, :],
        "w_conv": w_conv[0].astype(F32),
        "a_log": a_log[0], "dt_bias": dt_bias[0],
        "w_onorm": w_onorm[0][None, :].astype(F32),
        "w_mix": w_pool_mix[0].astype(BF16), "pool_scale": pool_scale[0][None, :].astype(F32),
        "w_a_out": w_a_out[0].astype(BF16), "w_b_out": w_b_out[0].astype(BF16), "w_o": w_o[0].astype(BF16),
        "w_up": w_up[0].astype(BF16), "w_down": w_down[0].astype(BF16),
    }
    bp = x_prompt.shape[0]
    rows_p, (conv_p, pool_p, ssm_p) = _mix(
        x_prompt, jnp.zeros((bp, CONV_W - 1, QKV_W), F32), jnp.zeros((bp, POOL_HIST, WIDTH_B), F32),
        jnp.zeros((bp, N_HEADS, HEAD_DIM, HEAD_DIM), F32), 0, prm,
        front_blk=(1, 512), delta_blk=(2, 256), chunk=GROUP, act_dtype=BF16)
    dec_len = x_sample.shape[1]
    rows_s, (conv_s, pool_s, ssm_s) = _mix(
        x_sample, state_conv[0].astype(F32), state_pool[0].astype(F32), state_ssm[0].astype(F32), PAST_LEN, prm,
        front_blk=(256 // dec_len, dec_len), delta_blk=(2 * GROUP // dec_len, dec_len), chunk=dec_len, act_dtype=F32)
    y_p, y_s = _merge_mlp(rows_p, rows_s, prm, tm=512, tm_second=256)
    y_p, y_s = y_p.reshape(x_prompt.shape), y_s.reshape(x_sample.shape)
    return (y_p, y_s, conv_p.astype(state_conv.dtype), pool_p.astype(state_pool.dtype),
            ssm_p.astype(state_ssm.dtype), conv_s.astype(state_conv.dtype), pool_s.astype(state_pool.dtype),
            ssm_s.astype(state_ssm.dtype))
```

```python
import functools
import math

import jax
import jax.numpy as jnp
from jax import lax
from jax.experimental import pallas as pl
from jax.experimental.pallas import tpu as pltpu

D_MODEL = 1024
N_HEADS = 4
HEAD_DIM = 128
QK_W = N_HEADS * HEAD_DIM
QKV_W = 3 * QK_W
CONV_W = 4
POOL_WINDOWS = (2, 4, 8, 16)
POOL_GROUP = 128
WIDTH_B = len(POOL_WINDOWS) * POOL_GROUP
POOL_HIST = 15
D_FF = 4 * D_MODEL
EPS = 1e-6
PAST_LEN = 16384
LANES = 128
SUBLANES = 8
MXU_COLS = 256
N_GATE_ROWS = 4

QKVZ_W = QKV_W + QK_W
GATE_SCALARS = 2 * N_HEADS
REST_W = WIDTH_B + 2 * D_MODEL

GROUP = 128
SERIES_BLOCK = 64
CONV_PAD = 8
POOL_LOOKBACK = 16
POOL_PAD = 24
FF_BLOCK = 1024

VMEM_LIMIT = 56 * 1024 * 1024

BF16 = jnp.bfloat16
F32 = jnp.float32


def _dot(a, b):
    return jnp.dot(a.astype(BF16), b.astype(BF16), preferred_element_type=F32)


def _sigmoid(x):
    return 1.0 / (1.0 + jnp.exp(-x))


def _silu(x):
    half = 0.5 * x
    return half * jnp.tanh(half) + half


def _rms_scale(x):
    return lax.rsqrt(jnp.mean(x * x, axis=-1, keepdims=True) + EPS)


def _const_spec(shape):
    zeros = (0,) * len(shape)
    return pl.BlockSpec(shape, lambda *_: zeros, pipeline_mode=pl.Buffered(1))


def _seq_spec(bb, tl, width):
    return pl.BlockSpec((bb, tl, width), lambda i, j: (i, j, 0))


def _state_spec(bb, n_rows, width):
    return pl.BlockSpec((bb, n_rows, width), lambda i, j: (i, 0, 0))


def _gate_rows_spec(bb, tl, n_tiles):
    return pl.BlockSpec((N_GATE_ROWS, SUBLANES, bb * tl), lambda i, j: (0, 0, i * n_tiles + j))


def _lane_prefix_sum(x, chunk):
    lane = lax.broadcasted_iota(jnp.int32, x.shape, 1)
    shift = 1
    while shift < chunk:
        x = x + jnp.where(lane % chunk >= shift, pltpu.roll(x, shift, axis=1), 0.0)
        shift *= 2
    return x


def _lane_suffix_sum(x, chunk):
    lane = lax.broadcasted_iota(jnp.int32, x.shape, 1)
    shift = 1
    while shift < chunk:
        x = x + jnp.where(lane % chunk + shift < chunk, pltpu.roll(x, x.shape[1] - shift, axis=1), 0.0)
        shift *= 2
    return x


def _front_kernel(x_ref, g_ref, wq_ref, wba_ref, wr_ref, wconv_ref, alog_ref, dtb_ref, keep_ref, convprev_ref,
                  qkv_ref, z_ref, gsc_ref, p_ref, gate_ref, convnew_ref,
                  ext_ref, *, bb, tl, n_tiles, chunk):
    tile = pl.program_id(1)
    rows = bb * tl
    hist = slice(CONV_PAD - (CONV_W - 1), CONV_PAD)

    @pl.when(tile == 0)
    def _():
        ext_ref[:, hist, :] = convprev_ref[...]
        ext_ref[:, 0:CONV_PAD - (CONV_W - 1), :] = jnp.zeros((bb, CONV_PAD - (CONV_W - 1), QKV_W), F32)

    x = x_ref[...].reshape(rows, D_MODEL)
    normed = (x * _rms_scale(x) * g_ref[...]).astype(BF16)
    contract_last = (((1,), (1,)), ((), ()))
    proj = lambda w_rows: lax.dot_general(normed, w_rows, contract_last, preferred_element_type=F32)
    ext_ref[:, CONV_PAD:CONV_PAD + tl, :] = proj(wq_ref[0:QKV_W, :]).reshape(bb, tl, QKV_W)
    convnew_ref[...] = ext_ref[:, CONV_PAD + tl - (CONV_W - 1):CONV_PAD + tl, :]

    x8 = lax.dot_general(wba_ref[...], normed, contract_last, preferred_element_type=F32)[0:SUBLANES, :]
    xs = x8 + dtb_ref[...]
    softplus = jnp.maximum(xs, 0.0) + jnp.log1p(jnp.exp(-jnp.abs(xs)))
    graw8 = -jnp.exp(alog_ref[...]) * softplus
    g8 = _lane_prefix_sum(graw8, chunk)
    gsc_ref[0] = _sigmoid(x8)
    gsc_ref[1] = g8
    gsc_ref[2] = jnp.exp(g8)
    gsc_ref[3] = jnp.exp(_lane_suffix_sum(graw8, chunk) - graw8)

    keep = keep_ref[...] != 0

    def conv_block(c0, dep):
        cols = slice(c0, c0 + LANES)
        xe = ext_ref[:, :, cols].reshape(bb * (CONV_PAD + tl), LANES)
        acc = wconv_ref[0:1, cols] * xe
        for j in range(1, CONV_W):
            acc = wconv_ref[j:j + 1, cols] * xe + pltpu.roll(acc, 1, axis=0)
        val = _silu(acc.reshape(bb, CONV_PAD + tl, LANES)[:, CONV_PAD:, :])
        if c0 < 2 * QK_W:
            scale = HEAD_DIM ** -0.5 if c0 < QK_W else 1.0
            val = val * (lax.rsqrt(jnp.sum(val * val, axis=-1, keepdims=True) + EPS) * scale)
        qkv_ref[:, :, cols] = jnp.where(keep, val, dep[:, 0:LANES].reshape(bb, tl, LANES)).astype(qkv_ref.dtype)

    def proj_block(out_ref, w_ref, w0, c0):
        val = proj(w_ref[w0 + c0:w0 + c0 + MXU_COLS, :])
        out_ref[:, :, c0:c0 + MXU_COLS] = val.reshape(bb, tl, MXU_COLS)
        return val

    mxu_work = ([functools.partial(proj_block, z_ref, wq_ref, QKV_W, c0) for c0 in range(0, QK_W, MXU_COLS)]
                + [functools.partial(proj_block, p_ref, wr_ref, 0, c0) for c0 in range(0, WIDTH_B, MXU_COLS)]
                + [functools.partial(proj_block, gate_ref, wr_ref, WIDTH_B, c0)
                   for c0 in range(0, 2 * D_MODEL, MXU_COLS)])
    for i, c0 in enumerate(range(0, QKV_W, LANES)):
        conv_block(c0, mxu_work[i]())

    if n_tiles > 1:
        ext_ref[:, hist, :] = convnew_ref[...]


def _front(x, g_attn, wq, wba, wr, w_conv, a_log, dt_bias, conv_prev, *, bb, tl, chunk, act_dtype):
    bsz, l, _ = x.shape
    n_tiles = l // tl
    kern = functools.partial(_front_kernel, bb=bb, tl=tl, n_tiles=n_tiles, chunk=chunk)
    sds = lambda *shape: jax.ShapeDtypeStruct(shape, F32)
    zeros4 = jnp.zeros((N_HEADS,), F32)
    rows8 = lambda v: jnp.broadcast_to(jnp.concatenate([zeros4, v.astype(F32)])[:, None], (SUBLANES, bb * tl))
    return pl.pallas_call(
        kern,
        grid=(bsz // bb, n_tiles),
        in_specs=[_seq_spec(bb, tl, D_MODEL), _const_spec((1, D_MODEL)), _const_spec((QKVZ_W, D_MODEL)),
                  _const_spec((2 * SUBLANES, D_MODEL)), _const_spec((REST_W, D_MODEL)), _const_spec((CONV_W, QKV_W)),
                  _const_spec((SUBLANES, bb * tl)), _const_spec((SUBLANES, bb * tl)), _const_spec((1, LANES)),
                  _state_spec(bb, CONV_W - 1, QKV_W)],
        out_specs=[_seq_spec(bb, tl, QKV_W), _seq_spec(bb, tl, QK_W), _gate_rows_spec(bb, tl, n_tiles),
                   _seq_spec(bb, tl, WIDTH_B), _seq_spec(bb, tl, 2 * D_MODEL), _state_spec(bb, CONV_W - 1, QKV_W)],
        out_shape=[jax.ShapeDtypeStruct((bsz, l, QKV_W), act_dtype), sds(bsz, l, QK_W),
                   sds(N_GATE_ROWS, SUBLANES, bsz * l),
                   sds(bsz, l, WIDTH_B), sds(bsz, l, 2 * D_MODEL), sds(bsz, CONV_W - 1, QKV_W)],
        scratch_shapes=[pltpu.VMEM((bb, CONV_PAD + tl, QKV_W), F32)],
        compiler_params=pltpu.CompilerParams(dimension_semantics=("arbitrary", "arbitrary"),
                                             vmem_limit_bytes=VMEM_LIMIT),
        name="front",
    )(x, g_attn, wq, wba, wr, w_conv, rows8(a_log), rows8(dt_bias), jnp.ones((1, LANES), jnp.int32), conv_prev)


def _unit_lower_inverses(neg_ms, chunk):
    nb = min(chunk, SERIES_BLOCK)
    n_blocks = GROUP // nb
    n_factors = int(math.log2(nb))
    assert chunk == nb or (chunk == 2 * nb and n_blocks == 2), "chunks are one or two series blocks"
    lane = lax.broadcasted_iota(jnp.int32, (nb, GROUP), 1)
    lane_block = lane // nb

    def packed(m):
        out = m[0:nb]
        for b in range(1, n_blocks):
            out = jnp.where(lane_block == b, m[b * nb:(b + 1) * nb], out)
        return out

    def block_diag(p):
        return jnp.concatenate([jnp.where(lane_block == b, p, 0.0) for b in range(n_blocks)], axis=0)

    nps = [packed(m) for m in neg_ms]
    eye_p = jnp.where(lane % nb == lax.broadcasted_iota(jnp.int32, (nb, GROUP), 0), 1.0, 0.0).astype(F32)
    invs = [eye_p + n for n in nps]
    if n_factors > 1:
        pws = [_dot(n, block_diag(n)) for n in nps]
        for _ in range(n_factors - 2):
            boths = [_dot(jnp.concatenate([inv, pw], axis=0), block_diag(pw)) for inv, pw in zip(invs, pws)]
            invs = [inv + both[0:nb] for inv, both in zip(invs, boths)]
            pws = [both[nb:] for both in boths]
        invs = [inv + _dot(inv, block_diag(pw)) for inv, pw in zip(invs, pws)]
    if chunk == nb:
        return [block_diag(inv) for inv in invs]
    zeros = jnp.zeros((nb, GROUP), F32)
    a_invs = [jnp.where(lane < nb, inv, 0.0) for inv in invs]
    neg_ls = [jnp.where(lane < nb, m[nb:], 0.0) for m in neg_ms]
    xs = [_dot(neg_l, jnp.concatenate([a_inv, zeros], axis=0)) for neg_l, a_inv in zip(neg_ls, a_invs)]
    ys = [_dot(inv, jnp.concatenate([zeros, x], axis=0)) for inv, x in zip(invs, xs)]
    return [jnp.concatenate([a_inv, y + jnp.where(lane >= nb, inv, 0.0)], axis=0)
            for a_inv, y, inv in zip(a_invs, ys, invs)]


def _delta_kernel(*refs, bb, tl, chunk, pos0, n_tiles, n_gate_refs):
    gsc_refs, refs = refs[:n_gate_refs], refs[n_gate_refs:]
    (qkv_ref, z_ref, p_ref, poolprev_ref, s0_ref, wonorm_ref, wmix_ref, pscale_ref,
     oa_ref, ob_ref, poolnew_ref, snew_ref, pext_ref) = refs
    tile = pl.program_id(1)
    rows_b = min(tl, GROUP)
    seqs_g = GROUP // rows_b
    groups_b = tl // rows_b
    n_groups = bb * tl // GROUP
    chained = chunk == GROUP
    assert chained or (chunk == tl and tl < GROUP), "chunk must be a whole group or a whole short sequence"
    pool_hist = slice(POOL_PAD - POOL_HIST, POOL_PAD)

    @pl.when(tile == 0)
    def _():
        snew_ref[...] = s0_ref[...]
        pext_ref[:, pool_hist, :] = poolprev_ref[...]
        pext_ref[:, POOL_PAD - POOL_LOOKBACK:POOL_PAD - POOL_HIST, :] = jnp.zeros((bb, 1, WIDTH_B), F32)

    pext_ref[:, POOL_PAD:POOL_PAD + tl, :] = p_ref[...]
    poolnew_ref[...] = pext_ref[:, POOL_PAD + tl - POOL_HIST:POOL_PAD + tl, :]

    row = lax.broadcasted_iota(jnp.int32, (GROUP, GROUP), 0)
    col = lax.broadcasted_iota(jnp.int32, (GROUP, GROUP), 1)
    causal = row >= col
    strict = row > col
    if not chained:
        same = (row // chunk) == (col // chunk)
        causal = causal & same
        strict = strict & same

    def origin(g):
        if tl >= GROUP:
            return g // groups_b, (g % groups_b) * GROUP
        return g * seqs_g, 0

    def load(c0, g):
        b0, t0 = origin(g)
        return qkv_ref[b0:b0 + seqs_g, t0:t0 + rows_b, c0:c0 + HEAD_DIM].astype(F32).reshape(GROUP, HEAD_DIM)

    def gate_rows(g):
        if n_gate_refs > 1:
            b0, t0 = origin(g)
            return tuple(gsc_refs[b0][q, :, t0:t0 + GROUP] for q in range(N_GATE_ROWS))
        return tuple(gsc_refs[0][q, :, g * GROUP:(g + 1) * GROUP] for q in range(N_GATE_ROWS))

    gates = [gate_rows(g) for g in range(n_groups)]

    probs = [(g, h) for g in range(n_groups) for h in range(N_HEADS)]
    st = []
    for g, h in probs:
        beta8, g8, eg8, kds8 = gates[g]
        d = {"q": load(h * HEAD_DIM, g), "k": load(QK_W + h * HEAD_DIM, g), "v": load(2 * QK_W + h * HEAD_DIM, g)}
        d["beta_row"] = beta8[h:h + 1, :]
        d["eg_row"] = eg8[N_HEADS + h:N_HEADS + h + 1, :]
        d["g_row"] = g8[N_HEADS + h:N_HEADS + h + 1, :]
        d["kb_row"] = kds8[N_HEADS + h:N_HEADS + h + 1, :] * d["beta_row"]
        d["g_col"] = jnp.broadcast_to(d["g_row"], (GROUP, GROUP)).T
        d["kt"] = d["k"].T
        st.append(d)
    for d in st:
        both = _dot(jnp.concatenate([d["k"], d["q"]], axis=0), d["kt"])
        d["kk"], d["qk"] = both[0:GROUP], both[GROUP:]
    for d in st:
        decay = jnp.exp(jnp.where(causal, d["g_col"] - d["g_row"], -jnp.inf))
        d["neg_m"] = jnp.where(strict, -(d.pop("kk") * decay), 0.0) * d["beta_row"]
        d["qkm"] = d.pop("qk") * decay * d["beta_row"]
        d["q_dec"] = d.pop("q") * jnp.exp(d["g_col"])
        d["kt_dec"] = d.pop("kt") * d["kb_row"]
    invs = _unit_lower_inverses([d.pop("neg_m") for d in st], chunk)
    for d, inv in zip(st, invs):
        d["uy"] = _dot(inv, d.pop("v"))
        d["wy"] = _dot(inv * d["eg_row"], d.pop("k"))

    outs = {}
    if chained:
        for j in range(groups_b):
            wave = [(i, g, h) for i, (g, h) in enumerate(probs) if g % groups_b == j]
            s_old = {i: snew_ref[origin(g)[0], h] for i, g, h in wave}
            ws = {i: _dot(jnp.concatenate([st[i]["wy"], st[i]["q_dec"]], axis=0), s_old[i]) for i, g, h in wave}
            ys = {i: st[i]["uy"] - ws[i][0:GROUP] for i, g, h in wave}
            for i, g, h in wave:
                outs[i] = ws[i][GROUP:] + _dot(st[i]["qkm"], ys[i])
            for i, g, h in wave:
                last = jnp.exp(st[i]["g_col"][GROUP - 1:GROUP, :])
                snew_ref[origin(g)[0], h] = s_old[i] * last + _dot(st[i]["kt_dec"], ys[i])
    else:
        for i, (g, h) in enumerate(probs):
            d, b0 = st[i], origin(g)[0]
            ws_w, ws_q = [], []
            for s_i in range(seqs_g):
                r = slice(s_i * rows_b, (s_i + 1) * rows_b)
                ws = _dot(jnp.concatenate([d["wy"][r], d["q_dec"][r]], axis=0), snew_ref[b0 + s_i, h])
                ws_w.append(ws[0:rows_b])
                ws_q.append(ws[rows_b:])
            d["y"] = d["uy"] - jnp.concatenate(ws_w, axis=0)
            outs[i] = jnp.concatenate(ws_q, axis=0) + _dot(d["qkm"], d["y"])
        for i, (g, h) in enumerate(probs):
            d, b0 = st[i], origin(g)[0]
            for s_i in range(seqs_g):
                last = jnp.exp(d["g_col"][(s_i + 1) * rows_b - 1:(s_i + 1) * rows_b, :])
                upd = _dot(jnp.where(col // rows_b == s_i, d["kt_dec"], 0.0), d["y"])
                snew_ref[b0 + s_i, h] = snew_ref[b0 + s_i, h] * last + upd

    def group_slab(ref, g, cols, row_off=0):
        b0, t0 = origin(g)
        return ref.at[b0:b0 + seqs_g, row_off + t0:row_off + t0 + rows_b, cols]

    for i, (g, h) in enumerate(probs):
        cols = slice(h * HEAD_DIM, (h + 1) * HEAD_DIM)
        o = outs[i]
        z = group_slab(z_ref, g, cols)[...].reshape(GROUP, HEAD_DIM)
        group_slab(oa_ref, g, cols)[...] = (o * _rms_scale(o) * wonorm_ref[...] * _silu(z)).reshape(
            seqs_g, rows_b, HEAD_DIM).astype(oa_ref.dtype)

    for g in range(n_groups):
        b0, t0 = origin(g)
        pos = pos0 + tile * tl + t0 + row % rows_b
        for gi, win in enumerate(POOL_WINDOWS):
            cols = slice(gi * POOL_GROUP, (gi + 1) * POOL_GROUP)
            slab = pext_ref[b0:b0 + seqs_g, POOL_PAD - POOL_LOOKBACK + t0:POOL_PAD + t0 + rows_b, cols]
            acc = slab.reshape(seqs_g * (POOL_LOOKBACK + rows_b), POOL_GROUP)
            shift = 1
            while shift < win:
                acc = acc + pltpu.roll(acc, shift, axis=0)
                shift *= 2
            acc = acc.reshape(seqs_g, POOL_LOOKBACK + rows_b, POOL_GROUP)[:, POOL_LOOKBACK:, :]
            cur = slab[:, POOL_LOOKBACK:, :]
            pooled = (acc / jnp.minimum(pos + 1, win).astype(F32).reshape(seqs_g, rows_b, POOL_GROUP) - cur)
            mixed = _dot(pooled.reshape(GROUP, POOL_GROUP), wmix_ref[gi]) * pscale_ref[:, cols]
            ob_ref[b0:b0 + seqs_g, t0:t0 + rows_b, cols] = mixed.reshape(seqs_g, rows_b, POOL_GROUP).astype(ob_ref.dtype)

    if n_tiles > 1:
        pext_ref[:, pool_hist, :] = poolnew_ref[...]


def _delta(qkv, z, gsc, p, pool_prev, s0, w_onorm, w_mix, pool_scale, *, bb, tl, chunk, pos0, act_dtype):
    bsz, l = qkv.shape[0], qkv.shape[1]
    n_tiles = l // tl
    state_s = pl.BlockSpec((bb, N_HEADS, HEAD_DIM, HEAD_DIM), lambda i, j: (i, 0, 0, 0))
    if n_tiles == 1:
        gate_specs = [_gate_rows_spec(bb, tl, 1)]
    else:
        gate_specs = [pl.BlockSpec((N_GATE_ROWS, SUBLANES, tl), lambda i, j, k=k: (0, 0, (i * bb + k) * n_tiles + j))
                      for k in range(bb)]
    kern = functools.partial(_delta_kernel, bb=bb, tl=tl, chunk=chunk, pos0=pos0, n_tiles=n_tiles,
                             n_gate_refs=len(gate_specs))
    sds = lambda *shape: jax.ShapeDtypeStruct(shape, F32)
    return pl.pallas_call(
        kern,
        grid=(bsz // bb, n_tiles),
        in_specs=gate_specs + [
                  _seq_spec(bb, tl, QKV_W), _seq_spec(bb, tl, QK_W),
                  _seq_spec(bb, tl, WIDTH_B), _state_spec(bb, POOL_HIST, WIDTH_B), state_s,
                  _const_spec((1, HEAD_DIM)), _const_spec((len(POOL_WINDOWS), POOL_GROUP, POOL_GROUP)),
                  _const_spec((1, WIDTH_B))],
        out_specs=[_seq_spec(bb, tl, QK_W), _seq_spec(bb, tl, WIDTH_B), _state_spec(bb, POOL_HIST, WIDTH_B), state_s],
        out_shape=[jax.ShapeDtypeStruct((bsz, l, QK_W), act_dtype), jax.ShapeDtypeStruct((bsz, l, WIDTH_B), act_dtype),
                   sds(bsz, POOL_HIST, WIDTH_B),
                   sds(bsz, N_HEADS, HEAD_DIM, HEAD_DIM)],
        scratch_shapes=[pltpu.VMEM((bb, POOL_PAD + tl, WIDTH_B), F32)],
        compiler_params=pltpu.CompilerParams(dimension_semantics=("arbitrary", "arbitrary"),
                                             vmem_limit_bytes=VMEM_LIMIT),
        name="delta",
    )(*([gsc] * len(gate_specs)), qkv, z, p, pool_prev, s0, w_onorm, w_mix, pool_scale)


def _mlp_kernel(x_ref, oa_ref, ob_ref, gate_ref, wa_ref, wb_ref, wo_ref, gmlp_ref, wup_ref, wdown_ref, gfin_ref,
                y_ref):
    ma = _dot(oa_ref[...], wa_ref[...])
    mb = _dot(ob_ref[...], wb_ref[...])
    merged = _sigmoid(gate_ref[:, 0:D_MODEL]) * ma + _sigmoid(gate_ref[:, D_MODEL:]) * mb
    x1 = x_ref[...] + _dot(merged, wo_ref[...])
    h2 = (x1 * _rms_scale(x1) * gmlp_ref[...]).astype(BF16)
    acc = x1
    for c0 in range(0, D_FF, FF_BLOCK):
        up = jnp.dot(h2, wup_ref[:, c0:c0 + FF_BLOCK], preferred_element_type=F32)
        act = jnp.square(jnp.maximum(up, 0.0))
        acc = acc + _dot(act, wdown_ref[c0:c0 + FF_BLOCK, :])
    y_ref[...] = acc * _rms_scale(acc) * gfin_ref[...]


def _merge_mlp(x2d, oa, ob, gates, prm, tm):
    t = x2d.shape[0]
    row = lambda w: pl.BlockSpec((tm, w), lambda i: (i, 0))
    return pl.pallas_call(
        _mlp_kernel,
        grid=(t // tm,),
        in_specs=[row(D_MODEL), row(QK_W), row(WIDTH_B), row(2 * D_MODEL),
                  _const_spec((QK_W, D_MODEL)), _const_spec((WIDTH_B, D_MODEL)), _const_spec((D_MODEL, D_MODEL)),
                  _const_spec((1, D_MODEL)), _const_spec((D_MODEL, D_FF)), _const_spec((D_FF, D_MODEL)),
                  _const_spec((1, D_MODEL))],
        out_specs=row(D_MODEL),
        out_shape=jax.ShapeDtypeStruct((t, D_MODEL), F32),
        compiler_params=pltpu.CompilerParams(dimension_semantics=("arbitrary",),
                                             vmem_limit_bytes=VMEM_LIMIT),
        name="merge_mlp",
    )(x2d, oa, ob, gates, prm["w_a_out"], prm["w_b_out"], prm["w_o"], prm["g_mlp"], prm["w_up"], prm["w_down"],
      prm["g_final"])


def _trunk(x, conv_prev, pool_prev, s_prev, pos0, prm, *, front_blk, delta_blk, tm, chunk, act_dtype):
    bsz, l, _ = x.shape
    t = bsz * l
    qkv, z, gsc, p, gates, conv_new = _front(
        x, prm["g_attn"], prm["wq"], prm["wba"], prm["wr"], prm["w_conv"], prm["a_log"], prm["dt_bias"],
        conv_prev, bb=front_blk[0], tl=front_blk[1], chunk=chunk, act_dtype=act_dtype)
    oa, ob, pool_new, s_new = _delta(qkv, z, gsc, p, pool_prev, s_prev, prm["w_onorm"], prm["w_mix"],
                                     prm["pool_scale"], bb=delta_blk[0], tl=delta_blk[1], chunk=chunk, pos0=pos0,
                                     act_dtype=act_dtype)
    y = _merge_mlp(x.reshape(t, D_MODEL), oa.reshape(t, QK_W), ob.reshape(t, WIDTH_B),
                   gates.reshape(t, 2 * D_MODEL), prm, tm)
    return y.reshape(bsz, l, D_MODEL), conv_new[None], pool_new[None], s_new[None]


def kernel(x_prompt, x_sample, state_conv, state_pool, state_ssm, w_in, w_conv, a_log, dt_bias, w_onorm,
           w_pool_mix, pool_scale, w_a_out, w_b_out, w_o, g_attn, g_mlp, w_up, w_down, g_final):
    assert w_in.shape[0] == 1, "single-layer decoder"
    wt = jnp.transpose(w_in[0])
    rest_off = QKVZ_W + GATE_SCALARS
    prm = {
        "wq": wt[0:QKVZ_W].astype(BF16),
        "wba": jnp.pad(wt[QKVZ_W:rest_off], ((0, 2 * SUBLANES - GATE_SCALARS), (0, 0))).astype(BF16),
        "wr": wt[rest_off:rest_off + REST_W].astype(BF16),
        "g_attn": g_attn[0][None, :], "g_mlp": g_mlp[0][None, :], "g_final": g_final[None, :],
        "w_conv": w_conv[0].astype(F32),
        "a_log": a_log[0], "dt_bias": dt_bias[0],
        "w_onorm": w_onorm[0][None, :].astype(F32),
        "w_mix": w_pool_mix[0].astype(BF16), "pool_scale": pool_scale[0][None, :].astype(F32),
        "w_a_out": w_a_out[0].astype(BF16), "w_b_out": w_b_out[0].astype(BF16), "w_o": w_o[0].astype(BF16),
        "w_up": w_up[0].astype(BF16), "w_down": w_down[0].astype(BF16),
    }
    bp = x_prompt.shape[0]
    y_p, conv_p, pool_p, ssm_p = _trunk(
        x_prompt, jnp.zeros((bp, CONV_W - 1, QKV_W), F32), jnp.zeros((bp, POOL_HIST, WIDTH_B), F32),
        jnp.zeros((bp, N_HEADS, HEAD_DIM, HEAD_DIM), F32), 0, prm,
        front_blk=(1, 512), delta_blk=(4, 256), tm=512, chunk=GROUP, act_dtype=BF16)
    dec_len = x_sample.shape[1]
    y_s, conv_s, pool_s, ssm_s = _trunk(
        x_sample, state_conv[0].astype(F32), state_pool[0].astype(F32), state_ssm[0].astype(F32), PAST_LEN, prm,
        front_blk=(256 // dec_len, dec_len), delta_blk=(2 * GROUP // dec_len, dec_len), tm=512,
        chunk=dec_len, act_dtype=F32)
    return (y_p, y_s, conv_p.astype(state_conv.dtype), pool_p.astype(state_pool.dtype),
            ssm_p.astype(state_ssm.dtype), conv_s.astype(state_conv.dtype), pool_s.astype(state_pool.dtype),
            ssm_s.astype(state_ssm.dtype))
```

```python
import functools
import math

import jax
import jax.numpy as jnp
from jax import lax
from jax.experimental import pallas as pl
from jax.experimental.pallas import tpu as pltpu

D_MODEL = 1024
N_HEADS = 4
HEAD_DIM = 128
QK_W = N_HEADS * HEAD_DIM
QKV_W = 3 * QK_W
CONV_W = 4
POOL_WINDOWS = (2, 4, 8, 16)
POOL_GROUP = 128
WIDTH_B = len(POOL_WINDOWS) * POOL_GROUP
POOL_HIST = 15
D_FF = 4 * D_MODEL
EPS = 1e-6
PAST_LEN = 16384
LANES = 128
SUBLANES = 8
MXU_COLS = 256
N_GATE_ROWS = 4

QKVZ_W = QKV_W + QK_W
GATE_SCALARS = 2 * N_HEADS
REST_W = WIDTH_B + 2 * D_MODEL

GROUP = 128
SERIES_BLOCK = 64
CONV_PAD = 8
POOL_LOOKBACK = 16
POOL_PAD = 24
FF_BLOCK = 1024

VMEM_LIMIT = 56 * 1024 * 1024

BF16 = jnp.bfloat16
F32 = jnp.float32


def _dot(a, b):
    return jnp.dot(a.astype(BF16), b.astype(BF16), preferred_element_type=F32)


def _sigmoid(x):
    return 1.0 / (1.0 + jnp.exp(-x))


def _silu(x):
    half = 0.5 * x
    return half * jnp.tanh(half) + half


def _rms_scale(x):
    return lax.rsqrt(jnp.mean(x * x, axis=-1, keepdims=True) + EPS)


def _const_spec(shape):
    zeros = (0,) * len(shape)
    return pl.BlockSpec(shape, lambda *_: zeros, pipeline_mode=pl.Buffered(1))


def _seq_spec(bb, tl, width):
    return pl.BlockSpec((bb, tl, width), lambda i, j: (i, j, 0))


def _state_spec(bb, n_rows, width, rows_first):
    if rows_first:
        return pl.BlockSpec((n_rows, bb, width), lambda i, j: (0, i, 0))
    return pl.BlockSpec((bb, n_rows, width), lambda i, j: (i, 0, 0))


def _state_shape(bsz, n_rows, width, rows_first):
    return jax.ShapeDtypeStruct((n_rows, bsz, width) if rows_first else (bsz, n_rows, width), F32)


def _history_to_scratch(scratch_ref, first_row, state_ref, rows_first):
    if rows_first:
        for j in range(state_ref.shape[0]):
            scratch_ref[:, first_row + j, :] = state_ref[j]
    else:
        scratch_ref[:, first_row:first_row + state_ref.shape[1], :] = state_ref[...]


def _history_from_scratch(state_ref, scratch_ref, first_row, rows_first):
    if rows_first:
        for j in range(state_ref.shape[0]):
            state_ref[j] = scratch_ref[:, first_row + j, :]
    else:
        state_ref[...] = scratch_ref[:, first_row:first_row + state_ref.shape[1], :]


def _gate_rows_spec(bb, tl, n_tiles):
    return pl.BlockSpec((N_GATE_ROWS, SUBLANES, bb * tl), lambda i, j: (0, 0, i * n_tiles + j))


def _lane_prefix_sum(x, chunk):
    lane = lax.broadcasted_iota(jnp.int32, x.shape, 1)
    shift = 1
    while shift < chunk:
        x = x + jnp.where(lane % chunk >= shift, pltpu.roll(x, shift, axis=1), 0.0)
        shift *= 2
    return x


def _lane_suffix_sum(x, chunk):
    lane = lax.broadcasted_iota(jnp.int32, x.shape, 1)
    shift = 1
    while shift < chunk:
        x = x + jnp.where(lane % chunk + shift < chunk, pltpu.roll(x, x.shape[1] - shift, axis=1), 0.0)
        shift *= 2
    return x


def _front_kernel(x_ref, g_ref, wq_ref, wba_ref, wr_ref, wconv_ref, alog_ref, dtb_ref, keep_ref, convprev_ref,
                  qkv_ref, z_ref, gsc_ref, p_ref, gate_ref, convnew_ref,
                  ext_ref, *, bb, tl, n_tiles, chunk, rows_first):
    tile = pl.program_id(1)
    rows = bb * tl
    hist = slice(CONV_PAD - (CONV_W - 1), CONV_PAD)

    @pl.when(tile == 0)
    def _():
        _history_to_scratch(ext_ref, hist.start, convprev_ref, rows_first)
        ext_ref[:, 0:CONV_PAD - (CONV_W - 1), :] = jnp.zeros((bb, CONV_PAD - (CONV_W - 1), QKV_W), F32)

    x = x_ref[...].reshape(rows, D_MODEL)
    normed = (x * _rms_scale(x) * g_ref[...]).astype(BF16)
    contract_last = (((1,), (1,)), ((), ()))
    proj = lambda w_rows: lax.dot_general(normed, w_rows, contract_last, preferred_element_type=F32)
    ext_ref[:, CONV_PAD:CONV_PAD + tl, :] = proj(wq_ref[0:QKV_W, :]).reshape(bb, tl, QKV_W)
    _history_from_scratch(convnew_ref, ext_ref, CONV_PAD + tl - (CONV_W - 1), rows_first)

    x8 = lax.dot_general(wba_ref[...], normed, contract_last, preferred_element_type=F32)[0:SUBLANES, :]
    xs = x8 + dtb_ref[...]
    softplus = jnp.maximum(xs, 0.0) + jnp.log1p(jnp.exp(-jnp.abs(xs)))
    graw8 = -jnp.exp(alog_ref[...]) * softplus
    g8 = _lane_prefix_sum(graw8, chunk)
    gsc_ref[0] = _sigmoid(x8)
    gsc_ref[1] = g8
    gsc_ref[2] = jnp.exp(g8)
    gsc_ref[3] = jnp.exp(_lane_suffix_sum(graw8, chunk) - graw8)

    keep = keep_ref[...] != 0

    def conv_block(c0, dep):
        cols = slice(c0, c0 + LANES)
        xe = ext_ref[:, :, cols].reshape(bb * (CONV_PAD + tl), LANES)
        acc = wconv_ref[0:1, cols] * xe
        for j in range(1, CONV_W):
            acc = wconv_ref[j:j + 1, cols] * xe + pltpu.roll(acc, 1, axis=0)
        val = _silu(acc.reshape(bb, CONV_PAD + tl, LANES)[:, CONV_PAD:, :])
        if c0 < 2 * QK_W:
            scale = HEAD_DIM ** -0.5 if c0 < QK_W else 1.0
            val = val * (lax.rsqrt(jnp.sum(val * val, axis=-1, keepdims=True) + EPS) * scale)
        qkv_ref[:, :, cols] = jnp.where(keep, val, dep[:, 0:LANES].reshape(bb, tl, LANES)).astype(qkv_ref.dtype)

    def proj_block(out_ref, w_ref, w0, c0):
        val = proj(w_ref[w0 + c0:w0 + c0 + MXU_COLS, :])
        out_ref[:, :, c0:c0 + MXU_COLS] = val.reshape(bb, tl, MXU_COLS)
        return val

    mxu_work = ([functools.partial(proj_block, z_ref, wq_ref, QKV_W, c0) for c0 in range(0, QK_W, MXU_COLS)]
                + [functools.partial(proj_block, p_ref, wr_ref, 0, c0) for c0 in range(0, WIDTH_B, MXU_COLS)]
                + [functools.partial(proj_block, gate_ref, wr_ref, WIDTH_B, c0)
                   for c0 in range(0, 2 * D_MODEL, MXU_COLS)])
    for i, c0 in enumerate(range(0, QKV_W, LANES)):
        conv_block(c0, mxu_work[i]())

    if n_tiles > 1:
        _history_to_scratch(ext_ref, hist.start, convnew_ref, rows_first)


def _front(x, g_attn, wq, wba, wr, w_conv, a_log, dt_bias, conv_prev, *, bb, tl, chunk, act_dtype, rows_first):
    bsz, l, _ = x.shape
    n_tiles = l // tl
    kern = functools.partial(_front_kernel, bb=bb, tl=tl, n_tiles=n_tiles, chunk=chunk, rows_first=rows_first)
    sds = lambda *shape: jax.ShapeDtypeStruct(shape, F32)
    zeros4 = jnp.zeros((N_HEADS,), F32)
    rows8 = lambda v: jnp.broadcast_to(jnp.concatenate([zeros4, v.astype(F32)])[:, None], (SUBLANES, bb * tl))
    return pl.pallas_call(
        kern,
        grid=(bsz // bb, n_tiles),
        in_specs=[_seq_spec(bb, tl, D_MODEL), _const_spec((1, D_MODEL)), _const_spec((QKVZ_W, D_MODEL)),
                  _const_spec((2 * SUBLANES, D_MODEL)), _const_spec((REST_W, D_MODEL)), _const_spec((CONV_W, QKV_W)),
                  _const_spec((SUBLANES, bb * tl)), _const_spec((SUBLANES, bb * tl)), _const_spec((1, LANES)),
                  _state_spec(bb, CONV_W - 1, QKV_W, rows_first)],
        out_specs=[_seq_spec(bb, tl, QKV_W), _seq_spec(bb, tl, QK_W), _gate_rows_spec(bb, tl, n_tiles),
                   _seq_spec(bb, tl, WIDTH_B), _seq_spec(bb, tl, 2 * D_MODEL),
                   _state_spec(bb, CONV_W - 1, QKV_W, rows_first)],
        out_shape=[jax.ShapeDtypeStruct((bsz, l, QKV_W), act_dtype), sds(bsz, l, QK_W),
                   sds(N_GATE_ROWS, SUBLANES, bsz * l),
                   sds(bsz, l, WIDTH_B), sds(bsz, l, 2 * D_MODEL),
                   _state_shape(bsz, CONV_W - 1, QKV_W, rows_first)],
        scratch_shapes=[pltpu.VMEM((bb, CONV_PAD + tl, QKV_W), F32)],
        compiler_params=pltpu.CompilerParams(dimension_semantics=("arbitrary", "arbitrary"),
                                             vmem_limit_bytes=VMEM_LIMIT),
        name="front",
    )(x, g_attn, wq, wba, wr, w_conv, rows8(a_log), rows8(dt_bias), jnp.ones((1, LANES), jnp.int32), conv_prev)


def _unit_lower_inverses(neg_ms, chunk):
    nb = min(chunk, SERIES_BLOCK)
    n_blocks = GROUP // nb
    n_factors = int(math.log2(nb))
    assert chunk == nb or (chunk == 2 * nb and n_blocks == 2), "chunks are one or two series blocks"
    lane = lax.broadcasted_iota(jnp.int32, (nb, GROUP), 1)
    lane_block = lane // nb

    def packed(m):
        out = m[0:nb]
        for b in range(1, n_blocks):
            out = jnp.where(lane_block == b, m[b * nb:(b + 1) * nb], out)
        return out

    def block_diag(p):
        return jnp.concatenate([jnp.where(lane_block == b, p, 0.0) for b in range(n_blocks)], axis=0)

    nps = [packed(m) for m in neg_ms]
    eye_p = jnp.where(lane % nb == lax.broadcasted_iota(jnp.int32, (nb, GROUP), 0), 1.0, 0.0).astype(F32)
    invs = [eye_p + n for n in nps]
    if n_factors > 1:
        pws = [_dot(n, block_diag(n)) for n in nps]
        for _ in range(n_factors - 2):
            boths = [_dot(jnp.concatenate([inv, pw], axis=0), block_diag(pw)) for inv, pw in zip(invs, pws)]
            invs = [inv + both[0:nb] for inv, both in zip(invs, boths)]
            pws = [both[nb:] for both in boths]
        invs = [inv + _dot(inv, block_diag(pw)) for inv, pw in zip(invs, pws)]
    if chunk == nb:
        return [block_diag(inv) for inv in invs]
    zeros = jnp.zeros((nb, GROUP), F32)
    a_invs = [jnp.where(lane < nb, inv, 0.0) for inv in invs]
    neg_ls = [jnp.where(lane < nb, m[nb:], 0.0) for m in neg_ms]
    xs = [_dot(neg_l, jnp.concatenate([a_inv, zeros], axis=0)) for neg_l, a_inv in zip(neg_ls, a_invs)]
    ys = [_dot(inv, jnp.concatenate([zeros, x], axis=0)) for inv, x in zip(invs, xs)]
    return [jnp.concatenate([a_inv, y + jnp.where(lane >= nb, inv, 0.0)], axis=0)
            for a_inv, y, inv in zip(a_invs, ys, invs)]


def _delta_kernel(*refs, bb, tl, chunk, pos0, n_tiles, n_gate_refs, rows_first):
    gsc_refs, refs = refs[:n_gate_refs], refs[n_gate_refs:]
    (qkv_ref, z_ref, p_ref, poolprev_ref, s0_ref, wonorm_ref, wmix_ref, pscale_ref,
     oa_ref, ob_ref, poolnew_ref, snew_ref, pext_ref) = refs
    tile = pl.program_id(1)
    rows_b = min(tl, GROUP)
    seqs_g = GROUP // rows_b
    groups_b = tl // rows_b
    n_groups = bb * tl // GROUP
    chained = chunk == GROUP
    assert chained or (chunk == tl and tl < GROUP), "chunk must be a whole group or a whole short sequence"
    pool_hist = slice(POOL_PAD - POOL_HIST, POOL_PAD)

    @pl.when(tile == 0)
    def _():
        snew_ref[...] = s0_ref[...]
        _history_to_scratch(pext_ref, pool_hist.start, poolprev_ref, rows_first)
        pext_ref[:, POOL_PAD - POOL_LOOKBACK:POOL_PAD - POOL_HIST, :] = jnp.zeros((bb, 1, WIDTH_B), F32)

    pext_ref[:, POOL_PAD:POOL_PAD + tl, :] = p_ref[...]
    _history_from_scratch(poolnew_ref, pext_ref, POOL_PAD + tl - POOL_HIST, rows_first)

    row = lax.broadcasted_iota(jnp.int32, (GROUP, GROUP), 0)
    col = lax.broadcasted_iota(jnp.int32, (GROUP, GROUP), 1)
    causal = row >= col
    strict = row > col
    if not chained:
        same = (row // chunk) == (col // chunk)
        causal = causal & same
        strict = strict & same

    def origin(g):
        if tl >= GROUP:
            return g // groups_b, (g % groups_b) * GROUP
        return g * seqs_g, 0

    def load(c0, g):
        b0, t0 = origin(g)
        return qkv_ref[b0:b0 + seqs_g, t0:t0 + rows_b, c0:c0 + HEAD_DIM].astype(F32).reshape(GROUP, HEAD_DIM)

    def gate_rows(g):
        if n_gate_refs > 1:
            b0, t0 = origin(g)
            return tuple(gsc_refs[b0][q, :, t0:t0 + GROUP] for q in range(N_GATE_ROWS))
        return tuple(gsc_refs[0][q, :, g * GROUP:(g + 1) * GROUP] for q in range(N_GATE_ROWS))

    gates = [gate_rows(g) for g in range(n_groups)]

    probs = [(g, h) for g in range(n_groups) for h in range(N_HEADS)]
    st = []
    for g, h in probs:
        beta8, g8, eg8, kds8 = gates[g]
        d = {"q": load(h * HEAD_DIM, g), "k": load(QK_W + h * HEAD_DIM, g), "v": load(2 * QK_W + h * HEAD_DIM, g)}
        d["beta_row"] = beta8[h:h + 1, :]
        d["eg_row"] = eg8[N_HEADS + h:N_HEADS + h + 1, :]
        d["g_row"] = g8[N_HEADS + h:N_HEADS + h + 1, :]
        d["kb_row"] = kds8[N_HEADS + h:N_HEADS + h + 1, :] * d["beta_row"]
        d["g_col"] = jnp.broadcast_to(d["g_row"], (GROUP, GROUP)).T
        d["kt"] = d["k"].T
        st.append(d)
    for d in st:
        both = _dot(jnp.concatenate([d["k"], d["q"]], axis=0), d["kt"])
        d["kk"], d["qk"] = both[0:GROUP], both[GROUP:]
    for d in st:
        decay = jnp.exp(jnp.where(causal, d["g_col"] - d["g_row"], -jnp.inf))
        d["neg_m"] = jnp.where(strict, -(d.pop("kk") * decay), 0.0) * d["beta_row"]
        d["qkm"] = d.pop("qk") * decay * d["beta_row"]
        d["q_dec"] = d.pop("q") * jnp.exp(d["g_col"])
        d["kt_dec"] = d.pop("kt") * d["kb_row"]
    invs = _unit_lower_inverses([d.pop("neg_m") for d in st], chunk)
    for d, inv in zip(st, invs):
        d["uy"] = _dot(inv, d.pop("v"))
        d["wy"] = _dot(inv * d["eg_row"], d.pop("k"))

    outs = {}
    if chained:
        for j in range(groups_b):
            wave = [(i, g, h) for i, (g, h) in enumerate(probs) if g % groups_b == j]
            s_old = {i: snew_ref[origin(g)[0], h] for i, g, h in wave}
            ws = {i: _dot(jnp.concatenate([st[i]["wy"], st[i]["q_dec"]], axis=0), s_old[i]) for i, g, h in wave}
            ys = {i: st[i]["uy"] - ws[i][0:GROUP] for i, g, h in wave}
            for i, g, h in wave:
                outs[i] = ws[i][GROUP:] + _dot(st[i]["qkm"], ys[i])
            for i, g, h in wave:
                last = jnp.exp(st[i]["g_col"][GROUP - 1:GROUP, :])
                snew_ref[origin(g)[0], h] = s_old[i] * last + _dot(st[i]["kt_dec"], ys[i])
    else:
        for i, (g, h) in enumerate(probs):
            d, b0 = st[i], origin(g)[0]
            ws_w, ws_q = [], []
            for s_i in range(seqs_g):
                r = slice(s_i * rows_b, (s_i + 1) * rows_b)
                ws = _dot(jnp.concatenate([d["wy"][r], d["q_dec"][r]], axis=0), snew_ref[b0 + s_i, h])
                ws_w.append(ws[0:rows_b])
                ws_q.append(ws[rows_b:])
            d["y"] = d["uy"] - jnp.concatenate(ws_w, axis=0)
            outs[i] = jnp.concatenate(ws_q, axis=0) + _dot(d["qkm"], d["y"])
        for i, (g, h) in enumerate(probs):
            d, b0 = st[i], origin(g)[0]
            for s_i in range(seqs_g):
                last = jnp.exp(d["g_col"][(s_i + 1) * rows_b - 1:(s_i + 1) * rows_b, :])
                upd = _dot(jnp.where(col // rows_b == s_i, d["kt_dec"], 0.0), d["y"])
                snew_ref[b0 + s_i, h] = snew_ref[b0 + s_i, h] * last + upd

    def group_slab(ref, g, cols, row_off=0):
        b0, t0 = origin(g)
        return ref.at[b0:b0 + seqs_g, row_off + t0:row_off + t0 + rows_b, cols]

    for i, (g, h) in enumerate(probs):
        cols = slice(h * HEAD_DIM, (h + 1) * HEAD_DIM)
        o = outs[i]
        z = group_slab(z_ref, g, cols)[...].reshape(GROUP, HEAD_DIM)
        group_slab(oa_ref, g, cols)[...] = (o * _rms_scale(o) * wonorm_ref[...] * _silu(z)).reshape(
            seqs_g, rows_b, HEAD_DIM).astype(oa_ref.dtype)

    for g in range(n_groups):
        b0, t0 = origin(g)
        pos = pos0 + tile * tl + t0 + row % rows_b
        for gi, win in enumerate(POOL_WINDOWS):
            cols = slice(gi * POOL_GROUP, (gi + 1) * POOL_GROUP)
            slab = pext_ref[b0:b0 + seqs_g, POOL_PAD - POOL_LOOKBACK + t0:POOL_PAD + t0 + rows_b, cols]
            acc = slab.reshape(seqs_g * (POOL_LOOKBACK + rows_b), POOL_GROUP)
            shift = 1
            while shift < win:
                acc = acc + pltpu.roll(acc, shift, axis=0)
                shift *= 2
            acc = acc.reshape(seqs_g, POOL_LOOKBACK + rows_b, POOL_GROUP)[:, POOL_LOOKBACK:, :]
            cur = slab[:, POOL_LOOKBACK:, :]
            pooled = (acc / jnp.minimum(pos + 1, win).astype(F32).reshape(seqs_g, rows_b, POOL_GROUP) - cur)
            mixed = _dot(pooled.reshape(GROUP, POOL_GROUP), wmix_ref[gi]) * pscale_ref[:, cols]
            ob_ref[b0:b0 + seqs_g, t0:t0 + rows_b, cols] = mixed.reshape(seqs_g, rows_b, POOL_GROUP).astype(ob_ref.dtype)

    if n_tiles > 1:
        _history_to_scratch(pext_ref, pool_hist.start, poolnew_ref, rows_first)


def _delta(qkv, z, gsc, p, pool_prev, s0, w_onorm, w_mix, pool_scale, *, bb, tl, chunk, pos0, act_dtype,
           rows_first):
    bsz, l = qkv.shape[0], qkv.shape[1]
    n_tiles = l // tl
    state_s = pl.BlockSpec((bb, N_HEADS, HEAD_DIM, HEAD_DIM), lambda i, j: (i, 0, 0, 0))
    if n_tiles == 1:
        gate_specs = [_gate_rows_spec(bb, tl, 1)]
    else:
        gate_specs = [pl.BlockSpec((N_GATE_ROWS, SUBLANES, tl), lambda i, j, k=k: (0, 0, (i * bb + k) * n_tiles + j))
                      for k in range(bb)]
    kern = functools.partial(_delta_kernel, bb=bb, tl=tl, chunk=chunk, pos0=pos0, n_tiles=n_tiles,
                             n_gate_refs=len(gate_specs), rows_first=rows_first)
    sds = lambda *shape: jax.ShapeDtypeStruct(shape, F32)
    return pl.pallas_call(
        kern,
        grid=(bsz // bb, n_tiles),
        in_specs=gate_specs + [
                  _seq_spec(bb, tl, QKV_W), _seq_spec(bb, tl, QK_W),
                  _seq_spec(bb, tl, WIDTH_B), _state_spec(bb, POOL_HIST, WIDTH_B, rows_first), state_s,
                  _const_spec((1, HEAD_DIM)), _const_spec((len(POOL_WINDOWS), POOL_GROUP, POOL_GROUP)),
                  _const_spec((1, WIDTH_B))],
        out_specs=[_seq_spec(bb, tl, QK_W), _seq_spec(bb, tl, WIDTH_B),
                   _state_spec(bb, POOL_HIST, WIDTH_B, rows_first), state_s],
        out_shape=[jax.ShapeDtypeStruct((bsz, l, QK_W), act_dtype), jax.ShapeDtypeStruct((bsz, l, WIDTH_B), act_dtype),
                   _state_shape(bsz, POOL_HIST, WIDTH_B, rows_first),
                   sds(bsz, N_HEADS, HEAD_DIM, HEAD_DIM)],
        scratch_shapes=[pltpu.VMEM((bb, POOL_PAD + tl, WIDTH_B), F32)],
        compiler_params=pltpu.CompilerParams(dimension_semantics=("arbitrary", "arbitrary"),
                                             vmem_limit_bytes=VMEM_LIMIT),
        name="delta",
    )(*([gsc] * len(gate_specs)), qkv, z, p, pool_prev, s0, w_onorm, w_mix, pool_scale)


def _mlp_kernel(x_ref, oa_ref, ob_ref, gate_ref, wa_ref, wb_ref, wo_ref, gmlp_ref, wup_ref, wdown_ref, gfin_ref,
                y_ref):
    ma = _dot(oa_ref[...], wa_ref[...])
    mb = _dot(ob_ref[...], wb_ref[...])
    merged = _sigmoid(gate_ref[:, 0:D_MODEL]) * ma + _sigmoid(gate_ref[:, D_MODEL:]) * mb
    x1 = x_ref[...] + _dot(merged, wo_ref[...])
    h2 = (x1 * _rms_scale(x1) * gmlp_ref[...]).astype(BF16)
    acc = x1
    for c0 in range(0, D_FF, FF_BLOCK):
        up = jnp.dot(h2, wup_ref[:, c0:c0 + FF_BLOCK], preferred_element_type=F32)
        act = jnp.square(jnp.maximum(up, 0.0))
        acc = acc + _dot(act, wdown_ref[c0:c0 + FF_BLOCK, :])
    y_ref[...] = acc * _rms_scale(acc) * gfin_ref[...]


def _merge_mlp(x2d, oa, ob, gates, prm, tm):
    t = x2d.shape[0]
    row = lambda w: pl.BlockSpec((tm, w), lambda i: (i, 0))
    return pl.pallas_call(
        _mlp_kernel,
        grid=(t // tm,),
        in_specs=[row(D_MODEL), row(QK_W), row(WIDTH_B), row(2 * D_MODEL),
                  _const_spec((QK_W, D_MODEL)), _const_spec((WIDTH_B, D_MODEL)), _const_spec((D_MODEL, D_MODEL)),
                  _const_spec((1, D_MODEL)), _const_spec((D_MODEL, D_FF)), _const_spec((D_FF, D_MODEL)),
                  _const_spec((1, D_MODEL))],
        out_specs=row(D_MODEL),
        out_shape=jax.ShapeDtypeStruct((t, D_MODEL), F32),
        compiler_params=pltpu.CompilerParams(dimension_semantics=("arbitrary",),
                                             vmem_limit_bytes=VMEM_LIMIT),
        name="merge_mlp",
    )(x2d, oa, ob, gates, prm["w_a_out"], prm["w_b_out"], prm["w_o"], prm["g_mlp"], prm["w_up"], prm["w_down"],
      prm["g_final"])


def _trunk(x, conv_prev, pool_prev, s_prev, pos0, prm, *, front_blk, delta_blk, tm, chunk, act_dtype, rows_first):
    bsz, l, _ = x.shape
    t = bsz * l
    qkv, z, gsc, p, gates, conv_new = _front(
        x, prm["g_attn"], prm["wq"], prm["wba"], prm["wr"], prm["w_conv"], prm["a_log"], prm["dt_bias"],
        conv_prev, bb=front_blk[0], tl=front_blk[1], chunk=chunk, act_dtype=act_dtype, rows_first=rows_first)
    oa, ob, pool_new, s_new = _delta(qkv, z, gsc, p, pool_prev, s_prev, prm["w_onorm"], prm["w_mix"],
                                     prm["pool_scale"], bb=delta_blk[0], tl=delta_blk[1], chunk=chunk, pos0=pos0,
                                     act_dtype=act_dtype, rows_first=rows_first)
    y = _merge_mlp(x.reshape(t, D_MODEL), oa.reshape(t, QK_W), ob.reshape(t, WIDTH_B),
                   gates.reshape(t, 2 * D_MODEL), prm, tm)
    return y.reshape(bsz, l, D_MODEL), conv_new, pool_new, s_new[None]


def kernel(x_prompt, x_sample, state_conv, state_pool, state_ssm, w_in, w_conv, a_log, dt_bias, w_onorm,
           w_pool_mix, pool_scale, w_a_out, w_b_out, w_o, g_attn, g_mlp, w_up, w_down, g_final):
    assert w_in.shape[0] == 1, "single-layer decoder"
    wt = jnp.transpose(w_in[0])
    rest_off = QKVZ_W + GATE_SCALARS
    prm = {
        "wq": wt[0:QKVZ_W].astype(BF16),
        "wba": jnp.pad(wt[QKVZ_W:rest_off], ((0, 2 * SUBLANES - GATE_SCALARS), (0, 0))).astype(BF16),
        "wr": wt[rest_off:rest_off + REST_W].astype(BF16),
        "g_attn": g_attn[0][None, :], "g_mlp": g_mlp[0][None, :], "g_final": g_final[None, :],
        "w_conv": w_conv[0].astype(F32),
        "a_log": a_log[0], "dt_bias": dt_bias[0],
        "w_onorm": w_onorm[0][None, :].astype(F32),
        "w_mix": w_pool_mix[0].astype(BF16), "pool_scale": pool_scale[0][None, :].astype(F32),
        "w_a_out": w_a_out[0].astype(BF16), "w_b_out": w_b_out[0].astype(BF16), "w_o": w_o[0].astype(BF16),
        "w_up": w_up[0].astype(BF16), "w_down": w_down[0].astype(BF16),
    }
    bp = x_prompt.shape[0]
    y_p, conv_p, pool_p, ssm_p = _trunk(
        x_prompt, jnp.zeros((bp, CONV_W - 1, QKV_W), F32), jnp.zeros((bp, POOL_HIST, WIDTH_B), F32),
        jnp.zeros((bp, N_HEADS, HEAD_DIM, HEAD_DIM), F32), 0, prm,
        front_blk=(1, 512), delta_blk=(4, 256), tm=512, chunk=GROUP, act_dtype=BF16, rows_first=False)
    conv_p, pool_p = conv_p[None], pool_p[None]
    dec_len = x_sample.shape[1]
    rows_major = lambda s: jnp.transpose(s[0].astype(F32), (1, 0, 2))
    y_s, conv_s, pool_s, ssm_s = _trunk(
        x_sample, rows_major(state_conv), rows_major(state_pool), state_ssm[0].astype(F32), PAST_LEN, prm,
        front_blk=(256 // dec_len, dec_len), delta_blk=(2 * GROUP // dec_len, dec_len), tm=512,
        chunk=dec_len, act_dtype=F32, rows_first=True)
    conv_s, pool_s = jnp.transpose(conv_s, (1, 0, 2))[None], jnp.transpose(pool_s, (1, 0, 2))[None]
    return (y_p, y_s, conv_p.astype(state_conv.dtype), pool_p.astype(state_pool.dtype),
            ssm_p.astype(state_ssm.dtype), conv_s.astype(state_conv.dtype), pool_s.astype(state_pool.dtype),
            ssm_s.astype(state_ssm.dtype))
```

```python
import functools
import math

import jax
import jax.numpy as jnp
from jax import lax
from jax.experimental import pallas as pl
from jax.experimental.pallas import tpu as pltpu

D_MODEL = 1024
N_HEADS = 4
HEAD_DIM = 128
QK_W = N_HEADS * HEAD_DIM
QKV_W = 3 * QK_W
CONV_W = 4
POOL_WINDOWS = (2, 4, 8, 16)
POOL_GROUP = 128
WIDTH_B = len(POOL_WINDOWS) * POOL_GROUP
POOL_HIST = 15
D_FF = 4 * D_MODEL
EPS = 1e-6
PAST_LEN = 16384
LANES = 128
SUBLANES = 8
MXU_COLS = 256
N_GATE_ROWS = 4

QKVZ_W = QKV_W + QK_W
GATE_SCALARS = 2 * N_HEADS
REST_W = WIDTH_B + 2 * D_MODEL
REST_Z0 = 2 * D_MODEL
REST_P0 = REST_Z0 + QK_W
REST_OUT_W = REST_P0 + WIDTH_B

GROUP = 128
SERIES_BLOCK = 64
CONV_PAD = 8
POOL_LOOKBACK = 16
POOL_PAD = 24
FF_BLOCK = 1024

VMEM_LIMIT = 56 * 1024 * 1024

BF16 = jnp.bfloat16
F32 = jnp.float32


def _dot(a, b):
    return jnp.dot(a.astype(BF16), b.astype(BF16), preferred_element_type=F32)


def _sigmoid(x):
    return 1.0 / (1.0 + jnp.exp(-x))


def _silu(x):
    half = 0.5 * x
    return half * jnp.tanh(half) + half


def _rms_scale(x):
    return lax.rsqrt(jnp.mean(x * x, axis=-1, keepdims=True) + EPS)


def _const_spec(shape):
    zeros = (0,) * len(shape)
    return pl.BlockSpec(shape, lambda *_: zeros, pipeline_mode=pl.Buffered(1))


def _seq_spec(bb, tl, width):
    return pl.BlockSpec((bb, tl, width), lambda i, j: (i, j, 0))


def _state_spec(bb, n_rows, width, rows_first):
    if rows_first:
        return pl.BlockSpec((n_rows, bb, width), lambda i, j: (0, i, 0))
    return pl.BlockSpec((bb, n_rows, width), lambda i, j: (i, 0, 0))


def _state_shape(bsz, n_rows, width, rows_first):
    return jax.ShapeDtypeStruct((n_rows, bsz, width) if rows_first else (bsz, n_rows, width), F32)


def _history_to_scratch(scratch_ref, first_row, state_ref, rows_first):
    if rows_first:
        for j in range(state_ref.shape[0]):
            scratch_ref[:, first_row + j, :] = state_ref[j]
    else:
        scratch_ref[:, first_row:first_row + state_ref.shape[1], :] = state_ref[...]


def _history_from_scratch(state_ref, scratch_ref, first_row, rows_first):
    if rows_first:
        for j in range(state_ref.shape[0]):
            state_ref[j] = scratch_ref[:, first_row + j, :]
    else:
        state_ref[...] = scratch_ref[:, first_row:first_row + state_ref.shape[1], :]


def _gate_rows_spec(bb, tl, n_tiles):
    return pl.BlockSpec((N_GATE_ROWS, SUBLANES, bb * tl), lambda i, j: (0, 0, i * n_tiles + j))


def _lane_prefix_sum(x, chunk):
    lane = lax.broadcasted_iota(jnp.int32, x.shape, 1)
    shift = 1
    while shift < chunk:
        x = x + jnp.where(lane % chunk >= shift, pltpu.roll(x, shift, axis=1), 0.0)
        shift *= 2
    return x


def _lane_suffix_sum(x, chunk):
    lane = lax.broadcasted_iota(jnp.int32, x.shape, 1)
    shift = 1
    while shift < chunk:
        x = x + jnp.where(lane % chunk + shift < chunk, pltpu.roll(x, x.shape[1] - shift, axis=1), 0.0)
        shift *= 2
    return x


def _front_kernel(x_ref, g_ref, wq_ref, wba_ref, wr_ref, wconv_ref, alog_ref, dtb_ref, keep_ref, convprev_ref,
                  qkv_ref, rest_ref, gsc_ref, convnew_ref,
                  ext_ref, *, bb, tl, n_tiles, chunk, rows_first):
    tile = pl.program_id(1)
    rows = bb * tl
    hist = slice(CONV_PAD - (CONV_W - 1), CONV_PAD)

    @pl.when(tile == 0)
    def _():
        _history_to_scratch(ext_ref, hist.start, convprev_ref, rows_first)
        ext_ref[:, 0:CONV_PAD - (CONV_W - 1), :] = jnp.zeros((bb, CONV_PAD - (CONV_W - 1), QKV_W), F32)

    x = x_ref[...].reshape(rows, D_MODEL)
    normed = (x * _rms_scale(x) * g_ref[...]).astype(BF16)
    contract_last = (((1,), (1,)), ((), ()))
    proj = lambda w_rows: lax.dot_general(normed, w_rows, contract_last, preferred_element_type=F32)
    ext_ref[:, CONV_PAD:CONV_PAD + tl, :] = proj(wq_ref[0:QKV_W, :]).reshape(bb, tl, QKV_W)
    _history_from_scratch(convnew_ref, ext_ref, CONV_PAD + tl - (CONV_W - 1), rows_first)

    x8 = lax.dot_general(wba_ref[...], normed, contract_last, preferred_element_type=F32)[0:SUBLANES, :]
    xs = x8 + dtb_ref[...]
    softplus = jnp.maximum(xs, 0.0) + jnp.log1p(jnp.exp(-jnp.abs(xs)))
    graw8 = -jnp.exp(alog_ref[...]) * softplus
    g8 = _lane_prefix_sum(graw8, chunk)
    gsc_ref[0] = _sigmoid(x8)
    gsc_ref[1] = g8
    gsc_ref[2] = jnp.exp(g8)
    gsc_ref[3] = jnp.exp(_lane_suffix_sum(graw8, chunk) - graw8)

    keep = keep_ref[...] != 0

    def conv_block(c0, dep):
        cols = slice(c0, c0 + LANES)
        xe = ext_ref[:, :, cols].reshape(bb * (CONV_PAD + tl), LANES)
        acc = wconv_ref[0:1, cols] * xe
        for j in range(1, CONV_W):
            acc = wconv_ref[j:j + 1, cols] * xe + pltpu.roll(acc, 1, axis=0)
        val = _silu(acc.reshape(bb, CONV_PAD + tl, LANES)[:, CONV_PAD:, :])
        if c0 < 2 * QK_W:
            scale = HEAD_DIM ** -0.5 if c0 < QK_W else 1.0
            val = val * (lax.rsqrt(jnp.sum(val * val, axis=-1, keepdims=True) + EPS) * scale)
        qkv_ref[:, :, cols] = jnp.where(keep, val, dep[:, 0:LANES].reshape(bb, tl, LANES)).astype(qkv_ref.dtype)

    def proj_block(out0, w_ref, w0, c0):
        val = proj(w_ref[w0 + c0:w0 + c0 + MXU_COLS, :])
        rest_ref[:, :, out0 + c0:out0 + c0 + MXU_COLS] = val.reshape(bb, tl, MXU_COLS)
        return val

    mxu_work = ([functools.partial(proj_block, REST_Z0, wq_ref, QKV_W, c0) for c0 in range(0, QK_W, MXU_COLS)]
                + [functools.partial(proj_block, REST_P0, wr_ref, 0, c0) for c0 in range(0, WIDTH_B, MXU_COLS)]
                + [functools.partial(proj_block, 0, wr_ref, WIDTH_B, c0)
                   for c0 in range(0, 2 * D_MODEL, MXU_COLS)])
    for i, c0 in enumerate(range(0, QKV_W, LANES)):
        conv_block(c0, mxu_work[i]())

    if n_tiles > 1:
        _history_to_scratch(ext_ref, hist.start, convnew_ref, rows_first)


def _front(x, g_attn, wq, wba, wr, w_conv, a_log, dt_bias, conv_prev, *, bb, tl, chunk, act_dtype, rows_first):
    bsz, l, _ = x.shape
    n_tiles = l // tl
    kern = functools.partial(_front_kernel, bb=bb, tl=tl, n_tiles=n_tiles, chunk=chunk, rows_first=rows_first)
    sds = lambda *shape: jax.ShapeDtypeStruct(shape, F32)
    zeros4 = jnp.zeros((N_HEADS,), F32)
    rows8 = lambda v: jnp.broadcast_to(jnp.concatenate([zeros4, v.astype(F32)])[:, None], (SUBLANES, bb * tl))
    return pl.pallas_call(
        kern,
        grid=(bsz // bb, n_tiles),
        in_specs=[_seq_spec(bb, tl, D_MODEL), _const_spec((1, D_MODEL)), _const_spec((QKVZ_W, D_MODEL)),
                  _const_spec((2 * SUBLANES, D_MODEL)), _const_spec((REST_W, D_MODEL)), _const_spec((CONV_W, QKV_W)),
                  _const_spec((SUBLANES, bb * tl)), _const_spec((SUBLANES, bb * tl)), _const_spec((1, LANES)),
                  _state_spec(bb, CONV_W - 1, QKV_W, rows_first)],
        out_specs=[_seq_spec(bb, tl, QKV_W), _seq_spec(bb, tl, REST_OUT_W), _gate_rows_spec(bb, tl, n_tiles),
                   _state_spec(bb, CONV_W - 1, QKV_W, rows_first)],
        out_shape=[jax.ShapeDtypeStruct((bsz, l, QKV_W), act_dtype), sds(bsz, l, REST_OUT_W),
                   sds(N_GATE_ROWS, SUBLANES, bsz * l),
                   _state_shape(bsz, CONV_W - 1, QKV_W, rows_first)],
        scratch_shapes=[pltpu.VMEM((bb, CONV_PAD + tl, QKV_W), F32)],
        compiler_params=pltpu.CompilerParams(dimension_semantics=("arbitrary", "arbitrary"),
                                             vmem_limit_bytes=VMEM_LIMIT),
        name="front",
    )(x, g_attn, wq, wba, wr, w_conv, rows8(a_log), rows8(dt_bias), jnp.ones((1, LANES), jnp.int32), conv_prev)


def _unit_lower_inverses(neg_ms, chunk):
    nb = min(chunk, SERIES_BLOCK)
    n_blocks = GROUP // nb
    n_factors = int(math.log2(nb))
    assert chunk == nb or (chunk == 2 * nb and n_blocks == 2), "chunks are one or two series blocks"
    lane = lax.broadcasted_iota(jnp.int32, (nb, GROUP), 1)
    lane_block = lane // nb

    def packed(m):
        out = m[0:nb]
        for b in range(1, n_blocks):
            out = jnp.where(lane_block == b, m[b * nb:(b + 1) * nb], out)
        return out

    def block_diag(p):
        return jnp.concatenate([jnp.where(lane_block == b, p, 0.0) for b in range(n_blocks)], axis=0)

    nps = [packed(m) for m in neg_ms]
    eye_p = jnp.where(lane % nb == lax.broadcasted_iota(jnp.int32, (nb, GROUP), 0), 1.0, 0.0).astype(F32)
    invs = [eye_p + n for n in nps]
    if n_factors > 1:
        pws = [_dot(n, block_diag(n)) for n in nps]
        for _ in range(n_factors - 2):
            boths = [_dot(jnp.concatenate([inv, pw], axis=0), block_diag(pw)) for inv, pw in zip(invs, pws)]
            invs = [inv + both[0:nb] for inv, both in zip(invs, boths)]
            pws = [both[nb:] for both in boths]
        invs = [inv + _dot(inv, block_diag(pw)) for inv, pw in zip(invs, pws)]
    if chunk == nb:
        return [block_diag(inv) for inv in invs]
    zeros = jnp.zeros((nb, GROUP), F32)
    a_invs = [jnp.where(lane < nb, inv, 0.0) for inv in invs]
    neg_ls = [jnp.where(lane < nb, m[nb:], 0.0) for m in neg_ms]
    xs = [_dot(neg_l, jnp.concatenate([a_inv, zeros], axis=0)) for neg_l, a_inv in zip(neg_ls, a_invs)]
    ys = [_dot(inv, jnp.concatenate([zeros, x], axis=0)) for inv, x in zip(invs, xs)]
    return [jnp.concatenate([a_inv, y + jnp.where(lane >= nb, inv, 0.0)], axis=0)
            for a_inv, y, inv in zip(a_invs, ys, invs)]


def _delta_kernel(*refs, bb, tl, chunk, pos0, n_tiles, n_gate_refs, rows_first):
    gsc_refs, refs = refs[:n_gate_refs], refs[n_gate_refs:]
    (qkv_ref, zp_ref, poolprev_ref, s0_ref, wonorm_ref, wmix_ref, pscale_ref,
     oab_ref, poolnew_ref, snew_ref, pext_ref) = refs
    tile = pl.program_id(1)
    rows_b = min(tl, GROUP)
    seqs_g = GROUP // rows_b
    groups_b = tl // rows_b
    n_groups = bb * tl // GROUP
    chained = chunk == GROUP
    assert chained or (chunk == tl and tl < GROUP), "chunk must be a whole group or a whole short sequence"
    pool_hist = slice(POOL_PAD - POOL_HIST, POOL_PAD)

    @pl.when(tile == 0)
    def _():
        snew_ref[...] = s0_ref[...]
        _history_to_scratch(pext_ref, pool_hist.start, poolprev_ref, rows_first)
        pext_ref[:, POOL_PAD - POOL_LOOKBACK:POOL_PAD - POOL_HIST, :] = jnp.zeros((bb, 1, WIDTH_B), F32)

    pext_ref[:, POOL_PAD:POOL_PAD + tl, :] = zp_ref[:, :, QK_W:QK_W + WIDTH_B]
    _history_from_scratch(poolnew_ref, pext_ref, POOL_PAD + tl - POOL_HIST, rows_first)

    row = lax.broadcasted_iota(jnp.int32, (GROUP, GROUP), 0)
    col = lax.broadcasted_iota(jnp.int32, (GROUP, GROUP), 1)
    causal = row >= col
    strict = row > col
    if not chained:
        same = (row // chunk) == (col // chunk)
        causal = causal & same
        strict = strict & same

    def origin(g):
        if tl >= GROUP:
            return g // groups_b, (g % groups_b) * GROUP
        return g * seqs_g, 0

    def load(c0, g):
        b0, t0 = origin(g)
        return qkv_ref[b0:b0 + seqs_g, t0:t0 + rows_b, c0:c0 + HEAD_DIM].astype(F32).reshape(GROUP, HEAD_DIM)

    def gate_rows(g):
        if n_gate_refs > 1:
            b0, t0 = origin(g)
            return tuple(gsc_refs[b0][q, :, t0:t0 + GROUP] for q in range(N_GATE_ROWS))
        return tuple(gsc_refs[0][q, :, g * GROUP:(g + 1) * GROUP] for q in range(N_GATE_ROWS))

    gates = [gate_rows(g) for g in range(n_groups)]

    probs = [(g, h) for g in range(n_groups) for h in range(N_HEADS)]
    st = []
    for g, h in probs:
        beta8, g8, eg8, kds8 = gates[g]
        d = {"q": load(h * HEAD_DIM, g), "k": load(QK_W + h * HEAD_DIM, g), "v": load(2 * QK_W + h * HEAD_DIM, g)}
        d["beta_row"] = beta8[h:h + 1, :]
        d["eg_row"] = eg8[N_HEADS + h:N_HEADS + h + 1, :]
        d["g_row"] = g8[N_HEADS + h:N_HEADS + h + 1, :]
        d["kb_row"] = kds8[N_HEADS + h:N_HEADS + h + 1, :] * d["beta_row"]
        d["g_col"] = jnp.broadcast_to(d["g_row"], (GROUP, GROUP)).T
        d["kt"] = d["k"].T
        st.append(d)
    for d in st:
        both = _dot(jnp.concatenate([d["k"], d["q"]], axis=0), d["kt"])
        d["kk"], d["qk"] = both[0:GROUP], both[GROUP:]
    for d in st:
        decay = jnp.exp(jnp.where(causal, d["g_col"] - d["g_row"], -jnp.inf))
        d["neg_m"] = jnp.where(strict, -(d.pop("kk") * decay), 0.0) * d["beta_row"]
        d["qkm"] = d.pop("qk") * decay * d["beta_row"]
        d["q_dec"] = d.pop("q") * jnp.exp(d["g_col"])
        d["kt_dec"] = d.pop("kt") * d["kb_row"]
    invs = _unit_lower_inverses([d.pop("neg_m") for d in st], chunk)
    for d, inv in zip(st, invs):
        d["uy"] = _dot(inv, d.pop("v"))
        d["wy"] = _dot(inv * d["eg_row"], d.pop("k"))

    outs = {}
    if chained:
        for j in range(groups_b):
            wave = [(i, g, h) for i, (g, h) in enumerate(probs) if g % groups_b == j]
            s_old = {i: snew_ref[origin(g)[0], h] for i, g, h in wave}
            ws = {i: _dot(jnp.concatenate([st[i]["wy"], st[i]["q_dec"]], axis=0), s_old[i]) for i, g, h in wave}
            ys = {i: st[i]["uy"] - ws[i][0:GROUP] for i, g, h in wave}
            for i, g, h in wave:
                outs[i] = ws[i][GROUP:] + _dot(st[i]["qkm"], ys[i])
            for i, g, h in wave:
                last = jnp.exp(st[i]["g_col"][GROUP - 1:GROUP, :])
                snew_ref[origin(g)[0], h] = s_old[i] * last + _dot(st[i]["kt_dec"], ys[i])
    else:
        for i, (g, h) in enumerate(probs):
            d, b0 = st[i], origin(g)[0]
            ws_w, ws_q = [], []
            for s_i in range(seqs_g):
                r = slice(s_i * rows_b, (s_i + 1) * rows_b)
                ws = _dot(jnp.concatenate([d["wy"][r], d["q_dec"][r]], axis=0), snew_ref[b0 + s_i, h])
                ws_w.append(ws[0:rows_b])
                ws_q.append(ws[rows_b:])
            d["y"] = d["uy"] - jnp.concatenate(ws_w, axis=0)
            outs[i] = jnp.concatenate(ws_q, axis=0) + _dot(d["qkm"], d["y"])
        for i, (g, h) in enumerate(probs):
            d, b0 = st[i], origin(g)[0]
            for s_i in range(seqs_g):
                last = jnp.exp(d["g_col"][(s_i + 1) * rows_b - 1:(s_i + 1) * rows_b, :])
                upd = _dot(jnp.where(col // rows_b == s_i, d["kt_dec"], 0.0), d["y"])
                snew_ref[b0 + s_i, h] = snew_ref[b0 + s_i, h] * last + upd

    def group_slab(ref, g, cols, row_off=0):
        b0, t0 = origin(g)
        return ref.at[b0:b0 + seqs_g, row_off + t0:row_off + t0 + rows_b, cols]

    for i, (g, h) in enumerate(probs):
        cols = slice(h * HEAD_DIM, (h + 1) * HEAD_DIM)
        o = outs[i]
        z = group_slab(zp_ref, g, cols)[...].reshape(GROUP, HEAD_DIM)
        group_slab(oab_ref, g, cols)[...] = (o * _rms_scale(o) * wonorm_ref[...] * _silu(z)).reshape(
            seqs_g, rows_b, HEAD_DIM).astype(oab_ref.dtype)

    for g in range(n_groups):
        b0, t0 = origin(g)
        pos = pos0 + tile * tl + t0 + row % rows_b
        for gi, win in enumerate(POOL_WINDOWS):
            cols = slice(gi * POOL_GROUP, (gi + 1) * POOL_GROUP)
            slab = pext_ref[b0:b0 + seqs_g, POOL_PAD - POOL_LOOKBACK + t0:POOL_PAD + t0 + rows_b, cols]
            acc = slab.reshape(seqs_g * (POOL_LOOKBACK + rows_b), POOL_GROUP)
            shift = 1
            while shift < win:
                acc = acc + pltpu.roll(acc, shift, axis=0)
                shift *= 2
            acc = acc.reshape(seqs_g, POOL_LOOKBACK + rows_b, POOL_GROUP)[:, POOL_LOOKBACK:, :]
            cur = slab[:, POOL_LOOKBACK:, :]
            pooled = (acc / jnp.minimum(pos + 1, win).astype(F32).reshape(seqs_g, rows_b, POOL_GROUP) - cur)
            mixed = _dot(pooled.reshape(GROUP, POOL_GROUP), wmix_ref[gi]) * pscale_ref[:, cols]
            oab_ref[b0:b0 + seqs_g, t0:t0 + rows_b, QK_W + gi * POOL_GROUP:QK_W + (gi + 1) * POOL_GROUP] = (
                mixed.reshape(seqs_g, rows_b, POOL_GROUP).astype(oab_ref.dtype))

    if n_tiles > 1:
        _history_to_scratch(pext_ref, pool_hist.start, poolnew_ref, rows_first)


def _delta(qkv, rest, gsc, pool_prev, s0, w_onorm, w_mix, pool_scale, *, bb, tl, chunk, pos0, act_dtype,
           rows_first):
    bsz, l = qkv.shape[0], qkv.shape[1]
    n_tiles = l // tl
    state_s = pl.BlockSpec((bb, N_HEADS, HEAD_DIM, HEAD_DIM), lambda i, j: (i, 0, 0, 0))
    if n_tiles == 1:
        gate_specs = [_gate_rows_spec(bb, tl, 1)]
    else:
        gate_specs = [pl.BlockSpec((N_GATE_ROWS, SUBLANES, tl), lambda i, j, k=k: (0, 0, (i * bb + k) * n_tiles + j))
                      for k in range(bb)]
    kern = functools.partial(_delta_kernel, bb=bb, tl=tl, chunk=chunk, pos0=pos0, n_tiles=n_tiles,
                             n_gate_refs=len(gate_specs), rows_first=rows_first)
    sds = lambda *shape: jax.ShapeDtypeStruct(shape, F32)
    return pl.pallas_call(
        kern,
        grid=(bsz // bb, n_tiles),
        in_specs=gate_specs + [
                  _seq_spec(bb, tl, QKV_W),
                  pl.BlockSpec((bb, tl, QK_W + WIDTH_B), lambda i, j: (i, j, REST_Z0 // (QK_W + WIDTH_B))),
                  _state_spec(bb, POOL_HIST, WIDTH_B, rows_first), state_s,
                  _const_spec((1, HEAD_DIM)), _const_spec((len(POOL_WINDOWS), POOL_GROUP, POOL_GROUP)),
                  _const_spec((1, WIDTH_B))],
        out_specs=[_seq_spec(bb, tl, QK_W + WIDTH_B),
                   _state_spec(bb, POOL_HIST, WIDTH_B, rows_first), state_s],
        out_shape=[jax.ShapeDtypeStruct((bsz, l, QK_W + WIDTH_B), act_dtype),
                   _state_shape(bsz, POOL_HIST, WIDTH_B, rows_first),
                   sds(bsz, N_HEADS, HEAD_DIM, HEAD_DIM)],
        scratch_shapes=[pltpu.VMEM((bb, POOL_PAD + tl, WIDTH_B), F32)],
        compiler_params=pltpu.CompilerParams(dimension_semantics=("arbitrary", "arbitrary"),
                                             vmem_limit_bytes=VMEM_LIMIT),
        name="delta",
    )(*([gsc] * len(gate_specs)), qkv, rest, pool_prev, s0, w_onorm, w_mix, pool_scale)


def _mlp_kernel(x_ref, oab_ref, gate_ref, wa_ref, wb_ref, wo_ref, gmlp_ref, wup_ref, wdown_ref, gfin_ref, y_ref):
    ma = _dot(oab_ref[:, 0:QK_W], wa_ref[...])
    mb = _dot(oab_ref[:, QK_W:QK_W + WIDTH_B], wb_ref[...])
    merged = _sigmoid(gate_ref[:, 0:D_MODEL]) * ma + _sigmoid(gate_ref[:, D_MODEL:]) * mb
    x1 = x_ref[...] + _dot(merged, wo_ref[...])
    h2 = (x1 * _rms_scale(x1) * gmlp_ref[...]).astype(BF16)
    acc = x1
    for c0 in range(0, D_FF, FF_BLOCK):
        up = jnp.dot(h2, wup_ref[:, c0:c0 + FF_BLOCK], preferred_element_type=F32)
        act = jnp.square(jnp.maximum(up, 0.0))
        acc = acc + _dot(act, wdown_ref[c0:c0 + FF_BLOCK, :])
    y_ref[...] = acc * _rms_scale(acc) * gfin_ref[...]


def _merge_mlp(x2d, oab, rest, prm, tm):
    t = x2d.shape[0]
    row = lambda w: pl.BlockSpec((tm, w), lambda i: (i, 0))
    return pl.pallas_call(
        _mlp_kernel,
        grid=(t // tm,),
        in_specs=[row(D_MODEL), row(QK_W + WIDTH_B), row(2 * D_MODEL),
                  _const_spec((QK_W, D_MODEL)), _const_spec((WIDTH_B, D_MODEL)), _const_spec((D_MODEL, D_MODEL)),
                  _const_spec((1, D_MODEL)), _const_spec((D_MODEL, D_FF)), _const_spec((D_FF, D_MODEL)),
                  _const_spec((1, D_MODEL))],
        out_specs=row(D_MODEL),
        out_shape=jax.ShapeDtypeStruct((t, D_MODEL), F32),
        compiler_params=pltpu.CompilerParams(dimension_semantics=("arbitrary",),
                                             vmem_limit_bytes=VMEM_LIMIT),
        name="merge_mlp",
    )(x2d, oab, rest, prm["w_a_out"], prm["w_b_out"], prm["w_o"], prm["g_mlp"], prm["w_up"], prm["w_down"],
      prm["g_final"])


def _trunk(x, conv_prev, pool_prev, s_prev, pos0, prm, *, front_blk, delta_blk, tm, chunk, act_dtype, rows_first):
    bsz, l, _ = x.shape
    t = bsz * l
    qkv, rest, gsc, conv_new = _front(
        x, prm["g_attn"], prm["wq"], prm["wba"], prm["wr"], prm["w_conv"], prm["a_log"], prm["dt_bias"],
        conv_prev, bb=front_blk[0], tl=front_blk[1], chunk=chunk, act_dtype=act_dtype, rows_first=rows_first)
    oab, pool_new, s_new = _delta(qkv, rest, gsc, pool_prev, s_prev, prm["w_onorm"], prm["w_mix"],
                                  prm["pool_scale"], bb=delta_blk[0], tl=delta_blk[1], chunk=chunk, pos0=pos0,
                                  act_dtype=act_dtype, rows_first=rows_first)
    y = _merge_mlp(x.reshape(t, D_MODEL), oab.reshape(t, QK_W + WIDTH_B), rest.reshape(t, REST_OUT_W), prm, tm)
    return y.reshape(bsz, l, D_MODEL), conv_new, pool_new, s_new[None]


def kernel(x_prompt, x_sample, state_conv, state_pool, state_ssm, w_in, w_conv, a_log, dt_bias, w_onorm,
           w_pool_mix, pool_scale, w_a_out, w_b_out, w_o, g_attn, g_mlp, w_up, w_down, g_final):
    assert w_in.shape[0] == 1, "single-layer decoder"
    wt = jnp.transpose(w_in[0])
    rest_off = QKVZ_W + GATE_SCALARS
    prm = {
        "wq": wt[0:QKVZ_W].astype(BF16),
        "wba": jnp.pad(wt[QKVZ_W:rest_off], ((0, 2 * SUBLANES - GATE_SCALARS), (0, 0))).astype(BF16),
        "wr": wt[rest_off:rest_off + REST_W].astype(BF16),
        "g_attn": g_attn[0][None, :], "g_mlp": g_mlp[0][None, :], "g_final": g_final[None, :],
        "w_conv": w_conv[0].astype(F32),
        "a_log": a_log[0], "dt_bias": dt_bias[0],
        "w_onorm": w_onorm[0][None, :].astype(F32),
        "w_mix": w_pool_mix[0].astype(BF16), "pool_scale": pool_scale[0][None, :].astype(F32),
        "w_a_out": w_a_out[0].astype(BF16), "w_b_out": w_b_out[0].astype(BF16), "w_o": w_o[0].astype(BF16),
        "w_up": w_up[0].astype(BF16), "w_down": w_down[0].astype(BF16),
    }
    bp = x_prompt.shape[0]
    y_p, conv_p, pool_p, ssm_p = _trunk(
        x_prompt, jnp.zeros((bp, CONV_W - 1, QKV_W), F32), jnp.zeros((bp, POOL_HIST, WIDTH_B), F32),
        jnp.zeros((bp, N_HEADS, HEAD_DIM, HEAD_DIM), F32), 0, prm,
        front_blk=(1, 512), delta_blk=(4, 256), tm=512, chunk=GROUP, act_dtype=BF16, rows_first=False)
    conv_p, pool_p = conv_p[None], pool_p[None]
    dec_len = x_sample.shape[1]
    rows_major = lambda s: jnp.transpose(s[0].astype(F32), (1, 0, 2))
    y_s, conv_s, pool_s, ssm_s = _trunk(
        x_sample, rows_major(state_conv), rows_major(state_pool), state_ssm[0].astype(F32), PAST_LEN, prm,
        front_blk=(256 // dec_len, dec_len), delta_blk=(2 * GROUP // dec_len, dec_len), tm=512,
        chunk=dec_len, act_dtype=F32, rows_first=True)
    conv_s, pool_s = jnp.transpose(conv_s, (1, 0, 2))[None], jnp.transpose(pool_s, (1, 0, 2))[None]
    return (y_p, y_s, conv_p.astype(state_conv.dtype), pool_p.astype(state_pool.dtype),
            ssm_p.astype(state_ssm.dtype), conv_s.astype(state_conv.dtype), pool_s.astype(state_pool.dtype),
            ssm_s.astype(state_ssm.dtype))
```

```python
import functools
import math

import jax
import jax.numpy as jnp
from jax import lax
from jax.experimental import pallas as pl
from jax.experimental.pallas import tpu as pltpu

D_MODEL = 1024
N_HEADS = 4
HEAD_DIM = 128
QK_W = N_HEADS * HEAD_DIM
QKV_W = 3 * QK_W
CONV_W = 4
POOL_WINDOWS = (2, 4, 8, 16)
POOL_GROUP = 128
WIDTH_B = len(POOL_WINDOWS) * POOL_GROUP
POOL_HIST = 15
D_FF = 4 * D_MODEL
EPS = 1e-6
PAST_LEN = 16384
LANES = 128
SUBLANES = 8
MXU_COLS = 256
N_GATE_ROWS = 4

QKVZ_W = QKV_W + QK_W
GATE_SCALARS = 2 * N_HEADS
REST_W = WIDTH_B + 2 * D_MODEL
REST_Z0 = 2 * D_MODEL
REST_P0 = REST_Z0 + QK_W
REST_OUT_W = REST_P0 + WIDTH_B

GROUP = 128
SERIES_BLOCK = 64
CONV_PAD = 8
POOL_LOOKBACK = 16
POOL_PAD = 24
FF_BLOCK = 1024

VMEM_LIMIT = 56 * 1024 * 1024

BF16 = jnp.bfloat16
F32 = jnp.float32


def _dot(a, b):
    return jnp.dot(a.astype(BF16), b.astype(BF16), preferred_element_type=F32)


def _sigmoid(x):
    return 1.0 / (1.0 + jnp.exp(-x))


def _silu(x):
    half = 0.5 * x
    return half * jnp.tanh(half) + half


def _rms_scale(x):
    return lax.rsqrt(jnp.mean(x * x, axis=-1, keepdims=True) + EPS)


def _const_spec(shape):
    zeros = (0,) * len(shape)
    return pl.BlockSpec(shape, lambda *_: zeros, pipeline_mode=pl.Buffered(1))


def _seq_spec(bb, tl, width):
    return pl.BlockSpec((bb, tl, width), lambda i, j: (i, j, 0))


def _state_spec(bb, n_rows, width, rows_first):
    if rows_first:
        return pl.BlockSpec((n_rows, bb, width), lambda i, j: (0, i, 0))
    return pl.BlockSpec((bb, n_rows, width), lambda i, j: (i, 0, 0))


def _state_shape(bsz, n_rows, width, rows_first):
    return jax.ShapeDtypeStruct((n_rows, bsz, width) if rows_first else (bsz, n_rows, width), F32)


def _history_to_scratch(scratch_ref, first_row, state_ref, rows_first):
    if rows_first:
        for j in range(state_ref.shape[0]):
            scratch_ref[:, first_row + j, :] = state_ref[j]
    else:
        scratch_ref[:, first_row:first_row + state_ref.shape[1], :] = state_ref[...]


def _history_from_scratch(state_ref, scratch_ref, first_row, rows_first):
    if rows_first:
        for j in range(state_ref.shape[0]):
            state_ref[j] = scratch_ref[:, first_row + j, :]
    else:
        state_ref[...] = scratch_ref[:, first_row:first_row + state_ref.shape[1], :]


def _gate_rows_spec(bb, tl, n_tiles):
    return pl.BlockSpec((N_GATE_ROWS, SUBLANES, bb * tl), lambda i, j: (0, 0, i * n_tiles + j))


def _lane_prefix_sum(x, chunk):
    lane = lax.broadcasted_iota(jnp.int32, x.shape, 1)
    shift = 1
    while shift < chunk:
        x = x + jnp.where(lane % chunk >= shift, pltpu.roll(x, shift, axis=1), 0.0)
        shift *= 2
    return x


def _lane_suffix_sum(x, chunk):
    lane = lax.broadcasted_iota(jnp.int32, x.shape, 1)
    shift = 1
    while shift < chunk:
        x = x + jnp.where(lane % chunk + shift < chunk, pltpu.roll(x, x.shape[1] - shift, axis=1), 0.0)
        shift *= 2
    return x


def _front_kernel(x_ref, g_ref, wq_ref, wba_ref, wr_ref, wconv_ref, alog_ref, dtb_ref, keep_ref, convprev_ref,
                  qkv_ref, rest_ref, gsc_ref, convnew_ref,
                  ext_ref, *, bb, tl, n_tiles, chunk, rows_first):
    tile = pl.program_id(1)
    rows = bb * tl
    hist = slice(CONV_PAD - (CONV_W - 1), CONV_PAD)

    @pl.when(tile == 0)
    def _():
        _history_to_scratch(ext_ref, hist.start, convprev_ref, rows_first)
        ext_ref[:, 0:CONV_PAD - (CONV_W - 1), :] = jnp.zeros((bb, CONV_PAD - (CONV_W - 1), QKV_W), F32)

    x = x_ref[...].reshape(rows, D_MODEL)
    normed = (x * _rms_scale(x) * g_ref[...]).astype(BF16)
    contract_last = (((1,), (1,)), ((), ()))
    proj = lambda w_rows: lax.dot_general(normed, w_rows, contract_last, preferred_element_type=F32)
    ext_ref[:, CONV_PAD:CONV_PAD + tl, :] = proj(wq_ref[0:QKV_W, :]).reshape(bb, tl, QKV_W)
    _history_from_scratch(convnew_ref, ext_ref, CONV_PAD + tl - (CONV_W - 1), rows_first)

    x8 = lax.dot_general(wba_ref[...], normed, contract_last, preferred_element_type=F32)[0:SUBLANES, :]
    xs = x8 + dtb_ref[...]
    softplus = jnp.maximum(xs, 0.0) + jnp.log1p(jnp.exp(-jnp.abs(xs)))
    graw8 = -jnp.exp(alog_ref[...]) * softplus
    g8 = _lane_prefix_sum(graw8, chunk)
    gsc_ref[0] = _sigmoid(x8)
    gsc_ref[1] = g8
    gsc_ref[2] = jnp.exp(g8)
    gsc_ref[3] = jnp.exp(_lane_suffix_sum(graw8, chunk) - graw8)

    keep = keep_ref[...] != 0

    def conv_block(c0, dep):
        cols = slice(c0, c0 + LANES)
        xe = ext_ref[:, :, cols].reshape(bb * (CONV_PAD + tl), LANES)
        acc = wconv_ref[0:1, cols] * xe
        for j in range(1, CONV_W):
            acc = wconv_ref[j:j + 1, cols] * xe + pltpu.roll(acc, 1, axis=0)
        val = _silu(acc.reshape(bb, CONV_PAD + tl, LANES)[:, CONV_PAD:, :])
        if c0 < 2 * QK_W:
            scale = HEAD_DIM ** -0.5 if c0 < QK_W else 1.0
            val = val * (lax.rsqrt(jnp.sum(val * val, axis=-1, keepdims=True) + EPS) * scale)
        qkv_ref[:, :, cols] = jnp.where(keep, val, dep[:, 0:LANES].reshape(bb, tl, LANES)).astype(qkv_ref.dtype)

    def proj_block(out0, w_ref, w0, c0):
        val = proj(w_ref[w0 + c0:w0 + c0 + MXU_COLS, :])
        rest_ref[:, :, out0 + c0:out0 + c0 + MXU_COLS] = val.reshape(bb, tl, MXU_COLS)
        return val

    mxu_work = ([functools.partial(proj_block, REST_Z0, wq_ref, QKV_W, c0) for c0 in range(0, QK_W, MXU_COLS)]
                + [functools.partial(proj_block, REST_P0, wr_ref, 0, c0) for c0 in range(0, WIDTH_B, MXU_COLS)]
                + [functools.partial(proj_block, 0, wr_ref, WIDTH_B, c0)
                   for c0 in range(0, 2 * D_MODEL, MXU_COLS)])
    for i, c0 in enumerate(range(0, QKV_W, LANES)):
        conv_block(c0, mxu_work[i]())

    if n_tiles > 1:
        _history_to_scratch(ext_ref, hist.start, convnew_ref, rows_first)


def _front(x, g_attn, wq, wba, wr, w_conv, a_log, dt_bias, conv_prev, *, bb, tl, chunk, act_dtype, rows_first):
    bsz, l, _ = x.shape
    n_tiles = l // tl
    kern = functools.partial(_front_kernel, bb=bb, tl=tl, n_tiles=n_tiles, chunk=chunk, rows_first=rows_first)
    sds = lambda *shape: jax.ShapeDtypeStruct(shape, F32)
    zeros4 = jnp.zeros((N_HEADS,), F32)
    rows8 = lambda v: jnp.broadcast_to(jnp.concatenate([zeros4, v.astype(F32)])[:, None], (SUBLANES, bb * tl))
    return pl.pallas_call(
        kern,
        grid=(bsz // bb, n_tiles),
        in_specs=[_seq_spec(bb, tl, D_MODEL), _const_spec((1, D_MODEL)), _const_spec((QKVZ_W, D_MODEL)),
                  _const_spec((2 * SUBLANES, D_MODEL)), _const_spec((REST_W, D_MODEL)), _const_spec((CONV_W, QKV_W)),
                  _const_spec((SUBLANES, bb * tl)), _const_spec((SUBLANES, bb * tl)), _const_spec((1, LANES)),
                  _state_spec(bb, CONV_W - 1, QKV_W, rows_first)],
        out_specs=[_seq_spec(bb, tl, QKV_W), _seq_spec(bb, tl, REST_OUT_W), _gate_rows_spec(bb, tl, n_tiles),
                   _state_spec(bb, CONV_W - 1, QKV_W, rows_first)],
        out_shape=[jax.ShapeDtypeStruct((bsz, l, QKV_W), act_dtype), sds(bsz, l, REST_OUT_W),
                   sds(N_GATE_ROWS, SUBLANES, bsz * l),
                   _state_shape(bsz, CONV_W - 1, QKV_W, rows_first)],
        scratch_shapes=[pltpu.VMEM((bb, CONV_PAD + tl, QKV_W), F32)],
        compiler_params=pltpu.CompilerParams(dimension_semantics=("arbitrary", "arbitrary"),
                                             vmem_limit_bytes=VMEM_LIMIT),
        name="front",
    )(x, g_attn, wq, wba, wr, w_conv, rows8(a_log), rows8(dt_bias), jnp.ones((1, LANES), jnp.int32), conv_prev)


def _unit_lower_inverses(neg_ms, chunk, side_jobs=()):
    nb = min(chunk, SERIES_BLOCK)
    n_blocks = GROUP // nb
    n_factors = int(math.log2(nb))
    assert chunk == nb or (chunk == 2 * nb and n_blocks == 2), "chunks are one or two series blocks"
    lane = lax.broadcasted_iota(jnp.int32, (nb, GROUP), 1)
    lane_block = lane // nb

    def packed(m):
        out = m[0:nb]
        for b in range(1, n_blocks):
            out = jnp.where(lane_block == b, m[b * nb:(b + 1) * nb], out)
        return out

    def block_diag(p):
        return jnp.concatenate([jnp.where(lane_block == b, p, 0.0) for b in range(n_blocks)], axis=0)

    side_jobs = list(side_jobs)
    per_step = -(-len(side_jobs) // max(n_factors - 1, 1))

    def issue_side_jobs(step_values):
        for k in range(min(per_step, len(side_jobs))):
            side_jobs.pop(0)(step_values[k % len(step_values)])

    nps = [packed(m) for m in neg_ms]
    eye_p = jnp.where(lane % nb == lax.broadcasted_iota(jnp.int32, (nb, GROUP), 0), 1.0, 0.0).astype(F32)
    invs = [eye_p + n for n in nps]
    if n_factors > 1:
        pws = [_dot(n, block_diag(n)) for n in nps]
        issue_side_jobs(pws)
        for _ in range(n_factors - 2):
            boths = [_dot(jnp.concatenate([inv, pw], axis=0), block_diag(pw)) for inv, pw in zip(invs, pws)]
            invs = [inv + both[0:nb] for inv, both in zip(invs, boths)]
            pws = [both[nb:] for both in boths]
            issue_side_jobs(pws)
        invs = [inv + _dot(inv, block_diag(pw)) for inv, pw in zip(invs, pws)]
    while side_jobs:
        side_jobs.pop(0)(None)
    if chunk == nb:
        return [block_diag(inv) for inv in invs]
    zeros = jnp.zeros((nb, GROUP), F32)
    a_invs = [jnp.where(lane < nb, inv, 0.0) for inv in invs]
    neg_ls = [jnp.where(lane < nb, m[nb:], 0.0) for m in neg_ms]
    xs = [_dot(neg_l, jnp.concatenate([a_inv, zeros], axis=0)) for neg_l, a_inv in zip(neg_ls, a_invs)]
    ys = [_dot(inv, jnp.concatenate([zeros, x], axis=0)) for inv, x in zip(invs, xs)]
    return [jnp.concatenate([a_inv, y + jnp.where(lane >= nb, inv, 0.0)], axis=0)
            for a_inv, y, inv in zip(a_invs, ys, invs)]


def _delta_kernel(*refs, bb, tl, chunk, pos0, n_tiles, n_gate_refs, rows_first):
    gsc_refs, refs = refs[:n_gate_refs], refs[n_gate_refs:]
    (qkv_ref, zp_ref, poolprev_ref, s0_ref, wonorm_ref, wmix_ref, pscale_ref, keep_ref,
     oab_ref, poolnew_ref, snew_ref, pext_ref) = refs
    tile = pl.program_id(1)
    rows_b = min(tl, GROUP)
    seqs_g = GROUP // rows_b
    groups_b = tl // rows_b
    n_groups = bb * tl // GROUP
    chained = chunk == GROUP
    assert chained or (chunk == tl and tl < GROUP), "chunk must be a whole group or a whole short sequence"
    pool_hist = slice(POOL_PAD - POOL_HIST, POOL_PAD)

    @pl.when(tile == 0)
    def _():
        snew_ref[...] = s0_ref[...]
        _history_to_scratch(pext_ref, pool_hist.start, poolprev_ref, rows_first)
        pext_ref[:, POOL_PAD - POOL_LOOKBACK:POOL_PAD - POOL_HIST, :] = jnp.zeros((bb, 1, WIDTH_B), F32)

    pext_ref[:, POOL_PAD:POOL_PAD + tl, :] = zp_ref[:, :, QK_W:QK_W + WIDTH_B]
    _history_from_scratch(poolnew_ref, pext_ref, POOL_PAD + tl - POOL_HIST, rows_first)

    row = lax.broadcasted_iota(jnp.int32, (GROUP, GROUP), 0)
    col = lax.broadcasted_iota(jnp.int32, (GROUP, GROUP), 1)
    causal = row >= col
    strict = row > col
    if not chained:
        same = (row // chunk) == (col // chunk)
        causal = causal & same
        strict = strict & same

    def origin(g):
        if tl >= GROUP:
            return g // groups_b, (g % groups_b) * GROUP
        return g * seqs_g, 0

    def load(c0, g):
        b0, t0 = origin(g)
        return qkv_ref[b0:b0 + seqs_g, t0:t0 + rows_b, c0:c0 + HEAD_DIM].astype(F32).reshape(GROUP, HEAD_DIM)

    def gate_rows(g):
        if n_gate_refs > 1:
            b0, t0 = origin(g)
            return tuple(gsc_refs[b0][q, :, t0:t0 + GROUP] for q in range(N_GATE_ROWS))
        return tuple(gsc_refs[0][q, :, g * GROUP:(g + 1) * GROUP] for q in range(N_GATE_ROWS))

    gates = [gate_rows(g) for g in range(n_groups)]

    probs = [(g, h) for g in range(n_groups) for h in range(N_HEADS)]
    st = []
    for g, h in probs:
        beta8, g8, eg8, kds8 = gates[g]
        d = {"q": load(h * HEAD_DIM, g), "k": load(QK_W + h * HEAD_DIM, g), "v": load(2 * QK_W + h * HEAD_DIM, g)}
        d["beta_row"] = beta8[h:h + 1, :]
        d["eg_row"] = eg8[N_HEADS + h:N_HEADS + h + 1, :]
        d["g_row"] = g8[N_HEADS + h:N_HEADS + h + 1, :]
        d["kb_row"] = kds8[N_HEADS + h:N_HEADS + h + 1, :] * d["beta_row"]
        d["g_col"] = jnp.broadcast_to(d["g_row"], (GROUP, GROUP)).T
        d["kt"] = d["k"].T
        st.append(d)
    for d in st:
        both = _dot(jnp.concatenate([d["k"], d["q"]], axis=0), d["kt"])
        d["kk"], d["qk"] = both[0:GROUP], both[GROUP:]
    for d in st:
        decay = jnp.exp(jnp.where(causal, d["g_col"] - d["g_row"], -jnp.inf))
        d["neg_m"] = jnp.where(strict, -(d.pop("kk") * decay), 0.0) * d["beta_row"]
        d["qkm"] = d.pop("qk") * decay * d["beta_row"]
        d["q_dec"] = d.pop("q") * jnp.exp(d["g_col"])
        d["kt_dec"] = d.pop("kt") * d["kb_row"]

    keep = keep_ref[...] != 0

    def pool_block(g, gi, issued_with):
        b0, t0 = origin(g)
        win = POOL_WINDOWS[gi]
        cols = slice(gi * POOL_GROUP, (gi + 1) * POOL_GROUP)
        pos = pos0 + tile * tl + t0 + row % rows_b
        slab = pext_ref[b0:b0 + seqs_g, POOL_PAD - POOL_LOOKBACK + t0:POOL_PAD + t0 + rows_b, cols]
        acc = slab.reshape(seqs_g * (POOL_LOOKBACK + rows_b), POOL_GROUP)
        shift = 1
        while shift < win:
            acc = acc + pltpu.roll(acc, shift, axis=0)
            shift *= 2
        acc = acc.reshape(seqs_g, POOL_LOOKBACK + rows_b, POOL_GROUP)[:, POOL_LOOKBACK:, :]
        cur = slab[:, POOL_LOOKBACK:, :]
        pooled = (acc / jnp.minimum(pos + 1, win).astype(F32).reshape(seqs_g, rows_b, POOL_GROUP) - cur)
        mixed = _dot(pooled.reshape(GROUP, POOL_GROUP), wmix_ref[gi]) * pscale_ref[:, cols]
        if issued_with is not None:
            reps = GROUP // issued_with.shape[0]
            mixed = jnp.where(keep, mixed, jnp.concatenate([issued_with] * reps, axis=0))
        oab_ref[b0:b0 + seqs_g, t0:t0 + rows_b, QK_W + gi * POOL_GROUP:QK_W + (gi + 1) * POOL_GROUP] = (
            mixed.reshape(seqs_g, rows_b, POOL_GROUP).astype(oab_ref.dtype))

    pool_jobs = [functools.partial(pool_block, g, gi) for g in range(n_groups) for gi in range(len(POOL_WINDOWS))]
    invs = _unit_lower_inverses([d.pop("neg_m") for d in st], chunk, pool_jobs)
    for d, inv in zip(st, invs):
        d["uy"] = _dot(inv, d.pop("v"))
        d["wy"] = _dot(inv * d["eg_row"], d.pop("k"))

    def group_slab(ref, g, cols, row_off=0):
        b0, t0 = origin(g)
        return ref.at[b0:b0 + seqs_g, row_off + t0:row_off + t0 + rows_b, cols]

    def head_out(g, h, o):
        cols = slice(h * HEAD_DIM, (h + 1) * HEAD_DIM)
        z = group_slab(zp_ref, g, cols)[...].reshape(GROUP, HEAD_DIM)
        group_slab(oab_ref, g, cols)[...] = (o * _rms_scale(o) * wonorm_ref[...] * _silu(z)).reshape(
            seqs_g, rows_b, HEAD_DIM).astype(oab_ref.dtype)

    if chained:
        for j in range(groups_b):
            wave = [(i, g, h) for i, (g, h) in enumerate(probs) if g % groups_b == j]
            s_old = {i: snew_ref[origin(g)[0], h] for i, g, h in wave}
            ws = {i: _dot(jnp.concatenate([st[i]["wy"], st[i]["q_dec"]], axis=0), s_old[i]) for i, g, h in wave}
            ys = {i: st[i]["uy"] - ws[i][0:GROUP] for i, g, h in wave}
            outs = {i: ws[i][GROUP:] + _dot(st[i]["qkm"], ys[i]) for i, g, h in wave}
            for i, g, h in wave:
                last = jnp.exp(st[i]["g_col"][GROUP - 1:GROUP, :])
                snew_ref[origin(g)[0], h] = s_old[i] * last + _dot(st[i]["kt_dec"], ys[i])
            for i, g, h in wave:
                head_out(g, h, outs[i])
    else:
        outs = {}
        for i, (g, h) in enumerate(probs):
            d, b0 = st[i], origin(g)[0]
            ws_w, ws_q = [], []
            for s_i in range(seqs_g):
                r = slice(s_i * rows_b, (s_i + 1) * rows_b)
                ws = _dot(jnp.concatenate([d["wy"][r], d["q_dec"][r]], axis=0), snew_ref[b0 + s_i, h])
                ws_w.append(ws[0:rows_b])
                ws_q.append(ws[rows_b:])
            d["y"] = d["uy"] - jnp.concatenate(ws_w, axis=0)
            outs[i] = jnp.concatenate(ws_q, axis=0) + _dot(d["qkm"], d["y"])
        for i, (g, h) in enumerate(probs):
            d, b0 = st[i], origin(g)[0]
            for s_i in range(seqs_g):
                last = jnp.exp(d["g_col"][(s_i + 1) * rows_b - 1:(s_i + 1) * rows_b, :])
                upd = _dot(jnp.where(col // rows_b == s_i, d["kt_dec"], 0.0), d["y"])
                snew_ref[b0 + s_i, h] = snew_ref[b0 + s_i, h] * last + upd
        for i, (g, h) in enumerate(probs):
            head_out(g, h, outs[i])

    if n_tiles > 1:
        _history_to_scratch(pext_ref, pool_hist.start, poolnew_ref, rows_first)


def _delta(qkv, rest, gsc, pool_prev, s0, w_onorm, w_mix, pool_scale, *, bb, tl, chunk, pos0, act_dtype,
           rows_first):
    bsz, l = qkv.shape[0], qkv.shape[1]
    n_tiles = l // tl
    state_s = pl.BlockSpec((bb, N_HEADS, HEAD_DIM, HEAD_DIM), lambda i, j: (i, 0, 0, 0))
    if n_tiles == 1:
        gate_specs = [_gate_rows_spec(bb, tl, 1)]
    else:
        gate_specs = [pl.BlockSpec((N_GATE_ROWS, SUBLANES, tl), lambda i, j, k=k: (0, 0, (i * bb + k) * n_tiles + j))
                      for k in range(bb)]
    kern = functools.partial(_delta_kernel, bb=bb, tl=tl, chunk=chunk, pos0=pos0, n_tiles=n_tiles,
                             n_gate_refs=len(gate_specs), rows_first=rows_first)
    sds = lambda *shape: jax.ShapeDtypeStruct(shape, F32)
    return pl.pallas_call(
        kern,
        grid=(bsz // bb, n_tiles),
        in_specs=gate_specs + [
                  _seq_spec(bb, tl, QKV_W),
                  pl.BlockSpec((bb, tl, QK_W + WIDTH_B), lambda i, j: (i, j, REST_Z0 // (QK_W + WIDTH_B))),
                  _state_spec(bb, POOL_HIST, WIDTH_B, rows_first), state_s,
                  _const_spec((1, HEAD_DIM)), _const_spec((len(POOL_WINDOWS), POOL_GROUP, POOL_GROUP)),
                  _const_spec((1, WIDTH_B)), _const_spec((1, LANES))],
        out_specs=[_seq_spec(bb, tl, QK_W + WIDTH_B),
                   _state_spec(bb, POOL_HIST, WIDTH_B, rows_first), state_s],
        out_shape=[jax.ShapeDtypeStruct((bsz, l, QK_W + WIDTH_B), act_dtype),
                   _state_shape(bsz, POOL_HIST, WIDTH_B, rows_first),
                   sds(bsz, N_HEADS, HEAD_DIM, HEAD_DIM)],
        scratch_shapes=[pltpu.VMEM((bb, POOL_PAD + tl, WIDTH_B), F32)],
        compiler_params=pltpu.CompilerParams(dimension_semantics=("arbitrary", "arbitrary"),
                                             vmem_limit_bytes=VMEM_LIMIT),
        name="delta",
    )(*([gsc] * len(gate_specs)), qkv, rest, pool_prev, s0, w_onorm, w_mix, pool_scale,
      jnp.ones((1, LANES), jnp.int32))


def _mlp_kernel(x_ref, oab_ref, gate_ref, wa_ref, wb_ref, wo_ref, gmlp_ref, wup_ref, wdown_ref, gfin_ref, y_ref):
    ma = _dot(oab_ref[:, 0:QK_W], wa_ref[...])
    mb = _dot(oab_ref[:, QK_W:QK_W + WIDTH_B], wb_ref[...])
    merged = _sigmoid(gate_ref[:, 0:D_MODEL]) * ma + _sigmoid(gate_ref[:, D_MODEL:]) * mb
    x1 = x_ref[...] + _dot(merged, wo_ref[...])
    h2 = (x1 * _rms_scale(x1) * gmlp_ref[...]).astype(BF16)
    acc = x1
    for c0 in range(0, D_FF, FF_BLOCK):
        up = jnp.dot(h2, wup_ref[:, c0:c0 + FF_BLOCK], preferred_element_type=F32)
        act = jnp.square(jnp.maximum(up, 0.0))
        acc = acc + _dot(act, wdown_ref[c0:c0 + FF_BLOCK, :])
    y_ref[...] = acc * _rms_scale(acc) * gfin_ref[...]


def _merge_mlp(x2d, oab, rest, prm, tm):
    t = x2d.shape[0]
    row = lambda w: pl.BlockSpec((tm, w), lambda i: (i, 0))
    return pl.pallas_call(
        _mlp_kernel,
        grid=(t // tm,),
        in_specs=[row(D_MODEL), row(QK_W + WIDTH_B), row(2 * D_MODEL),
                  _const_spec((QK_W, D_MODEL)), _const_spec((WIDTH_B, D_MODEL)), _const_spec((D_MODEL, D_MODEL)),
                  _const_spec((1, D_MODEL)), _const_spec((D_MODEL, D_FF)), _const_spec((D_FF, D_MODEL)),
                  _const_spec((1, D_MODEL))],
        out_specs=row(D_MODEL),
        out_shape=jax.ShapeDtypeStruct((t, D_MODEL), F32),
        compiler_params=pltpu.CompilerParams(dimension_semantics=("arbitrary",),
                                             vmem_limit_bytes=VMEM_LIMIT),
        name="merge_mlp",
    )(x2d, oab, rest, prm["w_a_out"], prm["w_b_out"], prm["w_o"], prm["g_mlp"], prm["w_up"], prm["w_down"],
      prm["g_final"])


def _trunk(x, conv_prev, pool_prev, s_prev, pos0, prm, *, front_blk, delta_blk, tm, chunk, act_dtype, rows_first):
    bsz, l, _ = x.shape
    t = bsz * l
    qkv, rest, gsc, conv_new = _front(
        x, prm["g_attn"], prm["wq"], prm["wba"], prm["wr"], prm["w_conv"], prm["a_log"], prm["dt_bias"],
        conv_prev, bb=front_blk[0], tl=front_blk[1], chunk=chunk, act_dtype=act_dtype, rows_first=rows_first)
    oab, pool_new, s_new = _delta(qkv, rest, gsc, pool_prev, s_prev, prm["w_onorm"], prm["w_mix"],
                                  prm["pool_scale"], bb=delta_blk[0], tl=delta_blk[1], chunk=chunk, pos0=pos0,
                                  act_dtype=act_dtype, rows_first=rows_first)
    y = _merge_mlp(x.reshape(t, D_MODEL), oab.reshape(t, QK_W + WIDTH_B), rest.reshape(t, REST_OUT_W), prm, tm)
    return y.reshape(bsz, l, D_MODEL), conv_new, pool_new, s_new[None]


def kernel(x_prompt, x_sample, state_conv, state_pool, state_ssm, w_in, w_conv, a_log, dt_bias, w_onorm,
           w_pool_mix, pool_scale, w_a_out, w_b_out, w_o, g_attn, g_mlp, w_up, w_down, g_final):
    assert w_in.shape[0] == 1, "single-layer decoder"
    wt = jnp.transpose(w_in[0])
    rest_off = QKVZ_W + GATE_SCALARS
    prm = {
        "wq": wt[0:QKVZ_W].astype(BF16),
        "wba": jnp.pad(wt[QKVZ_W:rest_off], ((0, 2 * SUBLANES - GATE_SCALARS), (0, 0))).astype(BF16),
        "wr": wt[rest_off:rest_off + REST_W].astype(BF16),
        "g_attn": g_attn[0][None, :], "g_mlp": g_mlp[0][None, :], "g_final": g_final[None, :],
        "w_conv": w_conv[0].astype(F32),
        "a_log": a_log[0], "dt_bias": dt_bias[0],
        "w_onorm": w_onorm[0][None, :].astype(F32),
        "w_mix": w_pool_mix[0].astype(BF16), "pool_scale": pool_scale[0][None, :].astype(F32),
        "w_a_out": w_a_out[0].astype(BF16), "w_b_out": w_b_out[0].astype(BF16), "w_o": w_o[0].astype(BF16),
        "w_up": w_up[0].astype(BF16), "w_down": w_down[0].astype(BF16),
    }
    bp = x_prompt.shape[0]
    y_p, conv_p, pool_p, ssm_p = _trunk(
        x_prompt, jnp.zeros((bp, CONV_W - 1, QKV_W), F32), jnp.zeros((bp, POOL_HIST, WIDTH_B), F32),
        jnp.zeros((bp, N_HEADS, HEAD_DIM, HEAD_DIM), F32), 0, prm,
        front_blk=(1, 512), delta_blk=(4, 256), tm=512, chunk=GROUP, act_dtype=BF16, rows_first=False)
    conv_p, pool_p = conv_p[None], pool_p[None]
    dec_len = x_sample.shape[1]
    rows_major = lambda s: jnp.transpose(s[0].astype(F32), (1, 0, 2))
    y_s, conv_s, pool_s, ssm_s = _trunk(
        x_sample, rows_major(state_conv), rows_major(state_pool), state_ssm[0].astype(F32), PAST_LEN, prm,
        front_blk=(256 // dec_len, dec_len), delta_blk=(2 * GROUP // dec_len, dec_len), tm=512,
        chunk=dec_len, act_dtype=F32, rows_first=True)
    conv_s, pool_s = jnp.transpose(conv_s, (1, 0, 2))[None], jnp.transpose(pool_s, (1, 0, 2))[None]
    return (y_p, y_s, conv_p.astype(state_conv.dtype), pool_p.astype(state_pool.dtype),
            ssm_p.astype(state_ssm.dtype), conv_s.astype(state_conv.dtype), pool_s.astype(state_pool.dtype),
            ssm_s.astype(state_ssm.dtype))
```

```python
import functools
import math

import jax
import jax.numpy as jnp
from jax import lax
from jax.experimental import pallas as pl
from jax.experimental.pallas import tpu as pltpu

D_MODEL = 1024
N_HEADS = 4
HEAD_DIM = 128
QK_W = N_HEADS * HEAD_DIM
QKV_W = 3 * QK_W
CONV_W = 4
POOL_WINDOWS = (2, 4, 8, 16)
POOL_GROUP = 128
WIDTH_B = len(POOL_WINDOWS) * POOL_GROUP
POOL_HIST = 15
D_FF = 4 * D_MODEL
EPS = 1e-6
PAST_LEN = 16384
LANES = 128
SUBLANES = 8
MXU_COLS = 256
N_GATE_ROWS = 4

QKVZ_W = QKV_W + QK_W
GATE_SCALARS = 2 * N_HEADS
REST_W = WIDTH_B + 2 * D_MODEL
REST_Z0 = 2 * D_MODEL
REST_P0 = REST_Z0 + QK_W
REST_OUT_W = REST_P0 + WIDTH_B

GROUP = 128
SERIES_BLOCK = 64
CONV_PAD = 8
POOL_LOOKBACK = 16
POOL_PAD = 24
FF_BLOCK = 1024

VMEM_LIMIT = 56 * 1024 * 1024

BF16 = jnp.bfloat16
F32 = jnp.float32


def _dot(a, b):
    return jnp.dot(a.astype(BF16), b.astype(BF16), preferred_element_type=F32)


def _sigmoid(x):
    return 1.0 / (1.0 + jnp.exp(-x))


def _silu(x):
    half = 0.5 * x
    return half * jnp.tanh(half) + half


def _rms_scale(x):
    return lax.rsqrt(jnp.mean(x * x, axis=-1, keepdims=True) + EPS)


def _const_spec(shape):
    zeros = (0,) * len(shape)
    return pl.BlockSpec(shape, lambda *_: zeros, pipeline_mode=pl.Buffered(1))


def _seq_spec(bb, tl, width):
    return pl.BlockSpec((bb, tl, width), lambda i, j: (i, j, 0))


def _state_spec(bb, n_rows, width, rows_first):
    if rows_first:
        return pl.BlockSpec((n_rows, bb, width), lambda i, j: (0, i, 0))
    return pl.BlockSpec((bb, n_rows, width), lambda i, j: (i, 0, 0))


def _state_shape(bsz, n_rows, width, rows_first):
    return jax.ShapeDtypeStruct((n_rows, bsz, width) if rows_first else (bsz, n_rows, width), F32)


def _history_to_scratch(scratch_ref, first_row, state_ref, rows_first):
    if rows_first:
        for j in range(state_ref.shape[0]):
            scratch_ref[:, first_row + j, :] = state_ref[j]
    else:
        scratch_ref[:, first_row:first_row + state_ref.shape[1], :] = state_ref[...]


def _history_from_scratch(state_ref, scratch_ref, first_row, rows_first):
    if rows_first:
        for j in range(state_ref.shape[0]):
            state_ref[j] = scratch_ref[:, first_row + j, :]
    else:
        state_ref[...] = scratch_ref[:, first_row:first_row + state_ref.shape[1], :]


def _gate_rows_spec(bb, tl, n_tiles):
    return pl.BlockSpec((N_GATE_ROWS, SUBLANES, bb * tl), lambda i, j: (0, 0, i * n_tiles + j))


def _lane_prefix_sum(x, chunk):
    lane = lax.broadcasted_iota(jnp.int32, x.shape, 1)
    shift = 1
    while shift < chunk:
        x = x + jnp.where(lane % chunk >= shift, pltpu.roll(x, shift, axis=1), 0.0)
        shift *= 2
    return x


def _lane_suffix_sum(x, chunk):
    lane = lax.broadcasted_iota(jnp.int32, x.shape, 1)
    shift = 1
    while shift < chunk:
        x = x + jnp.where(lane % chunk + shift < chunk, pltpu.roll(x, x.shape[1] - shift, axis=1), 0.0)
        shift *= 2
    return x


def _front_kernel(x_ref, g_ref, wq_ref, wba_ref, wr_ref, wconv_ref, alog_ref, dtb_ref, keep_ref, convprev_ref,
                  qkv_ref, rest_ref, gsc_ref, convnew_ref,
                  ext_ref, *, bb, tl, n_tiles, chunk, rows_first):
    tile = pl.program_id(1)
    rows = bb * tl
    hist = slice(CONV_PAD - (CONV_W - 1), CONV_PAD)

    @pl.when(tile == 0)
    def _():
        _history_to_scratch(ext_ref, hist.start, convprev_ref, rows_first)
        ext_ref[:, 0:CONV_PAD - (CONV_W - 1), :] = jnp.zeros((bb, CONV_PAD - (CONV_W - 1), QKV_W), F32)

    x = x_ref[...].reshape(rows, D_MODEL)
    normed = (x * _rms_scale(x) * g_ref[...]).astype(BF16)
    contract_last = (((1,), (1,)), ((), ()))
    proj = lambda w_rows: lax.dot_general(normed, w_rows, contract_last, preferred_element_type=F32)
    ext_ref[:, CONV_PAD:CONV_PAD + tl, :] = proj(wq_ref[0:QKV_W, :]).reshape(bb, tl, QKV_W)
    _history_from_scratch(convnew_ref, ext_ref, CONV_PAD + tl - (CONV_W - 1), rows_first)

    x8 = lax.dot_general(wba_ref[...], normed, contract_last, preferred_element_type=F32)[0:SUBLANES, :]
    xs = x8 + dtb_ref[...]
    softplus = jnp.maximum(xs, 0.0) + jnp.log1p(jnp.exp(-jnp.abs(xs)))
    graw8 = -jnp.exp(alog_ref[...]) * softplus
    g8 = _lane_prefix_sum(graw8, chunk)
    gsc_ref[0] = _sigmoid(x8)
    gsc_ref[1] = g8
    gsc_ref[2] = jnp.exp(g8)
    gsc_ref[3] = jnp.exp(_lane_suffix_sum(graw8, chunk) - graw8)

    keep = keep_ref[...] != 0

    def conv_block(c0, dep):
        cols = slice(c0, c0 + LANES)
        xe = ext_ref[:, :, cols].reshape(bb * (CONV_PAD + tl), LANES)
        acc = wconv_ref[0:1, cols] * xe
        for j in range(1, CONV_W):
            acc = wconv_ref[j:j + 1, cols] * xe + pltpu.roll(acc, 1, axis=0)
        val = _silu(acc.reshape(bb, CONV_PAD + tl, LANES)[:, CONV_PAD:, :])
        if c0 < 2 * QK_W:
            scale = HEAD_DIM ** -0.5 if c0 < QK_W else 1.0
            val = val * (lax.rsqrt(jnp.sum(val * val, axis=-1, keepdims=True) + EPS) * scale)
        qkv_ref[:, :, cols] = jnp.where(keep, val, dep[:, 0:LANES].reshape(bb, tl, LANES)).astype(qkv_ref.dtype)

    def proj_block(out0, w_ref, w0, c0):
        val = proj(w_ref[w0 + c0:w0 + c0 + MXU_COLS, :])
        rest_ref[:, :, out0 + c0:out0 + c0 + MXU_COLS] = val.reshape(bb, tl, MXU_COLS)
        return val

    mxu_work = ([functools.partial(proj_block, REST_Z0, wq_ref, QKV_W, c0) for c0 in range(0, QK_W, MXU_COLS)]
                + [functools.partial(proj_block, REST_P0, wr_ref, 0, c0) for c0 in range(0, WIDTH_B, MXU_COLS)]
                + [functools.partial(proj_block, 0, wr_ref, WIDTH_B, c0)
                   for c0 in range(0, 2 * D_MODEL, MXU_COLS)])
    for i, c0 in enumerate(range(0, QKV_W, LANES)):
        conv_block(c0, mxu_work[i]())

    if n_tiles > 1:
        _history_to_scratch(ext_ref, hist.start, convnew_ref, rows_first)


def _front(x, g_attn, wq, wba, wr, w_conv, a_log, dt_bias, conv_prev, *, bb, tl, chunk, act_dtype, rows_first):
    bsz, l, _ = x.shape
    n_tiles = l // tl
    kern = functools.partial(_front_kernel, bb=bb, tl=tl, n_tiles=n_tiles, chunk=chunk, rows_first=rows_first)
    sds = lambda *shape: jax.ShapeDtypeStruct(shape, F32)
    zeros4 = jnp.zeros((N_HEADS,), F32)
    rows8 = lambda v: jnp.broadcast_to(jnp.concatenate([zeros4, v.astype(F32)])[:, None], (SUBLANES, bb * tl))
    return pl.pallas_call(
        kern,
        grid=(bsz // bb, n_tiles),
        in_specs=[_seq_spec(bb, tl, D_MODEL), _const_spec((1, D_MODEL)), _const_spec((QKVZ_W, D_MODEL)),
                  _const_spec((2 * SUBLANES, D_MODEL)), _const_spec((REST_W, D_MODEL)), _const_spec((CONV_W, QKV_W)),
                  _const_spec((SUBLANES, bb * tl)), _const_spec((SUBLANES, bb * tl)), _const_spec((1, LANES)),
                  _state_spec(bb, CONV_W - 1, QKV_W, rows_first)],
        out_specs=[_seq_spec(bb, tl, QKV_W), _seq_spec(bb, tl, REST_OUT_W), _gate_rows_spec(bb, tl, n_tiles),
                   _state_spec(bb, CONV_W - 1, QKV_W, rows_first)],
        out_shape=[jax.ShapeDtypeStruct((bsz, l, QKV_W), act_dtype), sds(bsz, l, REST_OUT_W),
                   sds(N_GATE_ROWS, SUBLANES, bsz * l),
                   _state_shape(bsz, CONV_W - 1, QKV_W, rows_first)],
        scratch_shapes=[pltpu.VMEM((bb, CONV_PAD + tl, QKV_W), F32)],
        compiler_params=pltpu.CompilerParams(dimension_semantics=("arbitrary", "arbitrary"),
                                             vmem_limit_bytes=VMEM_LIMIT),
        name="front",
    )(x, g_attn, wq, wba, wr, w_conv, rows8(a_log), rows8(dt_bias), jnp.ones((1, LANES), jnp.int32), conv_prev)


def _unit_lower_inverses(neg_ms, chunk, side_jobs=()):
    nb = min(chunk, SERIES_BLOCK)
    n_blocks = GROUP // nb
    n_factors = int(math.log2(nb))
    assert chunk == nb or (chunk == 2 * nb and n_blocks == 2), "chunks are one or two series blocks"
    lane = lax.broadcasted_iota(jnp.int32, (nb, GROUP), 1)
    lane_block = lane // nb

    def packed(m):
        out = m[0:nb]
        for b in range(1, n_blocks):
            out = jnp.where(lane_block == b, m[b * nb:(b + 1) * nb], out)
        return out

    def block_diag(p):
        return jnp.concatenate([jnp.where(lane_block == b, p, 0.0) for b in range(n_blocks)], axis=0)

    side_jobs = list(side_jobs)
    per_step = -(-len(side_jobs) // max(n_factors - 1, 1))

    def issue_side_jobs(step_values):
        for k in range(min(per_step, len(side_jobs))):
            side_jobs.pop(0)(step_values[k % len(step_values)])

    nps = [packed(m) for m in neg_ms]
    eye_p = jnp.where(lane % nb == lax.broadcasted_iota(jnp.int32, (nb, GROUP), 0), 1.0, 0.0).astype(F32)
    invs = [eye_p + n for n in nps]
    if n_factors > 1:
        pws = [_dot(n, block_diag(n)) for n in nps]
        issue_side_jobs(pws)
        for _ in range(n_factors - 2):
            boths = [_dot(jnp.concatenate([inv, pw], axis=0), block_diag(pw)) for inv, pw in zip(invs, pws)]
            invs = [inv + both[0:nb] for inv, both in zip(invs, boths)]
            pws = [both[nb:] for both in boths]
            issue_side_jobs(pws)
        invs = [inv + _dot(inv, block_diag(pw)) for inv, pw in zip(invs, pws)]
    while side_jobs:
        side_jobs.pop(0)(None)
    if chunk == nb:
        return [block_diag(inv) for inv in invs]
    zeros = jnp.zeros((nb, GROUP), F32)
    a_invs = [jnp.where(lane < nb, inv, 0.0) for inv in invs]
    neg_ls = [jnp.where(lane < nb, m[nb:], 0.0) for m in neg_ms]
    xs = [_dot(neg_l, jnp.concatenate([a_inv, zeros], axis=0)) for neg_l, a_inv in zip(neg_ls, a_invs)]
    ys = [_dot(inv, jnp.concatenate([zeros, x], axis=0)) for inv, x in zip(invs, xs)]
    return [jnp.concatenate([a_inv, y + jnp.where(lane >= nb, inv, 0.0)], axis=0)
            for a_inv, y, inv in zip(a_invs, ys, invs)]


def _delta_kernel(*refs, bb, tl, chunk, pos0, n_tiles, n_gate_refs, rows_first):
    gsc_refs, refs = refs[:n_gate_refs], refs[n_gate_refs:]
    (qkv_ref, zp_ref, poolprev_ref, s0_ref, wonorm_ref, wmix_ref, pscale_ref, keep_ref,
     oab_ref, poolnew_ref, snew_ref, pext_ref) = refs
    tile = pl.program_id(1)
    rows_b = min(tl, GROUP)
    seqs_g = GROUP // rows_b
    groups_b = tl // rows_b
    n_groups = bb * tl // GROUP
    chained = chunk == GROUP
    assert chained or (chunk == tl and tl < GROUP), "chunk must be a whole group or a whole short sequence"
    pool_hist = slice(POOL_PAD - POOL_HIST, POOL_PAD)

    @pl.when(tile == 0)
    def _():
        snew_ref[...] = s0_ref[...]
        _history_to_scratch(pext_ref, pool_hist.start, poolprev_ref, rows_first)
        pext_ref[:, POOL_PAD - POOL_LOOKBACK:POOL_PAD - POOL_HIST, :] = jnp.zeros((bb, 1, WIDTH_B), F32)

    pext_ref[:, POOL_PAD:POOL_PAD + tl, :] = zp_ref[:, :, QK_W:QK_W + WIDTH_B]
    _history_from_scratch(poolnew_ref, pext_ref, POOL_PAD + tl - POOL_HIST, rows_first)

    row = lax.broadcasted_iota(jnp.int32, (GROUP, GROUP), 0)
    col = lax.broadcasted_iota(jnp.int32, (GROUP, GROUP), 1)
    causal = row >= col
    strict = row > col
    if not chained:
        same = (row // chunk) == (col // chunk)
        causal = causal & same
        strict = strict & same

    def origin(g):
        if tl >= GROUP:
            return g // groups_b, (g % groups_b) * GROUP
        return g * seqs_g, 0

    def load(c0, g):
        b0, t0 = origin(g)
        return qkv_ref[b0:b0 + seqs_g, t0:t0 + rows_b, c0:c0 + HEAD_DIM].astype(F32).reshape(GROUP, HEAD_DIM)

    def gate_rows(g):
        if n_gate_refs > 1:
            b0, t0 = origin(g)
            return tuple(gsc_refs[b0][q, :, t0:t0 + GROUP] for q in range(N_GATE_ROWS))
        return tuple(gsc_refs[0][q, :, g * GROUP:(g + 1) * GROUP] for q in range(N_GATE_ROWS))

    gates = [gate_rows(g) for g in range(n_groups)]

    probs = [(g, h) for g in range(n_groups) for h in range(N_HEADS)]
    st = []
    for g, h in probs:
        beta8, g8, eg8, kds8 = gates[g]
        d = {"q": load(h * HEAD_DIM, g), "k": load(QK_W + h * HEAD_DIM, g), "v": load(2 * QK_W + h * HEAD_DIM, g)}
        d["beta_row"] = beta8[h:h + 1, :]
        d["eg_row"] = eg8[N_HEADS + h:N_HEADS + h + 1, :]
        d["g_row"] = g8[N_HEADS + h:N_HEADS + h + 1, :]
        d["kb_row"] = kds8[N_HEADS + h:N_HEADS + h + 1, :] * d["beta_row"]
        d["g_col"] = jnp.broadcast_to(d["g_row"], (GROUP, GROUP)).T
        d["kt"] = d["k"].T
        st.append(d)
    for d in st:
        both = _dot(jnp.concatenate([d["k"], d["q"]], axis=0), d["kt"])
        d["kk"], d["qk"] = both[0:GROUP], both[GROUP:]
    for d in st:
        decay = jnp.exp(jnp.where(causal, d["g_col"] - d["g_row"], -jnp.inf))
        d["neg_m"] = jnp.where(strict, -(d.pop("kk") * decay), 0.0) * d["beta_row"]
        d["qkm"] = d.pop("qk") * decay * d["beta_row"]
        d["q_dec"] = d.pop("q") * jnp.exp(d["g_col"])
        d["kt_dec"] = d.pop("kt") * d["kb_row"]

    keep = keep_ref[...] != 0

    def pool_block(g, gi, issued_with):
        b0, t0 = origin(g)
        win = POOL_WINDOWS[gi]
        cols = slice(gi * POOL_GROUP, (gi + 1) * POOL_GROUP)
        pos = pos0 + tile * tl + t0 + row % rows_b
        slab = pext_ref[b0:b0 + seqs_g, POOL_PAD - POOL_LOOKBACK + t0:POOL_PAD + t0 + rows_b, cols]
        acc = slab.reshape(seqs_g * (POOL_LOOKBACK + rows_b), POOL_GROUP)
        shift = 1
        while shift < win:
            acc = acc + pltpu.roll(acc, shift, axis=0)
            shift *= 2
        acc = acc.reshape(seqs_g, POOL_LOOKBACK + rows_b, POOL_GROUP)[:, POOL_LOOKBACK:, :]
        cur = slab[:, POOL_LOOKBACK:, :]
        pooled = (acc / jnp.minimum(pos + 1, win).astype(F32).reshape(seqs_g, rows_b, POOL_GROUP) - cur)
        mixed = _dot(pooled.reshape(GROUP, POOL_GROUP), wmix_ref[gi]) * pscale_ref[:, cols]
        if issued_with is not None:
            reps = GROUP // issued_with.shape[0]
            mixed = jnp.where(keep, mixed, jnp.concatenate([issued_with] * reps, axis=0))
        oab_ref[b0:b0 + seqs_g, t0:t0 + rows_b, QK_W + gi * POOL_GROUP:QK_W + (gi + 1) * POOL_GROUP] = (
            mixed.reshape(seqs_g, rows_b, POOL_GROUP).astype(oab_ref.dtype))

    pool_jobs = [functools.partial(pool_block, g, gi) for g in range(n_groups) for gi in range(len(POOL_WINDOWS))]
    invs = _unit_lower_inverses([d.pop("neg_m") for d in st], chunk, pool_jobs)
    for d, inv in zip(st, invs):
        d["uy"] = _dot(inv, d.pop("v"))
        d["wy"] = _dot(inv * d["eg_row"], d.pop("k"))

    def group_slab(ref, g, cols, row_off=0):
        b0, t0 = origin(g)
        return ref.at[b0:b0 + seqs_g, row_off + t0:row_off + t0 + rows_b, cols]

    def head_out(g, h, o):
        cols = slice(h * HEAD_DIM, (h + 1) * HEAD_DIM)
        z = group_slab(zp_ref, g, cols)[...].reshape(GROUP, HEAD_DIM)
        group_slab(oab_ref, g, cols)[...] = (o * _rms_scale(o) * wonorm_ref[...] * _silu(z)).reshape(
            seqs_g, rows_b, HEAD_DIM).astype(oab_ref.dtype)

    if chained:
        for j in range(groups_b):
            wave = [(i, g, h) for i, (g, h) in enumerate(probs) if g % groups_b == j]
            s_old = {i: snew_ref[origin(g)[0], h] for i, g, h in wave}
            ws = {i: _dot(jnp.concatenate([st[i]["wy"], st[i]["q_dec"]], axis=0), s_old[i]) for i, g, h in wave}
            ys = {i: st[i]["uy"] - ws[i][0:GROUP] for i, g, h in wave}
            outs = {i: ws[i][GROUP:] + _dot(st[i]["qkm"], ys[i]) for i, g, h in wave}
            for i, g, h in wave:
                last = jnp.exp(st[i]["g_col"][GROUP - 1:GROUP, :])
                snew_ref[origin(g)[0], h] = s_old[i] * last + _dot(st[i]["kt_dec"], ys[i])
            for i, g, h in wave:
                head_out(g, h, outs[i])
    else:
        outs = {}
        for i, (g, h) in enumerate(probs):
            d, b0 = st[i], origin(g)[0]
            ws_w, ws_q = [], []
            for s_i in range(seqs_g):
                r = slice(s_i * rows_b, (s_i + 1) * rows_b)
                ws = _dot(jnp.concatenate([d["wy"][r], d["q_dec"][r]], axis=0), snew_ref[b0 + s_i, h])
                ws_w.append(ws[0:rows_b])
                ws_q.append(ws[rows_b:])
            d["y"] = d["uy"] - jnp.concatenate(ws_w, axis=0)
            outs[i] = jnp.concatenate(ws_q, axis=0) + _dot(d["qkm"], d["y"])
        for i, (g, h) in enumerate(probs):
            d, b0 = st[i], origin(g)[0]
            for s_i in range(seqs_g):
                last = jnp.exp(d["g_col"][(s_i + 1) * rows_b - 1:(s_i + 1) * rows_b, :])
                upd = _dot(jnp.where(col // rows_b == s_i, d["kt_dec"], 0.0), d["y"])
                snew_ref[b0 + s_i, h] = snew_ref[b0 + s_i, h] * last + upd
        for i, (g, h) in enumerate(probs):
            head_out(g, h, outs[i])

    if n_tiles > 1:
        _history_to_scratch(pext_ref, pool_hist.start, poolnew_ref, rows_first)


def _delta(qkv, rest, gsc, pool_prev, s0, w_onorm, w_mix, pool_scale, *, bb, tl, chunk, pos0, act_dtype,
           rows_first):
    bsz, l = qkv.shape[0], qkv.shape[1]
    n_tiles = l // tl
    state_s = pl.BlockSpec((bb, N_HEADS, HEAD_DIM, HEAD_DIM), lambda i, j: (i, 0, 0, 0))
    if n_tiles == 1:
        gate_specs = [_gate_rows_spec(bb, tl, 1)]
    else:
        gate_specs = [pl.BlockSpec((N_GATE_ROWS, SUBLANES, tl), lambda i, j, k=k: (0, 0, (i * bb + k) * n_tiles + j))
                      for k in range(bb)]
    kern = functools.partial(_delta_kernel, bb=bb, tl=tl, chunk=chunk, pos0=pos0, n_tiles=n_tiles,
                             n_gate_refs=len(gate_specs), rows_first=rows_first)
    sds = lambda *shape: jax.ShapeDtypeStruct(shape, F32)
    return pl.pallas_call(
        kern,
        grid=(bsz // bb, n_tiles),
        in_specs=gate_specs + [
                  _seq_spec(bb, tl, QKV_W),
                  pl.BlockSpec((bb, tl, QK_W + WIDTH_B), lambda i, j: (i, j, REST_Z0 // (QK_W + WIDTH_B))),
                  _state_spec(bb, POOL_HIST, WIDTH_B, rows_first), state_s,
                  _const_spec((1, HEAD_DIM)), _const_spec((len(POOL_WINDOWS), POOL_GROUP, POOL_GROUP)),
                  _const_spec((1, WIDTH_B)), _const_spec((1, LANES))],
        out_specs=[_seq_spec(bb, tl, QK_W + WIDTH_B),
                   _state_spec(bb, POOL_HIST, WIDTH_B, rows_first), state_s],
        out_shape=[jax.ShapeDtypeStruct((bsz, l, QK_W + WIDTH_B), act_dtype),
                   _state_shape(bsz, POOL_HIST, WIDTH_B, rows_first),
                   sds(bsz, N_HEADS, HEAD_DIM, HEAD_DIM)],
        scratch_shapes=[pltpu.VMEM((bb, POOL_PAD + tl, WIDTH_B), F32)],
        compiler_params=pltpu.CompilerParams(dimension_semantics=("arbitrary", "arbitrary"),
                                             vmem_limit_bytes=VMEM_LIMIT),
        name="delta",
    )(*([gsc] * len(gate_specs)), qkv, rest, pool_prev, s0, w_onorm, w_mix, pool_scale,
      jnp.ones((1, LANES), jnp.int32))


def _mlp_kernel(x_ref, oab_ref, gate_ref, wa_ref, wb_ref, wo_ref, gmlp_ref, wup_ref, wdown_ref, gfin_ref, y_ref):
    ma = _dot(oab_ref[:, 0:QK_W], wa_ref[...])
    mb = _dot(oab_ref[:, QK_W:QK_W + WIDTH_B], wb_ref[...])
    merged = _sigmoid(gate_ref[:, 0:D_MODEL]) * ma + _sigmoid(gate_ref[:, D_MODEL:]) * mb
    x1 = x_ref[...] + _dot(merged, wo_ref[...])
    h2 = (x1 * _rms_scale(x1) * gmlp_ref[...]).astype(BF16)
    acc = x1
    for c0 in range(0, D_FF, FF_BLOCK):
        up = jnp.dot(h2, wup_ref[:, c0:c0 + FF_BLOCK], preferred_element_type=F32)
        act = jnp.square(jnp.maximum(up, 0.0))
        acc = acc + _dot(act, wdown_ref[c0:c0 + FF_BLOCK, :])
    y_ref[...] = acc * _rms_scale(acc) * gfin_ref[...]


def _merge_mlp(x2d, oab, rest, prm, tm):
    t = x2d.shape[0]
    row = lambda w: pl.BlockSpec((tm, w), lambda i: (i, 0))
    return pl.pallas_call(
        _mlp_kernel,
        grid=(t // tm,),
        in_specs=[row(D_MODEL), row(QK_W + WIDTH_B), row(2 * D_MODEL),
                  _const_spec((QK_W, D_MODEL)), _const_spec((WIDTH_B, D_MODEL)), _const_spec((D_MODEL, D_MODEL)),
                  _const_spec((1, D_MODEL)), _const_spec((D_MODEL, D_FF)), _const_spec((D_FF, D_MODEL)),
                  _const_spec((1, D_MODEL))],
        out_specs=row(D_MODEL),
        out_shape=jax.ShapeDtypeStruct((t, D_MODEL), F32),
        compiler_params=pltpu.CompilerParams(dimension_semantics=("arbitrary",),
                                             vmem_limit_bytes=VMEM_LIMIT),
        name="merge_mlp",
    )(x2d, oab, rest, prm["w_a_out"], prm["w_b_out"], prm["w_o"], prm["g_mlp"], prm["w_up"], prm["w_down"],
      prm["g_final"])


def _trunk(x, conv_prev, pool_prev, s_prev, pos0, prm, *, front_blk, delta_blk, tm, chunk, act_dtype, rows_first):
    bsz, l, _ = x.shape
    t = bsz * l
    qkv, rest, gsc, conv_new = _front(
        x, prm["g_attn"], prm["wq"], prm["wba"], prm["wr"], prm["w_conv"], prm["a_log"], prm["dt_bias"],
        conv_prev, bb=front_blk[0], tl=front_blk[1], chunk=chunk, act_dtype=act_dtype, rows_first=rows_first)
    oab, pool_new, s_new = _delta(qkv, rest, gsc, pool_prev, s_prev, prm["w_onorm"], prm["w_mix"],
                                  prm["pool_scale"], bb=delta_blk[0], tl=delta_blk[1], chunk=chunk, pos0=pos0,
                                  act_dtype=act_dtype, rows_first=rows_first)
    y = _merge_mlp(x.reshape(t, D_MODEL), oab.reshape(t, QK_W + WIDTH_B), rest.reshape(t, REST_OUT_W), prm, tm)
    return y.reshape(bsz, l, D_MODEL), conv_new, pool_new, s_new[None]


def kernel(x_prompt, x_sample, state_conv, state_pool, state_ssm, w_in, w_conv, a_log, dt_bias, w_onorm,
           w_pool_mix, pool_scale, w_a_out, w_b_out, w_o, g_attn, g_mlp, w_up, w_down, g_final):
    assert w_in.shape[0] == 1, "single-layer decoder"
    wt = jnp.transpose(w_in[0])
    rest_off = QKVZ_W + GATE_SCALARS
    prm = {
        "wq": wt[0:QKVZ_W].astype(BF16),
        "wba": jnp.pad(wt[QKVZ_W:rest_off], ((0, 2 * SUBLANES - GATE_SCALARS), (0, 0))).astype(BF16),
        "wr": wt[rest_off:rest_off + REST_W].astype(BF16),
        "g_attn": g_attn[0][None, :], "g_mlp": g_mlp[0][None, :], "g_final": g_final[None, :],
        "w_conv": w_conv[0].astype(F32),
        "a_log": a_log[0], "dt_bias": dt_bias[0],
        "w_onorm": w_onorm[0][None, :].astype(F32),
        "w_mix": w_pool_mix[0].astype(BF16), "pool_scale": pool_scale[0][None, :].astype(F32),
        "w_a_out": w_a_out[0].astype(BF16), "w_b_out": w_b_out[0].astype(BF16), "w_o": w_o[0].astype(BF16),
        "w_up": w_up[0].astype(BF16), "w_down": w_down[0].astype(BF16),
    }
    bp = x_prompt.shape[0]
    y_p, conv_p, pool_p, ssm_p = _trunk(
        x_prompt, jnp.zeros((bp, CONV_W - 1, QKV_W), F32), jnp.zeros((bp, POOL_HIST, WIDTH_B), F32),
        jnp.zeros((bp, N_HEADS, HEAD_DIM, HEAD_DIM), F32), 0, prm,
        front_blk=(1, 512), delta_blk=(4, 256), tm=512, chunk=GROUP, act_dtype=BF16, rows_first=False)
    conv_p, pool_p = conv_p[None], pool_p[None]
    dec_len = x_sample.shape[1]
    rows_major = lambda s: jnp.transpose(s[0].astype(F32), (1, 0, 2))
    y_s, conv_s, pool_s, ssm_s = _trunk(
        x_sample, rows_major(state_conv), rows_major(state_pool), state_ssm[0].astype(F32), PAST_LEN, prm,
        front_blk=(512 // dec_len, dec_len), delta_blk=(2 * GROUP // dec_len, dec_len), tm=256,
        chunk=dec_len, act_dtype=F32, rows_first=True)
    conv_s, pool_s = jnp.transpose(conv_s, (1, 0, 2))[None], jnp.transpose(pool_s, (1, 0, 2))[None]
    return (y_p, y_s, conv_p.astype(state_conv.dtype), pool_p.astype(state_pool.dtype),
            ssm_p.astype(state_ssm.dtype), conv_s.astype(state_conv.dtype), pool_s.astype(state_pool.dtype),
            ssm_s.astype(state_ssm.dtype))
```

```python
import functools
import math

import jax
import jax.numpy as jnp
from jax import lax
from jax.experimental import pallas as pl
from jax.experimental.pallas import tpu as pltpu

D_MODEL = 1024
N_HEADS = 4
HEAD_DIM = 128
QK_W = N_HEADS * HEAD_DIM
QKV_W = 3 * QK_W
CONV_W = 4
POOL_WINDOWS = (2, 4, 8, 16)
POOL_GROUP = 128
WIDTH_B = len(POOL_WINDOWS) * POOL_GROUP
POOL_HIST = 15
D_FF = 4 * D_MODEL
EPS = 1e-6
PAST_LEN = 16384
LANES = 128
SUBLANES = 8
MXU_COLS = 256
N_GATE_ROWS = 4

QKVZ_W = QKV_W + QK_W
GATE_SCALARS = 2 * N_HEADS
REST_W = WIDTH_B + 2 * D_MODEL
REST_Z0 = 2 * D_MODEL
REST_P0 = REST_Z0 + QK_W
REST_OUT_W = REST_P0 + WIDTH_B

GROUP = 128
SERIES_BLOCK = 64
CONV_PAD = 8
POOL_LOOKBACK = 16
POOL_PAD = 24
FF_BLOCK = 1024

VMEM_LIMIT = 56 * 1024 * 1024

BF16 = jnp.bfloat16
F32 = jnp.float32


def _dot(a, b):
    return jnp.dot(a.astype(BF16), b.astype(BF16), preferred_element_type=F32)


def _sigmoid(x):
    return 1.0 / (1.0 + jnp.exp(-x))


def _silu(x):
    half = 0.5 * x
    return half * jnp.tanh(half) + half


def _rms_scale(x):
    return lax.rsqrt(jnp.mean(x * x, axis=-1, keepdims=True) + EPS)


def _const_spec(shape):
    zeros = (0,) * len(shape)
    return pl.BlockSpec(shape, lambda *_: zeros, pipeline_mode=pl.Buffered(1))


def _seq_spec(bb, tl, width):
    return pl.BlockSpec((bb, tl, width), lambda i, j: (i, j, 0))


def _state_spec(bb, n_rows, width, rows_first):
    if rows_first:
        return pl.BlockSpec((n_rows, bb, width), lambda i, j: (0, i, 0))
    return pl.BlockSpec((bb, n_rows, width), lambda i, j: (i, 0, 0))


def _state_shape(bsz, n_rows, width, rows_first):
    return jax.ShapeDtypeStruct((n_rows, bsz, width) if rows_first else (bsz, n_rows, width), F32)


def _history_to_scratch(scratch_ref, first_row, state_ref, rows_first):
    if rows_first:
        for j in range(state_ref.shape[0]):
            scratch_ref[:, first_row + j, :] = state_ref[j]
    else:
        scratch_ref[:, first_row:first_row + state_ref.shape[1], :] = state_ref[...]


def _history_from_scratch(state_ref, scratch_ref, first_row, rows_first):
    if rows_first:
        for j in range(state_ref.shape[0]):
            state_ref[j] = scratch_ref[:, first_row + j, :]
    else:
        state_ref[...] = scratch_ref[:, first_row:first_row + state_ref.shape[1], :]


def _gate_rows_spec(bb, tl, n_tiles):
    return pl.BlockSpec((N_GATE_ROWS, SUBLANES, bb * tl), lambda i, j: (0, 0, i * n_tiles + j))


def _lane_prefix_sum(x, chunk):
    lane = lax.broadcasted_iota(jnp.int32, x.shape, 1)
    shift = 1
    while shift < chunk:
        x = x + jnp.where(lane % chunk >= shift, pltpu.roll(x, shift, axis=1), 0.0)
        shift *= 2
    return x


def _lane_suffix_sum(x, chunk):
    lane = lax.broadcasted_iota(jnp.int32, x.shape, 1)
    shift = 1
    while shift < chunk:
        x = x + jnp.where(lane % chunk + shift < chunk, pltpu.roll(x, x.shape[1] - shift, axis=1), 0.0)
        shift *= 2
    return x


def _front_kernel(x_ref, g_ref, wq_ref, wba_ref, wr_ref, wconv_ref, alog_ref, dtb_ref, keep_ref, convprev_ref,
                  qkv_ref, rest_ref, gsc_ref, convnew_ref,
                  ext_ref, *, bb, tl, n_tiles, chunk, rows_first):
    tile = pl.program_id(1)
    rows = bb * tl
    hist = slice(CONV_PAD - (CONV_W - 1), CONV_PAD)

    @pl.when(tile == 0)
    def _():
        _history_to_scratch(ext_ref, hist.start, convprev_ref, rows_first)
        ext_ref[:, 0:CONV_PAD - (CONV_W - 1), :] = jnp.zeros((bb, CONV_PAD - (CONV_W - 1), QKV_W), F32)

    x = x_ref[...].reshape(rows, D_MODEL)
    normed = (x * _rms_scale(x) * g_ref[...]).astype(BF16)
    contract_last = (((1,), (1,)), ((), ()))
    proj = lambda w_rows: lax.dot_general(normed, w_rows, contract_last, preferred_element_type=F32)
    ext_ref[:, CONV_PAD:CONV_PAD + tl, :] = proj(wq_ref[0:QKV_W, :]).reshape(bb, tl, QKV_W)
    _history_from_scratch(convnew_ref, ext_ref, CONV_PAD + tl - (CONV_W - 1), rows_first)

    x8 = lax.dot_general(wba_ref[...], normed, contract_last, preferred_element_type=F32)[0:SUBLANES, :]
    xs = x8 + dtb_ref[...]
    softplus = jnp.maximum(xs, 0.0) + jnp.log1p(jnp.exp(-jnp.abs(xs)))
    graw8 = -jnp.exp(alog_ref[...]) * softplus
    g8 = _lane_prefix_sum(graw8, chunk)
    gsc_ref[0] = _sigmoid(x8)
    gsc_ref[1] = g8
    gsc_ref[2] = jnp.exp(g8)
    gsc_ref[3] = jnp.exp(_lane_suffix_sum(graw8, chunk) - graw8)

    keep = keep_ref[...] != 0

    def conv_block(c0, dep):
        cols = slice(c0, c0 + LANES)
        xe = ext_ref[:, :, cols].reshape(bb * (CONV_PAD + tl), LANES)
        acc = wconv_ref[0:1, cols] * xe
        for j in range(1, CONV_W):
            acc = wconv_ref[j:j + 1, cols] * xe + pltpu.roll(acc, 1, axis=0)
        val = _silu(acc.reshape(bb, CONV_PAD + tl, LANES)[:, CONV_PAD:, :])
        if c0 < 2 * QK_W:
            scale = HEAD_DIM ** -0.5 if c0 < QK_W else 1.0
            val = val * (lax.rsqrt(jnp.sum(val * val, axis=-1, keepdims=True) + EPS) * scale)
        qkv_ref[:, :, cols] = jnp.where(keep, val, dep[:, 0:LANES].reshape(bb, tl, LANES)).astype(qkv_ref.dtype)

    def proj_block(out0, w_ref, w0, c0):
        val = proj(w_ref[w0 + c0:w0 + c0 + MXU_COLS, :])
        rest_ref[:, :, out0 + c0:out0 + c0 + MXU_COLS] = val.reshape(bb, tl, MXU_COLS)
        return val

    mxu_work = ([functools.partial(proj_block, REST_Z0, wq_ref, QKV_W, c0) for c0 in range(0, QK_W, MXU_COLS)]
                + [functools.partial(proj_block, REST_P0, wr_ref, 0, c0) for c0 in range(0, WIDTH_B, MXU_COLS)]
                + [functools.partial(proj_block, 0, wr_ref, WIDTH_B, c0)
                   for c0 in range(0, 2 * D_MODEL, MXU_COLS)])
    for i, c0 in enumerate(range(0, QKV_W, LANES)):
        conv_block(c0, mxu_work[i]())

    if n_tiles > 1:
        _history_to_scratch(ext_ref, hist.start, convnew_ref, rows_first)


def _front(x, g_attn, wq, wba, wr, w_conv, a_log, dt_bias, conv_prev, *, bb, tl, chunk, act_dtype, rows_first):
    bsz, l, _ = x.shape
    assert bsz % bb == 0 and l % tl == 0 and (bb * tl) % chunk == 0, "blocks must tile the batch, length and chunks"
    assert bb == 1 or tl == l, "a block of several sequences must hold them whole (gate rows are token-major)"
    n_tiles = l // tl
    kern = functools.partial(_front_kernel, bb=bb, tl=tl, n_tiles=n_tiles, chunk=chunk, rows_first=rows_first)
    sds = lambda *shape: jax.ShapeDtypeStruct(shape, F32)
    zeros4 = jnp.zeros((N_HEADS,), F32)
    rows8 = lambda v: jnp.broadcast_to(jnp.concatenate([zeros4, v.astype(F32)])[:, None], (SUBLANES, bb * tl))
    return pl.pallas_call(
        kern,
        grid=(bsz // bb, n_tiles),
        in_specs=[_seq_spec(bb, tl, D_MODEL), _const_spec((1, D_MODEL)), _const_spec((QKVZ_W, D_MODEL)),
                  _const_spec((2 * SUBLANES, D_MODEL)), _const_spec((REST_W, D_MODEL)), _const_spec((CONV_W, QKV_W)),
                  _const_spec((SUBLANES, bb * tl)), _const_spec((SUBLANES, bb * tl)), _const_spec((1, LANES)),
                  _state_spec(bb, CONV_W - 1, QKV_W, rows_first)],
        out_specs=[_seq_spec(bb, tl, QKV_W), _seq_spec(bb, tl, REST_OUT_W), _gate_rows_spec(bb, tl, n_tiles),
                   _state_spec(bb, CONV_W - 1, QKV_W, rows_first)],
        out_shape=[jax.ShapeDtypeStruct((bsz, l, QKV_W), act_dtype), sds(bsz, l, REST_OUT_W),
                   sds(N_GATE_ROWS, SUBLANES, bsz * l),
                   _state_shape(bsz, CONV_W - 1, QKV_W, rows_first)],
        scratch_shapes=[pltpu.VMEM((bb, CONV_PAD + tl, QKV_W), F32)],
        compiler_params=pltpu.CompilerParams(dimension_semantics=("arbitrary", "arbitrary"),
                                             vmem_limit_bytes=VMEM_LIMIT),
        name="front",
    )(x, g_attn, wq, wba, wr, w_conv, rows8(a_log), rows8(dt_bias), jnp.ones((1, LANES), jnp.int32), conv_prev)


def _unit_lower_inverses(neg_ms, chunk, side_jobs=()):
    nb = min(chunk, SERIES_BLOCK)
    n_blocks = GROUP // nb
    n_factors = int(math.log2(nb))
    assert chunk == nb or (chunk == 2 * nb and n_blocks == 2), "chunks are one or two series blocks"
    lane = lax.broadcasted_iota(jnp.int32, (nb, GROUP), 1)
    lane_block = lane // nb

    def packed(m):
        out = m[0:nb]
        for b in range(1, n_blocks):
            out = jnp.where(lane_block == b, m[b * nb:(b + 1) * nb], out)
        return out

    def block_diag(p):
        return jnp.concatenate([jnp.where(lane_block == b, p, 0.0) for b in range(n_blocks)], axis=0)

    side_jobs = list(side_jobs)
    per_step = -(-len(side_jobs) // max(n_factors - 1, 1))

    def issue_side_jobs(step_values):
        for k in range(min(per_step, len(side_jobs))):
            side_jobs.pop(0)(step_values[k % len(step_values)])

    nps = [packed(m) for m in neg_ms]
    eye_p = jnp.where(lane % nb == lax.broadcasted_iota(jnp.int32, (nb, GROUP), 0), 1.0, 0.0).astype(F32)
    invs = [eye_p + n for n in nps]
    if n_factors > 1:
        pws = [_dot(n, block_diag(n)) for n in nps]
        issue_side_jobs(pws)
        for _ in range(n_factors - 2):
            boths = [_dot(jnp.concatenate([inv, pw], axis=0), block_diag(pw)) for inv, pw in zip(invs, pws)]
            invs = [inv + both[0:nb] for inv, both in zip(invs, boths)]
            pws = [both[nb:] for both in boths]
            issue_side_jobs(pws)
        invs = [inv + _dot(inv, block_diag(pw)) for inv, pw in zip(invs, pws)]
    while side_jobs:
        side_jobs.pop(0)(None)
    if chunk == nb:
        return [block_diag(inv) for inv in invs]
    zeros = jnp.zeros((nb, GROUP), F32)
    a_invs = [jnp.where(lane < nb, inv, 0.0) for inv in invs]
    neg_ls = [jnp.where(lane < nb, m[nb:], 0.0) for m in neg_ms]
    xs = [_dot(neg_l, jnp.concatenate([a_inv, zeros], axis=0)) for neg_l, a_inv in zip(neg_ls, a_invs)]
    ys = [_dot(inv, jnp.concatenate([zeros, x], axis=0)) for inv, x in zip(invs, xs)]
    return [jnp.concatenate([a_inv, y + jnp.where(lane >= nb, inv, 0.0)], axis=0)
            for a_inv, y, inv in zip(a_invs, ys, invs)]


def _delta_kernel(*refs, bb, tl, chunk, pos0, n_tiles, n_gate_refs, n_casts, rows_first):
    gsc_refs, refs = refs[:n_gate_refs], refs[n_gate_refs:]
    (qkv_ref, zp_ref, poolprev_ref, s0_ref, wonorm_ref, wmix_ref, pscale_ref, keep_ref) = refs[0:8]
    cast_src, refs = refs[8:8 + n_casts], refs[8 + n_casts:]
    oab_ref, poolnew_ref, snew_ref = refs[0:3]
    cast_dst, pext_ref = refs[3:3 + n_casts], refs[3 + n_casts]
    tile = pl.program_id(1)

    for src, dst in zip(cast_src, cast_dst):
        dst[...] = src[...].astype(dst.dtype)

    rows_b = min(tl, GROUP)
    seqs_g = GROUP // rows_b
    groups_b = tl // rows_b
    n_groups = bb * tl // GROUP
    chained = chunk == GROUP
    assert chained or (chunk == tl and tl < GROUP), "chunk must be a whole group or a whole short sequence"
    pool_hist = slice(POOL_PAD - POOL_HIST, POOL_PAD)

    @pl.when(tile == 0)
    def _():
        snew_ref[...] = s0_ref[...]
        _history_to_scratch(pext_ref, pool_hist.start, poolprev_ref, rows_first)
        pext_ref[:, POOL_PAD - POOL_LOOKBACK:POOL_PAD - POOL_HIST, :] = jnp.zeros((bb, 1, WIDTH_B), F32)

    pext_ref[:, POOL_PAD:POOL_PAD + tl, :] = zp_ref[:, :, QK_W:QK_W + WIDTH_B]
    _history_from_scratch(poolnew_ref, pext_ref, POOL_PAD + tl - POOL_HIST, rows_first)

    row = lax.broadcasted_iota(jnp.int32, (GROUP, GROUP), 0)
    col = lax.broadcasted_iota(jnp.int32, (GROUP, GROUP), 1)
    causal = row >= col
    strict = row > col
    if not chained:
        same = (row // chunk) == (col // chunk)
        causal = causal & same
        strict = strict & same

    def origin(g):
        if tl >= GROUP:
            return g // groups_b, (g % groups_b) * GROUP
        return g * seqs_g, 0

    def load(c0, g):
        b0, t0 = origin(g)
        return qkv_ref[b0:b0 + seqs_g, t0:t0 + rows_b, c0:c0 + HEAD_DIM].astype(F32).reshape(GROUP, HEAD_DIM)

    def gate_rows(g):
        if n_gate_refs > 1:
            b0, t0 = origin(g)
            return tuple(gsc_refs[b0][q, :, t0:t0 + GROUP] for q in range(N_GATE_ROWS))
        return tuple(gsc_refs[0][q, :, g * GROUP:(g + 1) * GROUP] for q in range(N_GATE_ROWS))

    gates = [gate_rows(g) for g in range(n_groups)]

    probs = [(g, h) for g in range(n_groups) for h in range(N_HEADS)]
    st = []
    for g, h in probs:
        beta8, g8, eg8, kds8 = gates[g]
        d = {"q": load(h * HEAD_DIM, g), "k": load(QK_W + h * HEAD_DIM, g), "v": load(2 * QK_W + h * HEAD_DIM, g)}
        d["beta_row"] = beta8[h:h + 1, :]
        d["eg_row"] = eg8[N_HEADS + h:N_HEADS + h + 1, :]
        d["g_row"] = g8[N_HEADS + h:N_HEADS + h + 1, :]
        d["kb_row"] = kds8[N_HEADS + h:N_HEADS + h + 1, :] * d["beta_row"]
        d["g_col"] = jnp.broadcast_to(d["g_row"], (GROUP, GROUP)).T
        d["kt"] = d["k"].T
        st.append(d)
    for d in st:
        both = _dot(jnp.concatenate([d["k"], d["q"]], axis=0), d["kt"])
        d["kk"], d["qk"] = both[0:GROUP], both[GROUP:]
    for d in st:
        decay = jnp.exp(jnp.where(causal, d["g_col"] - d["g_row"], -jnp.inf))
        d["neg_m"] = jnp.where(strict, -(d.pop("kk") * decay), 0.0) * d["beta_row"]
        d["qkm"] = d.pop("qk") * decay * d["beta_row"]
        d["q_dec"] = d.pop("q") * jnp.exp(d["g_col"])
        d["kt_dec"] = d.pop("kt") * d["kb_row"]

    keep = keep_ref[...] != 0

    def pool_block(g, gi, issued_with):
        b0, t0 = origin(g)
        win = POOL_WINDOWS[gi]
        cols = slice(gi * POOL_GROUP, (gi + 1) * POOL_GROUP)
        pos = pos0 + tile * tl + t0 + row % rows_b
        slab = pext_ref[b0:b0 + seqs_g, POOL_PAD - POOL_LOOKBACK + t0:POOL_PAD + t0 + rows_b, cols]
        acc = slab.reshape(seqs_g * (POOL_LOOKBACK + rows_b), POOL_GROUP)
        shift = 1
        while shift < win:
            acc = acc + pltpu.roll(acc, shift, axis=0)
            shift *= 2
        acc = acc.reshape(seqs_g, POOL_LOOKBACK + rows_b, POOL_GROUP)[:, POOL_LOOKBACK:, :]
        cur = slab[:, POOL_LOOKBACK:, :]
        pooled = (acc / jnp.minimum(pos + 1, win).astype(F32).reshape(seqs_g, rows_b, POOL_GROUP) - cur)
        mixed = _dot(pooled.reshape(GROUP, POOL_GROUP), wmix_ref[gi]) * pscale_ref[:, cols]
        if issued_with is not None:
            reps = GROUP // issued_with.shape[0]
            mixed = jnp.where(keep, mixed, jnp.concatenate([issued_with] * reps, axis=0))
        oab_ref[b0:b0 + seqs_g, t0:t0 + rows_b, QK_W + gi * POOL_GROUP:QK_W + (gi + 1) * POOL_GROUP] = (
            mixed.reshape(seqs_g, rows_b, POOL_GROUP).astype(oab_ref.dtype))

    pool_jobs = [functools.partial(pool_block, g, gi) for g in range(n_groups) for gi in range(len(POOL_WINDOWS))]
    invs = _unit_lower_inverses([d.pop("neg_m") for d in st], chunk, pool_jobs)
    for d, inv in zip(st, invs):
        d["uy"] = _dot(inv, d.pop("v"))
        d["wy"] = _dot(inv * d["eg_row"], d.pop("k"))

    def group_slab(ref, g, cols, row_off=0):
        b0, t0 = origin(g)
        return ref.at[b0:b0 + seqs_g, row_off + t0:row_off + t0 + rows_b, cols]

    def head_out(g, h, o):
        cols = slice(h * HEAD_DIM, (h + 1) * HEAD_DIM)
        z = group_slab(zp_ref, g, cols)[...].reshape(GROUP, HEAD_DIM)
        group_slab(oab_ref, g, cols)[...] = (o * _rms_scale(o) * wonorm_ref[...] * _silu(z)).reshape(
            seqs_g, rows_b, HEAD_DIM).astype(oab_ref.dtype)

    if chained:
        for j in range(groups_b):
            wave = [(i, g, h) for i, (g, h) in enumerate(probs) if g % groups_b == j]
            s_old = {i: snew_ref[origin(g)[0], h] for i, g, h in wave}
            ws = {i: _dot(jnp.concatenate([st[i]["wy"], st[i]["q_dec"]], axis=0), s_old[i]) for i, g, h in wave}
            ys = {i: st[i]["uy"] - ws[i][0:GROUP] for i, g, h in wave}
            outs = {i: ws[i][GROUP:] + _dot(st[i]["qkm"], ys[i]) for i, g, h in wave}
            for i, g, h in wave:
                last = jnp.exp(st[i]["g_col"][GROUP - 1:GROUP, :])
                snew_ref[origin(g)[0], h] = s_old[i] * last + _dot(st[i]["kt_dec"], ys[i])
            for i, g, h in wave:
                head_out(g, h, outs[i])
    else:
        outs = {}
        for i, (g, h) in enumerate(probs):
            d, b0 = st[i], origin(g)[0]
            ws_w, ws_q = [], []
            for s_i in range(seqs_g):
                r = slice(s_i * rows_b, (s_i + 1) * rows_b)
                ws = _dot(jnp.concatenate([d["wy"][r], d["q_dec"][r]], axis=0), snew_ref[b0 + s_i, h])
                ws_w.append(ws[0:rows_b])
                ws_q.append(ws[rows_b:])
            d["y"] = d["uy"] - jnp.concatenate(ws_w, axis=0)
            outs[i] = jnp.concatenate(ws_q, axis=0) + _dot(d["qkm"], d["y"])
        for i, (g, h) in enumerate(probs):
            d, b0 = st[i], origin(g)[0]
            for s_i in range(seqs_g):
                last = jnp.exp(d["g_col"][(s_i + 1) * rows_b - 1:(s_i + 1) * rows_b, :])
                upd = _dot(jnp.where(col // rows_b == s_i, d["kt_dec"], 0.0), d["y"])
                snew_ref[b0 + s_i, h] = snew_ref[b0 + s_i, h] * last + upd
        for i, (g, h) in enumerate(probs):
            head_out(g, h, outs[i])

    if n_tiles > 1:
        _history_to_scratch(pext_ref, pool_hist.start, poolnew_ref, rows_first)


def _delta(qkv, rest, gsc, pool_prev, s0, w_onorm, w_mix, pool_scale, *, bb, tl, chunk, pos0, act_dtype,
           rows_first, cast_to_bf16=()):
    bsz, l = qkv.shape[0], qkv.shape[1]
    assert bsz % bb == 0 and l % tl == 0 and (bb * tl) % GROUP == 0, "blocks must tile the batch, length and groups"
    n_tiles = l // tl
    n_steps = (bsz // bb) * n_tiles
    cast_specs = []
    for arr, axis in cast_to_bf16:
        tile = LANES if axis == 1 else 2 * SUBLANES
        assert arr.shape[axis] % (n_steps * tile) == 0, "each step converts a tile-aligned slice"
        blk = tuple(d // n_steps if a == axis else d for a, d in enumerate(arr.shape))
        cast_specs.append(pl.BlockSpec(blk, lambda i, j, axis=axis: tuple(
            i * n_tiles + j if a == axis else 0 for a in range(2))))
    state_s = pl.BlockSpec((bb, N_HEADS, HEAD_DIM, HEAD_DIM), lambda i, j: (i, 0, 0, 0))
    if n_tiles == 1:
        gate_specs = [_gate_rows_spec(bb, tl, 1)]
    else:
        gate_specs = [pl.BlockSpec((N_GATE_ROWS, SUBLANES, tl), lambda i, j, k=k: (0, 0, (i * bb + k) * n_tiles + j))
                      for k in range(bb)]
    kern = functools.partial(_delta_kernel, bb=bb, tl=tl, chunk=chunk, pos0=pos0, n_tiles=n_tiles,
                             n_gate_refs=len(gate_specs), n_casts=len(cast_specs), rows_first=rows_first)
    sds = lambda *shape: jax.ShapeDtypeStruct(shape, F32)
    return pl.pallas_call(
        kern,
        grid=(bsz // bb, n_tiles),
        in_specs=gate_specs + [
                  _seq_spec(bb, tl, QKV_W),
                  pl.BlockSpec((bb, tl, QK_W + WIDTH_B), lambda i, j: (i, j, REST_Z0 // (QK_W + WIDTH_B))),
                  _state_spec(bb, POOL_HIST, WIDTH_B, rows_first), state_s,
                  _const_spec((1, HEAD_DIM)), _const_spec((len(POOL_WINDOWS), POOL_GROUP, POOL_GROUP)),
                  _const_spec((1, WIDTH_B)), _const_spec((1, LANES))] + cast_specs,
        out_specs=[_seq_spec(bb, tl, QK_W + WIDTH_B),
                   _state_spec(bb, POOL_HIST, WIDTH_B, rows_first), state_s] + cast_specs,
        out_shape=[jax.ShapeDtypeStruct((bsz, l, QK_W + WIDTH_B), act_dtype),
                   _state_shape(bsz, POOL_HIST, WIDTH_B, rows_first),
                   sds(bsz, N_HEADS, HEAD_DIM, HEAD_DIM)]
                  + [jax.ShapeDtypeStruct(arr.shape, BF16) for arr, _ in cast_to_bf16],
        scratch_shapes=[pltpu.VMEM((bb, POOL_PAD + tl, WIDTH_B), F32)],
        compiler_params=pltpu.CompilerParams(dimension_semantics=("arbitrary", "arbitrary"),
                                             vmem_limit_bytes=VMEM_LIMIT),
        name="delta",
    )(*([gsc] * len(gate_specs)), qkv, rest, pool_prev, s0, w_onorm, w_mix, pool_scale,
      jnp.ones((1, LANES), jnp.int32), *[arr for arr, _ in cast_to_bf16])


def _mlp_kernel(x_ref, oab_ref, gate_ref, wa_ref, wb_ref, wo_ref, gmlp_ref, wup_ref, wdown_ref, gfin_ref, y_ref):
    ma = _dot(oab_ref[:, 0:QK_W], wa_ref[...])
    mb = _dot(oab_ref[:, QK_W:QK_W + WIDTH_B], wb_ref[...])
    merged = _sigmoid(gate_ref[:, 0:D_MODEL]) * ma + _sigmoid(gate_ref[:, D_MODEL:]) * mb
    x1 = x_ref[...] + _dot(merged, wo_ref[...])
    h2 = (x1 * _rms_scale(x1) * gmlp_ref[...]).astype(BF16)
    acc = x1
    for c0 in range(0, D_FF, FF_BLOCK):
        up = jnp.dot(h2, wup_ref[:, c0:c0 + FF_BLOCK], preferred_element_type=F32)
        act = jnp.square(jnp.maximum(up, 0.0))
        acc = acc + _dot(act, wdown_ref[c0:c0 + FF_BLOCK, :])
    y_ref[...] = acc * _rms_scale(acc) * gfin_ref[...]


def _merge_mlp(x2d, oab, rest, prm, tm):
    t = x2d.shape[0]
    assert t % tm == 0, "row tiles must cover the tokens exactly"
    row = lambda w: pl.BlockSpec((tm, w), lambda i: (i, 0))
    return pl.pallas_call(
        _mlp_kernel,
        grid=(t // tm,),
        in_specs=[row(D_MODEL), row(QK_W + WIDTH_B), row(2 * D_MODEL),
                  _const_spec((QK_W, D_MODEL)), _const_spec((WIDTH_B, D_MODEL)), _const_spec((D_MODEL, D_MODEL)),
                  _const_spec((1, D_MODEL)), _const_spec((D_MODEL, D_FF)), _const_spec((D_FF, D_MODEL)),
                  _const_spec((1, D_MODEL))],
        out_specs=row(D_MODEL),
        out_shape=jax.ShapeDtypeStruct((t, D_MODEL), F32),
        compiler_params=pltpu.CompilerParams(dimension_semantics=("arbitrary",),
                                             vmem_limit_bytes=VMEM_LIMIT),
        name="merge_mlp",
    )(x2d, oab, rest, prm["w_a_out"], prm["w_b_out"], prm["w_o"], prm["g_mlp"], prm["w_up"], prm["w_down"],
      prm["g_final"])


def _mix(x, conv_prev, pool_prev, s_prev, pos0, prm, *, front_blk, delta_blk, chunk, act_dtype, rows_first,
         cast_to_bf16=()):
    bsz, l, _ = x.shape
    t = bsz * l
    qkv, rest, gsc, conv_new = _front(
        x, prm["g_attn"], prm["wq"], prm["wba"], prm["wr"], prm["w_conv"], prm["a_log"], prm["dt_bias"],
        conv_prev, bb=front_blk[0], tl=front_blk[1], chunk=chunk, act_dtype=act_dtype, rows_first=rows_first)
    oab, pool_new, s_new, *casted = _delta(
        qkv, rest, gsc, pool_prev, s_prev, prm["w_onorm"], prm["w_mix"], prm["pool_scale"], bb=delta_blk[0],
        tl=delta_blk[1], chunk=chunk, pos0=pos0, act_dtype=act_dtype, rows_first=rows_first,
        cast_to_bf16=cast_to_bf16)
    rows = (x.reshape(t, D_MODEL), oab.reshape(t, QK_W + WIDTH_B), rest.reshape(t, REST_OUT_W))
    return rows, (conv_new, pool_new, s_new[None]), casted


def kernel(x_prompt, x_sample, state_conv, state_pool, state_ssm, w_in, w_conv, a_log, dt_bias, w_onorm,
           w_pool_mix, pool_scale, w_a_out, w_b_out, w_o, g_attn, g_mlp, w_up, w_down, g_final):
    assert w_in.shape[0] == 1, "single-layer decoder"
    wt = jnp.transpose(w_in[0])
    rest_off = QKVZ_W + GATE_SCALARS
    prm = {
        "wq": wt[0:QKVZ_W].astype(BF16),
        "wba": jnp.pad(wt[QKVZ_W:rest_off], ((0, 2 * SUBLANES - GATE_SCALARS), (0, 0))).astype(BF16),
        "wr": wt[rest_off:rest_off + REST_W].astype(BF16),
        "g_attn": g_attn[0][None, :], "g_mlp": g_mlp[0][None, :], "g_final": g_final[None, :],
        "w_conv": w_conv[0].astype(F32),
        "a_log": a_log[0], "dt_bias": dt_bias[0],
        "w_onorm": w_onorm[0][None, :].astype(F32),
        "w_mix": w_pool_mix[0].astype(BF16), "pool_scale": pool_scale[0][None, :].astype(F32),
    }
    bp = x_prompt.shape[0]
    mlp_weights = ("w_up", "w_down", "w_o", "w_a_out", "w_b_out")
    rows_p, (conv_p, pool_p, ssm_p), casted = _mix(
        x_prompt, jnp.zeros((bp, CONV_W - 1, QKV_W), F32), jnp.zeros((bp, POOL_HIST, WIDTH_B), F32),
        jnp.zeros((bp, N_HEADS, HEAD_DIM, HEAD_DIM), F32), 0, prm,
        front_blk=(1, 512), delta_blk=(4, 256), chunk=GROUP, act_dtype=BF16, rows_first=False,
        cast_to_bf16=((w_up[0].astype(F32), 1), (w_down[0].astype(F32), 0), (w_o[0].astype(F32), 0),
                      (w_a_out[0].astype(F32), 0), (w_b_out[0].astype(F32), 0)))
    prm.update(zip(mlp_weights, casted))
    conv_p, pool_p = conv_p[None], pool_p[None]
    dec_len = x_sample.shape[1]
    rows_major = lambda s: jnp.transpose(s[0].astype(F32), (1, 0, 2))
    rows_s, (conv_s, pool_s, ssm_s), _ = _mix(
        x_sample, rows_major(state_conv), rows_major(state_pool), state_ssm[0].astype(F32), PAST_LEN, prm,
        front_blk=(256 // dec_len, dec_len), delta_blk=(2 * GROUP // dec_len, dec_len),
        chunk=dec_len, act_dtype=F32, rows_first=True)
    conv_s, pool_s = jnp.transpose(conv_s, (1, 0, 2))[None], jnp.transpose(pool_s, (1, 0, 2))[None]
    y_p = _merge_mlp(*rows_p, prm, tm=512).reshape(x_prompt.shape)
    y_s = _merge_mlp(*rows_s, prm, tm=256).reshape(x_sample.shape)
    return (y_p, y_s, conv_p.astype(state_conv.dtype), pool_p.astype(state_pool.dtype),
            ssm_p.astype(state_ssm.dtype), conv_s.astype(state_conv.dtype), pool_s.astype(state_pool.dtype),
            ssm_s.astype(state_ssm.dtype))
```

```python
import functools
import math

import jax
import jax.numpy as jnp
from jax import lax
from jax.experimental import pallas as pl
from jax.experimental.pallas import tpu as pltpu

D_MODEL = 1024
N_HEADS = 4
HEAD_DIM = 128
QK_W = N_HEADS * HEAD_DIM
QKV_W = 3 * QK_W
CONV_W = 4
POOL_WINDOWS = (2, 4, 8, 16)
POOL_GROUP = 128
WIDTH_B = len(POOL_WINDOWS) * POOL_GROUP
POOL_HIST = 15
D_FF = 4 * D_MODEL
EPS = 1e-6
PAST_LEN = 16384
LANES = 128
SUBLANES = 8
MXU_COLS = 256
N_GATE_ROWS = 4

QKVZ_W = QKV_W + QK_W
GATE_SCALARS = 2 * N_HEADS
REST_W = WIDTH_B + 2 * D_MODEL
REST_Z0 = 2 * D_MODEL
REST_P0 = REST_Z0 + QK_W
REST_OUT_W = REST_P0 + WIDTH_B

GROUP = 128
SERIES_BLOCK = 64
CONV_PAD = 8
POOL_LOOKBACK = 16
POOL_PAD = 24
FF_BLOCK = 1024

VMEM_LIMIT = 56 * 1024 * 1024

BF16 = jnp.bfloat16
F32 = jnp.float32


def _dot(a, b):
    return jnp.dot(a.astype(BF16), b.astype(BF16), preferred_element_type=F32)


def _sigmoid(x):
    return 1.0 / (1.0 + jnp.exp(-x))


def _silu_of_twice(half):
    return half * jnp.tanh(half) + half


def _silu(x):
    return _silu_of_twice(0.5 * x)


def _rms_scale(x):
    return lax.rsqrt(jnp.mean(x * x, axis=-1, keepdims=True) + EPS)


def _const_spec(shape):
    zeros = (0,) * len(shape)
    return pl.BlockSpec(shape, lambda *_: zeros, pipeline_mode=pl.Buffered(1))


def _seq_spec(bb, tl, width):
    return pl.BlockSpec((bb, tl, width), lambda i, j: (i, j, 0))


def _state_spec(bb, n_rows, width, rows_first):
    if rows_first:
        return pl.BlockSpec((n_rows, bb, width), lambda i, j: (0, i, 0))
    return pl.BlockSpec((bb, n_rows, width), lambda i, j: (i, 0, 0))


def _state_shape(bsz, n_rows, width, rows_first):
    return jax.ShapeDtypeStruct((n_rows, bsz, width) if rows_first else (bsz, n_rows, width), F32)


def _history_to_scratch(scratch_ref, first_row, state_ref, rows_first):
    if rows_first:
        for j in range(state_ref.shape[0]):
            scratch_ref[:, first_row + j, :] = state_ref[j]
    else:
        scratch_ref[:, first_row:first_row + state_ref.shape[1], :] = state_ref[...]


def _history_from_scratch(state_ref, scratch_ref, first_row, rows_first):
    if rows_first:
        for j in range(state_ref.shape[0]):
            state_ref[j] = scratch_ref[:, first_row + j, :]
    else:
        state_ref[...] = scratch_ref[:, first_row:first_row + state_ref.shape[1], :]


def _gate_rows_spec(bb, tl, n_tiles):
    return pl.BlockSpec((N_GATE_ROWS, SUBLANES, bb * tl), lambda i, j: (0, 0, i * n_tiles + j))


def _lane_prefix_sum(x, chunk):
    lane = lax.broadcasted_iota(jnp.int32, x.shape, 1)
    shift = 1
    while shift < chunk:
        x = x + jnp.where(lane % chunk >= shift, pltpu.roll(x, shift, axis=1), 0.0)
        shift *= 2
    return x


def _lane_suffix_sum(x, chunk):
    lane = lax.broadcasted_iota(jnp.int32, x.shape, 1)
    shift = 1
    while shift < chunk:
        x = x + jnp.where(lane % chunk + shift < chunk, pltpu.roll(x, x.shape[1] - shift, axis=1), 0.0)
        shift *= 2
    return x


def _front_kernel(x_ref, g_ref, wq_ref, wba_ref, wr_ref, wconv_ref, alog_ref, dtb_ref, keep_ref, convprev_ref,
                  qkv_ref, rest_ref, gsc_ref, convnew_ref,
                  ext_ref, *, bb, tl, n_tiles, chunk, rows_first):
    tile = pl.program_id(1)
    rows = bb * tl
    hist = slice(CONV_PAD - (CONV_W - 1), CONV_PAD)

    @pl.when(tile == 0)
    def _():
        _history_to_scratch(ext_ref, hist.start, convprev_ref, rows_first)
        ext_ref[:, 0:CONV_PAD - (CONV_W - 1), :] = jnp.zeros((bb, CONV_PAD - (CONV_W - 1), QKV_W), F32)

    x = x_ref[...].reshape(rows, D_MODEL)
    normed = (x * _rms_scale(x) * g_ref[...]).astype(BF16)
    contract_last = (((1,), (1,)), ((), ()))
    proj = lambda w_rows: lax.dot_general(normed, w_rows, contract_last, preferred_element_type=F32)
    ext_ref[:, CONV_PAD:CONV_PAD + tl, :] = proj(wq_ref[0:QKV_W, :]).reshape(bb, tl, QKV_W)
    _history_from_scratch(convnew_ref, ext_ref, CONV_PAD + tl - (CONV_W - 1), rows_first)

    x8 = lax.dot_general(wba_ref[...], normed, contract_last, preferred_element_type=F32)[0:SUBLANES, :]
    xs = x8 + dtb_ref[...]
    softplus = jnp.maximum(xs, 0.0) + jnp.log1p(jnp.exp(-jnp.abs(xs)))
    graw8 = -jnp.exp(alog_ref[...]) * softplus
    g8 = _lane_prefix_sum(graw8, chunk)
    gsc_ref[0] = _sigmoid(x8)
    gsc_ref[1] = g8
    gsc_ref[2] = jnp.exp(g8)
    gsc_ref[3] = jnp.exp(_lane_suffix_sum(graw8, chunk) - graw8)

    keep = keep_ref[...] != 0

    def conv_block(c0, dep):
        cols = slice(c0, c0 + LANES)
        xe = ext_ref[:, :, cols].reshape(bb * (CONV_PAD + tl), LANES)
        x1 = pltpu.roll(xe, 1, axis=0)
        tap = lambda j: 0.5 * wconv_ref[j:j + 1, cols]
        acc = (tap(3) * xe + tap(2) * x1) + pltpu.roll(tap(1) * xe + tap(0) * x1, 2, axis=0)
        val = _silu_of_twice(acc.reshape(bb, CONV_PAD + tl, LANES)[:, CONV_PAD:, :])
        if c0 < 2 * QK_W:
            scale = HEAD_DIM ** -0.5 if c0 < QK_W else 1.0
            val = val * (lax.rsqrt(jnp.sum(val * val, axis=-1, keepdims=True) + EPS) * scale)
        qkv_ref[:, :, cols] = jnp.where(keep, val, dep[:, 0:LANES].reshape(bb, tl, LANES)).astype(qkv_ref.dtype)

    def proj_block(out0, w_ref, w0, c0):
        val = proj(w_ref[w0 + c0:w0 + c0 + MXU_COLS, :])
        rest_ref[:, :, out0 + c0:out0 + c0 + MXU_COLS] = val.reshape(bb, tl, MXU_COLS)
        return val

    mxu_work = ([functools.partial(proj_block, REST_Z0, wq_ref, QKV_W, c0) for c0 in range(0, QK_W, MXU_COLS)]
                + [functools.partial(proj_block, REST_P0, wr_ref, 0, c0) for c0 in range(0, WIDTH_B, MXU_COLS)]
                + [functools.partial(proj_block, 0, wr_ref, WIDTH_B, c0)
                   for c0 in range(0, 2 * D_MODEL, MXU_COLS)])
    for i, c0 in enumerate(range(0, QKV_W, LANES)):
        conv_block(c0, mxu_work[i]())

    if n_tiles > 1:
        _history_to_scratch(ext_ref, hist.start, convnew_ref, rows_first)


def _front(x, g_attn, wq, wba, wr, w_conv, a_log, dt_bias, conv_prev, *, bb, tl, chunk, act_dtype, rows_first):
    bsz, l, _ = x.shape
    assert bsz % bb == 0 and l % tl == 0 and (bb * tl) % chunk == 0, "blocks must tile the batch, length and chunks"
    assert bb == 1 or tl == l, "a block of several sequences must hold them whole (gate rows are token-major)"
    assert CONV_W == 4, "the conv is written out as two pairs of taps"
    n_tiles = l // tl
    kern = functools.partial(_front_kernel, bb=bb, tl=tl, n_tiles=n_tiles, chunk=chunk, rows_first=rows_first)
    sds = lambda *shape: jax.ShapeDtypeStruct(shape, F32)
    zeros4 = jnp.zeros((N_HEADS,), F32)
    rows8 = lambda v: jnp.broadcast_to(jnp.concatenate([zeros4, v.astype(F32)])[:, None], (SUBLANES, bb * tl))
    return pl.pallas_call(
        kern,
        grid=(bsz // bb, n_tiles),
        in_specs=[_seq_spec(bb, tl, D_MODEL), _const_spec((1, D_MODEL)), _const_spec((QKVZ_W, D_MODEL)),
                  _const_spec((2 * SUBLANES, D_MODEL)), _const_spec((REST_W, D_MODEL)), _const_spec((CONV_W, QKV_W)),
                  _const_spec((SUBLANES, bb * tl)), _const_spec((SUBLANES, bb * tl)), _const_spec((1, LANES)),
                  _state_spec(bb, CONV_W - 1, QKV_W, rows_first)],
        out_specs=[_seq_spec(bb, tl, QKV_W), _seq_spec(bb, tl, REST_OUT_W), _gate_rows_spec(bb, tl, n_tiles),
                   _state_spec(bb, CONV_W - 1, QKV_W, rows_first)],
        out_shape=[jax.ShapeDtypeStruct((bsz, l, QKV_W), act_dtype), sds(bsz, l, REST_OUT_W),
                   sds(N_GATE_ROWS, SUBLANES, bsz * l),
                   _state_shape(bsz, CONV_W - 1, QKV_W, rows_first)],
        scratch_shapes=[pltpu.VMEM((bb, CONV_PAD + tl, QKV_W), F32)],
        compiler_params=pltpu.CompilerParams(dimension_semantics=("arbitrary", "arbitrary"),
                                             vmem_limit_bytes=VMEM_LIMIT),
        name="front",
    )(x, g_attn, wq, wba, wr, w_conv, rows8(a_log), rows8(dt_bias), jnp.ones((1, LANES), jnp.int32), conv_prev)


def _unit_lower_inverses(ms, chunk, side_jobs=()):
    nb = min(chunk, SERIES_BLOCK)
    n_blocks = GROUP // nb
    n_factors = int(math.log2(nb))
    assert chunk == nb or (chunk == 2 * nb and n_blocks == 2), "chunks are one or two series blocks"
    lane = lax.broadcasted_iota(jnp.int32, (nb, GROUP), 1)
    lane_block = lane // nb

    def packed(m):
        out = m[0:nb]
        for b in range(1, n_blocks):
            out = jnp.where(lane_block == b, m[b * nb:(b + 1) * nb], out)
        return out

    def block_diag(p):
        return jnp.concatenate([jnp.where(lane_block == b, p, 0.0) for b in range(n_blocks)], axis=0)

    side_jobs = list(side_jobs)
    per_step = -(-len(side_jobs) // max(n_factors - 1, 1))

    def issue_side_jobs(step_values):
        for k in range(min(per_step, len(side_jobs))):
            side_jobs.pop(0)(step_values[k % len(step_values)])

    nps = [packed(m) for m in ms]
    eye_p = jnp.where(lane % nb == lax.broadcasted_iota(jnp.int32, (nb, GROUP), 0), 1.0, 0.0).astype(F32)
    invs = [eye_p - n for n in nps]
    if n_factors > 1:
        pws = [_dot(n, block_diag(n)) for n in nps]
        issue_side_jobs(pws)
        for _ in range(n_factors - 2):
            boths = [_dot(jnp.concatenate([inv, pw], axis=0), block_diag(pw)) for inv, pw in zip(invs, pws)]
            invs = [inv + both[0:nb] for inv, both in zip(invs, boths)]
            pws = [both[nb:] for both in boths]
            issue_side_jobs(pws)
        invs = [inv + _dot(inv, block_diag(pw)) for inv, pw in zip(invs, pws)]
    while side_jobs:
        side_jobs.pop(0)(None)
    if chunk == nb:
        return [block_diag(inv) for inv in invs]
    zeros = jnp.zeros((nb, GROUP), F32)
    a_invs = [jnp.where(lane < nb, inv, 0.0) for inv in invs]
    lows = [jnp.where(lane < nb, m[nb:], 0.0) for m in ms]
    xs = [_dot(low, jnp.concatenate([a_inv, zeros], axis=0)) for low, a_inv in zip(lows, a_invs)]
    ys = [_dot(inv, jnp.concatenate([zeros, x], axis=0)) for inv, x in zip(invs, xs)]
    return [jnp.concatenate([a_inv, jnp.where(lane >= nb, inv, 0.0) - y], axis=0)
            for a_inv, y, inv in zip(a_invs, ys, invs)]


def _delta_kernel(*refs, bb, tl, chunk, pos0, n_tiles, n_gate_refs, n_casts, rows_first):
    gsc_refs, refs = refs[:n_gate_refs], refs[n_gate_refs:]
    (qkv_ref, zp_ref, poolprev_ref, s0_ref, wonorm_ref, wmix_ref, pscale_ref, keep_ref) = refs[0:8]
    cast_src, refs = refs[8:8 + n_casts], refs[8 + n_casts:]
    oab_ref, poolnew_ref, snew_ref = refs[0:3]
    cast_dst, pext_ref = refs[3:3 + n_casts], refs[3 + n_casts]
    tile = pl.program_id(1)

    for src, dst in zip(cast_src, cast_dst):
        dst[...] = src[...].astype(dst.dtype)

    rows_b = min(tl, GROUP)
    seqs_g = GROUP // rows_b
    groups_b = tl // rows_b
    n_groups = bb * tl // GROUP
    chained = chunk == GROUP
    assert chained or (chunk == tl and tl < GROUP), "chunk must be a whole group or a whole short sequence"
    pool_hist = slice(POOL_PAD - POOL_HIST, POOL_PAD)

    @pl.when(tile == 0)
    def _():
        snew_ref[...] = s0_ref[...]
        _history_to_scratch(pext_ref, pool_hist.start, poolprev_ref, rows_first)
        pext_ref[:, POOL_PAD - POOL_LOOKBACK:POOL_PAD - POOL_HIST, :] = jnp.zeros((bb, 1, WIDTH_B), F32)

    pext_ref[:, POOL_PAD:POOL_PAD + tl, :] = zp_ref[:, :, QK_W:QK_W + WIDTH_B]
    _history_from_scratch(poolnew_ref, pext_ref, POOL_PAD + tl - POOL_HIST, rows_first)

    row = lax.broadcasted_iota(jnp.int32, (GROUP, GROUP), 0)
    col = lax.broadcasted_iota(jnp.int32, (GROUP, GROUP), 1)
    causal = row >= col
    strict = row > col
    if not chained:
        same = (row // chunk) == (col // chunk)
        causal = causal & same
        strict = strict & same

    def origin(g):
        if tl >= GROUP:
            return g // groups_b, (g % groups_b) * GROUP
        return g * seqs_g, 0

    def load(c0, g):
        b0, t0 = origin(g)
        return qkv_ref[b0:b0 + seqs_g, t0:t0 + rows_b, c0:c0 + HEAD_DIM].astype(F32).reshape(GROUP, HEAD_DIM)

    def gate_rows(g):
        if n_gate_refs > 1:
            b0, t0 = origin(g)
            return tuple(gsc_refs[b0][q, :, t0:t0 + GROUP] for q in range(N_GATE_ROWS))
        return tuple(gsc_refs[0][q, :, g * GROUP:(g + 1) * GROUP] for q in range(N_GATE_ROWS))

    gates = [gate_rows(g) for g in range(n_groups)]

    probs = [(g, h) for g in range(n_groups) for h in range(N_HEADS)]
    st = []
    for g, h in probs:
        beta8, g8, eg8, kds8 = gates[g]
        d = {"q": load(h * HEAD_DIM, g), "k": load(QK_W + h * HEAD_DIM, g), "v": load(2 * QK_W + h * HEAD_DIM, g)}
        d["beta_row"] = beta8[h:h + 1, :]
        d["eg_row"] = eg8[N_HEADS + h:N_HEADS + h + 1, :]
        d["g_row"] = g8[N_HEADS + h:N_HEADS + h + 1, :]
        d["kb_row"] = kds8[N_HEADS + h:N_HEADS + h + 1, :] * d["beta_row"]
        d["g_col"] = jnp.broadcast_to(d["g_row"], (GROUP, GROUP)).T
        d["kt"] = d["k"].T
        st.append(d)
    for d in st:
        both = _dot(jnp.concatenate([d["k"], d["q"]], axis=0), d["kt"])
        d["kk"], d["qk"] = both[0:GROUP], both[GROUP:]
    for d in st:
        decay = jnp.exp(jnp.where(causal, d["g_col"] - d["g_row"], -jnp.inf))
        decay_beta = decay * d["beta_row"]
        d["m"] = jnp.where(strict, d.pop("kk") * decay_beta, 0.0)
        d["qkm"] = d.pop("qk") * decay_beta
        d["q_dec"] = d.pop("q") * jnp.exp(d["g_col"])
        d["kt_dec"] = d.pop("kt") * d["kb_row"]

    keep = keep_ref[...] != 0

    def pool_block(g, gi, issued_with):
        b0, t0 = origin(g)
        win = POOL_WINDOWS[gi]
        cols = slice(gi * POOL_GROUP, (gi + 1) * POOL_GROUP)
        pos = pos0 + tile * tl + t0 + row % rows_b
        slab = pext_ref[b0:b0 + seqs_g, POOL_PAD - POOL_LOOKBACK + t0:POOL_PAD + t0 + rows_b, cols]
        acc = slab.reshape(seqs_g * (POOL_LOOKBACK + rows_b), POOL_GROUP)
        shift = 1
        while shift < win:
            acc = acc + pltpu.roll(acc, shift, axis=0)
            shift *= 2
        acc = acc.reshape(seqs_g, POOL_LOOKBACK + rows_b, POOL_GROUP)[:, POOL_LOOKBACK:, :]
        cur = slab[:, POOL_LOOKBACK:, :]
        pooled = (acc / jnp.minimum(pos + 1, win).astype(F32).reshape(seqs_g, rows_b, POOL_GROUP) - cur)
        mixed = _dot(pooled.reshape(GROUP, POOL_GROUP), wmix_ref[gi]) * pscale_ref[:, cols]
        if issued_with is not None:
            reps = GROUP // issued_with.shape[0]
            mixed = jnp.where(keep, mixed, jnp.concatenate([issued_with] * reps, axis=0))
        oab_ref[b0:b0 + seqs_g, t0:t0 + rows_b, QK_W + gi * POOL_GROUP:QK_W + (gi + 1) * POOL_GROUP] = (
            mixed.reshape(seqs_g, rows_b, POOL_GROUP).astype(oab_ref.dtype))

    pool_jobs = [functools.partial(pool_block, g, gi) for g in range(n_groups) for gi in range(len(POOL_WINDOWS))]
    invs = _unit_lower_inverses([d.pop("m") for d in st], chunk, pool_jobs)
    for d, inv in zip(st, invs):
        d["uy"] = _dot(inv, d.pop("v"))
        d["wy"] = _dot(inv * d["eg_row"], d.pop("k"))

    def group_slab(ref, g, cols, row_off=0):
        b0, t0 = origin(g)
        return ref.at[b0:b0 + seqs_g, row_off + t0:row_off + t0 + rows_b, cols]

    def head_out(g, h, o):
        cols = slice(h * HEAD_DIM, (h + 1) * HEAD_DIM)
        z = group_slab(zp_ref, g, cols)[...].reshape(GROUP, HEAD_DIM)
        group_slab(oab_ref, g, cols)[...] = (o * _rms_scale(o) * wonorm_ref[...] * _silu(z)).reshape(
            seqs_g, rows_b, HEAD_DIM).astype(oab_ref.dtype)

    if chained:
        for j in range(groups_b):
            wave = [(i, g, h) for i, (g, h) in enumerate(probs) if g % groups_b == j]
            s_old = {i: snew_ref[origin(g)[0], h] for i, g, h in wave}
            ws = {i: _dot(jnp.concatenate([st[i]["wy"], st[i]["q_dec"]], axis=0), s_old[i]) for i, g, h in wave}
            ys = {i: st[i]["uy"] - ws[i][0:GROUP] for i, g, h in wave}
            outs = {i: ws[i][GROUP:] + _dot(st[i]["qkm"], ys[i]) for i, g, h in wave}
            for i, g, h in wave:
                last = jnp.exp(st[i]["g_col"][GROUP - 1:GROUP, :])
                snew_ref[origin(g)[0], h] = s_old[i] * last + _dot(st[i]["kt_dec"], ys[i])
            for i, g, h in wave:
                head_out(g, h, outs[i])
    else:
        outs = {}
        for i, (g, h) in enumerate(probs):
            d, b0 = st[i], origin(g)[0]
            ws_w, ws_q = [], []
            for s_i in range(seqs_g):
                r = slice(s_i * rows_b, (s_i + 1) * rows_b)
                ws = _dot(jnp.concatenate([d["wy"][r], d["q_dec"][r]], axis=0), snew_ref[b0 + s_i, h])
                ws_w.append(ws[0:rows_b])
                ws_q.append(ws[rows_b:])
            d["y"] = d["uy"] - jnp.concatenate(ws_w, axis=0)
            outs[i] = jnp.concatenate(ws_q, axis=0) + _dot(d["qkm"], d["y"])
        for i, (g, h) in enumerate(probs):
            d, b0 = st[i], origin(g)[0]
            for s_i in range(seqs_g):
                last = jnp.exp(d["g_col"][(s_i + 1) * rows_b - 1:(s_i + 1) * rows_b, :])
                upd = _dot(jnp.where(col // rows_b == s_i, d["kt_dec"], 0.0), d["y"])
                snew_ref[b0 + s_i, h] = snew_ref[b0 + s_i, h] * last + upd
        for i, (g, h) in enumerate(probs):
            head_out(g, h, outs[i])

    if n_tiles > 1:
        _history_to_scratch(pext_ref, pool_hist.start, poolnew_ref, rows_first)


def _delta(qkv, rest, gsc, pool_prev, s0, w_onorm, w_mix, pool_scale, *, bb, tl, chunk, pos0, act_dtype,
           rows_first, cast_to_bf16=()):
    bsz, l = qkv.shape[0], qkv.shape[1]
    assert bsz % bb == 0 and l % tl == 0 and (bb * tl) % GROUP == 0, "blocks must tile the batch, length and groups"
    n_tiles = l // tl
    n_steps = (bsz // bb) * n_tiles
    cast_specs = []
    for arr, axis in cast_to_bf16:
        tile = LANES if axis == 1 else 2 * SUBLANES
        assert arr.shape[axis] % (n_steps * tile) == 0, "each step converts a tile-aligned slice"
        blk = tuple(d // n_steps if a == axis else d for a, d in enumerate(arr.shape))
        cast_specs.append(pl.BlockSpec(blk, lambda i, j, axis=axis: tuple(
            i * n_tiles + j if a == axis else 0 for a in range(2))))
    state_s = pl.BlockSpec((bb, N_HEADS, HEAD_DIM, HEAD_DIM), lambda i, j: (i, 0, 0, 0))
    if n_tiles == 1:
        gate_specs = [_gate_rows_spec(bb, tl, 1)]
    else:
        gate_specs = [pl.BlockSpec((N_GATE_ROWS, SUBLANES, tl), lambda i, j, k=k: (0, 0, (i * bb + k) * n_tiles + j))
                      for k in range(bb)]
    kern = functools.partial(_delta_kernel, bb=bb, tl=tl, chunk=chunk, pos0=pos0, n_tiles=n_tiles,
                             n_gate_refs=len(gate_specs), n_casts=len(cast_specs), rows_first=rows_first)
    sds = lambda *shape: jax.ShapeDtypeStruct(shape, F32)
    return pl.pallas_call(
        kern,
        grid=(bsz // bb, n_tiles),
        in_specs=gate_specs + [
                  _seq_spec(bb, tl, QKV_W),
                  pl.BlockSpec((bb, tl, QK_W + WIDTH_B), lambda i, j: (i, j, REST_Z0 // (QK_W + WIDTH_B))),
                  _state_spec(bb, POOL_HIST, WIDTH_B, rows_first), state_s,
                  _const_spec((1, HEAD_DIM)), _const_spec((len(POOL_WINDOWS), POOL_GROUP, POOL_GROUP)),
                  _const_spec((1, WIDTH_B)), _const_spec((1, LANES))] + cast_specs,
        out_specs=[_seq_spec(bb, tl, QK_W + WIDTH_B),
                   _state_spec(bb, POOL_HIST, WIDTH_B, rows_first), state_s] + cast_specs,
        out_shape=[jax.ShapeDtypeStruct((bsz, l, QK_W + WIDTH_B), act_dtype),
                   _state_shape(bsz, POOL_HIST, WIDTH_B, rows_first),
                   sds(bsz, N_HEADS, HEAD_DIM, HEAD_DIM)]
                  + [jax.ShapeDtypeStruct(arr.shape, BF16) for arr, _ in cast_to_bf16],
        scratch_shapes=[pltpu.VMEM((bb, POOL_PAD + tl, WIDTH_B), F32)],
        compiler_params=pltpu.CompilerParams(dimension_semantics=("arbitrary", "arbitrary"),
                                             vmem_limit_bytes=VMEM_LIMIT),
        name="delta",
    )(*([gsc] * len(gate_specs)), qkv, rest, pool_prev, s0, w_onorm, w_mix, pool_scale,
      jnp.ones((1, LANES), jnp.int32), *[arr for arr, _ in cast_to_bf16])


def _mlp_kernel(x_ref, oab_ref, gate_ref, wa_ref, wb_ref, wo_ref, gmlp_ref, wup_ref, wdown_ref, gfin_ref, y_ref):
    ma = _dot(oab_ref[:, 0:QK_W], wa_ref[...])
    mb = _dot(oab_ref[:, QK_W:QK_W + WIDTH_B], wb_ref[...])
    merged = _sigmoid(gate_ref[:, 0:D_MODEL]) * ma + _sigmoid(gate_ref[:, D_MODEL:]) * mb
    x1 = x_ref[...] + _dot(merged, wo_ref[...])
    h2 = (x1 * _rms_scale(x1) * gmlp_ref[...]).astype(BF16)
    acc = x1
    for c0 in range(0, D_FF, FF_BLOCK):
        up = jnp.dot(h2, wup_ref[:, c0:c0 + FF_BLOCK], preferred_element_type=F32)
        act = jnp.square(jnp.maximum(up, 0.0))
        acc = acc + _dot(act, wdown_ref[c0:c0 + FF_BLOCK, :])
    y_ref[...] = acc * _rms_scale(acc) * gfin_ref[...]


def _merge_mlp(x2d, oab, rest, prm, tm):
    t = x2d.shape[0]
    assert t % tm == 0, "row tiles must cover the tokens exactly"
    row = lambda w: pl.BlockSpec((tm, w), lambda i: (i, 0))
    return pl.pallas_call(
        _mlp_kernel,
        grid=(t // tm,),
        in_specs=[row(D_MODEL), row(QK_W + WIDTH_B), row(2 * D_MODEL),
                  _const_spec((QK_W, D_MODEL)), _const_spec((WIDTH_B, D_MODEL)), _const_spec((D_MODEL, D_MODEL)),
                  _const_spec((1, D_MODEL)), _const_spec((D_MODEL, D_FF)), _const_spec((D_FF, D_MODEL)),
                  _const_spec((1, D_MODEL))],
        out_specs=row(D_MODEL),
        out_shape=jax.ShapeDtypeStruct((t, D_MODEL), F32),
        compiler_params=pltpu.CompilerParams(dimension_semantics=("arbitrary",),
                                             vmem_limit_bytes=VMEM_LIMIT),
        name="merge_mlp",
    )(x2d, oab, rest, prm["w_a_out"], prm["w_b_out"], prm["w_o"], prm["g_mlp"], prm["w_up"], prm["w_down"],
      prm["g_final"])


def _mix(x, conv_prev, pool_prev, s_prev, pos0, prm, *, front_blk, delta_blk, chunk, act_dtype, rows_first,
         cast_to_bf16=()):
    bsz, l, _ = x.shape
    t = bsz * l
    qkv, rest, gsc, conv_new = _front(
        x, prm["g_attn"], prm["wq"], prm["wba"], prm["wr"], prm["w_conv"], prm["a_log"], prm["dt_bias"],
        conv_prev, bb=front_blk[0], tl=front_blk[1], chunk=chunk, act_dtype=act_dtype, rows_first=rows_first)
    oab, pool_new, s_new, *casted = _delta(
        qkv, rest, gsc, pool_prev, s_prev, prm["w_onorm"], prm["w_mix"], prm["pool_scale"], bb=delta_blk[0],
        tl=delta_blk[1], chunk=chunk, pos0=pos0, act_dtype=act_dtype, rows_first=rows_first,
        cast_to_bf16=cast_to_bf16)
    rows = (x.reshape(t, D_MODEL), oab.reshape(t, QK_W + WIDTH_B), rest.reshape(t, REST_OUT_W))
    return rows, (conv_new, pool_new, s_new[None]), casted


def kernel(x_prompt, x_sample, state_conv, state_pool, state_ssm, w_in, w_conv, a_log, dt_bias, w_onorm,
           w_pool_mix, pool_scale, w_a_out, w_b_out, w_o, g_attn, g_mlp, w_up, w_down, g_final):
    assert w_in.shape[0] == 1, "single-layer decoder"
    wt = jnp.transpose(w_in[0])
    rest_off = QKVZ_W + GATE_SCALARS
    prm = {
        "wq": wt[0:QKVZ_W].astype(BF16),
        "wba": jnp.pad(wt[QKVZ_W:rest_off], ((0, 2 * SUBLANES - GATE_SCALARS), (0, 0))).astype(BF16),
        "wr": wt[rest_off:rest_off + REST_W].astype(BF16),
        "g_attn": g_attn[0][None, :], "g_mlp": g_mlp[0][None, :], "g_final": g_final[None, :],
        "w_conv": w_conv[0].astype(F32),
        "a_log": a_log[0], "dt_bias": dt_bias[0],
        "w_onorm": w_onorm[0][None, :].astype(F32),
        "w_mix": w_pool_mix[0].astype(BF16), "pool_scale": pool_scale[0][None, :].astype(F32),
    }
    bp = x_prompt.shape[0]
    mlp_weights = ("w_up", "w_down", "w_o", "w_a_out", "w_b_out")
    rows_p, (conv_p, pool_p, ssm_p), casted = _mix(
        x_prompt, jnp.zeros((bp, CONV_W - 1, QKV_W), F32), jnp.zeros((bp, POOL_HIST, WIDTH_B), F32),
        jnp.zeros((bp, N_HEADS, HEAD_DIM, HEAD_DIM), F32), 0, prm,
        front_blk=(1, 512), delta_blk=(4, 256), chunk=GROUP, act_dtype=BF16, rows_first=False,
        cast_to_bf16=((w_up[0].astype(F32), 1), (w_down[0].astype(F32), 0), (w_o[0].astype(F32), 0),
                      (w_a_out[0].astype(F32), 0), (w_b_out[0].astype(F32), 0)))
    prm.update(zip(mlp_weights, casted))
    conv_p, pool_p = conv_p[None], pool_p[None]
    dec_len = x_sample.shape[1]
    rows_major = lambda s: jnp.transpose(s[0].astype(F32), (1, 0, 2))
    rows_s, (conv_s, pool_s, ssm_s), _ = _mix(
        x_sample, rows_major(state_conv), rows_major(state_pool), state_ssm[0].astype(F32), PAST_LEN, prm,
        front_blk=(256 // dec_len, dec_len), delta_blk=(2 * GROUP // dec_len, dec_len),
        chunk=dec_len, act_dtype=F32, rows_first=True)
    conv_s, pool_s = jnp.transpose(conv_s, (1, 0, 2))[None], jnp.transpose(pool_s, (1, 0, 2))[None]
    y_p = _merge_mlp(*rows_p, prm, tm=512).reshape(x_prompt.shape)
    y_s = _merge_mlp(*rows_s, prm, tm=256).reshape(x_sample.shape)
    return (y_p, y_s, conv_p.astype(state_conv.dtype), pool_p.astype(state_pool.dtype),
            ssm_p.astype(state_ssm.dtype), conv_s.astype(state_conv.dtype), pool_s.astype(state_pool.dtype),
            ssm_s.astype(state_ssm.dtype))
```

```python
import functools
import math

import jax
import jax.numpy as jnp
from jax import lax
from jax.experimental import pallas as pl
from jax.experimental.pallas import tpu as pltpu

D_MODEL = 1024
N_HEADS = 4
HEAD_DIM = 128
QK_W = N_HEADS * HEAD_DIM
QKV_W = 3 * QK_W
CONV_W = 4
POOL_WINDOWS = (2, 4, 8, 16)
POOL_GROUP = 128
WIDTH_B = len(POOL_WINDOWS) * POOL_GROUP
POOL_HIST = 15
D_FF = 4 * D_MODEL
EPS = 1e-6
PAST_LEN = 16384
LANES = 128
SUBLANES = 8
MXU_COLS = 256
N_GATE_ROWS = 4

QKVZ_W = QKV_W + QK_W
GATE_SCALARS = 2 * N_HEADS
REST_W = WIDTH_B + 2 * D_MODEL
W_GATE0 = QKVZ_W
W_REST0 = W_GATE0 + 2 * SUBLANES
W_ROWS = W_REST0 + REST_W
REST_Z0 = 2 * D_MODEL
REST_P0 = REST_Z0 + QK_W
REST_OUT_W = REST_P0 + WIDTH_B

GROUP = 128
SERIES_BLOCK = 64
CONV_PAD = 8
POOL_LOOKBACK = 16
POOL_PAD = 24
FF_BLOCK = 1024

VMEM_LIMIT = 56 * 1024 * 1024

BF16 = jnp.bfloat16
F32 = jnp.float32


def _dot(a, b):
    return jnp.dot(a.astype(BF16), b.astype(BF16), preferred_element_type=F32)


def _sigmoid(x):
    return 1.0 / (1.0 + jnp.exp(-x))


def _silu_of_twice(half):
    return half * jnp.tanh(half) + half


def _silu(x):
    return _silu_of_twice(0.5 * x)


def _rms_scale(x):
    return lax.rsqrt(jnp.mean(x * x, axis=-1, keepdims=True) + EPS)


def _const_spec(shape):
    zeros = (0,) * len(shape)
    return pl.BlockSpec(shape, lambda *_: zeros, pipeline_mode=pl.Buffered(1))


def _seq_spec(bb, tl, width):
    return pl.BlockSpec((bb, tl, width), lambda i, j: (i, j, 0))


def _state_spec(bb, n_rows, width, rows_first):
    if rows_first:
        return pl.BlockSpec((n_rows, bb, width), lambda i, j: (0, i, 0))
    return pl.BlockSpec((bb, n_rows, width), lambda i, j: (i, 0, 0))


def _state_shape(bsz, n_rows, width, rows_first):
    return jax.ShapeDtypeStruct((n_rows, bsz, width) if rows_first else (bsz, n_rows, width), F32)


def _history_to_scratch(scratch_ref, first_row, state_ref, rows_first):
    if rows_first:
        for j in range(state_ref.shape[0]):
            scratch_ref[:, first_row + j, :] = state_ref[j]
    else:
        scratch_ref[:, first_row:first_row + state_ref.shape[1], :] = state_ref[...]


def _history_from_scratch(state_ref, scratch_ref, first_row, rows_first):
    if rows_first:
        for j in range(state_ref.shape[0]):
            state_ref[j] = scratch_ref[:, first_row + j, :]
    else:
        state_ref[...] = scratch_ref[:, first_row:first_row + state_ref.shape[1], :]


def _gate_rows_spec(bb, tl, n_tiles):
    return pl.BlockSpec((N_GATE_ROWS, SUBLANES, bb * tl), lambda i, j: (0, 0, i * n_tiles + j))


def _lane_prefix_sum(x, chunk):
    lane = lax.broadcasted_iota(jnp.int32, x.shape, 1)
    shift = 1
    while shift < chunk:
        x = x + jnp.where(lane % chunk >= shift, pltpu.roll(x, shift, axis=1), 0.0)
        shift *= 2
    return x


def _lane_suffix_sum(x, chunk):
    lane = lax.broadcasted_iota(jnp.int32, x.shape, 1)
    shift = 1
    while shift < chunk:
        x = x + jnp.where(lane % chunk + shift < chunk, pltpu.roll(x, x.shape[1] - shift, axis=1), 0.0)
        shift *= 2
    return x


def _front_kernel(x_ref, g_ref, w_ref, wconv_ref, alog_ref, dtb_ref, keep_ref, convprev_ref,
                  qkv_ref, rest_ref, gsc_ref, convnew_ref,
                  *tail, bb, tl, n_tiles, chunk, rows_first, convert_weights):
    tile = pl.program_id(1)
    rows = bb * tl
    hist = slice(CONV_PAD - (CONV_W - 1), CONV_PAD)
    ext_ref = tail[-1]

    if convert_weights:
        w_src_ref, w_ref = w_ref, tail[0]
        src_rest0 = W_GATE0 + GATE_SCALARS

        @pl.when((pl.program_id(0) == 0) & (tile == 0))
        def _():
            w_ref[0:W_GATE0, :] = w_src_ref[0:W_GATE0, :].astype(BF16)
            gate_rows = jnp.concatenate([w_src_ref[W_GATE0:src_rest0, :],
                                         jnp.zeros((W_REST0 - W_GATE0 - GATE_SCALARS, D_MODEL), F32)], axis=0)
            w_ref[W_GATE0:W_REST0, :] = gate_rows.astype(BF16)
            w_ref[W_REST0:W_ROWS, :] = w_src_ref[src_rest0:src_rest0 + REST_W, :].astype(BF16)

    @pl.when(tile == 0)
    def _():
        _history_to_scratch(ext_ref, hist.start, convprev_ref, rows_first)
        ext_ref[:, 0:CONV_PAD - (CONV_W - 1), :] = jnp.zeros((bb, CONV_PAD - (CONV_W - 1), QKV_W), F32)

    x = x_ref[...].reshape(rows, D_MODEL)
    normed = (x * _rms_scale(x) * g_ref[...]).astype(BF16)
    contract_last = (((1,), (1,)), ((), ()))
    proj = lambda w_rows: lax.dot_general(normed, w_rows, contract_last, preferred_element_type=F32)
    ext_ref[:, CONV_PAD:CONV_PAD + tl, :] = proj(w_ref[0:QKV_W, :]).reshape(bb, tl, QKV_W)
    _history_from_scratch(convnew_ref, ext_ref, CONV_PAD + tl - (CONV_W - 1), rows_first)

    x8 = lax.dot_general(w_ref[W_GATE0:W_REST0, :], normed, contract_last, preferred_element_type=F32)[0:SUBLANES, :]
    xs = x8 + dtb_ref[...]
    softplus = jnp.maximum(xs, 0.0) + jnp.log1p(jnp.exp(-jnp.abs(xs)))
    graw8 = -jnp.exp(alog_ref[...]) * softplus
    g8 = _lane_prefix_sum(graw8, chunk)
    gsc_ref[0] = _sigmoid(x8)
    gsc_ref[1] = g8
    gsc_ref[2] = jnp.exp(g8)
    gsc_ref[3] = jnp.exp(_lane_suffix_sum(graw8, chunk) - graw8)

    keep = keep_ref[...] != 0

    def conv_block(c0, dep):
        cols = slice(c0, c0 + LANES)
        xe = ext_ref[:, :, cols].reshape(bb * (CONV_PAD + tl), LANES)
        x1 = pltpu.roll(xe, 1, axis=0)
        tap = lambda j: 0.5 * wconv_ref[j:j + 1, cols]
        acc = (tap(3) * xe + tap(2) * x1) + pltpu.roll(tap(1) * xe + tap(0) * x1, 2, axis=0)
        val = _silu_of_twice(acc.reshape(bb, CONV_PAD + tl, LANES)[:, CONV_PAD:, :])
        if c0 < 2 * QK_W:
            scale = HEAD_DIM ** -0.5 if c0 < QK_W else 1.0
            val = val * (lax.rsqrt(jnp.sum(val * val, axis=-1, keepdims=True) + EPS) * scale)
        qkv_ref[:, :, cols] = jnp.where(keep, val, dep[:, 0:LANES].reshape(bb, tl, LANES)).astype(qkv_ref.dtype)

    def proj_block(out0, w0, c0):
        val = proj(w_ref[w0 + c0:w0 + c0 + MXU_COLS, :])
        rest_ref[:, :, out0 + c0:out0 + c0 + MXU_COLS] = val.reshape(bb, tl, MXU_COLS)
        return val

    mxu_work = ([functools.partial(proj_block, REST_Z0, QKV_W, c0) for c0 in range(0, QK_W, MXU_COLS)]
                + [functools.partial(proj_block, REST_P0, W_REST0, c0) for c0 in range(0, WIDTH_B, MXU_COLS)]
                + [functools.partial(proj_block, 0, W_REST0 + WIDTH_B, c0)
                   for c0 in range(0, 2 * D_MODEL, MXU_COLS)])
    for i, c0 in enumerate(range(0, QKV_W, LANES)):
        conv_block(c0, mxu_work[i]())

    if n_tiles > 1:
        _history_to_scratch(ext_ref, hist.start, convnew_ref, rows_first)


def _front(x, g_attn, w_proj, w_conv, a_log, dt_bias, conv_prev, *, bb, tl, chunk, act_dtype, rows_first):
    convert_weights = w_proj.dtype != BF16
    assert w_proj.shape == ((W_ROWS - (W_REST0 - W_GATE0 - GATE_SCALARS) if convert_weights else W_ROWS), D_MODEL)
    bsz, l, _ = x.shape
    assert bsz % bb == 0 and l % tl == 0 and (bb * tl) % chunk == 0, "blocks must tile the batch, length and chunks"
    assert bb == 1 or tl == l, "a block of several sequences must hold them whole (gate rows are token-major)"
    assert CONV_W == 4, "the conv is written out as two pairs of taps"
    n_tiles = l // tl
    kern = functools.partial(_front_kernel, bb=bb, tl=tl, n_tiles=n_tiles, chunk=chunk, rows_first=rows_first,
                             convert_weights=convert_weights)
    sds = lambda *shape: jax.ShapeDtypeStruct(shape, F32)
    w_out_specs = [_const_spec((W_ROWS, D_MODEL))] if convert_weights else []
    w_out_shapes = [jax.ShapeDtypeStruct((W_ROWS, D_MODEL), BF16)] if convert_weights else []
    zeros4 = jnp.zeros((N_HEADS,), F32)
    rows8 = lambda v: jnp.broadcast_to(jnp.concatenate([zeros4, v.astype(F32)])[:, None], (SUBLANES, bb * tl))
    return pl.pallas_call(
        kern,
        grid=(bsz // bb, n_tiles),
        in_specs=[_seq_spec(bb, tl, D_MODEL), _const_spec((1, D_MODEL)), _const_spec(w_proj.shape),
                  _const_spec((CONV_W, QKV_W)),
                  _const_spec((SUBLANES, bb * tl)), _const_spec((SUBLANES, bb * tl)), _const_spec((1, LANES)),
                  _state_spec(bb, CONV_W - 1, QKV_W, rows_first)],
        out_specs=[_seq_spec(bb, tl, QKV_W), _seq_spec(bb, tl, REST_OUT_W), _gate_rows_spec(bb, tl, n_tiles),
                   _state_spec(bb, CONV_W - 1, QKV_W, rows_first)] + w_out_specs,
        out_shape=[jax.ShapeDtypeStruct((bsz, l, QKV_W), act_dtype), sds(bsz, l, REST_OUT_W),
                   sds(N_GATE_ROWS, SUBLANES, bsz * l),
                   _state_shape(bsz, CONV_W - 1, QKV_W, rows_first)] + w_out_shapes,
        scratch_shapes=[pltpu.VMEM((bb, CONV_PAD + tl, QKV_W), F32)],
        compiler_params=pltpu.CompilerParams(dimension_semantics=("arbitrary", "arbitrary"),
                                             vmem_limit_bytes=VMEM_LIMIT),
        name="front",
    )(x, g_attn, w_proj, w_conv, rows8(a_log), rows8(dt_bias), jnp.ones((1, LANES), jnp.int32), conv_prev)


def _unit_lower_inverses(ms, chunk, side_jobs=()):
    nb = min(chunk, SERIES_BLOCK)
    n_blocks = GROUP // nb
    n_factors = int(math.log2(nb))
    assert chunk == nb or (chunk == 2 * nb and n_blocks == 2), "chunks are one or two series blocks"
    lane = lax.broadcasted_iota(jnp.int32, (nb, GROUP), 1)
    lane_block = lane // nb

    def packed(m):
        out = m[0:nb]
        for b in range(1, n_blocks):
            out = jnp.where(lane_block == b, m[b * nb:(b + 1) * nb], out)
        return out

    def block_diag(p):
        return jnp.concatenate([jnp.where(lane_block == b, p, 0.0) for b in range(n_blocks)], axis=0)

    side_jobs = list(side_jobs)
    per_step = -(-len(side_jobs) // max(n_factors - 1, 1))

    def issue_side_jobs(step_values):
        for k in range(min(per_step, len(side_jobs))):
            side_jobs.pop(0)(step_values[k % len(step_values)])

    nps = [packed(m) for m in ms]
    eye_p = jnp.where(lane % nb == lax.broadcasted_iota(jnp.int32, (nb, GROUP), 0), 1.0, 0.0).astype(F32)
    invs = [eye_p - n for n in nps]
    if n_factors > 1:
        pws = [_dot(n, block_diag(n)) for n in nps]
        issue_side_jobs(pws)
        for _ in range(n_factors - 2):
            boths = [_dot(jnp.concatenate([inv, pw], axis=0), block_diag(pw)) for inv, pw in zip(invs, pws)]
            invs = [inv + both[0:nb] for inv, both in zip(invs, boths)]
            pws = [both[nb:] for both in boths]
            issue_side_jobs(pws)
        invs = [inv + _dot(inv, block_diag(pw)) for inv, pw in zip(invs, pws)]
    while side_jobs:
        side_jobs.pop(0)(None)
    if chunk == nb:
        return [block_diag(inv) for inv in invs]
    zeros = jnp.zeros((nb, GROUP), F32)
    a_invs = [jnp.where(lane < nb, inv, 0.0) for inv in invs]
    lows = [jnp.where(lane < nb, m[nb:], 0.0) for m in ms]
    xs = [_dot(low, jnp.concatenate([a_inv, zeros], axis=0)) for low, a_inv in zip(lows, a_invs)]
    ys = [_dot(inv, jnp.concatenate([zeros, x], axis=0)) for inv, x in zip(invs, xs)]
    return [jnp.concatenate([a_inv, jnp.where(lane >= nb, inv, 0.0) - y], axis=0)
            for a_inv, y, inv in zip(a_invs, ys, invs)]


def _delta_kernel(*refs, bb, tl, chunk, pos0, n_tiles, n_gate_refs, n_casts, rows_first):
    gsc_refs, refs = refs[:n_gate_refs], refs[n_gate_refs:]
    (qkv_ref, zp_ref, poolprev_ref, s0_ref, wonorm_ref, wmix_ref, pscale_ref, keep_ref) = refs[0:8]
    cast_src, refs = refs[8:8 + n_casts], refs[8 + n_casts:]
    oab_ref, poolnew_ref, snew_ref = refs[0:3]
    cast_dst, pext_ref = refs[3:3 + n_casts], refs[3 + n_casts]
    tile = pl.program_id(1)

    for src, dst in zip(cast_src, cast_dst):
        dst[...] = src[...].astype(dst.dtype)

    rows_b = min(tl, GROUP)
    seqs_g = GROUP // rows_b
    groups_b = tl // rows_b
    n_groups = bb * tl // GROUP
    chained = chunk == GROUP
    assert chained or (chunk == tl and tl < GROUP), "chunk must be a whole group or a whole short sequence"
    pool_hist = slice(POOL_PAD - POOL_HIST, POOL_PAD)

    @pl.when(tile == 0)
    def _():
        snew_ref[...] = s0_ref[...]
        _history_to_scratch(pext_ref, pool_hist.start, poolprev_ref, rows_first)
        pext_ref[:, POOL_PAD - POOL_LOOKBACK:POOL_PAD - POOL_HIST, :] = jnp.zeros((bb, 1, WIDTH_B), F32)

    pext_ref[:, POOL_PAD:POOL_PAD + tl, :] = zp_ref[:, :, QK_W:QK_W + WIDTH_B]
    _history_from_scratch(poolnew_ref, pext_ref, POOL_PAD + tl - POOL_HIST, rows_first)

    row = lax.broadcasted_iota(jnp.int32, (GROUP, GROUP), 0)
    col = lax.broadcasted_iota(jnp.int32, (GROUP, GROUP), 1)
    causal = row >= col
    strict = row > col
    if not chained:
        same = (row // chunk) == (col // chunk)
        causal = causal & same
        strict = strict & same

    def origin(g):
        if tl >= GROUP:
            return g // groups_b, (g % groups_b) * GROUP
        return g * seqs_g, 0

    def load(c0, g):
        b0, t0 = origin(g)
        return qkv_ref[b0:b0 + seqs_g, t0:t0 + rows_b, c0:c0 + HEAD_DIM].astype(F32).reshape(GROUP, HEAD_DIM)

    def gate_rows(g):
        if n_gate_refs > 1:
            b0, t0 = origin(g)
            return tuple(gsc_refs[b0][q, :, t0:t0 + GROUP] for q in range(N_GATE_ROWS))
        return tuple(gsc_refs[0][q, :, g * GROUP:(g + 1) * GROUP] for q in range(N_GATE_ROWS))

    gates = [gate_rows(g) for g in range(n_groups)]

    probs = [(g, h) for g in range(n_groups) for h in range(N_HEADS)]
    st = []
    for g, h in probs:
        beta8, g8, eg8, kds8 = gates[g]
        d = {"q": load(h * HEAD_DIM, g), "k": load(QK_W + h * HEAD_DIM, g), "v": load(2 * QK_W + h * HEAD_DIM, g)}
        d["beta_row"] = beta8[h:h + 1, :]
        d["eg_row"] = eg8[N_HEADS + h:N_HEADS + h + 1, :]
        d["g_row"] = g8[N_HEADS + h:N_HEADS + h + 1, :]
        d["kb_row"] = kds8[N_HEADS + h:N_HEADS + h + 1, :] * d["beta_row"]
        d["g_col"] = jnp.broadcast_to(d["g_row"], (GROUP, GROUP)).T
        d["kt"] = d["k"].T
        st.append(d)
    for d in st:
        both = _dot(jnp.concatenate([d["k"], d["q"]], axis=0), d["kt"])
        d["kk"], d["qk"] = both[0:GROUP], both[GROUP:]
    for d in st:
        decay = jnp.exp(jnp.where(causal, d["g_col"] - d["g_row"], -jnp.inf))
        decay_beta = decay * d["beta_row"]
        d["m"] = jnp.where(strict, d.pop("kk") * decay_beta, 0.0)
        d["qkm"] = d.pop("qk") * decay_beta
        d["q_dec"] = d.pop("q") * jnp.exp(d["g_col"])
        d["kt_dec"] = d.pop("kt") * d["kb_row"]

    keep = keep_ref[...] != 0

    def pool_block(g, gi, issued_with):
        b0, t0 = origin(g)
        win = POOL_WINDOWS[gi]
        cols = slice(gi * POOL_GROUP, (gi + 1) * POOL_GROUP)
        pos = pos0 + tile * tl + t0 + row % rows_b
        slab = pext_ref[b0:b0 + seqs_g, POOL_PAD - POOL_LOOKBACK + t0:POOL_PAD + t0 + rows_b, cols]
        acc = slab.reshape(seqs_g * (POOL_LOOKBACK + rows_b), POOL_GROUP)
        shift = 1
        while shift < win:
            acc = acc + pltpu.roll(acc, shift, axis=0)
            shift *= 2
        acc = acc.reshape(seqs_g, POOL_LOOKBACK + rows_b, POOL_GROUP)[:, POOL_LOOKBACK:, :]
        cur = slab[:, POOL_LOOKBACK:, :]
        pooled = (acc / jnp.minimum(pos + 1, win).astype(F32).reshape(seqs_g, rows_b, POOL_GROUP) - cur)
        mixed = _dot(pooled.reshape(GROUP, POOL_GROUP), wmix_ref[gi]) * pscale_ref[:, cols]
        if issued_with is not None:
            reps = GROUP // issued_with.shape[0]
            mixed = jnp.where(keep, mixed, jnp.concatenate([issued_with] * reps, axis=0))
        oab_ref[b0:b0 + seqs_g, t0:t0 + rows_b, QK_W + gi * POOL_GROUP:QK_W + (gi + 1) * POOL_GROUP] = (
            mixed.reshape(seqs_g, rows_b, POOL_GROUP).astype(oab_ref.dtype))

    pool_jobs = [functools.partial(pool_block, g, gi) for g in range(n_groups) for gi in range(len(POOL_WINDOWS))]
    invs = _unit_lower_inverses([d.pop("m") for d in st], chunk, pool_jobs)
    for d, inv in zip(st, invs):
        d["uy"] = _dot(inv, d.pop("v"))
        d["wy"] = _dot(inv * d["eg_row"], d.pop("k"))

    def group_slab(ref, g, cols, row_off=0):
        b0, t0 = origin(g)
        return ref.at[b0:b0 + seqs_g, row_off + t0:row_off + t0 + rows_b, cols]

    def head_out(g, h, o):
        cols = slice(h * HEAD_DIM, (h + 1) * HEAD_DIM)
        z = group_slab(zp_ref, g, cols)[...].reshape(GROUP, HEAD_DIM)
        group_slab(oab_ref, g, cols)[...] = (o * _rms_scale(o) * wonorm_ref[...] * _silu(z)).reshape(
            seqs_g, rows_b, HEAD_DIM).astype(oab_ref.dtype)

    if chained:
        for j in range(groups_b):
            wave = [(i, g, h) for i, (g, h) in enumerate(probs) if g % groups_b == j]
            s_old = {i: snew_ref[origin(g)[0], h] for i, g, h in wave}
            ws = {i: _dot(jnp.concatenate([st[i]["wy"], st[i]["q_dec"]], axis=0), s_old[i]) for i, g, h in wave}
            ys = {i: st[i]["uy"] - ws[i][0:GROUP] for i, g, h in wave}
            outs = {i: ws[i][GROUP:] + _dot(st[i]["qkm"], ys[i]) for i, g, h in wave}
            for i, g, h in wave:
                last = jnp.exp(st[i]["g_col"][GROUP - 1:GROUP, :])
                snew_ref[origin(g)[0], h] = s_old[i] * last + _dot(st[i]["kt_dec"], ys[i])
            for i, g, h in wave:
                head_out(g, h, outs[i])
    else:
        outs = {}
        for i, (g, h) in enumerate(probs):
            d, b0 = st[i], origin(g)[0]
            ws_w, ws_q = [], []
            for s_i in range(seqs_g):
                r = slice(s_i * rows_b, (s_i + 1) * rows_b)
                ws = _dot(jnp.concatenate([d["wy"][r], d["q_dec"][r]], axis=0), snew_ref[b0 + s_i, h])
                ws_w.append(ws[0:rows_b])
                ws_q.append(ws[rows_b:])
            d["y"] = d["uy"] - jnp.concatenate(ws_w, axis=0)
            outs[i] = jnp.concatenate(ws_q, axis=0) + _dot(d["qkm"], d["y"])
        for i, (g, h) in enumerate(probs):
            d, b0 = st[i], origin(g)[0]
            for s_i in range(seqs_g):
                last = jnp.exp(d["g_col"][(s_i + 1) * rows_b - 1:(s_i + 1) * rows_b, :])
                upd = _dot(jnp.where(col // rows_b == s_i, d["kt_dec"], 0.0), d["y"])
                snew_ref[b0 + s_i, h] = snew_ref[b0 + s_i, h] * last + upd
        for i, (g, h) in enumerate(probs):
            head_out(g, h, outs[i])

    if n_tiles > 1:
        _history_to_scratch(pext_ref, pool_hist.start, poolnew_ref, rows_first)


def _delta(qkv, rest, gsc, pool_prev, s0, w_onorm, w_mix, pool_scale, *, bb, tl, chunk, pos0, act_dtype,
           rows_first, cast_to_bf16=()):
    bsz, l = qkv.shape[0], qkv.shape[1]
    assert bsz % bb == 0 and l % tl == 0 and (bb * tl) % GROUP == 0, "blocks must tile the batch, length and groups"
    n_tiles = l // tl
    n_steps = (bsz // bb) * n_tiles
    cast_specs = []
    for arr, axis in cast_to_bf16:
        tile = LANES if axis == 1 else 2 * SUBLANES
        assert arr.shape[axis] % (n_steps * tile) == 0, "each step converts a tile-aligned slice"
        blk = tuple(d // n_steps if a == axis else d for a, d in enumerate(arr.shape))
        cast_specs.append(pl.BlockSpec(blk, lambda i, j, axis=axis: tuple(
            i * n_tiles + j if a == axis else 0 for a in range(2))))
    state_s = pl.BlockSpec((bb, N_HEADS, HEAD_DIM, HEAD_DIM), lambda i, j: (i, 0, 0, 0))
    if n_tiles == 1:
        gate_specs = [_gate_rows_spec(bb, tl, 1)]
    else:
        gate_specs = [pl.BlockSpec((N_GATE_ROWS, SUBLANES, tl), lambda i, j, k=k: (0, 0, (i * bb + k) * n_tiles + j))
                      for k in range(bb)]
    kern = functools.partial(_delta_kernel, bb=bb, tl=tl, chunk=chunk, pos0=pos0, n_tiles=n_tiles,
                             n_gate_refs=len(gate_specs), n_casts=len(cast_specs), rows_first=rows_first)
    sds = lambda *shape: jax.ShapeDtypeStruct(shape, F32)
    return pl.pallas_call(
        kern,
        grid=(bsz // bb, n_tiles),
        in_specs=gate_specs + [
                  _seq_spec(bb, tl, QKV_W),
                  pl.BlockSpec((bb, tl, QK_W + WIDTH_B), lambda i, j: (i, j, REST_Z0 // (QK_W + WIDTH_B))),
                  _state_spec(bb, POOL_HIST, WIDTH_B, rows_first), state_s,
                  _const_spec((1, HEAD_DIM)), _const_spec((len(POOL_WINDOWS), POOL_GROUP, POOL_GROUP)),
                  _const_spec((1, WIDTH_B)), _const_spec((1, LANES))] + cast_specs,
        out_specs=[_seq_spec(bb, tl, QK_W + WIDTH_B),
                   _state_spec(bb, POOL_HIST, WIDTH_B, rows_first), state_s] + cast_specs,
        out_shape=[jax.ShapeDtypeStruct((bsz, l, QK_W + WIDTH_B), act_dtype),
                   _state_shape(bsz, POOL_HIST, WIDTH_B, rows_first),
                   sds(bsz, N_HEADS, HEAD_DIM, HEAD_DIM)]
                  + [jax.ShapeDtypeStruct(arr.shape, BF16) for arr, _ in cast_to_bf16],
        scratch_shapes=[pltpu.VMEM((bb, POOL_PAD + tl, WIDTH_B), F32)],
        compiler_params=pltpu.CompilerParams(dimension_semantics=("arbitrary", "arbitrary"),
                                             vmem_limit_bytes=VMEM_LIMIT),
        name="delta",
    )(*([gsc] * len(gate_specs)), qkv, rest, pool_prev, s0, w_onorm, w_mix, pool_scale,
      jnp.ones((1, LANES), jnp.int32), *[arr for arr, _ in cast_to_bf16])


def _mlp_kernel(x_ref, oab_ref, gate_ref, wa_ref, wb_ref, wo_ref, gmlp_ref, wup_ref, wdown_ref, gfin_ref, y_ref):
    ma = _dot(oab_ref[:, 0:QK_W], wa_ref[...])
    mb = _dot(oab_ref[:, QK_W:QK_W + WIDTH_B], wb_ref[...])
    merged = _sigmoid(gate_ref[:, 0:D_MODEL]) * ma + _sigmoid(gate_ref[:, D_MODEL:]) * mb
    x1 = x_ref[...] + _dot(merged, wo_ref[...])
    h2 = (x1 * _rms_scale(x1) * gmlp_ref[...]).astype(BF16)
    acc = x1
    for c0 in range(0, D_FF, FF_BLOCK):
        up = jnp.dot(h2, wup_ref[:, c0:c0 + FF_BLOCK], preferred_element_type=F32)
        act = jnp.square(jnp.maximum(up, 0.0))
        acc = acc + _dot(act, wdown_ref[c0:c0 + FF_BLOCK, :])
    y_ref[...] = acc * _rms_scale(acc) * gfin_ref[...]


def _merge_mlp(x2d, oab, rest, prm, tm):
    t = x2d.shape[0]
    assert t % tm == 0, "row tiles must cover the tokens exactly"
    row = lambda w: pl.BlockSpec((tm, w), lambda i: (i, 0))
    return pl.pallas_call(
        _mlp_kernel,
        grid=(t // tm,),
        in_specs=[row(D_MODEL), row(QK_W + WIDTH_B), row(2 * D_MODEL),
                  _const_spec((QK_W, D_MODEL)), _const_spec((WIDTH_B, D_MODEL)), _const_spec((D_MODEL, D_MODEL)),
                  _const_spec((1, D_MODEL)), _const_spec((D_MODEL, D_FF)), _const_spec((D_FF, D_MODEL)),
                  _const_spec((1, D_MODEL))],
        out_specs=row(D_MODEL),
        out_shape=jax.ShapeDtypeStruct((t, D_MODEL), F32),
        compiler_params=pltpu.CompilerParams(dimension_semantics=("arbitrary",),
                                             vmem_limit_bytes=VMEM_LIMIT),
        name="merge_mlp",
    )(x2d, oab, rest, prm["w_a_out"], prm["w_b_out"], prm["w_o"], prm["g_mlp"], prm["w_up"], prm["w_down"],
      prm["g_final"])


def _mix(x, conv_prev, pool_prev, s_prev, pos0, prm, *, front_blk, delta_blk, chunk, act_dtype, rows_first,
         cast_to_bf16=()):
    bsz, l, _ = x.shape
    t = bsz * l
    qkv, rest, gsc, conv_new, *w_proj_bf16 = _front(
        x, prm["g_attn"], prm["w_proj"], prm["w_conv"], prm["a_log"], prm["dt_bias"],
        conv_prev, bb=front_blk[0], tl=front_blk[1], chunk=chunk, act_dtype=act_dtype, rows_first=rows_first)
    oab, pool_new, s_new, *casted = _delta(
        qkv, rest, gsc, pool_prev, s_prev, prm["w_onorm"], prm["w_mix"], prm["pool_scale"], bb=delta_blk[0],
        tl=delta_blk[1], chunk=chunk, pos0=pos0, act_dtype=act_dtype, rows_first=rows_first,
        cast_to_bf16=cast_to_bf16)
    rows = (x.reshape(t, D_MODEL), oab.reshape(t, QK_W + WIDTH_B), rest.reshape(t, REST_OUT_W))
    return rows, (conv_new, pool_new, s_new[None]), w_proj_bf16 + casted


def kernel(x_prompt, x_sample, state_conv, state_pool, state_ssm, w_in, w_conv, a_log, dt_bias, w_onorm,
           w_pool_mix, pool_scale, w_a_out, w_b_out, w_o, g_attn, g_mlp, w_up, w_down, g_final):
    assert w_in.shape[0] == 1, "single-layer decoder"
    prm = {
        "w_proj": jnp.transpose(w_in[0]).astype(F32),
        "g_attn": g_attn[0][None, :], "g_mlp": g_mlp[0][None, :], "g_final": g_final[None, :],
        "w_conv": w_conv[0].astype(F32),
        "a_log": a_log[0], "dt_bias": dt_bias[0],
        "w_onorm": w_onorm[0][None, :].astype(F32),
        "w_mix": w_pool_mix[0].astype(BF16), "pool_scale": pool_scale[0][None, :].astype(F32),
    }
    bp = x_prompt.shape[0]
    converted = ("w_proj", "w_up", "w_down", "w_o", "w_a_out", "w_b_out")
    rows_p, (conv_p, pool_p, ssm_p), casted = _mix(
        x_prompt, jnp.zeros((bp, CONV_W - 1, QKV_W), F32), jnp.zeros((bp, POOL_HIST, WIDTH_B), F32),
        jnp.zeros((bp, N_HEADS, HEAD_DIM, HEAD_DIM), F32), 0, prm,
        front_blk=(1, 512), delta_blk=(4, 256), chunk=GROUP, act_dtype=BF16, rows_first=False,
        cast_to_bf16=((w_up[0].astype(F32), 1), (w_down[0].astype(F32), 0), (w_o[0].astype(F32), 0),
                      (w_a_out[0].astype(F32), 0), (w_b_out[0].astype(F32), 0)))
    prm.update(zip(converted, casted, strict=True))
    conv_p, pool_p = conv_p[None], pool_p[None]
    dec_len = x_sample.shape[1]
    rows_major = lambda s: jnp.transpose(s[0].astype(F32), (1, 0, 2))
    rows_s, (conv_s, pool_s, ssm_s), _ = _mix(
        x_sample, rows_major(state_conv), rows_major(state_pool), state_ssm[0].astype(F32), PAST_LEN, prm,
        front_blk=(256 // dec_len, dec_len), delta_blk=(2 * GROUP // dec_len, dec_len),
        chunk=dec_len, act_dtype=F32, rows_first=True)
    conv_s, pool_s = jnp.transpose(conv_s, (1, 0, 2))[None], jnp.transpose(pool_s, (1, 0, 2))[None]
    y_p = _merge_mlp(*rows_p, prm, tm=512).reshape(x_prompt.shape)
    y_s = _merge_mlp(*rows_s, prm, tm=256).reshape(x_sample.shape)
    return (y_p, y_s, conv_p.astype(state_conv.dtype), pool_p.astype(state_pool.dtype),
            ssm_p.astype(state_ssm.dtype), conv_s.astype(state_conv.dtype), pool_s.astype(state_pool.dtype),
            ssm_s.astype(state_ssm.dtype))
```

```python
import functools
import math

import jax
import jax.numpy as jnp
from jax import lax
from jax.experimental import pallas as pl
from jax.experimental.pallas import tpu as pltpu

D_MODEL = 1024
N_HEADS = 4
HEAD_DIM = 128
QK_W = N_HEADS * HEAD_DIM
QKV_W = 3 * QK_W
CONV_W = 4
POOL_WINDOWS = (2, 4, 8, 16)
POOL_GROUP = 128
WIDTH_B = len(POOL_WINDOWS) * POOL_GROUP
POOL_HIST = 15
D_FF = 4 * D_MODEL
EPS = 1e-6
PAST_LEN = 16384
LANES = 128
SUBLANES = 8
MXU_COLS = 256
N_GATE_ROWS = 4

QKVZ_W = QKV_W + QK_W
GATE_SCALARS = 2 * N_HEADS
REST_W = WIDTH_B + 2 * D_MODEL
W_GATE0 = QKVZ_W
W_REST0 = W_GATE0 + 2 * SUBLANES
W_ROWS = W_REST0 + REST_W
REST_Z0 = 2 * D_MODEL
REST_P0 = REST_Z0 + QK_W
REST_OUT_W = REST_P0 + WIDTH_B

GROUP = 128
SERIES_BLOCK = 64
CONV_PAD = 8
POOL_LOOKBACK = 16
POOL_PAD = 24
FF_BLOCK = 1024

VMEM_LIMIT = 56 * 1024 * 1024

BF16 = jnp.bfloat16
F32 = jnp.float32


def _dot(a, b):
    return jnp.dot(a.astype(BF16), b.astype(BF16), preferred_element_type=F32)


def _sigmoid(x):
    return 1.0 / (1.0 + jnp.exp(-x))


def _silu_of_twice(half):
    return half * jnp.tanh(half) + half


def _silu(x):
    return _silu_of_twice(0.5 * x)


def _rms_scale(x):
    return lax.rsqrt(jnp.mean(x * x, axis=-1, keepdims=True) + EPS)


def _const_spec(shape):
    zeros = (0,) * len(shape)
    return pl.BlockSpec(shape, lambda *_: zeros, pipeline_mode=pl.Buffered(1))


def _seq_spec(bb, tl, width):
    return pl.BlockSpec((bb, tl, width), lambda i, j: (i, j, 0))


def _state_spec(bb, n_rows, width, rows_first):
    if rows_first:
        return pl.BlockSpec((n_rows, bb, width), lambda i, j: (0, i, 0))
    return pl.BlockSpec((bb, n_rows, width), lambda i, j: (i, 0, 0))


def _state_shape(bsz, n_rows, width, rows_first):
    return jax.ShapeDtypeStruct((n_rows, bsz, width) if rows_first else (bsz, n_rows, width), F32)


def _history_to_scratch(scratch_ref, first_row, state_ref, rows_first):
    if rows_first:
        for j in range(state_ref.shape[0]):
            scratch_ref[:, first_row + j, :] = state_ref[j]
    else:
        scratch_ref[:, first_row:first_row + state_ref.shape[1], :] = state_ref[...]


def _history_from_scratch(state_ref, scratch_ref, first_row, rows_first):
    if rows_first:
        for j in range(state_ref.shape[0]):
            state_ref[j] = scratch_ref[:, first_row + j, :]
    else:
        state_ref[...] = scratch_ref[:, first_row:first_row + state_ref.shape[1], :]


def _gate_rows_spec(bb, tl, n_tiles):
    return pl.BlockSpec((N_GATE_ROWS, SUBLANES, bb * tl), lambda i, j: (0, 0, i * n_tiles + j))


def _lane_prefix_sum(x, chunk):
    lane = lax.broadcasted_iota(jnp.int32, x.shape, 1)
    shift = 1
    while shift < chunk:
        x = x + jnp.where(lane % chunk >= shift, pltpu.roll(x, shift, axis=1), 0.0)
        shift *= 2
    return x


def _lane_suffix_sum(x, chunk):
    lane = lax.broadcasted_iota(jnp.int32, x.shape, 1)
    shift = 1
    while shift < chunk:
        x = x + jnp.where(lane % chunk + shift < chunk, pltpu.roll(x, x.shape[1] - shift, axis=1), 0.0)
        shift *= 2
    return x


def _front_kernel(x_ref, g_ref, w_ref, wconv_ref, decay_ref, keep_ref, convprev_ref,
                  qkv_ref, rest_ref, gsc_ref, convnew_ref,
                  *tail, bb, tl, n_tiles, chunk, rows_first, convert_weights):
    tile = pl.program_id(1)
    rows = bb * tl
    hist = slice(CONV_PAD - (CONV_W - 1), CONV_PAD)
    ext_ref = tail[-1]

    if convert_weights:
        w_src_ref, w_ref = w_ref, tail[0]
        src_rest0 = W_GATE0 + GATE_SCALARS

        @pl.when((pl.program_id(0) == 0) & (tile == 0))
        def _():
            w_ref[0:W_GATE0, :] = w_src_ref[0:W_GATE0, :].astype(BF16)
            gate_rows = jnp.concatenate([w_src_ref[W_GATE0:src_rest0, :],
                                         jnp.zeros((W_REST0 - W_GATE0 - GATE_SCALARS, D_MODEL), F32)], axis=0)
            w_ref[W_GATE0:W_REST0, :] = gate_rows.astype(BF16)
            w_ref[W_REST0:W_ROWS, :] = w_src_ref[src_rest0:src_rest0 + REST_W, :].astype(BF16)

    @pl.when(tile == 0)
    def _():
        _history_to_scratch(ext_ref, hist.start, convprev_ref, rows_first)
        ext_ref[:, 0:CONV_PAD - (CONV_W - 1), :] = jnp.zeros((bb, CONV_PAD - (CONV_W - 1), QKV_W), F32)

    x = x_ref[...].reshape(rows, D_MODEL)
    normed = (x * _rms_scale(x) * g_ref[...]).astype(BF16)
    contract_last = (((1,), (1,)), ((), ()))
    proj = lambda w_rows: lax.dot_general(normed, w_rows, contract_last, preferred_element_type=F32)
    ext_ref[:, CONV_PAD:CONV_PAD + tl, :] = proj(w_ref[0:QKV_W, :]).reshape(bb, tl, QKV_W)
    _history_from_scratch(convnew_ref, ext_ref, CONV_PAD + tl - (CONV_W - 1), rows_first)

    x8 = lax.dot_general(w_ref[W_GATE0:W_REST0, :], normed, contract_last, preferred_element_type=F32)[0:SUBLANES, :]
    xs = x8 + decay_ref[:, 1:2]
    softplus = jnp.maximum(xs, 0.0) + jnp.log1p(jnp.exp(-jnp.abs(xs)))
    graw8 = -jnp.exp(decay_ref[:, 0:1]) * softplus
    g8 = _lane_prefix_sum(graw8, chunk)
    gsc_ref[0] = _sigmoid(x8)
    gsc_ref[1] = g8
    gsc_ref[2] = jnp.exp(g8)
    gsc_ref[3] = jnp.exp(_lane_suffix_sum(graw8, chunk) - graw8)

    keep = keep_ref[...] != 0

    def conv_block(c0, dep):
        cols = slice(c0, c0 + LANES)
        xe = ext_ref[:, :, cols].reshape(bb * (CONV_PAD + tl), LANES)
        x1 = pltpu.roll(xe, 1, axis=0)
        tap = lambda j: 0.5 * wconv_ref[j:j + 1, cols]
        acc = (tap(3) * xe + tap(2) * x1) + pltpu.roll(tap(1) * xe + tap(0) * x1, 2, axis=0)
        val = _silu_of_twice(acc.reshape(bb, CONV_PAD + tl, LANES)[:, CONV_PAD:, :])
        if c0 < 2 * QK_W:
            scale = HEAD_DIM ** -0.5 if c0 < QK_W else 1.0
            val = val * (lax.rsqrt(jnp.sum(val * val, axis=-1, keepdims=True) + EPS) * scale)
        qkv_ref[:, :, cols] = jnp.where(keep, val, dep[:, 0:LANES].reshape(bb, tl, LANES)).astype(qkv_ref.dtype)

    def proj_block(out0, w0, c0):
        val = proj(w_ref[w0 + c0:w0 + c0 + MXU_COLS, :])
        rest_ref[:, :, out0 + c0:out0 + c0 + MXU_COLS] = val.reshape(bb, tl, MXU_COLS)
        return val

    mxu_work = ([functools.partial(proj_block, REST_Z0, QKV_W, c0) for c0 in range(0, QK_W, MXU_COLS)]
                + [functools.partial(proj_block, REST_P0, W_REST0, c0) for c0 in range(0, WIDTH_B, MXU_COLS)]
                + [functools.partial(proj_block, 0, W_REST0 + WIDTH_B, c0)
                   for c0 in range(0, 2 * D_MODEL, MXU_COLS)])
    for i, c0 in enumerate(range(0, QKV_W, LANES)):
        conv_block(c0, mxu_work[i]())

    if n_tiles > 1:
        _history_to_scratch(ext_ref, hist.start, convnew_ref, rows_first)


def _front(x, g_attn, w_proj, w_conv, a_log, dt_bias, conv_prev, *, bb, tl, chunk, act_dtype, rows_first):
    convert_weights = w_proj.dtype != BF16
    assert w_proj.shape == ((W_ROWS - (W_REST0 - W_GATE0 - GATE_SCALARS) if convert_weights else W_ROWS), D_MODEL)
    bsz, l, _ = x.shape
    assert bsz % bb == 0 and l % tl == 0 and (bb * tl) % chunk == 0, "blocks must tile the batch, length and chunks"
    assert bb == 1 or tl == l, "a block of several sequences must hold them whole (gate rows are token-major)"
    assert CONV_W == 4, "the conv is written out as two pairs of taps"
    n_tiles = l // tl
    kern = functools.partial(_front_kernel, bb=bb, tl=tl, n_tiles=n_tiles, chunk=chunk, rows_first=rows_first,
                             convert_weights=convert_weights)
    sds = lambda *shape: jax.ShapeDtypeStruct(shape, F32)
    w_out_specs = [_const_spec((W_ROWS, D_MODEL))] if convert_weights else []
    w_out_shapes = [jax.ShapeDtypeStruct((W_ROWS, D_MODEL), BF16)] if convert_weights else []
    decay_prm = jnp.pad(jnp.stack([a_log, dt_bias], axis=1).astype(F32), ((N_HEADS, 0), (0, 0)))
    return pl.pallas_call(
        kern,
        grid=(bsz // bb, n_tiles),
        in_specs=[_seq_spec(bb, tl, D_MODEL), _const_spec((1, D_MODEL)), _const_spec(w_proj.shape),
                  _const_spec((CONV_W, QKV_W)), _const_spec((SUBLANES, 2)), _const_spec((1, LANES)),
                  _state_spec(bb, CONV_W - 1, QKV_W, rows_first)],
        out_specs=[_seq_spec(bb, tl, QKV_W), _seq_spec(bb, tl, REST_OUT_W), _gate_rows_spec(bb, tl, n_tiles),
                   _state_spec(bb, CONV_W - 1, QKV_W, rows_first)] + w_out_specs,
        out_shape=[jax.ShapeDtypeStruct((bsz, l, QKV_W), act_dtype), sds(bsz, l, REST_OUT_W),
                   sds(N_GATE_ROWS, SUBLANES, bsz * l),
                   _state_shape(bsz, CONV_W - 1, QKV_W, rows_first)] + w_out_shapes,
        scratch_shapes=[pltpu.VMEM((bb, CONV_PAD + tl, QKV_W), F32)],
        compiler_params=pltpu.CompilerParams(dimension_semantics=("arbitrary", "arbitrary"),
                                             vmem_limit_bytes=VMEM_LIMIT),
        name="front",
    )(x, g_attn, w_proj, w_conv, decay_prm, jnp.ones((1, LANES), jnp.int32), conv_prev)


def _unit_lower_inverses(ms, chunk, side_jobs=()):
    nb = min(chunk, SERIES_BLOCK)
    n_blocks = GROUP // nb
    n_factors = int(math.log2(nb))
    assert chunk == nb or (chunk == 2 * nb and n_blocks == 2), "chunks are one or two series blocks"
    lane = lax.broadcasted_iota(jnp.int32, (nb, GROUP), 1)
    lane_block = lane // nb

    def packed(m):
        out = m[0:nb]
        for b in range(1, n_blocks):
            out = jnp.where(lane_block == b, m[b * nb:(b + 1) * nb], out)
        return out

    def block_diag(p):
        return jnp.concatenate([jnp.where(lane_block == b, p, 0.0) for b in range(n_blocks)], axis=0)

    side_jobs = list(side_jobs)
    per_step = -(-len(side_jobs) // max(n_factors - 1, 1))

    def issue_side_jobs(step_values):
        for k in range(min(per_step, len(side_jobs))):
            side_jobs.pop(0)(step_values[k % len(step_values)])

    nps = [packed(m) for m in ms]
    eye_p = jnp.where(lane % nb == lax.broadcasted_iota(jnp.int32, (nb, GROUP), 0), 1.0, 0.0).astype(F32)
    invs = [eye_p - n for n in nps]
    if n_factors > 1:
        pws = [_dot(n, block_diag(n)) for n in nps]
        issue_side_jobs(pws)
        for _ in range(n_factors - 2):
            boths = [_dot(jnp.concatenate([inv, pw], axis=0), block_diag(pw)) for inv, pw in zip(invs, pws)]
            invs = [inv + both[0:nb] for inv, both in zip(invs, boths)]
            pws = [both[nb:] for both in boths]
            issue_side_jobs(pws)
        invs = [inv + _dot(inv, block_diag(pw)) for inv, pw in zip(invs, pws)]
    while side_jobs:
        side_jobs.pop(0)(None)
    if chunk == nb:
        return [block_diag(inv) for inv in invs]
    zeros = jnp.zeros((nb, GROUP), F32)
    a_invs = [jnp.where(lane < nb, inv, 0.0) for inv in invs]
    lows = [jnp.where(lane < nb, m[nb:], 0.0) for m in ms]
    xs = [_dot(low, jnp.concatenate([a_inv, zeros], axis=0)) for low, a_inv in zip(lows, a_invs)]
    ys = [_dot(inv, jnp.concatenate([zeros, x], axis=0)) for inv, x in zip(invs, xs)]
    return [jnp.concatenate([a_inv, jnp.where(lane >= nb, inv, 0.0) - y], axis=0)
            for a_inv, y, inv in zip(a_invs, ys, invs)]


def _delta_kernel(*refs, bb, tl, chunk, pos0, n_tiles, n_gate_refs, n_casts, rows_first):
    gsc_refs, refs = refs[:n_gate_refs], refs[n_gate_refs:]
    (qkv_ref, zp_ref, poolprev_ref, s0_ref, wonorm_ref, wmix_ref, pscale_ref, keep_ref) = refs[0:8]
    cast_src, refs = refs[8:8 + n_casts], refs[8 + n_casts:]
    oab_ref, poolnew_ref, snew_ref = refs[0:3]
    cast_dst, pext_ref = refs[3:3 + n_casts], refs[3 + n_casts]
    tile = pl.program_id(1)

    for src, dst in zip(cast_src, cast_dst):
        dst[...] = src[...].astype(dst.dtype)

    rows_b = min(tl, GROUP)
    seqs_g = GROUP // rows_b
    groups_b = tl // rows_b
    n_groups = bb * tl // GROUP
    chained = chunk == GROUP
    assert chained or (chunk == tl and tl < GROUP), "chunk must be a whole group or a whole short sequence"
    pool_hist = slice(POOL_PAD - POOL_HIST, POOL_PAD)

    @pl.when(tile == 0)
    def _():
        snew_ref[...] = s0_ref[...]
        _history_to_scratch(pext_ref, pool_hist.start, poolprev_ref, rows_first)
        pext_ref[:, POOL_PAD - POOL_LOOKBACK:POOL_PAD - POOL_HIST, :] = jnp.zeros((bb, 1, WIDTH_B), F32)

    pext_ref[:, POOL_PAD:POOL_PAD + tl, :] = zp_ref[:, :, QK_W:QK_W + WIDTH_B]
    _history_from_scratch(poolnew_ref, pext_ref, POOL_PAD + tl - POOL_HIST, rows_first)

    row = lax.broadcasted_iota(jnp.int32, (GROUP, GROUP), 0)
    col = lax.broadcasted_iota(jnp.int32, (GROUP, GROUP), 1)
    causal = row >= col
    strict = row > col
    if not chained:
        same = (row // chunk) == (col // chunk)
        causal = causal & same
        strict = strict & same

    def origin(g):
        if tl >= GROUP:
            return g // groups_b, (g % groups_b) * GROUP
        return g * seqs_g, 0

    def load(c0, g):
        b0, t0 = origin(g)
        return qkv_ref[b0:b0 + seqs_g, t0:t0 + rows_b, c0:c0 + HEAD_DIM].astype(F32).reshape(GROUP, HEAD_DIM)

    def gate_rows(g):
        if n_gate_refs > 1:
            b0, t0 = origin(g)
            return tuple(gsc_refs[b0][q, :, t0:t0 + GROUP] for q in range(N_GATE_ROWS))
        return tuple(gsc_refs[0][q, :, g * GROUP:(g + 1) * GROUP] for q in range(N_GATE_ROWS))

    gates = [gate_rows(g) for g in range(n_groups)]

    probs = [(g, h) for g in range(n_groups) for h in range(N_HEADS)]
    st = []
    for g, h in probs:
        beta8, g8, eg8, kds8 = gates[g]
        d = {"q": load(h * HEAD_DIM, g), "k": load(QK_W + h * HEAD_DIM, g), "v": load(2 * QK_W + h * HEAD_DIM, g)}
        d["beta_row"] = beta8[h:h + 1, :]
        d["eg_row"] = eg8[N_HEADS + h:N_HEADS + h + 1, :]
        d["g_row"] = g8[N_HEADS + h:N_HEADS + h + 1, :]
        d["kb_row"] = kds8[N_HEADS + h:N_HEADS + h + 1, :] * d["beta_row"]
        d["g_col"] = jnp.broadcast_to(d["g_row"], (GROUP, GROUP)).T
        d["kt"] = d["k"].T
        st.append(d)
    for d in st:
        both = _dot(jnp.concatenate([d["k"], d["q"]], axis=0), d["kt"])
        d["kk"], d["qk"] = both[0:GROUP], both[GROUP:]
    for d in st:
        decay = jnp.exp(jnp.where(causal, d["g_col"] - d["g_row"], -jnp.inf))
        decay_beta = decay * d["beta_row"]
        d["m"] = jnp.where(strict, d.pop("kk") * decay_beta, 0.0)
        d["qkm"] = d.pop("qk") * decay_beta
        d["q_dec"] = d.pop("q") * jnp.exp(d["g_col"])
        d["kt_dec"] = d.pop("kt") * d["kb_row"]

    keep = keep_ref[...] != 0

    def pool_block(g, gi, issued_with):
        b0, t0 = origin(g)
        win = POOL_WINDOWS[gi]
        cols = slice(gi * POOL_GROUP, (gi + 1) * POOL_GROUP)
        pos = pos0 + tile * tl + t0 + row % rows_b
        slab = pext_ref[b0:b0 + seqs_g, POOL_PAD - POOL_LOOKBACK + t0:POOL_PAD + t0 + rows_b, cols]
        acc = slab.reshape(seqs_g * (POOL_LOOKBACK + rows_b), POOL_GROUP)
        shift = 1
        while shift < win:
            acc = acc + pltpu.roll(acc, shift, axis=0)
            shift *= 2
        acc = acc.reshape(seqs_g, POOL_LOOKBACK + rows_b, POOL_GROUP)[:, POOL_LOOKBACK:, :]
        cur = slab[:, POOL_LOOKBACK:, :]
        pooled = (acc / jnp.minimum(pos + 1, win).astype(F32).reshape(seqs_g, rows_b, POOL_GROUP) - cur)
        mixed = _dot(pooled.reshape(GROUP, POOL_GROUP), wmix_ref[gi]) * pscale_ref[:, cols]
        if issued_with is not None:
            reps = GROUP // issued_with.shape[0]
            mixed = jnp.where(keep, mixed, jnp.concatenate([issued_with] * reps, axis=0))
        oab_ref[b0:b0 + seqs_g, t0:t0 + rows_b, QK_W + gi * POOL_GROUP:QK_W + (gi + 1) * POOL_GROUP] = (
            mixed.reshape(seqs_g, rows_b, POOL_GROUP).astype(oab_ref.dtype))

    pool_jobs = [functools.partial(pool_block, g, gi) for g in range(n_groups) for gi in range(len(POOL_WINDOWS))]
    invs = _unit_lower_inverses([d.pop("m") for d in st], chunk, pool_jobs)
    for d, inv in zip(st, invs):
        d["uy"] = _dot(inv, d.pop("v"))
        d["wy"] = _dot(inv * d["eg_row"], d.pop("k"))

    def group_slab(ref, g, cols, row_off=0):
        b0, t0 = origin(g)
        return ref.at[b0:b0 + seqs_g, row_off + t0:row_off + t0 + rows_b, cols]

    def head_out(g, h, o):
        cols = slice(h * HEAD_DIM, (h + 1) * HEAD_DIM)
        z = group_slab(zp_ref, g, cols)[...].reshape(GROUP, HEAD_DIM)
        group_slab(oab_ref, g, cols)[...] = (o * _rms_scale(o) * wonorm_ref[...] * _silu(z)).reshape(
            seqs_g, rows_b, HEAD_DIM).astype(oab_ref.dtype)

    if chained:
        for j in range(groups_b):
            wave = [(i, g, h) for i, (g, h) in enumerate(probs) if g % groups_b == j]
            s_old = {i: snew_ref[origin(g)[0], h] for i, g, h in wave}
            ws = {i: _dot(jnp.concatenate([st[i]["wy"], st[i]["q_dec"]], axis=0), s_old[i]) for i, g, h in wave}
            ys = {i: st[i]["uy"] - ws[i][0:GROUP] for i, g, h in wave}
            outs = {i: ws[i][GROUP:] + _dot(st[i]["qkm"], ys[i]) for i, g, h in wave}
            for i, g, h in wave:
                last = jnp.exp(st[i]["g_col"][GROUP - 1:GROUP, :])
                snew_ref[origin(g)[0], h] = s_old[i] * last + _dot(st[i]["kt_dec"], ys[i])
            for i, g, h in wave:
                head_out(g, h, outs[i])
    else:
        outs = {}
        for i, (g, h) in enumerate(probs):
            d, b0 = st[i], origin(g)[0]
            ws_w, ws_q = [], []
            for s_i in range(seqs_g):
                r = slice(s_i * rows_b, (s_i + 1) * rows_b)
                ws = _dot(jnp.concatenate([d["wy"][r], d["q_dec"][r]], axis=0), snew_ref[b0 + s_i, h])
                ws_w.append(ws[0:rows_b])
                ws_q.append(ws[rows_b:])
            d["y"] = d["uy"] - jnp.concatenate(ws_w, axis=0)
            outs[i] = jnp.concatenate(ws_q, axis=0) + _dot(d["qkm"], d["y"])
        for i, (g, h) in enumerate(probs):
            d, b0 = st[i], origin(g)[0]
            for s_i in range(seqs_g):
                last = jnp.exp(d["g_col"][(s_i + 1) * rows_b - 1:(s_i + 1) * rows_b, :])
                upd = _dot(jnp.where(col // rows_b == s_i, d["kt_dec"], 0.0), d["y"])
                snew_ref[b0 + s_i, h] = snew_ref[b0 + s_i, h] * last + upd
        for i, (g, h) in enumerate(probs):
            head_out(g, h, outs[i])

    if n_tiles > 1:
        _history_to_scratch(pext_ref, pool_hist.start, poolnew_ref, rows_first)


def _delta(qkv, rest, gsc, pool_prev, s0, w_onorm, w_mix, pool_scale, *, bb, tl, chunk, pos0, act_dtype,
           rows_first, cast_to_bf16=()):
    bsz, l = qkv.shape[0], qkv.shape[1]
    assert bsz % bb == 0 and l % tl == 0 and (bb * tl) % GROUP == 0, "blocks must tile the batch, length and groups"
    n_tiles = l // tl
    n_steps = (bsz // bb) * n_tiles
    cast_specs = []
    for arr, axis in cast_to_bf16:
        tile = LANES if axis == 1 else 2 * SUBLANES
        assert arr.shape[axis] % (n_steps * tile) == 0, "each step converts a tile-aligned slice"
        blk = tuple(d // n_steps if a == axis else d for a, d in enumerate(arr.shape))
        cast_specs.append(pl.BlockSpec(blk, lambda i, j, axis=axis: tuple(
            i * n_tiles + j if a == axis else 0 for a in range(2))))
    state_s = pl.BlockSpec((bb, N_HEADS, HEAD_DIM, HEAD_DIM), lambda i, j: (i, 0, 0, 0))
    if n_tiles == 1:
        gate_specs = [_gate_rows_spec(bb, tl, 1)]
    else:
        gate_specs = [pl.BlockSpec((N_GATE_ROWS, SUBLANES, tl), lambda i, j, k=k: (0, 0, (i * bb + k) * n_tiles + j))
                      for k in range(bb)]
    kern = functools.partial(_delta_kernel, bb=bb, tl=tl, chunk=chunk, pos0=pos0, n_tiles=n_tiles,
                             n_gate_refs=len(gate_specs), n_casts=len(cast_specs), rows_first=rows_first)
    sds = lambda *shape: jax.ShapeDtypeStruct(shape, F32)
    return pl.pallas_call(
        kern,
        grid=(bsz // bb, n_tiles),
        in_specs=gate_specs + [
                  _seq_spec(bb, tl, QKV_W),
                  pl.BlockSpec((bb, tl, QK_W + WIDTH_B), lambda i, j: (i, j, REST_Z0 // (QK_W + WIDTH_B))),
                  _state_spec(bb, POOL_HIST, WIDTH_B, rows_first), state_s,
                  _const_spec((1, HEAD_DIM)), _const_spec((len(POOL_WINDOWS), POOL_GROUP, POOL_GROUP)),
                  _const_spec((1, WIDTH_B)), _const_spec((1, LANES))] + cast_specs,
        out_specs=[_seq_spec(bb, tl, QK_W + WIDTH_B),
                   _state_spec(bb, POOL_HIST, WIDTH_B, rows_first), state_s] + cast_specs,
        out_shape=[jax.ShapeDtypeStruct((bsz, l, QK_W + WIDTH_B), act_dtype),
                   _state_shape(bsz, POOL_HIST, WIDTH_B, rows_first),
                   sds(bsz, N_HEADS, HEAD_DIM, HEAD_DIM)]
                  + [jax.ShapeDtypeStruct(arr.shape, BF16) for arr, _ in cast_to_bf16],
        scratch_shapes=[pltpu.VMEM((bb, POOL_PAD + tl, WIDTH_B), F32)],
        compiler_params=pltpu.CompilerParams(dimension_semantics=("arbitrary", "arbitrary"),
                                             vmem_limit_bytes=VMEM_LIMIT),
        name="delta",
    )(*([gsc] * len(gate_specs)), qkv, rest, pool_prev, s0, w_onorm, w_mix, pool_scale,
      jnp.ones((1, LANES), jnp.int32), *[arr for arr, _ in cast_to_bf16])


def _mlp_kernel(x_ref, oab_ref, gate_ref, wa_ref, wb_ref, wo_ref, gmlp_ref, wup_ref, wdown_ref, gfin_ref, y_ref):
    ma = _dot(oab_ref[:, 0:QK_W], wa_ref[...])
    mb = _dot(oab_ref[:, QK_W:QK_W + WIDTH_B], wb_ref[...])
    merged = _sigmoid(gate_ref[:, 0:D_MODEL]) * ma + _sigmoid(gate_ref[:, D_MODEL:]) * mb
    x1 = x_ref[...] + _dot(merged, wo_ref[...])
    h2 = (x1 * _rms_scale(x1) * gmlp_ref[...]).astype(BF16)
    acc = x1
    for c0 in range(0, D_FF, FF_BLOCK):
        up = jnp.dot(h2, wup_ref[:, c0:c0 + FF_BLOCK], preferred_element_type=F32)
        act = jnp.square(jnp.maximum(up, 0.0))
        acc = acc + _dot(act, wdown_ref[c0:c0 + FF_BLOCK, :])
    y_ref[...] = acc * _rms_scale(acc) * gfin_ref[...]


def _merge_mlp(x2d, oab, rest, prm, tm):
    t = x2d.shape[0]
    assert t % tm == 0, "row tiles must cover the tokens exactly"
    row = lambda w: pl.BlockSpec((tm, w), lambda i: (i, 0))
    return pl.pallas_call(
        _mlp_kernel,
        grid=(t // tm,),
        in_specs=[row(D_MODEL), row(QK_W + WIDTH_B), row(2 * D_MODEL),
                  _const_spec((QK_W, D_MODEL)), _const_spec((WIDTH_B, D_MODEL)), _const_spec((D_MODEL, D_MODEL)),
                  _const_spec((1, D_MODEL)), _const_spec((D_MODEL, D_FF)), _const_spec((D_FF, D_MODEL)),
                  _const_spec((1, D_MODEL))],
        out_specs=row(D_MODEL),
        out_shape=jax.ShapeDtypeStruct((t, D_MODEL), F32),
        compiler_params=pltpu.CompilerParams(dimension_semantics=("arbitrary",),
                                             vmem_limit_bytes=VMEM_LIMIT),
        name="merge_mlp",
    )(x2d, oab, rest, prm["w_a_out"], prm["w_b_out"], prm["w_o"], prm["g_mlp"], prm["w_up"], prm["w_down"],
      prm["g_final"])


def _mix(x, conv_prev, pool_prev, s_prev, pos0, prm, *, front_blk, delta_blk, chunk, act_dtype, rows_first,
         cast_to_bf16=()):
    bsz, l, _ = x.shape
    t = bsz * l
    qkv, rest, gsc, conv_new, *w_proj_bf16 = _front(
        x, prm["g_attn"], prm["w_proj"], prm["w_conv"], prm["a_log"], prm["dt_bias"],
        conv_prev, bb=front_blk[0], tl=front_blk[1], chunk=chunk, act_dtype=act_dtype, rows_first=rows_first)
    oab, pool_new, s_new, *casted = _delta(
        qkv, rest, gsc, pool_prev, s_prev, prm["w_onorm"], prm["w_mix"], prm["pool_scale"], bb=delta_blk[0],
        tl=delta_blk[1], chunk=chunk, pos0=pos0, act_dtype=act_dtype, rows_first=rows_first,
        cast_to_bf16=cast_to_bf16)
    rows = (x.reshape(t, D_MODEL), oab.reshape(t, QK_W + WIDTH_B), rest.reshape(t, REST_OUT_W))
    return rows, (conv_new, pool_new, s_new[None]), w_proj_bf16 + casted


def kernel(x_prompt, x_sample, state_conv, state_pool, state_ssm, w_in, w_conv, a_log, dt_bias, w_onorm,
           w_pool_mix, pool_scale, w_a_out, w_b_out, w_o, g_attn, g_mlp, w_up, w_down, g_final):
    assert w_in.shape[0] == 1, "single-layer decoder"
    prm = {
        "w_proj": jnp.transpose(w_in[0]).astype(F32),
        "g_attn": g_attn[0][None, :], "g_mlp": g_mlp[0][None, :], "g_final": g_final[None, :],
        "w_conv": w_conv[0].astype(F32),
        "a_log": a_log[0], "dt_bias": dt_bias[0],
        "w_onorm": w_onorm[0][None, :].astype(F32),
        "w_mix": w_pool_mix[0].astype(BF16), "pool_scale": pool_scale[0][None, :].astype(F32),
    }
    bp = x_prompt.shape[0]
    converted = ("w_proj", "w_up", "w_down", "w_o", "w_a_out", "w_b_out")
    rows_p, (conv_p, pool_p, ssm_p), casted = _mix(
        x_prompt, jnp.zeros((bp, CONV_W - 1, QKV_W), F32), jnp.zeros((bp, POOL_HIST, WIDTH_B), F32),
        jnp.zeros((bp, N_HEADS, HEAD_DIM, HEAD_DIM), F32), 0, prm,
        front_blk=(1, 512), delta_blk=(4, 256), chunk=GROUP, act_dtype=BF16, rows_first=False,
        cast_to_bf16=((w_up[0].astype(F32), 1), (w_down[0].astype(F32), 0), (w_o[0].astype(F32), 0),
                      (w_a_out[0].astype(F32), 0), (w_b_out[0].astype(F32), 0)))
    prm.update(zip(converted, casted, strict=True))
    conv_p, pool_p = conv_p[None], pool_p[None]
    dec_len = x_sample.shape[1]
    rows_major = lambda s: jnp.transpose(s[0].astype(F32), (1, 0, 2))
    rows_s, (conv_s, pool_s, ssm_s), _ = _mix(
        x_sample, rows_major(state_conv), rows_major(state_pool), state_ssm[0].astype(F32), PAST_LEN, prm,
        front_blk=(256 // dec_len, dec_len), delta_blk=(2 * GROUP // dec_len, dec_len),
        chunk=dec_len, act_dtype=F32, rows_first=True)
    conv_s, pool_s = jnp.transpose(conv_s, (1, 0, 2))[None], jnp.transpose(pool_s, (1, 0, 2))[None]
    y_p = _merge_mlp(*rows_p, prm, tm=512).reshape(x_prompt.shape)
    y_s = _merge_mlp(*rows_s, prm, tm=256).reshape(x_sample.shape)
    return (y_p, y_s, conv_p.astype(state_conv.dtype), pool_p.astype(state_pool.dtype),
            ssm_p.astype(state_ssm.dtype), conv_s.astype(state_conv.dtype), pool_s.astype(state_pool.dtype),
            ssm_s.astype(state_ssm.dtype))
```

```python
import functools
import math

import jax
import jax.numpy as jnp
from jax import lax
from jax.experimental import pallas as pl
from jax.experimental.pallas import tpu as pltpu

D_MODEL = 1024
N_HEADS = 4
HEAD_DIM = 128
QK_W = N_HEADS * HEAD_DIM
QKV_W = 3 * QK_W
CONV_W = 4
POOL_WINDOWS = (2, 4, 8, 16)
POOL_GROUP = 128
WIDTH_B = len(POOL_WINDOWS) * POOL_GROUP
POOL_HIST = 15
D_FF = 4 * D_MODEL
EPS = 1e-6
PAST_LEN = 16384
LANES = 128
SUBLANES = 8
MXU_COLS = 256
N_GATE_ROWS = 4

QKVZ_W = QKV_W + QK_W
GATE_SCALARS = 2 * N_HEADS
REST_W = WIDTH_B + 2 * D_MODEL
W_GATE0 = QKVZ_W
W_REST0 = W_GATE0 + 2 * SUBLANES
W_ROWS = W_REST0 + REST_W
WT_COLS = QKVZ_W + REST_W
WT_ROWS = D_MODEL + 2 * SUBLANES
WT_CHUNK = 512
REST_Z0 = 2 * D_MODEL
REST_P0 = REST_Z0 + QK_W
REST_OUT_W = REST_P0 + WIDTH_B

GROUP = 128
SERIES_BLOCK = 64
CONV_PAD = 8
POOL_LOOKBACK = 16
POOL_PAD = 24
FF_BLOCK = 1024

VMEM_LIMIT = 56 * 1024 * 1024

BF16 = jnp.bfloat16
F32 = jnp.float32


def _dot(a, b):
    return jnp.dot(a.astype(BF16), b.astype(BF16), preferred_element_type=F32)


def _sigmoid(x):
    return 1.0 / (1.0 + jnp.exp(-x))


def _silu_of_twice(half):
    return half * jnp.tanh(half) + half


def _silu(x):
    return _silu_of_twice(0.5 * x)


def _rms_scale(x):
    return lax.rsqrt(jnp.mean(x * x, axis=-1, keepdims=True) + EPS)


def _const_spec(shape):
    zeros = (0,) * len(shape)
    return pl.BlockSpec(shape, lambda *_: zeros, pipeline_mode=pl.Buffered(1))


def _seq_spec(bb, tl, width):
    return pl.BlockSpec((bb, tl, width), lambda i, j: (i, j, 0))


def _state_spec(bb, n_rows, width, rows_first):
    if rows_first:
        return pl.BlockSpec((n_rows, bb, width), lambda i, j: (0, i, 0))
    return pl.BlockSpec((bb, n_rows, width), lambda i, j: (i, 0, 0))


def _state_shape(bsz, n_rows, width, rows_first):
    return jax.ShapeDtypeStruct((n_rows, bsz, width) if rows_first else (bsz, n_rows, width), F32)


def _history_to_scratch(scratch_ref, first_row, state_ref, rows_first):
    if rows_first:
        for j in range(state_ref.shape[0]):
            scratch_ref[:, first_row + j, :] = state_ref[j]
    else:
        scratch_ref[:, first_row:first_row + state_ref.shape[1], :] = state_ref[...]


def _history_from_scratch(state_ref, scratch_ref, first_row, rows_first):
    if rows_first:
        for j in range(state_ref.shape[0]):
            state_ref[j] = scratch_ref[:, first_row + j, :]
    else:
        state_ref[...] = scratch_ref[:, first_row:first_row + state_ref.shape[1], :]


def _gate_rows_spec(bb, tl, n_tiles):
    return pl.BlockSpec((N_GATE_ROWS, SUBLANES, bb * tl), lambda i, j: (0, 0, i * n_tiles + j))


def _lane_prefix_sum(x, chunk):
    lane = lax.broadcasted_iota(jnp.int32, x.shape, 1)
    shift = 1
    while shift < chunk:
        x = x + jnp.where(lane % chunk >= shift, pltpu.roll(x, shift, axis=1), 0.0)
        shift *= 2
    return x


def _lane_suffix_sum(x, chunk):
    lane = lax.broadcasted_iota(jnp.int32, x.shape, 1)
    shift = 1
    while shift < chunk:
        x = x + jnp.where(lane % chunk + shift < chunk, pltpu.roll(x, x.shape[1] - shift, axis=1), 0.0)
        shift *= 2
    return x


def _front_kernel(x_ref, g_ref, w_ref, wconv_ref, decay_ref, keep_ref, convprev_ref,
                  qkv_ref, rest_ref, gsc_ref, convnew_ref,
                  *tail, bb, tl, n_tiles, chunk, rows_first, convert_weights):
    tile = pl.program_id(1)
    rows = bb * tl
    hist = slice(CONV_PAD - (CONV_W - 1), CONV_PAD)
    ext_ref = tail[-1]

    if convert_weights:
        w_src_ref, w_ref = w_ref, tail[0]
        src_rest0 = W_GATE0 + GATE_SCALARS

        @pl.when((pl.program_id(0) == 0) & (tile == 0))
        def _():
            for src0, dst0, n in ((0, 0, QKVZ_W), (src_rest0, QKVZ_W, REST_W)):
                for c in range(0, n, WT_CHUNK):
                    w_ref[0:D_MODEL, dst0 + c:dst0 + c + WT_CHUNK] = (
                        w_src_ref[src0 + c:src0 + c + WT_CHUNK, :].T.astype(BF16))
            gate_rows = jnp.concatenate([w_src_ref[W_GATE0:src_rest0, :],
                                         jnp.zeros((WT_ROWS - D_MODEL - GATE_SCALARS, D_MODEL), F32)], axis=0)
            w_ref[D_MODEL:WT_ROWS, 0:D_MODEL] = gate_rows.astype(BF16)
            w_ref[D_MODEL:WT_ROWS, D_MODEL:WT_COLS] = jnp.zeros((WT_ROWS - D_MODEL, WT_COLS - D_MODEL), BF16)

    @pl.when(tile == 0)
    def _():
        _history_to_scratch(ext_ref, hist.start, convprev_ref, rows_first)
        ext_ref[:, 0:CONV_PAD - (CONV_W - 1), :] = jnp.zeros((bb, CONV_PAD - (CONV_W - 1), QKV_W), F32)

    x = x_ref[...].reshape(rows, D_MODEL)
    normed = (x * _rms_scale(x) * g_ref[...]).astype(BF16)
    proj = lambda c0, n: jnp.dot(normed, w_ref[0:D_MODEL, c0:c0 + n], preferred_element_type=F32)
    ext_ref[:, CONV_PAD:CONV_PAD + tl, :] = proj(0, QKV_W).reshape(bb, tl, QKV_W)
    _history_from_scratch(convnew_ref, ext_ref, CONV_PAD + tl - (CONV_W - 1), rows_first)

    x8 = lax.dot_general(w_ref[D_MODEL:WT_ROWS, 0:D_MODEL], normed, (((1,), (1,)), ((), ())),
                         preferred_element_type=F32)[0:SUBLANES, :]
    xs = x8 + decay_ref[:, 1:2]
    softplus = jnp.maximum(xs, 0.0) + jnp.log1p(jnp.exp(-jnp.abs(xs)))
    graw8 = -jnp.exp(decay_ref[:, 0:1]) * softplus
    g8 = _lane_prefix_sum(graw8, chunk)
    gsc_ref[0] = _sigmoid(x8)
    gsc_ref[1] = g8
    gsc_ref[2] = jnp.exp(g8)
    gsc_ref[3] = jnp.exp(_lane_suffix_sum(graw8, chunk) - graw8)

    keep = keep_ref[...] != 0

    def conv_block(c0, dep):
        cols = slice(c0, c0 + LANES)
        xe = ext_ref[:, :, cols].reshape(bb * (CONV_PAD + tl), LANES)
        x1 = pltpu.roll(xe, 1, axis=0)
        tap = lambda j: 0.5 * wconv_ref[j:j + 1, cols]
        acc = (tap(3) * xe + tap(2) * x1) + pltpu.roll(tap(1) * xe + tap(0) * x1, 2, axis=0)
        val = _silu_of_twice(acc.reshape(bb, CONV_PAD + tl, LANES)[:, CONV_PAD:, :])
        if c0 < 2 * QK_W:
            scale = HEAD_DIM ** -0.5 if c0 < QK_W else 1.0
            val = val * (lax.rsqrt(jnp.sum(val * val, axis=-1, keepdims=True) + EPS) * scale)
        qkv_ref[:, :, cols] = jnp.where(keep, val, dep[:, 0:LANES].reshape(bb, tl, LANES)).astype(qkv_ref.dtype)

    def proj_block(out0, w0, c0):
        val = proj(w0 + c0, MXU_COLS)
        rest_ref[:, :, out0 + c0:out0 + c0 + MXU_COLS] = val.reshape(bb, tl, MXU_COLS)
        return val

    mxu_work = ([functools.partial(proj_block, REST_Z0, QKV_W, c0) for c0 in range(0, QK_W, MXU_COLS)]
                + [functools.partial(proj_block, REST_P0, QKVZ_W, c0) for c0 in range(0, WIDTH_B, MXU_COLS)]
                + [functools.partial(proj_block, 0, QKVZ_W + WIDTH_B, c0)
                   for c0 in range(0, 2 * D_MODEL, MXU_COLS)])
    for i, c0 in enumerate(range(0, QKV_W, LANES)):
        conv_block(c0, mxu_work[i]())

    if n_tiles > 1:
        _history_to_scratch(ext_ref, hist.start, convnew_ref, rows_first)


def _front(x, g_attn, w_proj, w_conv, a_log, dt_bias, conv_prev, *, bb, tl, chunk, act_dtype, rows_first):
    convert_weights = w_proj.dtype != BF16
    assert w_proj.shape == ((QKVZ_W + GATE_SCALARS + REST_W, D_MODEL) if convert_weights else (WT_ROWS, WT_COLS))
    bsz, l, _ = x.shape
    assert bsz % bb == 0 and l % tl == 0 and (bb * tl) % chunk == 0, "blocks must tile the batch, length and chunks"
    assert bb == 1 or tl == l, "a block of several sequences must hold them whole (gate rows are token-major)"
    assert CONV_W == 4, "the conv is written out as two pairs of taps"
    n_tiles = l // tl
    kern = functools.partial(_front_kernel, bb=bb, tl=tl, n_tiles=n_tiles, chunk=chunk, rows_first=rows_first,
                             convert_weights=convert_weights)
    sds = lambda *shape: jax.ShapeDtypeStruct(shape, F32)
    w_out_specs = [_const_spec((WT_ROWS, WT_COLS))] if convert_weights else []
    w_out_shapes = [jax.ShapeDtypeStruct((WT_ROWS, WT_COLS), BF16)] if convert_weights else []
    decay_prm = jnp.pad(jnp.stack([a_log, dt_bias], axis=1).astype(F32), ((N_HEADS, 0), (0, 0)))
    return pl.pallas_call(
        kern,
        grid=(bsz // bb, n_tiles),
        in_specs=[_seq_spec(bb, tl, D_MODEL), _const_spec((1, D_MODEL)), _const_spec(w_proj.shape),
                  _const_spec((CONV_W, QKV_W)), _const_spec((SUBLANES, 2)), _const_spec((1, LANES)),
                  _state_spec(bb, CONV_W - 1, QKV_W, rows_first)],
        out_specs=[_seq_spec(bb, tl, QKV_W), _seq_spec(bb, tl, REST_OUT_W), _gate_rows_spec(bb, tl, n_tiles),
                   _state_spec(bb, CONV_W - 1, QKV_W, rows_first)] + w_out_specs,
        out_shape=[jax.ShapeDtypeStruct((bsz, l, QKV_W), act_dtype), sds(bsz, l, REST_OUT_W),
                   sds(N_GATE_ROWS, SUBLANES, bsz * l),
                   _state_shape(bsz, CONV_W - 1, QKV_W, rows_first)] + w_out_shapes,
        scratch_shapes=[pltpu.VMEM((bb, CONV_PAD + tl, QKV_W), F32)],
        compiler_params=pltpu.CompilerParams(dimension_semantics=("arbitrary", "arbitrary"),
                                             vmem_limit_bytes=VMEM_LIMIT),
        name="front",
    )(x, g_attn, w_proj, w_conv, decay_prm, jnp.ones((1, LANES), jnp.int32), conv_prev)


def _unit_lower_inverses(ms, chunk, side_jobs=()):
    nb = min(chunk, SERIES_BLOCK)
    n_blocks = GROUP // nb
    n_factors = int(math.log2(nb))
    assert chunk == nb or (chunk == 2 * nb and n_blocks == 2), "chunks are one or two series blocks"
    lane = lax.broadcasted_iota(jnp.int32, (nb, GROUP), 1)
    lane_block = lane // nb

    def packed(m):
        out = m[0:nb]
        for b in range(1, n_blocks):
            out = jnp.where(lane_block == b, m[b * nb:(b + 1) * nb], out)
        return out

    def block_diag(p):
        return jnp.concatenate([jnp.where(lane_block == b, p, 0.0) for b in range(n_blocks)], axis=0)

    side_jobs = list(side_jobs)
    per_step = -(-len(side_jobs) // max(n_factors - 1, 1))

    def issue_side_jobs(step_values):
        for k in range(min(per_step, len(side_jobs))):
            side_jobs.pop(0)(step_values[k % len(step_values)])

    nps = [packed(m) for m in ms]
    eye_p = jnp.where(lane % nb == lax.broadcasted_iota(jnp.int32, (nb, GROUP), 0), 1.0, 0.0).astype(F32)
    invs = [eye_p - n for n in nps]
    if n_factors > 1:
        pws = [_dot(n, block_diag(n)) for n in nps]
        issue_side_jobs(pws)
        for _ in range(n_factors - 2):
            boths = [_dot(jnp.concatenate([inv, pw], axis=0), block_diag(pw)) for inv, pw in zip(invs, pws)]
            invs = [inv + both[0:nb] for inv, both in zip(invs, boths)]
            pws = [both[nb:] for both in boths]
            issue_side_jobs(pws)
        invs = [inv + _dot(inv, block_diag(pw)) for inv, pw in zip(invs, pws)]
    while side_jobs:
        side_jobs.pop(0)(None)
    if chunk == nb:
        return [block_diag(inv) for inv in invs]
    zeros = jnp.zeros((nb, GROUP), F32)
    a_invs = [jnp.where(lane < nb, inv, 0.0) for inv in invs]
    lows = [jnp.where(lane < nb, m[nb:], 0.0) for m in ms]
    xs = [_dot(low, jnp.concatenate([a_inv, zeros], axis=0)) for low, a_inv in zip(lows, a_invs)]
    ys = [_dot(inv, jnp.concatenate([zeros, x], axis=0)) for inv, x in zip(invs, xs)]
    return [jnp.concatenate([a_inv, jnp.where(lane >= nb, inv, 0.0) - y], axis=0)
            for a_inv, y, inv in zip(a_invs, ys, invs)]


def _delta_kernel(*refs, bb, tl, chunk, pos0, n_tiles, n_gate_refs, n_casts, rows_first):
    gsc_refs, refs = refs[:n_gate_refs], refs[n_gate_refs:]
    (qkv_ref, zp_ref, poolprev_ref, s0_ref, wonorm_ref, wmix_ref, pscale_ref, keep_ref) = refs[0:8]
    cast_src, refs = refs[8:8 + n_casts], refs[8 + n_casts:]
    oab_ref, poolnew_ref, snew_ref = refs[0:3]
    cast_dst, pext_ref = refs[3:3 + n_casts], refs[3 + n_casts]
    tile = pl.program_id(1)

    for src, dst in zip(cast_src, cast_dst):
        dst[...] = src[...].astype(dst.dtype)

    rows_b = min(tl, GROUP)
    seqs_g = GROUP // rows_b
    groups_b = tl // rows_b
    n_groups = bb * tl // GROUP
    chained = chunk == GROUP
    assert chained or (chunk == tl and tl < GROUP), "chunk must be a whole group or a whole short sequence"
    pool_hist = slice(POOL_PAD - POOL_HIST, POOL_PAD)

    @pl.when(tile == 0)
    def _():
        snew_ref[...] = s0_ref[...]
        _history_to_scratch(pext_ref, pool_hist.start, poolprev_ref, rows_first)
        pext_ref[:, POOL_PAD - POOL_LOOKBACK:POOL_PAD - POOL_HIST, :] = jnp.zeros((bb, 1, WIDTH_B), F32)

    pext_ref[:, POOL_PAD:POOL_PAD + tl, :] = zp_ref[:, :, QK_W:QK_W + WIDTH_B]
    _history_from_scratch(poolnew_ref, pext_ref, POOL_PAD + tl - POOL_HIST, rows_first)

    row = lax.broadcasted_iota(jnp.int32, (GROUP, GROUP), 0)
    col = lax.broadcasted_iota(jnp.int32, (GROUP, GROUP), 1)
    causal = row >= col
    strict = row > col
    if not chained:
        same = (row // chunk) == (col // chunk)
        causal = causal & same
        strict = strict & same

    def origin(g):
        if tl >= GROUP:
            return g // groups_b, (g % groups_b) * GROUP
        return g * seqs_g, 0

    def load(c0, g):
        b0, t0 = origin(g)
        return qkv_ref[b0:b0 + seqs_g, t0:t0 + rows_b, c0:c0 + HEAD_DIM].astype(F32).reshape(GROUP, HEAD_DIM)

    def gate_rows(g):
        if n_gate_refs > 1:
            b0, t0 = origin(g)
            return tuple(gsc_refs[b0][q, :, t0:t0 + GROUP] for q in range(N_GATE_ROWS))
        return tuple(gsc_refs[0][q, :, g * GROUP:(g + 1) * GROUP] for q in range(N_GATE_ROWS))

    gates = [gate_rows(g) for g in range(n_groups)]

    probs = [(g, h) for g in range(n_groups) for h in range(N_HEADS)]
    st = []
    for g, h in probs:
        beta8, g8, eg8, kds8 = gates[g]
        d = {"q": load(h * HEAD_DIM, g), "k": load(QK_W + h * HEAD_DIM, g), "v": load(2 * QK_W + h * HEAD_DIM, g)}
        d["beta_row"] = beta8[h:h + 1, :]
        d["eg_row"] = eg8[N_HEADS + h:N_HEADS + h + 1, :]
        d["g_row"] = g8[N_HEADS + h:N_HEADS + h + 1, :]
        d["kb_row"] = kds8[N_HEADS + h:N_HEADS + h + 1, :] * d["beta_row"]
        d["g_col"] = jnp.broadcast_to(d["g_row"], (GROUP, GROUP)).T
        d["kt"] = d["k"].T
        st.append(d)
    for d in st:
        both = _dot(jnp.concatenate([d["k"], d["q"]], axis=0), d["kt"])
        d["kk"], d["qk"] = both[0:GROUP], both[GROUP:]
    for d in st:
        decay = jnp.exp(jnp.where(causal, d["g_col"] - d["g_row"], -jnp.inf))
        decay_beta = decay * d["beta_row"]
        d["m"] = jnp.where(strict, d.pop("kk") * decay_beta, 0.0)
        d["qkm"] = d.pop("qk") * decay_beta
        d["q_dec"] = d.pop("q") * jnp.exp(d["g_col"])
        d["kt_dec"] = d.pop("kt") * d["kb_row"]

    keep = keep_ref[...] != 0

    def pool_block(g, gi, issued_with):
        b0, t0 = origin(g)
        win = POOL_WINDOWS[gi]
        cols = slice(gi * POOL_GROUP, (gi + 1) * POOL_GROUP)
        pos = pos0 + tile * tl + t0 + row % rows_b
        slab = pext_ref[b0:b0 + seqs_g, POOL_PAD - POOL_LOOKBACK + t0:POOL_PAD + t0 + rows_b, cols]
        acc = slab.reshape(seqs_g * (POOL_LOOKBACK + rows_b), POOL_GROUP)
        shift = 1
        while shift < win:
            acc = acc + pltpu.roll(acc, shift, axis=0)
            shift *= 2
        acc = acc.reshape(seqs_g, POOL_LOOKBACK + rows_b, POOL_GROUP)[:, POOL_LOOKBACK:, :]
        cur = slab[:, POOL_LOOKBACK:, :]
        pooled = (acc / jnp.minimum(pos + 1, win).astype(F32).reshape(seqs_g, rows_b, POOL_GROUP) - cur)
        mixed = _dot(pooled.reshape(GROUP, POOL_GROUP), wmix_ref[gi]) * pscale_ref[:, cols]
        if issued_with is not None:
            reps = GROUP // issued_with.shape[0]
            mixed = jnp.where(keep, mixed, jnp.concatenate([issued_with] * reps, axis=0))
        oab_ref[b0:b0 + seqs_g, t0:t0 + rows_b, QK_W + gi * POOL_GROUP:QK_W + (gi + 1) * POOL_GROUP] = (
            mixed.reshape(seqs_g, rows_b, POOL_GROUP).astype(oab_ref.dtype))

    pool_jobs = [functools.partial(pool_block, g, gi) for g in range(n_groups) for gi in range(len(POOL_WINDOWS))]
    invs = _unit_lower_inverses([d.pop("m") for d in st], chunk, pool_jobs)
    for d, inv in zip(st, invs):
        d["uy"] = _dot(inv, d.pop("v"))
        d["wy"] = _dot(inv * d["eg_row"], d.pop("k"))

    def group_slab(ref, g, cols, row_off=0):
        b0, t0 = origin(g)
        return ref.at[b0:b0 + seqs_g, row_off + t0:row_off + t0 + rows_b, cols]

    def head_out(g, h, o):
        cols = slice(h * HEAD_DIM, (h + 1) * HEAD_DIM)
        z = group_slab(zp_ref, g, cols)[...].reshape(GROUP, HEAD_DIM)
        group_slab(oab_ref, g, cols)[...] = (o * _rms_scale(o) * wonorm_ref[...] * _silu(z)).reshape(
            seqs_g, rows_b, HEAD_DIM).astype(oab_ref.dtype)

    if chained:
        for j in range(groups_b):
            wave = [(i, g, h) for i, (g, h) in enumerate(probs) if g % groups_b == j]
            s_old = {i: snew_ref[origin(g)[0], h] for i, g, h in wave}
            ws = {i: _dot(jnp.concatenate([st[i]["wy"], st[i]["q_dec"]], axis=0), s_old[i]) for i, g, h in wave}
            ys = {i: st[i]["uy"] - ws[i][0:GROUP] for i, g, h in wave}
            outs = {i: ws[i][GROUP:] + _dot(st[i]["qkm"], ys[i]) for i, g, h in wave}
            for i, g, h in wave:
                last = jnp.exp(st[i]["g_col"][GROUP - 1:GROUP, :])
                snew_ref[origin(g)[0], h] = s_old[i] * last + _dot(st[i]["kt_dec"], ys[i])
            for i, g, h in wave:
                head_out(g, h, outs[i])
    else:
        outs = {}
        for i, (g, h) in enumerate(probs):
            d, b0 = st[i], origin(g)[0]
            ws_w, ws_q = [], []
            for s_i in range(seqs_g):
                r = slice(s_i * rows_b, (s_i + 1) * rows_b)
                ws = _dot(jnp.concatenate([d["wy"][r], d["q_dec"][r]], axis=0), snew_ref[b0 + s_i, h])
                ws_w.append(ws[0:rows_b])
                ws_q.append(ws[rows_b:])
            d["y"] = d["uy"] - jnp.concatenate(ws_w, axis=0)
            outs[i] = jnp.concatenate(ws_q, axis=0) + _dot(d["qkm"], d["y"])
        for i, (g, h) in enumerate(probs):
            d, b0 = st[i], origin(g)[0]
            for s_i in range(seqs_g):
                last = jnp.exp(d["g_col"][(s_i + 1) * rows_b - 1:(s_i + 1) * rows_b, :])
                upd = _dot(jnp.where(col // rows_b == s_i, d["kt_dec"], 0.0), d["y"])
                snew_ref[b0 + s_i, h] = snew_ref[b0 + s_i, h] * last + upd
        for i, (g, h) in enumerate(probs):
            head_out(g, h, outs[i])

    if n_tiles > 1:
        _history_to_scratch(pext_ref, pool_hist.start, poolnew_ref, rows_first)


def _delta(qkv, rest, gsc, pool_prev, s0, w_onorm, w_mix, pool_scale, *, bb, tl, chunk, pos0, act_dtype,
           rows_first, cast_to_bf16=()):
    bsz, l = qkv.shape[0], qkv.shape[1]
    assert bsz % bb == 0 and l % tl == 0 and (bb * tl) % GROUP == 0, "blocks must tile the batch, length and groups"
    n_tiles = l // tl
    n_steps = (bsz // bb) * n_tiles
    cast_specs = []
    for arr, axis in cast_to_bf16:
        tile = LANES if axis == 1 else 2 * SUBLANES
        assert arr.shape[axis] % (n_steps * tile) == 0, "each step converts a tile-aligned slice"
        blk = tuple(d // n_steps if a == axis else d for a, d in enumerate(arr.shape))
        cast_specs.append(pl.BlockSpec(blk, lambda i, j, axis=axis: tuple(
            i * n_tiles + j if a == axis else 0 for a in range(2))))
    state_s = pl.BlockSpec((bb, N_HEADS, HEAD_DIM, HEAD_DIM), lambda i, j: (i, 0, 0, 0))
    if n_tiles == 1:
        gate_specs = [_gate_rows_spec(bb, tl, 1)]
    else:
        gate_specs = [pl.BlockSpec((N_GATE_ROWS, SUBLANES, tl), lambda i, j, k=k: (0, 0, (i * bb + k) * n_tiles + j))
                      for k in range(bb)]
    kern = functools.partial(_delta_kernel, bb=bb, tl=tl, chunk=chunk, pos0=pos0, n_tiles=n_tiles,
                             n_gate_refs=len(gate_specs), n_casts=len(cast_specs), rows_first=rows_first)
    sds = lambda *shape: jax.ShapeDtypeStruct(shape, F32)
    return pl.pallas_call(
        kern,
        grid=(bsz // bb, n_tiles),
        in_specs=gate_specs + [
                  _seq_spec(bb, tl, QKV_W),
                  pl.BlockSpec((bb, tl, QK_W + WIDTH_B), lambda i, j: (i, j, REST_Z0 // (QK_W + WIDTH_B))),
                  _state_spec(bb, POOL_HIST, WIDTH_B, rows_first), state_s,
                  _const_spec((1, HEAD_DIM)), _const_spec((len(POOL_WINDOWS), POOL_GROUP, POOL_GROUP)),
                  _const_spec((1, WIDTH_B)), _const_spec((1, LANES))] + cast_specs,
        out_specs=[_seq_spec(bb, tl, QK_W + WIDTH_B),
                   _state_spec(bb, POOL_HIST, WIDTH_B, rows_first), state_s] + cast_specs,
        out_shape=[jax.ShapeDtypeStruct((bsz, l, QK_W + WIDTH_B), act_dtype),
                   _state_shape(bsz, POOL_HIST, WIDTH_B, rows_first),
                   sds(bsz, N_HEADS, HEAD_DIM, HEAD_DIM)]
                  + [jax.ShapeDtypeStruct(arr.shape, BF16) for arr, _ in cast_to_bf16],
        scratch_shapes=[pltpu.VMEM((bb, POOL_PAD + tl, WIDTH_B), F32)],
        compiler_params=pltpu.CompilerParams(dimension_semantics=("arbitrary", "arbitrary"),
                                             vmem_limit_bytes=VMEM_LIMIT),
        name="delta",
    )(*([gsc] * len(gate_specs)), qkv, rest, pool_prev, s0, w_onorm, w_mix, pool_scale,
      jnp.ones((1, LANES), jnp.int32), *[arr for arr, _ in cast_to_bf16])


def _mlp_kernel(x_ref, oab_ref, gate_ref, wa_ref, wb_ref, wo_ref, gmlp_ref, wup_ref, wdown_ref, gfin_ref, y_ref):
    ma = _dot(oab_ref[:, 0:QK_W], wa_ref[...])
    mb = _dot(oab_ref[:, QK_W:QK_W + WIDTH_B], wb_ref[...])
    merged = _sigmoid(gate_ref[:, 0:D_MODEL]) * ma + _sigmoid(gate_ref[:, D_MODEL:]) * mb
    x1 = x_ref[...] + _dot(merged, wo_ref[...])
    h2 = (x1 * _rms_scale(x1) * gmlp_ref[...]).astype(BF16)
    acc = x1
    for c0 in range(0, D_FF, FF_BLOCK):
        up = jnp.dot(h2, wup_ref[:, c0:c0 + FF_BLOCK], preferred_element_type=F32)
        act = jnp.square(jnp.maximum(up, 0.0))
        acc = acc + _dot(act, wdown_ref[c0:c0 + FF_BLOCK, :])
    y_ref[...] = acc * _rms_scale(acc) * gfin_ref[...]


def _merge_mlp(x2d, oab, rest, prm, tm):
    t = x2d.shape[0]
    assert t % tm == 0, "row tiles must cover the tokens exactly"
    row = lambda w: pl.BlockSpec((tm, w), lambda i: (i, 0))
    return pl.pallas_call(
        _mlp_kernel,
        grid=(t // tm,),
        in_specs=[row(D_MODEL), row(QK_W + WIDTH_B), row(2 * D_MODEL),
                  _const_spec((QK_W, D_MODEL)), _const_spec((WIDTH_B, D_MODEL)), _const_spec((D_MODEL, D_MODEL)),
                  _const_spec((1, D_MODEL)), _const_spec((D_MODEL, D_FF)), _const_spec((D_FF, D_MODEL)),
                  _const_spec((1, D_MODEL))],
        out_specs=row(D_MODEL),
        out_shape=jax.ShapeDtypeStruct((t, D_MODEL), F32),
        compiler_params=pltpu.CompilerParams(dimension_semantics=("arbitrary",),
                                             vmem_limit_bytes=VMEM_LIMIT),
        name="merge_mlp",
    )(x2d, oab, rest, prm["w_a_out"], prm["w_b_out"], prm["w_o"], prm["g_mlp"], prm["w_up"], prm["w_down"],
      prm["g_final"])


def _mix(x, conv_prev, pool_prev, s_prev, pos0, prm, *, front_blk, delta_blk, chunk, act_dtype, rows_first,
         cast_to_bf16=()):
    bsz, l, _ = x.shape
    t = bsz * l
    qkv, rest, gsc, conv_new, *w_proj_bf16 = _front(
        x, prm["g_attn"], prm["w_proj"], prm["w_conv"], prm["a_log"], prm["dt_bias"],
        conv_prev, bb=front_blk[0], tl=front_blk[1], chunk=chunk, act_dtype=act_dtype, rows_first=rows_first)
    oab, pool_new, s_new, *casted = _delta(
        qkv, rest, gsc, pool_prev, s_prev, prm["w_onorm"], prm["w_mix"], prm["pool_scale"], bb=delta_blk[0],
        tl=delta_blk[1], chunk=chunk, pos0=pos0, act_dtype=act_dtype, rows_first=rows_first,
        cast_to_bf16=cast_to_bf16)
    rows = (x.reshape(t, D_MODEL), oab.reshape(t, QK_W + WIDTH_B), rest.reshape(t, REST_OUT_W))
    return rows, (conv_new, pool_new, s_new[None]), w_proj_bf16 + casted


def kernel(x_prompt, x_sample, state_conv, state_pool, state_ssm, w_in, w_conv, a_log, dt_bias, w_onorm,
           w_pool_mix, pool_scale, w_a_out, w_b_out, w_o, g_attn, g_mlp, w_up, w_down, g_final):
    assert w_in.shape[0] == 1, "single-layer decoder"
    prm = {
        "w_proj": jnp.transpose(w_in[0]).astype(F32),
        "g_attn": g_attn[0][None, :], "g_mlp": g_mlp[0][None, :], "g_final": g_final[None, :],
        "w_conv": w_conv[0].astype(F32),
        "a_log": a_log[0], "dt_bias": dt_bias[0],
        "w_onorm": w_onorm[0][None, :].astype(F32),
        "w_mix": w_pool_mix[0].astype(BF16), "pool_scale": pool_scale[0][None, :].astype(F32),
    }
    bp = x_prompt.shape[0]
    converted = ("w_proj", "w_up", "w_down", "w_o", "w_a_out", "w_b_out")
    rows_p, (conv_p, pool_p, ssm_p), casted = _mix(
        x_prompt, jnp.zeros((bp, CONV_W - 1, QKV_W), F32), jnp.zeros((bp, POOL_HIST, WIDTH_B), F32),
        jnp.zeros((bp, N_HEADS, HEAD_DIM, HEAD_DIM), F32), 0, prm,
        front_blk=(1, 512), delta_blk=(4, 256), chunk=GROUP, act_dtype=BF16, rows_first=False,
        cast_to_bf16=((w_up[0].astype(F32), 1), (w_down[0].astype(F32), 0), (w_o[0].astype(F32), 0),
                      (w_a_out[0].astype(F32), 0), (w_b_out[0].astype(F32), 0)))
    prm.update(zip(converted, casted, strict=True))
    conv_p, pool_p = conv_p[None], pool_p[None]
    dec_len = x_sample.shape[1]
    rows_major = lambda s: jnp.transpose(s[0].astype(F32), (1, 0, 2))
    rows_s, (conv_s, pool_s, ssm_s), _ = _mix(
        x_sample, rows_major(state_conv), rows_major(state_pool), state_ssm[0].astype(F32), PAST_LEN, prm,
        front_blk=(256 // dec_len, dec_len), delta_blk=(2 * GROUP // dec_len, dec_len),
        chunk=dec_len, act_dtype=F32, rows_first=True)
    conv_s, pool_s = jnp.transpose(conv_s, (1, 0, 2))[None], jnp.transpose(pool_s, (1, 0, 2))[None]
    y_p = _merge_mlp(*rows_p, prm, tm=512).reshape(x_prompt.shape)
    y_s = _merge_mlp(*rows_s, prm, tm=256).reshape(x_sample.shape)
    return (y_p, y_s, conv_p.astype(state_conv.dtype), pool_p.astype(state_pool.dtype),
            ssm_p.astype(state_ssm.dtype), conv_s.astype(state_conv.dtype), pool_s.astype(state_pool.dtype),
            ssm_s.astype(state_ssm.dtype))
```

```python
import functools
import math

import jax
import jax.numpy as jnp
from jax import lax
from jax.experimental import pallas as pl
from jax.experimental.pallas import tpu as pltpu

D_MODEL = 1024
N_HEADS = 4
HEAD_DIM = 128
QK_W = N_HEADS * HEAD_DIM
QKV_W = 3 * QK_W
CONV_W = 4
POOL_WINDOWS = (2, 4, 8, 16)
POOL_GROUP = 128
WIDTH_B = len(POOL_WINDOWS) * POOL_GROUP
POOL_HIST = 15
D_FF = 4 * D_MODEL
EPS = 1e-6
PAST_LEN = 16384
LANES = 128
SUBLANES = 8
MXU_COLS = 256
N_GATE_ROWS = 4

QKVZ_W = QKV_W + QK_W
GATE_SCALARS = 2 * N_HEADS
REST_W = WIDTH_B + 2 * D_MODEL
W_GATE0 = QKVZ_W
WT_COLS = QKVZ_W + REST_W
WT_ROWS = D_MODEL + 2 * SUBLANES
WT_CHUNK = 512
REST_Z0 = 2 * D_MODEL
REST_P0 = REST_Z0 + QK_W
REST_OUT_W = REST_P0 + WIDTH_B

GROUP = 128
SERIES_BLOCK = 64
CONV_PAD = 8
POOL_LOOKBACK = 16
POOL_PAD = 24
FF_BLOCK = 1024

VMEM_LIMIT = 56 * 1024 * 1024

BF16 = jnp.bfloat16
F32 = jnp.float32


def _dot(a, b):
    return jnp.dot(a.astype(BF16), b.astype(BF16), preferred_element_type=F32)


def _sigmoid(x):
    return 1.0 / (1.0 + jnp.exp(-x))


def _silu_of_twice(half):
    return half * jnp.tanh(half) + half


def _silu(x):
    return _silu_of_twice(0.5 * x)


def _rms_scale(x):
    return lax.rsqrt(jnp.mean(x * x, axis=-1, keepdims=True) + EPS)


def _const_spec(shape):
    zeros = (0,) * len(shape)
    return pl.BlockSpec(shape, lambda *_: zeros, pipeline_mode=pl.Buffered(1))


def _seq_spec(bb, tl, width):
    return pl.BlockSpec((bb, tl, width), lambda i, j: (i, j, 0))


def _state_spec(bb, n_rows, width, rows_first):
    if rows_first:
        return pl.BlockSpec((n_rows, bb, width), lambda i, j: (0, i, 0))
    return pl.BlockSpec((bb, n_rows, width), lambda i, j: (i, 0, 0))


def _state_shape(bsz, n_rows, width, rows_first):
    return jax.ShapeDtypeStruct((n_rows, bsz, width) if rows_first else (bsz, n_rows, width), F32)


def _history_to_scratch(scratch_ref, first_row, state_ref, rows_first):
    if rows_first:
        for j in range(state_ref.shape[0]):
            scratch_ref[:, first_row + j, :] = state_ref[j]
    else:
        scratch_ref[:, first_row:first_row + state_ref.shape[1], :] = state_ref[...]


def _history_from_scratch(state_ref, scratch_ref, first_row, rows_first):
    if rows_first:
        for j in range(state_ref.shape[0]):
            state_ref[j] = scratch_ref[:, first_row + j, :]
    else:
        state_ref[...] = scratch_ref[:, first_row:first_row + state_ref.shape[1], :]


def _gate_rows_spec(bb, tl, n_tiles):
    return pl.BlockSpec((N_GATE_ROWS, SUBLANES, bb * tl), lambda i, j: (0, 0, i * n_tiles + j))


def _lane_prefix_sum(x, chunk):
    lane = lax.broadcasted_iota(jnp.int32, x.shape, 1)
    shift = 1
    while shift < chunk:
        x = x + jnp.where(lane % chunk >= shift, pltpu.roll(x, shift, axis=1), 0.0)
        shift *= 2
    return x


def _lane_suffix_sum(x, chunk):
    lane = lax.broadcasted_iota(jnp.int32, x.shape, 1)
    shift = 1
    while shift < chunk:
        x = x + jnp.where(lane % chunk + shift < chunk, pltpu.roll(x, x.shape[1] - shift, axis=1), 0.0)
        shift *= 2
    return x


def _front_kernel(x_ref, g_ref, w_ref, wconv_ref, decay_ref, keep_ref, convprev_ref,
                  qkv_ref, rest_ref, gsc_ref, convnew_ref,
                  *tail, bb, tl, n_tiles, chunk, rows_first, convert_weights):
    tile = pl.program_id(1)
    rows = bb * tl
    hist = slice(CONV_PAD - (CONV_W - 1), CONV_PAD)
    ext_ref = tail[-1]

    if convert_weights:
        w_src_ref, w_ref = w_ref, tail[0]
        src_rest0 = W_GATE0 + GATE_SCALARS

        @pl.when((pl.program_id(0) == 0) & (tile == 0))
        def _():
            for src0, dst0, n in ((0, 0, QKVZ_W), (src_rest0, QKVZ_W, REST_W)):
                for c in range(0, n, WT_CHUNK):
                    w_ref[0:D_MODEL, dst0 + c:dst0 + c + WT_CHUNK] = (
                        w_src_ref[src0 + c:src0 + c + WT_CHUNK, :].T.astype(BF16))
            gate_rows = jnp.concatenate([w_src_ref[W_GATE0:src_rest0, :],
                                         jnp.zeros((WT_ROWS - D_MODEL - GATE_SCALARS, D_MODEL), F32)], axis=0)
            w_ref[D_MODEL:WT_ROWS, 0:D_MODEL] = gate_rows.astype(BF16)
            w_ref[D_MODEL:WT_ROWS, D_MODEL:WT_COLS] = jnp.zeros((WT_ROWS - D_MODEL, WT_COLS - D_MODEL), BF16)

    @pl.when(tile == 0)
    def _():
        _history_to_scratch(ext_ref, hist.start, convprev_ref, rows_first)
        ext_ref[:, 0:CONV_PAD - (CONV_W - 1), :] = jnp.zeros((bb, CONV_PAD - (CONV_W - 1), QKV_W), F32)

    x = x_ref[...].reshape(rows, D_MODEL)
    normed = (x * _rms_scale(x) * g_ref[...]).astype(BF16)
    proj = lambda c0, n: jnp.dot(normed, w_ref[0:D_MODEL, c0:c0 + n], preferred_element_type=F32)
    ext_ref[:, CONV_PAD:CONV_PAD + tl, :] = proj(0, QKV_W).reshape(bb, tl, QKV_W)
    _history_from_scratch(convnew_ref, ext_ref, CONV_PAD + tl - (CONV_W - 1), rows_first)

    x8 = lax.dot_general(w_ref[D_MODEL:WT_ROWS, 0:D_MODEL], normed, (((1,), (1,)), ((), ())),
                         preferred_element_type=F32)[0:SUBLANES, :]
    xs = x8 + decay_ref[:, 1:2]
    softplus = jnp.maximum(xs, 0.0) + jnp.log1p(jnp.exp(-jnp.abs(xs)))
    graw8 = -jnp.exp(decay_ref[:, 0:1]) * softplus
    g8 = _lane_prefix_sum(graw8, chunk)
    gsc_ref[0] = _sigmoid(x8)
    gsc_ref[1] = g8
    gsc_ref[2] = jnp.exp(g8)
    gsc_ref[3] = jnp.exp(_lane_suffix_sum(graw8, chunk) - graw8)

    keep = keep_ref[...] != 0

    def conv_block(c0, dep):
        cols = slice(c0, c0 + LANES)
        xe = ext_ref[:, :, cols].reshape(bb * (CONV_PAD + tl), LANES)
        x1 = pltpu.roll(xe, 1, axis=0)
        tap = lambda j: 0.5 * wconv_ref[j:j + 1, cols]
        acc = (tap(3) * xe + tap(2) * x1) + pltpu.roll(tap(1) * xe + tap(0) * x1, 2, axis=0)
        val = _silu_of_twice(acc.reshape(bb, CONV_PAD + tl, LANES)[:, CONV_PAD:, :])
        if c0 < 2 * QK_W:
            scale = HEAD_DIM ** -0.5 if c0 < QK_W else 1.0
            val = val * (lax.rsqrt(jnp.sum(val * val, axis=-1, keepdims=True) + EPS) * scale)
        qkv_ref[:, :, cols] = jnp.where(keep, val, dep[:, 0:LANES].reshape(bb, tl, LANES)).astype(qkv_ref.dtype)

    def proj_block(out0, w0, c0):
        val = proj(w0 + c0, MXU_COLS)
        rest_ref[:, :, out0 + c0:out0 + c0 + MXU_COLS] = val.reshape(bb, tl, MXU_COLS)
        return val

    mxu_work = ([functools.partial(proj_block, REST_Z0, QKV_W, c0) for c0 in range(0, QK_W, MXU_COLS)]
                + [functools.partial(proj_block, REST_P0, QKVZ_W, c0) for c0 in range(0, WIDTH_B, MXU_COLS)]
                + [functools.partial(proj_block, 0, QKVZ_W + WIDTH_B, c0)
                   for c0 in range(0, 2 * D_MODEL, MXU_COLS)])
    for i, c0 in enumerate(range(0, QKV_W, LANES)):
        conv_block(c0, mxu_work[i]())

    if n_tiles > 1:
        _history_to_scratch(ext_ref, hist.start, convnew_ref, rows_first)


def _front(x, g_attn, w_proj, w_conv, a_log, dt_bias, conv_prev, *, bb, tl, chunk, act_dtype, rows_first):
    convert_weights = w_proj.dtype != BF16
    assert w_proj.shape == ((QKVZ_W + GATE_SCALARS + REST_W, D_MODEL) if convert_weights else (WT_ROWS, WT_COLS))
    bsz, l, _ = x.shape
    assert bsz % bb == 0 and l % tl == 0 and (bb * tl) % chunk == 0, "blocks must tile the batch, length and chunks"
    assert bb == 1 or tl == l, "a block of several sequences must hold them whole (gate rows are token-major)"
    assert CONV_W == 4, "the conv is written out as two pairs of taps"
    n_tiles = l // tl
    kern = functools.partial(_front_kernel, bb=bb, tl=tl, n_tiles=n_tiles, chunk=chunk, rows_first=rows_first,
                             convert_weights=convert_weights)
    sds = lambda *shape: jax.ShapeDtypeStruct(shape, F32)
    w_out_specs = [_const_spec((WT_ROWS, WT_COLS))] if convert_weights else []
    w_out_shapes = [jax.ShapeDtypeStruct((WT_ROWS, WT_COLS), BF16)] if convert_weights else []
    decay_prm = jnp.pad(jnp.stack([a_log, dt_bias], axis=1).astype(F32), ((N_HEADS, 0), (0, 0)))
    return pl.pallas_call(
        kern,
        grid=(bsz // bb, n_tiles),
        in_specs=[_seq_spec(bb, tl, D_MODEL), _const_spec((1, D_MODEL)), _const_spec(w_proj.shape),
                  _const_spec((CONV_W, QKV_W)), _const_spec((SUBLANES, 2)), _const_spec((1, LANES)),
                  _state_spec(bb, CONV_W - 1, QKV_W, rows_first)],
        out_specs=[_seq_spec(bb, tl, QKV_W), _seq_spec(bb, tl, REST_OUT_W), _gate_rows_spec(bb, tl, n_tiles),
                   _state_spec(bb, CONV_W - 1, QKV_W, rows_first)] + w_out_specs,
        out_shape=[jax.ShapeDtypeStruct((bsz, l, QKV_W), act_dtype), sds(bsz, l, REST_OUT_W),
                   sds(N_GATE_ROWS, SUBLANES, bsz * l),
                   _state_shape(bsz, CONV_W - 1, QKV_W, rows_first)] + w_out_shapes,
        scratch_shapes=[pltpu.VMEM((bb, CONV_PAD + tl, QKV_W), F32)],
        compiler_params=pltpu.CompilerParams(dimension_semantics=("arbitrary", "arbitrary"),
                                             vmem_limit_bytes=VMEM_LIMIT),
        name="front",
    )(x, g_attn, w_proj, w_conv, decay_prm, jnp.ones((1, LANES), jnp.int32), conv_prev)


def _unit_lower_inverses(ms, chunk, side_jobs=()):
    nb = min(chunk, SERIES_BLOCK)
    n_blocks = GROUP // nb
    n_factors = int(math.log2(nb))
    assert chunk == nb or (chunk == 2 * nb and n_blocks == 2), "chunks are one or two series blocks"
    lane = lax.broadcasted_iota(jnp.int32, (nb, GROUP), 1)
    lane_block = lane // nb

    def packed(m):
        out = m[0:nb]
        for b in range(1, n_blocks):
            out = jnp.where(lane_block == b, m[b * nb:(b + 1) * nb], out)
        return out

    def block_diag(p):
        return jnp.concatenate([jnp.where(lane_block == b, p, 0.0) for b in range(n_blocks)], axis=0)

    side_jobs = list(side_jobs)
    per_step = -(-len(side_jobs) // max(n_factors - 1, 1))

    def issue_side_jobs(step_values):
        for k in range(min(per_step, len(side_jobs))):
            side_jobs.pop(0)(step_values[k % len(step_values)])

    nps = [packed(m) for m in ms]
    eye_p = jnp.where(lane % nb == lax.broadcasted_iota(jnp.int32, (nb, GROUP), 0), 1.0, 0.0).astype(F32)
    invs = [eye_p - n for n in nps]
    if n_factors > 1:
        pws = [_dot(n, block_diag(n)) for n in nps]
        issue_side_jobs(pws)
        for _ in range(n_factors - 2):
            boths = [_dot(jnp.concatenate([inv, pw], axis=0), block_diag(pw)) for inv, pw in zip(invs, pws)]
            invs = [inv + both[0:nb] for inv, both in zip(invs, boths)]
            pws = [both[nb:] for both in boths]
            issue_side_jobs(pws)
        invs = [inv + _dot(inv, block_diag(pw)) for inv, pw in zip(invs, pws)]
    while side_jobs:
        side_jobs.pop(0)(None)
    if chunk == nb:
        return [block_diag(inv) for inv in invs]
    zeros = jnp.zeros((nb, GROUP), F32)
    a_invs = [jnp.where(lane < nb, inv, 0.0) for inv in invs]
    lows = [jnp.where(lane < nb, m[nb:], 0.0) for m in ms]
    xs = [_dot(low, jnp.concatenate([a_inv, zeros], axis=0)) for low, a_inv in zip(lows, a_invs)]
    ys = [_dot(inv, jnp.concatenate([zeros, x], axis=0)) for inv, x in zip(invs, xs)]
    return [jnp.concatenate([a_inv, jnp.where(lane >= nb, inv, 0.0) - y], axis=0)
            for a_inv, y, inv in zip(a_invs, ys, invs)]


def _delta_kernel(*refs, bb, tl, chunk, pos0, n_tiles, n_gate_refs, n_casts, rows_first):
    gsc_refs, refs = refs[:n_gate_refs], refs[n_gate_refs:]
    (qkv_ref, zp_ref, poolprev_ref, s0_ref, wonorm_ref, wmix_ref, pscale_ref, keep_ref) = refs[0:8]
    cast_src, refs = refs[8:8 + n_casts], refs[8 + n_casts:]
    oab_ref, poolnew_ref, snew_ref = refs[0:3]
    cast_dst, pext_ref = refs[3:3 + n_casts], refs[3 + n_casts]
    tile = pl.program_id(1)

    for src, dst in zip(cast_src, cast_dst):
        dst[...] = src[...].astype(dst.dtype)

    rows_b = min(tl, GROUP)
    seqs_g = GROUP // rows_b
    groups_b = tl // rows_b
    n_groups = bb * tl // GROUP
    chained = chunk == GROUP
    assert chained or (chunk == tl and tl < GROUP), "chunk must be a whole group or a whole short sequence"
    pool_hist = slice(POOL_PAD - POOL_HIST, POOL_PAD)

    @pl.when(tile == 0)
    def _():
        snew_ref[...] = s0_ref[...]
        _history_to_scratch(pext_ref, pool_hist.start, poolprev_ref, rows_first)
        pext_ref[:, POOL_PAD - POOL_LOOKBACK:POOL_PAD - POOL_HIST, :] = jnp.zeros((bb, 1, WIDTH_B), F32)

    pext_ref[:, POOL_PAD:POOL_PAD + tl, :] = zp_ref[:, :, QK_W:QK_W + WIDTH_B]
    _history_from_scratch(poolnew_ref, pext_ref, POOL_PAD + tl - POOL_HIST, rows_first)

    row = lax.broadcasted_iota(jnp.int32, (GROUP, GROUP), 0)
    col = lax.broadcasted_iota(jnp.int32, (GROUP, GROUP), 1)
    causal = row >= col
    strict = row > col
    if not chained:
        same = (row // chunk) == (col // chunk)
        causal = causal & same
        strict = strict & same

    def origin(g):
        if tl >= GROUP:
            return g // groups_b, (g % groups_b) * GROUP
        return g * seqs_g, 0

    def load(c0, g):
        b0, t0 = origin(g)
        return qkv_ref[b0:b0 + seqs_g, t0:t0 + rows_b, c0:c0 + HEAD_DIM].astype(F32).reshape(GROUP, HEAD_DIM)

    def gate_rows(g):
        if n_gate_refs > 1:
            b0, t0 = origin(g)
            return tuple(gsc_refs[b0][q, :, t0:t0 + GROUP] for q in range(N_GATE_ROWS))
        return tuple(gsc_refs[0][q, :, g * GROUP:(g + 1) * GROUP] for q in range(N_GATE_ROWS))

    gates = [gate_rows(g) for g in range(n_groups)]

    probs = [(g, h) for g in range(n_groups) for h in range(N_HEADS)]
    st = []
    for g, h in probs:
        beta8, g8, eg8, kds8 = gates[g]
        d = {"q": load(h * HEAD_DIM, g), "k": load(QK_W + h * HEAD_DIM, g), "v": load(2 * QK_W + h * HEAD_DIM, g)}
        d["beta_row"] = beta8[h:h + 1, :]
        d["eg_row"] = eg8[N_HEADS + h:N_HEADS + h + 1, :]
        d["g_row"] = g8[N_HEADS + h:N_HEADS + h + 1, :]
        d["kb_row"] = kds8[N_HEADS + h:N_HEADS + h + 1, :] * d["beta_row"]
        d["g_col"] = jnp.broadcast_to(d["g_row"], (GROUP, GROUP)).T
        d["kt"] = d["k"].T
        st.append(d)
    for d in st:
        both = _dot(jnp.concatenate([d["k"], d["q"]], axis=0), d["kt"])
        d["kk"], d["qk"] = both[0:GROUP], both[GROUP:]
    for d in st:
        decay = jnp.exp(jnp.where(causal, d["g_col"] - d["g_row"], -jnp.inf))
        decay_beta = decay * d["beta_row"]
        d["m"] = jnp.where(strict, d.pop("kk") * decay_beta, 0.0)
        d["qkm"] = d.pop("qk") * decay_beta
        d["q_dec"] = d.pop("q") * jnp.exp(d["g_col"])
        d["kt_dec"] = d.pop("kt") * d["kb_row"]

    keep = keep_ref[...] != 0

    def pool_block(g, gi, issued_with):
        b0, t0 = origin(g)
        win = POOL_WINDOWS[gi]
        cols = slice(gi * POOL_GROUP, (gi + 1) * POOL_GROUP)
        pos = pos0 + tile * tl + t0 + row % rows_b
        slab = pext_ref[b0:b0 + seqs_g, POOL_PAD - POOL_LOOKBACK + t0:POOL_PAD + t0 + rows_b, cols]
        acc = slab.reshape(seqs_g * (POOL_LOOKBACK + rows_b), POOL_GROUP)
        shift = 1
        while shift < win:
            acc = acc + pltpu.roll(acc, shift, axis=0)
            shift *= 2
        acc = acc.reshape(seqs_g, POOL_LOOKBACK + rows_b, POOL_GROUP)[:, POOL_LOOKBACK:, :]
        cur = slab[:, POOL_LOOKBACK:, :]
        pooled = (acc / jnp.minimum(pos + 1, win).astype(F32).reshape(seqs_g, rows_b, POOL_GROUP) - cur)
        mixed = _dot(pooled.reshape(GROUP, POOL_GROUP), wmix_ref[gi]) * pscale_ref[:, cols]
        if issued_with is not None:
            reps = GROUP // issued_with.shape[0]
            mixed = jnp.where(keep, mixed, jnp.concatenate([issued_with] * reps, axis=0))
        oab_ref[b0:b0 + seqs_g, t0:t0 + rows_b, QK_W + gi * POOL_GROUP:QK_W + (gi + 1) * POOL_GROUP] = (
            mixed.reshape(seqs_g, rows_b, POOL_GROUP).astype(oab_ref.dtype))

    pool_jobs = [functools.partial(pool_block, g, gi) for g in range(n_groups) for gi in range(len(POOL_WINDOWS))]
    invs = _unit_lower_inverses([d.pop("m") for d in st], chunk, pool_jobs)
    for d, inv in zip(st, invs):
        d["uy"] = _dot(inv, d.pop("v"))
        d["wy"] = _dot(inv * d["eg_row"], d.pop("k"))

    def group_slab(ref, g, cols, row_off=0):
        b0, t0 = origin(g)
        return ref.at[b0:b0 + seqs_g, row_off + t0:row_off + t0 + rows_b, cols]

    def head_out(g, h, o):
        cols = slice(h * HEAD_DIM, (h + 1) * HEAD_DIM)
        z = group_slab(zp_ref, g, cols)[...].reshape(GROUP, HEAD_DIM)
        group_slab(oab_ref, g, cols)[...] = (o * _rms_scale(o) * wonorm_ref[...] * _silu(z)).reshape(
            seqs_g, rows_b, HEAD_DIM).astype(oab_ref.dtype)

    if chained:
        for j in range(groups_b):
            wave = [(i, g, h) for i, (g, h) in enumerate(probs) if g % groups_b == j]
            s_old = {i: snew_ref[origin(g)[0], h] for i, g, h in wave}
            ws = {i: _dot(jnp.concatenate([st[i]["wy"], st[i]["q_dec"]], axis=0), s_old[i]) for i, g, h in wave}
            ys = {i: st[i]["uy"] - ws[i][0:GROUP] for i, g, h in wave}
            outs = {i: ws[i][GROUP:] + _dot(st[i]["qkm"], ys[i]) for i, g, h in wave}
            for i, g, h in wave:
                last = jnp.exp(st[i]["g_col"][GROUP - 1:GROUP, :])
                snew_ref[origin(g)[0], h] = s_old[i] * last + _dot(st[i]["kt_dec"], ys[i])
            for i, g, h in wave:
                head_out(g, h, outs[i])
    else:
        outs = {}
        for i, (g, h) in enumerate(probs):
            d, b0 = st[i], origin(g)[0]
            ws_w, ws_q = [], []
            for s_i in range(seqs_g):
                r = slice(s_i * rows_b, (s_i + 1) * rows_b)
                ws = _dot(jnp.concatenate([d["wy"][r], d["q_dec"][r]], axis=0), snew_ref[b0 + s_i, h])
                ws_w.append(ws[0:rows_b])
                ws_q.append(ws[rows_b:])
            d["y"] = d["uy"] - jnp.concatenate(ws_w, axis=0)
            outs[i] = jnp.concatenate(ws_q, axis=0) + _dot(d["qkm"], d["y"])
        for i, (g, h) in enumerate(probs):
            d, b0 = st[i], origin(g)[0]
            for s_i in range(seqs_g):
                last = jnp.exp(d["g_col"][(s_i + 1) * rows_b - 1:(s_i + 1) * rows_b, :])
                upd = _dot(jnp.where(col // rows_b == s_i, d["kt_dec"], 0.0), d["y"])
                snew_ref[b0 + s_i, h] = snew_ref[b0 + s_i, h] * last + upd
        for i, (g, h) in enumerate(probs):
            head_out(g, h, outs[i])

    if n_tiles > 1:
        _history_to_scratch(pext_ref, pool_hist.start, poolnew_ref, rows_first)


def _delta(qkv, rest, gsc, pool_prev, s0, w_onorm, w_mix, pool_scale, *, bb, tl, chunk, pos0, act_dtype,
           rows_first, cast_to_bf16=()):
    bsz, l = qkv.shape[0], qkv.shape[1]
    assert bsz % bb == 0 and l % tl == 0 and (bb * tl) % GROUP == 0, "blocks must tile the batch, length and groups"
    n_tiles = l // tl
    n_steps = (bsz // bb) * n_tiles
    cast_specs = []
    for arr, axis in cast_to_bf16:
        tile = LANES if axis == 1 else 2 * SUBLANES
        assert arr.shape[axis] % (n_steps * tile) == 0, "each step converts a tile-aligned slice"
        blk = tuple(d // n_steps if a == axis else d for a, d in enumerate(arr.shape))
        cast_specs.append(pl.BlockSpec(blk, lambda i, j, axis=axis: tuple(
            i * n_tiles + j if a == axis else 0 for a in range(2))))
    state_s = pl.BlockSpec((bb, N_HEADS, HEAD_DIM, HEAD_DIM), lambda i, j: (i, 0, 0, 0))
    if n_tiles == 1:
        gate_specs = [_gate_rows_spec(bb, tl, 1)]
    else:
        gate_specs = [pl.BlockSpec((N_GATE_ROWS, SUBLANES, tl), lambda i, j, k=k: (0, 0, (i * bb + k) * n_tiles + j))
                      for k in range(bb)]
    kern = functools.partial(_delta_kernel, bb=bb, tl=tl, chunk=chunk, pos0=pos0, n_tiles=n_tiles,
                             n_gate_refs=len(gate_specs), n_casts=len(cast_specs), rows_first=rows_first)
    sds = lambda *shape: jax.ShapeDtypeStruct(shape, F32)
    return pl.pallas_call(
        kern,
        grid=(bsz // bb, n_tiles),
        in_specs=gate_specs + [
                  _seq_spec(bb, tl, QKV_W),
                  pl.BlockSpec((bb, tl, QK_W + WIDTH_B), lambda i, j: (i, j, REST_Z0 // (QK_W + WIDTH_B))),
                  _state_spec(bb, POOL_HIST, WIDTH_B, rows_first), state_s,
                  _const_spec((1, HEAD_DIM)), _const_spec((len(POOL_WINDOWS), POOL_GROUP, POOL_GROUP)),
                  _const_spec((1, WIDTH_B)), _const_spec((1, LANES))] + cast_specs,
        out_specs=[_seq_spec(bb, tl, QK_W + WIDTH_B),
                   _state_spec(bb, POOL_HIST, WIDTH_B, rows_first), state_s] + cast_specs,
        out_shape=[jax.ShapeDtypeStruct((bsz, l, QK_W + WIDTH_B), act_dtype),
                   _state_shape(bsz, POOL_HIST, WIDTH_B, rows_first),
                   sds(bsz, N_HEADS, HEAD_DIM, HEAD_DIM)]
                  + [jax.ShapeDtypeStruct(arr.shape, BF16) for arr, _ in cast_to_bf16],
        scratch_shapes=[pltpu.VMEM((bb, POOL_PAD + tl, WIDTH_B), F32)],
        compiler_params=pltpu.CompilerParams(dimension_semantics=("arbitrary", "arbitrary"),
                                             vmem_limit_bytes=VMEM_LIMIT),
        name="delta",
    )(*([gsc] * len(gate_specs)), qkv, rest, pool_prev, s0, w_onorm, w_mix, pool_scale,
      jnp.ones((1, LANES), jnp.int32), *[arr for arr, _ in cast_to_bf16])


def _mlp_kernel(x_ref, oab_ref, gate_ref, wa_ref, wb_ref, wo_ref, gmlp_ref, wup_ref, wdown_ref, gfin_ref, y_ref):
    ma = _dot(oab_ref[:, 0:QK_W], wa_ref[...])
    mb = _dot(oab_ref[:, QK_W:QK_W + WIDTH_B], wb_ref[...])
    merged = _sigmoid(gate_ref[:, 0:D_MODEL]) * ma + _sigmoid(gate_ref[:, D_MODEL:]) * mb
    x1 = x_ref[...] + _dot(merged, wo_ref[...])
    h2 = (x1 * _rms_scale(x1) * gmlp_ref[...]).astype(BF16)
    acc = x1
    for c0 in range(0, D_FF, FF_BLOCK):
        up = jnp.dot(h2, wup_ref[:, c0:c0 + FF_BLOCK], preferred_element_type=F32)
        act = jnp.square(jnp.maximum(up, 0.0))
        acc = acc + _dot(act, wdown_ref[c0:c0 + FF_BLOCK, :])
    y_ref[...] = acc * _rms_scale(acc) * gfin_ref[...]


def _merge_mlp(x2d, oab, rest, prm, tm):
    t = x2d.shape[0]
    assert t % tm == 0, "row tiles must cover the tokens exactly"
    row = lambda w: pl.BlockSpec((tm, w), lambda i: (i, 0))
    return pl.pallas_call(
        _mlp_kernel,
        grid=(t // tm,),
        in_specs=[row(D_MODEL), row(QK_W + WIDTH_B), row(2 * D_MODEL),
                  _const_spec((QK_W, D_MODEL)), _const_spec((WIDTH_B, D_MODEL)), _const_spec((D_MODEL, D_MODEL)),
                  _const_spec((1, D_MODEL)), _const_spec((D_MODEL, D_FF)), _const_spec((D_FF, D_MODEL)),
                  _const_spec((1, D_MODEL))],
        out_specs=row(D_MODEL),
        out_shape=jax.ShapeDtypeStruct((t, D_MODEL), F32),
        compiler_params=pltpu.CompilerParams(dimension_semantics=("arbitrary",),
                                             vmem_limit_bytes=VMEM_LIMIT),
        name="merge_mlp",
    )(x2d, oab, rest, prm["w_a_out"], prm["w_b_out"], prm["w_o"], prm["g_mlp"], prm["w_up"], prm["w_down"],
      prm["g_final"])


def _mix(x, conv_prev, pool_prev, s_prev, pos0, prm, *, front_blk, delta_blk, chunk, act_dtype, rows_first,
         cast_to_bf16=()):
    bsz, l, _ = x.shape
    t = bsz * l
    qkv, rest, gsc, conv_new, *w_proj_bf16 = _front(
        x, prm["g_attn"], prm["w_proj"], prm["w_conv"], prm["a_log"], prm["dt_bias"],
        conv_prev, bb=front_blk[0], tl=front_blk[1], chunk=chunk, act_dtype=act_dtype, rows_first=rows_first)
    oab, pool_new, s_new, *casted = _delta(
        qkv, rest, gsc, pool_prev, s_prev, prm["w_onorm"], prm["w_mix"], prm["pool_scale"], bb=delta_blk[0],
        tl=delta_blk[1], chunk=chunk, pos0=pos0, act_dtype=act_dtype, rows_first=rows_first,
        cast_to_bf16=cast_to_bf16)
    rows = (x.reshape(t, D_MODEL), oab.reshape(t, QK_W + WIDTH_B), rest.reshape(t, REST_OUT_W))
    return rows, (conv_new, pool_new, s_new[None]), w_proj_bf16 + casted


def kernel(x_prompt, x_sample, state_conv, state_pool, state_ssm, w_in, w_conv, a_log, dt_bias, w_onorm,
           w_pool_mix, pool_scale, w_a_out, w_b_out, w_o, g_attn, g_mlp, w_up, w_down, g_final):
    assert w_in.shape[0] == 1, "single-layer decoder"
    prm = {
        "w_proj": jnp.transpose(w_in[0]).astype(F32),
        "g_attn": g_attn[0][None, :], "g_mlp": g_mlp[0][None, :], "g_final": g_final[None, :],
        "w_conv": w_conv[0].astype(F32),
        "a_log": a_log[0], "dt_bias": dt_bias[0],
        "w_onorm": w_onorm[0][None, :].astype(F32),
        "w_mix": w_pool_mix[0].astype(BF16), "pool_scale": pool_scale[0][None, :].astype(F32),
    }
    bp = x_prompt.shape[0]
    converted = ("w_proj", "w_up", "w_down", "w_o", "w_a_out", "w_b_out")
    rows_p, (conv_p, pool_p, ssm_p), casted = _mix(
        x_prompt, jnp.zeros((bp, CONV_W - 1, QKV_W), F32), jnp.zeros((bp, POOL_HIST, WIDTH_B), F32),
        jnp.zeros((bp, N_HEADS, HEAD_DIM, HEAD_DIM), F32), 0, prm,
        front_blk=(1, 512), delta_blk=(4, 256), chunk=GROUP, act_dtype=BF16, rows_first=False,
        cast_to_bf16=((w_up[0].astype(F32), 1), (w_down[0].astype(F32), 0), (w_o[0].astype(F32), 0),
                      (w_a_out[0].astype(F32), 0), (w_b_out[0].astype(F32), 0)))
    prm.update(zip(converted, casted, strict=True))
    conv_p, pool_p = conv_p[None], pool_p[None]
    dec_len = x_sample.shape[1]
    rows_major = lambda s: jnp.transpose(s[0].astype(F32), (1, 0, 2))
    rows_s, (conv_s, pool_s, ssm_s), _ = _mix(
        x_sample, rows_major(state_conv), rows_major(state_pool), state_ssm[0].astype(F32), PAST_LEN, prm,
        front_blk=(256 // dec_len, dec_len), delta_blk=(2 * GROUP // dec_len, dec_len),
        chunk=dec_len, act_dtype=F32, rows_first=True)
    conv_s, pool_s = jnp.transpose(conv_s, (1, 0, 2))[None], jnp.transpose(pool_s, (1, 0, 2))[None]
    y_p = _merge_mlp(*rows_p, prm, tm=512).reshape(x_prompt.shape)
    y_s = _merge_mlp(*rows_s, prm, tm=256).reshape(x_sample.shape)
    return (y_p, y_s, conv_p.astype(state_conv.dtype), pool_p.astype(state_pool.dtype),
            ssm_p.astype(state_ssm.dtype), conv_s.astype(state_conv.dtype), pool_s.astype(state_pool.dtype),
            ssm_s.astype(state_ssm.dtype))
```

```python
import functools
import math

import jax
import jax.numpy as jnp
from jax import lax
from jax.experimental import pallas as pl
from jax.experimental.pallas import tpu as pltpu

D_MODEL = 1024
N_HEADS = 4
HEAD_DIM = 128
QK_W = N_HEADS * HEAD_DIM
QKV_W = 3 * QK_W
CONV_W = 4
POOL_WINDOWS = (2, 4, 8, 16)
POOL_GROUP = 128
WIDTH_B = len(POOL_WINDOWS) * POOL_GROUP
POOL_HIST = 15
D_FF = 4 * D_MODEL
EPS = 1e-6
PAST_LEN = 16384
LANES = 128
SUBLANES = 8
MXU_COLS = 256
N_GATE_ROWS = 4

QKVZ_W = QKV_W + QK_W
GATE_SCALARS = 2 * N_HEADS
REST_W = WIDTH_B + 2 * D_MODEL
W_GATE0 = QKVZ_W
WT_COLS = QKVZ_W + REST_W
WT_ROWS = D_MODEL + 2 * SUBLANES
WT_CHUNK = 512
REST_Z0 = 2 * D_MODEL
REST_P0 = REST_Z0 + QK_W
REST_OUT_W = REST_P0 + WIDTH_B

GROUP = 128
SERIES_BLOCK = 64
CONV_PAD = 8
POOL_LOOKBACK = 16
POOL_PAD = 24
FF_BLOCK = 1024

VMEM_LIMIT = 58 * 1024 * 1024
X_SLOTS = 3

BF16 = jnp.bfloat16
F32 = jnp.float32


def _dot(a, b):
    return jnp.dot(a.astype(BF16), b.astype(BF16), preferred_element_type=F32)


def _sigmoid(x):
    return 1.0 / (1.0 + jnp.exp(-x))


def _silu_of_twice(half):
    return half * jnp.tanh(half) + half


def _silu(x):
    return _silu_of_twice(0.5 * x)


def _rms_scale(x):
    return lax.rsqrt(jnp.mean(x * x, axis=-1, keepdims=True) + EPS)


def _const_spec(shape):
    zeros = (0,) * len(shape)
    return pl.BlockSpec(shape, lambda *_: zeros, pipeline_mode=pl.Buffered(1))


def _seq_spec(bb, tl, width):
    return pl.BlockSpec((bb, tl, width), lambda i, j: (i, j, 0))


def _state_spec(bb, n_rows, width, rows_first):
    if rows_first:
        return pl.BlockSpec((n_rows, bb, width), lambda i, j: (0, i, 0))
    return pl.BlockSpec((bb, n_rows, width), lambda i, j: (i, 0, 0))


def _state_shape(bsz, n_rows, width, rows_first):
    return jax.ShapeDtypeStruct((n_rows, bsz, width) if rows_first else (bsz, n_rows, width), F32)


def _history_to_scratch(scratch_ref, first_row, state_ref, rows_first):
    if rows_first:
        for j in range(state_ref.shape[0]):
            scratch_ref[:, first_row + j, :] = state_ref[j]
    else:
        scratch_ref[:, first_row:first_row + state_ref.shape[1], :] = state_ref[...]


def _history_from_scratch(state_ref, scratch_ref, first_row, rows_first):
    if rows_first:
        for j in range(state_ref.shape[0]):
            state_ref[j] = scratch_ref[:, first_row + j, :]
    else:
        state_ref[...] = scratch_ref[:, first_row:first_row + state_ref.shape[1], :]


def _gate_rows_spec(bb, tl, n_tiles):
    return pl.BlockSpec((N_GATE_ROWS, SUBLANES, bb * tl), lambda i, j: (0, 0, i * n_tiles + j))


def _lane_prefix_sum(x, chunk):
    lane = lax.broadcasted_iota(jnp.int32, x.shape, 1)
    shift = 1
    while shift < chunk:
        x = x + jnp.where(lane % chunk >= shift, pltpu.roll(x, shift, axis=1), 0.0)
        shift *= 2
    return x


def _lane_suffix_sum(x, chunk):
    lane = lax.broadcasted_iota(jnp.int32, x.shape, 1)
    shift = 1
    while shift < chunk:
        x = x + jnp.where(lane % chunk + shift < chunk, pltpu.roll(x, x.shape[1] - shift, axis=1), 0.0)
        shift *= 2
    return x


def _front_kernel(x_ref, g_ref, w_ref, wconv_ref, decay_ref, keep_ref, convprev_ref,
                  qkv_ref, rest_ref, gsc_ref, convnew_ref,
                  *tail, bb, tl, n_tiles, chunk, rows_first, convert_weights):
    tile = pl.program_id(1)
    rows = bb * tl
    hist = slice(CONV_PAD - (CONV_W - 1), CONV_PAD)
    ext_ref, xbuf_ref, xsem_ref = tail[-3:]

    n_steps = pl.num_programs(0) * n_tiles
    step = pl.program_id(0) * n_tiles + tile

    def x_copy(s):
        slot = s % X_SLOTS
        block = x_ref.at[pl.ds((s // n_tiles) * bb, bb), pl.ds((s % n_tiles) * tl, tl)]
        return pltpu.make_async_copy(block, xbuf_ref.at[slot], xsem_ref.at[slot])

    @pl.when(step == 0)
    def _():
        for s in range(X_SLOTS - 1):
            x_copy(s).start()

    @pl.when(step + (X_SLOTS - 1) < n_steps)
    def _():
        x_copy(step + (X_SLOTS - 1)).start()

    if convert_weights:
        w_src_ref, w_ref = w_ref, tail[0]
        src_rest0 = W_GATE0 + GATE_SCALARS

        @pl.when((pl.program_id(0) == 0) & (tile == 0))
        def _():
            for src0, dst0, n in ((0, 0, QKVZ_W), (src_rest0, QKVZ_W, REST_W)):
                for c in range(0, n, WT_CHUNK):
                    w_ref[0:D_MODEL, dst0 + c:dst0 + c + WT_CHUNK] = (
                        w_src_ref[src0 + c:src0 + c + WT_CHUNK, :].T.astype(BF16))
            gate_rows = jnp.concatenate([w_src_ref[W_GATE0:src_rest0, :],
                                         jnp.zeros((WT_ROWS - D_MODEL - GATE_SCALARS, D_MODEL), F32)], axis=0)
            w_ref[D_MODEL:WT_ROWS, 0:D_MODEL] = gate_rows.astype(BF16)
            w_ref[D_MODEL:WT_ROWS, D_MODEL:WT_COLS] = jnp.zeros((WT_ROWS - D_MODEL, WT_COLS - D_MODEL), BF16)

    @pl.when(tile == 0)
    def _():
        _history_to_scratch(ext_ref, hist.start, convprev_ref, rows_first)
        ext_ref[:, 0:CONV_PAD - (CONV_W - 1), :] = jnp.zeros((bb, CONV_PAD - (CONV_W - 1), QKV_W), F32)

    x_copy(step).wait()
    x = xbuf_ref[step % X_SLOTS].reshape(rows, D_MODEL)
    normed = (x * _rms_scale(x) * g_ref[...]).astype(BF16)
    proj = lambda c0, n: jnp.dot(normed, w_ref[0:D_MODEL, c0:c0 + n], preferred_element_type=F32)
    ext_ref[:, CONV_PAD:CONV_PAD + tl, :] = proj(0, QKV_W).reshape(bb, tl, QKV_W)
    _history_from_scratch(convnew_ref, ext_ref, CONV_PAD + tl - (CONV_W - 1), rows_first)

    x8 = lax.dot_general(w_ref[D_MODEL:WT_ROWS, 0:D_MODEL], normed, (((1,), (1,)), ((), ())),
                         preferred_element_type=F32)[0:SUBLANES, :]
    xs = x8 + decay_ref[:, 1:2]
    softplus = jnp.maximum(xs, 0.0) + jnp.log1p(jnp.exp(-jnp.abs(xs)))
    graw8 = -jnp.exp(decay_ref[:, 0:1]) * softplus
    g8 = _lane_prefix_sum(graw8, chunk)
    gsc_ref[0] = _sigmoid(x8)
    gsc_ref[1] = g8
    gsc_ref[2] = jnp.exp(g8)
    gsc_ref[3] = jnp.exp(_lane_suffix_sum(graw8, chunk) - graw8)

    keep = keep_ref[...] != 0

    def conv_block(c0, dep):
        cols = slice(c0, c0 + LANES)
        xe = ext_ref[:, :, cols].reshape(bb * (CONV_PAD + tl), LANES)
        x1 = pltpu.roll(xe, 1, axis=0)
        tap = lambda j: 0.5 * wconv_ref[j:j + 1, cols]
        acc = (tap(3) * xe + tap(2) * x1) + pltpu.roll(tap(1) * xe + tap(0) * x1, 2, axis=0)
        val = _silu_of_twice(acc.reshape(bb, CONV_PAD + tl, LANES)[:, CONV_PAD:, :])
        if c0 < 2 * QK_W:
            scale = HEAD_DIM ** -0.5 if c0 < QK_W else 1.0
            val = val * (lax.rsqrt(jnp.sum(val * val, axis=-1, keepdims=True) + EPS) * scale)
        qkv_ref[:, :, cols] = jnp.where(keep, val, dep[:, 0:LANES].reshape(bb, tl, LANES)).astype(qkv_ref.dtype)

    def proj_block(out0, w0, c0):
        val = proj(w0 + c0, MXU_COLS)
        rest_ref[:, :, out0 + c0:out0 + c0 + MXU_COLS] = val.reshape(bb, tl, MXU_COLS)
        return val

    mxu_work = ([functools.partial(proj_block, REST_Z0, QKV_W, c0) for c0 in range(0, QK_W, MXU_COLS)]
                + [functools.partial(proj_block, REST_P0, QKVZ_W, c0) for c0 in range(0, WIDTH_B, MXU_COLS)]
                + [functools.partial(proj_block, 0, QKVZ_W + WIDTH_B, c0)
                   for c0 in range(0, 2 * D_MODEL, MXU_COLS)])
    for i, c0 in enumerate(range(0, QKV_W, LANES)):
        conv_block(c0, mxu_work[i]())

    if n_tiles > 1:
        _history_to_scratch(ext_ref, hist.start, convnew_ref, rows_first)


def _front(x, g_attn, w_proj, w_conv, a_log, dt_bias, conv_prev, *, bb, tl, chunk, act_dtype, rows_first):
    convert_weights = w_proj.dtype != BF16
    assert w_proj.shape == ((QKVZ_W + GATE_SCALARS + REST_W, D_MODEL) if convert_weights else (WT_ROWS, WT_COLS))
    bsz, l, _ = x.shape
    assert bsz % bb == 0 and l % tl == 0 and (bb * tl) % chunk == 0, "blocks must tile the batch, length and chunks"
    assert bb == 1 or tl == l, "a block of several sequences must hold them whole (gate rows are token-major)"
    assert CONV_W == 4, "the conv is written out as two pairs of taps"
    n_tiles = l // tl
    assert (bsz // bb) * n_tiles >= X_SLOTS - 1, "the first step starts the copies of the first X_SLOTS - 1 steps"
    kern = functools.partial(_front_kernel, bb=bb, tl=tl, n_tiles=n_tiles, chunk=chunk, rows_first=rows_first,
                             convert_weights=convert_weights)
    sds = lambda *shape: jax.ShapeDtypeStruct(shape, F32)
    w_out_specs = [_const_spec((WT_ROWS, WT_COLS))] if convert_weights else []
    w_out_shapes = [jax.ShapeDtypeStruct((WT_ROWS, WT_COLS), BF16)] if convert_weights else []
    decay_prm = jnp.pad(jnp.stack([a_log, dt_bias], axis=1).astype(F32), ((N_HEADS, 0), (0, 0)))
    return pl.pallas_call(
        kern,
        grid=(bsz // bb, n_tiles),
        in_specs=[pl.BlockSpec(memory_space=pl.ANY), _const_spec((1, D_MODEL)), _const_spec(w_proj.shape),
                  _const_spec((CONV_W, QKV_W)), _const_spec((SUBLANES, 2)), _const_spec((1, LANES)),
                  _state_spec(bb, CONV_W - 1, QKV_W, rows_first)],
        out_specs=[_seq_spec(bb, tl, QKV_W), _seq_spec(bb, tl, REST_OUT_W), _gate_rows_spec(bb, tl, n_tiles),
                   _state_spec(bb, CONV_W - 1, QKV_W, rows_first)] + w_out_specs,
        out_shape=[jax.ShapeDtypeStruct((bsz, l, QKV_W), act_dtype), sds(bsz, l, REST_OUT_W),
                   sds(N_GATE_ROWS, SUBLANES, bsz * l),
                   _state_shape(bsz, CONV_W - 1, QKV_W, rows_first)] + w_out_shapes,
        scratch_shapes=[pltpu.VMEM((bb, CONV_PAD + tl, QKV_W), F32), pltpu.VMEM((X_SLOTS, bb, tl, D_MODEL), F32),
                        pltpu.SemaphoreType.DMA((X_SLOTS,))],
        compiler_params=pltpu.CompilerParams(dimension_semantics=("arbitrary", "arbitrary"),
                                             vmem_limit_bytes=VMEM_LIMIT),
        name="front",
    )(x, g_attn, w_proj, w_conv, decay_prm, jnp.ones((1, LANES), jnp.int32), conv_prev)


def _unit_lower_inverses(ms, chunk, side_jobs=()):
    nb = min(chunk, SERIES_BLOCK)
    n_blocks = GROUP // nb
    n_factors = int(math.log2(nb))
    assert chunk == nb or (chunk == 2 * nb and n_blocks == 2), "chunks are one or two series blocks"
    lane = lax.broadcasted_iota(jnp.int32, (nb, GROUP), 1)
    lane_block = lane // nb

    def packed(m):
        out = m[0:nb]
        for b in range(1, n_blocks):
            out = jnp.where(lane_block == b, m[b * nb:(b + 1) * nb], out)
        return out

    def block_diag(p):
        return jnp.concatenate([jnp.where(lane_block == b, p, 0.0) for b in range(n_blocks)], axis=0)

    side_jobs = list(side_jobs)
    per_step = -(-len(side_jobs) // max(n_factors - 1, 1))

    def issue_side_jobs(step_values):
        for k in range(min(per_step, len(side_jobs))):
            side_jobs.pop(0)(step_values[k % len(step_values)])

    nps = [packed(m) for m in ms]
    eye_p = jnp.where(lane % nb == lax.broadcasted_iota(jnp.int32, (nb, GROUP), 0), 1.0, 0.0).astype(F32)
    invs = [eye_p - n for n in nps]
    if n_factors > 1:
        pws = [_dot(n, block_diag(n)) for n in nps]
        issue_side_jobs(pws)
        for _ in range(n_factors - 2):
            boths = [_dot(jnp.concatenate([inv, pw], axis=0), block_diag(pw)) for inv, pw in zip(invs, pws)]
            invs = [inv + both[0:nb] for inv, both in zip(invs, boths)]
            pws = [both[nb:] for both in boths]
            issue_side_jobs(pws)
        invs = [inv + _dot(inv, block_diag(pw)) for inv, pw in zip(invs, pws)]
    while side_jobs:
        side_jobs.pop(0)(None)
    if chunk == nb:
        return [block_diag(inv) for inv in invs]
    zeros = jnp.zeros((nb, GROUP), F32)
    a_invs = [jnp.where(lane < nb, inv, 0.0) for inv in invs]
    lows = [jnp.where(lane < nb, m[nb:], 0.0) for m in ms]
    xs = [_dot(low, jnp.concatenate([a_inv, zeros], axis=0)) for low, a_inv in zip(lows, a_invs)]
    ys = [_dot(inv, jnp.concatenate([zeros, x], axis=0)) for inv, x in zip(invs, xs)]
    return [jnp.concatenate([a_inv, jnp.where(lane >= nb, inv, 0.0) - y], axis=0)
            for a_inv, y, inv in zip(a_invs, ys, invs)]


def _delta_kernel(*refs, bb, tl, chunk, pos0, n_tiles, n_gate_refs, n_casts, rows_first):
    gsc_refs, refs = refs[:n_gate_refs], refs[n_gate_refs:]
    (qkv_ref, zp_ref, poolprev_ref, s0_ref, wonorm_ref, wmix_ref, pscale_ref, keep_ref) = refs[0:8]
    cast_src, refs = refs[8:8 + n_casts], refs[8 + n_casts:]
    oab_ref, poolnew_ref, snew_ref = refs[0:3]
    cast_dst, pext_ref = refs[3:3 + n_casts], refs[3 + n_casts]
    tile = pl.program_id(1)

    for src, dst in zip(cast_src, cast_dst):
        dst[...] = src[...].astype(dst.dtype)

    rows_b = min(tl, GROUP)
    seqs_g = GROUP // rows_b
    groups_b = tl // rows_b
    n_groups = bb * tl // GROUP
    chained = chunk == GROUP
    assert chained or (chunk == tl and tl < GROUP), "chunk must be a whole group or a whole short sequence"
    pool_hist = slice(POOL_PAD - POOL_HIST, POOL_PAD)

    @pl.when(tile == 0)
    def _():
        snew_ref[...] = s0_ref[...]
        _history_to_scratch(pext_ref, pool_hist.start, poolprev_ref, rows_first)
        pext_ref[:, POOL_PAD - POOL_LOOKBACK:POOL_PAD - POOL_HIST, :] = jnp.zeros((bb, 1, WIDTH_B), F32)

    pext_ref[:, POOL_PAD:POOL_PAD + tl, :] = zp_ref[:, :, QK_W:QK_W + WIDTH_B]
    _history_from_scratch(poolnew_ref, pext_ref, POOL_PAD + tl - POOL_HIST, rows_first)

    row = lax.broadcasted_iota(jnp.int32, (GROUP, GROUP), 0)
    col = lax.broadcasted_iota(jnp.int32, (GROUP, GROUP), 1)
    causal = row >= col
    strict = row > col
    if not chained:
        same = (row // chunk) == (col // chunk)
        causal = causal & same
        strict = strict & same

    def origin(g):
        if tl >= GROUP:
            return g // groups_b, (g % groups_b) * GROUP
        return g * seqs_g, 0

    def load(c0, g):
        b0, t0 = origin(g)
        return qkv_ref[b0:b0 + seqs_g, t0:t0 + rows_b, c0:c0 + HEAD_DIM].astype(F32).reshape(GROUP, HEAD_DIM)

    def gate_rows(g):
        if n_gate_refs > 1:
            b0, t0 = origin(g)
            return tuple(gsc_refs[b0][q, :, t0:t0 + GROUP] for q in range(N_GATE_ROWS))
        return tuple(gsc_refs[0][q, :, g * GROUP:(g + 1) * GROUP] for q in range(N_GATE_ROWS))

    gates = [gate_rows(g) for g in range(n_groups)]

    probs = [(g, h) for g in range(n_groups) for h in range(N_HEADS)]
    st = []
    for g, h in probs:
        beta8, g8, eg8, kds8 = gates[g]
        d = {"q": load(h * HEAD_DIM, g), "k": load(QK_W + h * HEAD_DIM, g), "v": load(2 * QK_W + h * HEAD_DIM, g)}
        d["beta_row"] = beta8[h:h + 1, :]
        d["eg_row"] = eg8[N_HEADS + h:N_HEADS + h + 1, :]
        d["g_row"] = g8[N_HEADS + h:N_HEADS + h + 1, :]
        d["kb_row"] = kds8[N_HEADS + h:N_HEADS + h + 1, :] * d["beta_row"]
        d["g_col"] = jnp.broadcast_to(d["g_row"], (GROUP, GROUP)).T
        d["kt"] = d["k"].T
        st.append(d)
    for d in st:
        both = _dot(jnp.concatenate([d["k"], d["q"]], axis=0), d["kt"])
        d["kk"], d["qk"] = both[0:GROUP], both[GROUP:]
    for d in st:
        decay = jnp.exp(jnp.where(causal, d["g_col"] - d["g_row"], -jnp.inf))
        decay_beta = decay * d["beta_row"]
        d["m"] = jnp.where(strict, d.pop("kk") * decay_beta, 0.0)
        d["qkm"] = d.pop("qk") * decay_beta
        d["q_dec"] = d.pop("q") * jnp.exp(d["g_col"])
        d["kt_dec"] = d.pop("kt") * d["kb_row"]

    keep = keep_ref[...] != 0

    def pool_block(g, gi, issued_with):
        b0, t0 = origin(g)
        win = POOL_WINDOWS[gi]
        cols = slice(gi * POOL_GROUP, (gi + 1) * POOL_GROUP)
        pos = pos0 + tile * tl + t0 + row % rows_b
        slab = pext_ref[b0:b0 + seqs_g, POOL_PAD - POOL_LOOKBACK + t0:POOL_PAD + t0 + rows_b, cols]
        acc = slab.reshape(seqs_g * (POOL_LOOKBACK + rows_b), POOL_GROUP)
        shift = 1
        while shift < win:
            acc = acc + pltpu.roll(acc, shift, axis=0)
            shift *= 2
        acc = acc.reshape(seqs_g, POOL_LOOKBACK + rows_b, POOL_GROUP)[:, POOL_LOOKBACK:, :]
        cur = slab[:, POOL_LOOKBACK:, :]
        pooled = (acc / jnp.minimum(pos + 1, win).astype(F32).reshape(seqs_g, rows_b, POOL_GROUP) - cur)
        mixed = _dot(pooled.reshape(GROUP, POOL_GROUP), wmix_ref[gi]) * pscale_ref[:, cols]
        if issued_with is not None:
            reps = GROUP // issued_with.shape[0]
            mixed = jnp.where(keep, mixed, jnp.concatenate([issued_with] * reps, axis=0))
        oab_ref[b0:b0 + seqs_g, t0:t0 + rows_b, QK_W + gi * POOL_GROUP:QK_W + (gi + 1) * POOL_GROUP] = (
            mixed.reshape(seqs_g, rows_b, POOL_GROUP).astype(oab_ref.dtype))

    pool_jobs = [functools.partial(pool_block, g, gi) for g in range(n_groups) for gi in range(len(POOL_WINDOWS))]
    invs = _unit_lower_inverses([d.pop("m") for d in st], chunk, pool_jobs)
    for d, inv in zip(st, invs):
        d["uy"] = _dot(inv, d.pop("v"))
        d["wy"] = _dot(inv * d["eg_row"], d.pop("k"))

    def group_slab(ref, g, cols, row_off=0):
        b0, t0 = origin(g)
        return ref.at[b0:b0 + seqs_g, row_off + t0:row_off + t0 + rows_b, cols]

    def head_out(g, h, o):
        cols = slice(h * HEAD_DIM, (h + 1) * HEAD_DIM)
        z = group_slab(zp_ref, g, cols)[...].reshape(GROUP, HEAD_DIM)
        group_slab(oab_ref, g, cols)[...] = (o * _rms_scale(o) * wonorm_ref[...] * _silu(z)).reshape(
            seqs_g, rows_b, HEAD_DIM).astype(oab_ref.dtype)

    if chained:
        for j in range(groups_b):
            wave = [(i, g, h) for i, (g, h) in enumerate(probs) if g % groups_b == j]
            s_old = {i: snew_ref[origin(g)[0], h] for i, g, h in wave}
            ws = {i: _dot(jnp.concatenate([st[i]["wy"], st[i]["q_dec"]], axis=0), s_old[i]) for i, g, h in wave}
            ys = {i: st[i]["uy"] - ws[i][0:GROUP] for i, g, h in wave}
            outs = {i: ws[i][GROUP:] + _dot(st[i]["qkm"], ys[i]) for i, g, h in wave}
            for i, g, h in wave:
                last = jnp.exp(st[i]["g_col"][GROUP - 1:GROUP, :])
                snew_ref[origin(g)[0], h] = s_old[i] * last + _dot(st[i]["kt_dec"], ys[i])
            for i, g, h in wave:
                head_out(g, h, outs[i])
    else:
        outs = {}
        for i, (g, h) in enumerate(probs):
            d, b0 = st[i], origin(g)[0]
            ws_w, ws_q = [], []
            for s_i in range(seqs_g):
                r = slice(s_i * rows_b, (s_i + 1) * rows_b)
                ws = _dot(jnp.concatenate([d["wy"][r], d["q_dec"][r]], axis=0), snew_ref[b0 + s_i, h])
                ws_w.append(ws[0:rows_b])
                ws_q.append(ws[rows_b:])
            d["y"] = d["uy"] - jnp.concatenate(ws_w, axis=0)
            outs[i] = jnp.concatenate(ws_q, axis=0) + _dot(d["qkm"], d["y"])
        for i, (g, h) in enumerate(probs):
            d, b0 = st[i], origin(g)[0]
            for s_i in range(seqs_g):
                last = jnp.exp(d["g_col"][(s_i + 1) * rows_b - 1:(s_i + 1) * rows_b, :])
                upd = _dot(jnp.where(col // rows_b == s_i, d["kt_dec"], 0.0), d["y"])
                snew_ref[b0 + s_i, h] = snew_ref[b0 + s_i, h] * last + upd
        for i, (g, h) in enumerate(probs):
            head_out(g, h, outs[i])

    if n_tiles > 1:
        _history_to_scratch(pext_ref, pool_hist.start, poolnew_ref, rows_first)


def _delta(qkv, rest, gsc, pool_prev, s0, w_onorm, w_mix, pool_scale, *, bb, tl, chunk, pos0, act_dtype,
           rows_first, cast_to_bf16=()):
    bsz, l = qkv.shape[0], qkv.shape[1]
    assert bsz % bb == 0 and l % tl == 0 and (bb * tl) % GROUP == 0, "blocks must tile the batch, length and groups"
    n_tiles = l // tl
    n_steps = (bsz // bb) * n_tiles
    cast_specs = []
    for arr, axis in cast_to_bf16:
        tile = LANES if axis == 1 else 2 * SUBLANES
        assert arr.shape[axis] % (n_steps * tile) == 0, "each step converts a tile-aligned slice"
        blk = tuple(d // n_steps if a == axis else d for a, d in enumerate(arr.shape))
        cast_specs.append(pl.BlockSpec(blk, lambda i, j, axis=axis: tuple(
            i * n_tiles + j if a == axis else 0 for a in range(2))))
    state_s = pl.BlockSpec((bb, N_HEADS, HEAD_DIM, HEAD_DIM), lambda i, j: (i, 0, 0, 0))
    if n_tiles == 1:
        gate_specs = [_gate_rows_spec(bb, tl, 1)]
    else:
        gate_specs = [pl.BlockSpec((N_GATE_ROWS, SUBLANES, tl), lambda i, j, k=k: (0, 0, (i * bb + k) * n_tiles + j))
                      for k in range(bb)]
    kern = functools.partial(_delta_kernel, bb=bb, tl=tl, chunk=chunk, pos0=pos0, n_tiles=n_tiles,
                             n_gate_refs=len(gate_specs), n_casts=len(cast_specs), rows_first=rows_first)
    sds = lambda *shape: jax.ShapeDtypeStruct(shape, F32)
    return pl.pallas_call(
        kern,
        grid=(bsz // bb, n_tiles),
        in_specs=gate_specs + [
                  _seq_spec(bb, tl, QKV_W),
                  pl.BlockSpec((bb, tl, QK_W + WIDTH_B), lambda i, j: (i, j, REST_Z0 // (QK_W + WIDTH_B))),
                  _state_spec(bb, POOL_HIST, WIDTH_B, rows_first), state_s,
                  _const_spec((1, HEAD_DIM)), _const_spec((len(POOL_WINDOWS), POOL_GROUP, POOL_GROUP)),
                  _const_spec((1, WIDTH_B)), _const_spec((1, LANES))] + cast_specs,
        out_specs=[_seq_spec(bb, tl, QK_W + WIDTH_B),
                   _state_spec(bb, POOL_HIST, WIDTH_B, rows_first), state_s] + cast_specs,
        out_shape=[jax.ShapeDtypeStruct((bsz, l, QK_W + WIDTH_B), act_dtype),
                   _state_shape(bsz, POOL_HIST, WIDTH_B, rows_first),
                   sds(bsz, N_HEADS, HEAD_DIM, HEAD_DIM)]
                  + [jax.ShapeDtypeStruct(arr.shape, BF16) for arr, _ in cast_to_bf16],
        scratch_shapes=[pltpu.VMEM((bb, POOL_PAD + tl, WIDTH_B), F32)],
        compiler_params=pltpu.CompilerParams(dimension_semantics=("arbitrary", "arbitrary"),
                                             vmem_limit_bytes=VMEM_LIMIT),
        name="delta",
    )(*([gsc] * len(gate_specs)), qkv, rest, pool_prev, s0, w_onorm, w_mix, pool_scale,
      jnp.ones((1, LANES), jnp.int32), *[arr for arr, _ in cast_to_bf16])


def _mlp_kernel(x_ref, oab_ref, gate_ref, wa_ref, wb_ref, wo_ref, gmlp_ref, wup_ref, wdown_ref, gfin_ref, y_ref):
    ma = _dot(oab_ref[:, 0:QK_W], wa_ref[...])
    mb = _dot(oab_ref[:, QK_W:QK_W + WIDTH_B], wb_ref[...])
    merged = _sigmoid(gate_ref[:, 0:D_MODEL]) * ma + _sigmoid(gate_ref[:, D_MODEL:]) * mb
    x1 = x_ref[...] + _dot(merged, wo_ref[...])
    h2 = (x1 * _rms_scale(x1) * gmlp_ref[...]).astype(BF16)
    acc = x1
    for c0 in range(0, D_FF, FF_BLOCK):
        up = jnp.dot(h2, wup_ref[:, c0:c0 + FF_BLOCK], preferred_element_type=F32)
        act = jnp.square(jnp.maximum(up, 0.0))
        acc = acc + _dot(act, wdown_ref[c0:c0 + FF_BLOCK, :])
    y_ref[...] = acc * _rms_scale(acc) * gfin_ref[...]


def _merge_mlp(x2d, oab, rest, prm, tm):
    t = x2d.shape[0]
    assert t % tm == 0, "row tiles must cover the tokens exactly"
    row = lambda w: pl.BlockSpec((tm, w), lambda i: (i, 0))
    return pl.pallas_call(
        _mlp_kernel,
        grid=(t // tm,),
        in_specs=[row(D_MODEL), row(QK_W + WIDTH_B), row(2 * D_MODEL),
                  _const_spec((QK_W, D_MODEL)), _const_spec((WIDTH_B, D_MODEL)), _const_spec((D_MODEL, D_MODEL)),
                  _const_spec((1, D_MODEL)), _const_spec((D_MODEL, D_FF)), _const_spec((D_FF, D_MODEL)),
                  _const_spec((1, D_MODEL))],
        out_specs=row(D_MODEL),
        out_shape=jax.ShapeDtypeStruct((t, D_MODEL), F32),
        compiler_params=pltpu.CompilerParams(dimension_semantics=("arbitrary",),
                                             vmem_limit_bytes=VMEM_LIMIT),
        name="merge_mlp",
    )(x2d, oab, rest, prm["w_a_out"], prm["w_b_out"], prm["w_o"], prm["g_mlp"], prm["w_up"], prm["w_down"],
      prm["g_final"])


def _mix(x, conv_prev, pool_prev, s_prev, pos0, prm, *, front_blk, delta_blk, chunk, act_dtype, rows_first,
         cast_to_bf16=()):
    bsz, l, _ = x.shape
    t = bsz * l
    qkv, rest, gsc, conv_new, *w_proj_bf16 = _front(
        x, prm["g_attn"], prm["w_proj"], prm["w_conv"], prm["a_log"], prm["dt_bias"],
        conv_prev, bb=front_blk[0], tl=front_blk[1], chunk=chunk, act_dtype=act_dtype, rows_first=rows_first)
    oab, pool_new, s_new, *casted = _delta(
        qkv, rest, gsc, pool_prev, s_prev, prm["w_onorm"], prm["w_mix"], prm["pool_scale"], bb=delta_blk[0],
        tl=delta_blk[1], chunk=chunk, pos0=pos0, act_dtype=act_dtype, rows_first=rows_first,
        cast_to_bf16=cast_to_bf16)
    rows = (x.reshape(t, D_MODEL), oab.reshape(t, QK_W + WIDTH_B), rest.reshape(t, REST_OUT_W))
    return rows, (conv_new, pool_new, s_new[None]), w_proj_bf16 + casted


def kernel(x_prompt, x_sample, state_conv, state_pool, state_ssm, w_in, w_conv, a_log, dt_bias, w_onorm,
           w_pool_mix, pool_scale, w_a_out, w_b_out, w_o, g_attn, g_mlp, w_up, w_down, g_final):
    assert w_in.shape[0] == 1, "single-layer decoder"
    prm = {
        "w_proj": jnp.transpose(w_in[0]).astype(F32),
        "g_attn": g_attn[0][None, :], "g_mlp": g_mlp[0][None, :], "g_final": g_final[None, :],
        "w_conv": w_conv[0].astype(F32),
        "a_log": a_log[0], "dt_bias": dt_bias[0],
        "w_onorm": w_onorm[0][None, :].astype(F32),
        "w_mix": w_pool_mix[0].astype(BF16), "pool_scale": pool_scale[0][None, :].astype(F32),
    }
    bp = x_prompt.shape[0]
    converted = ("w_proj", "w_up", "w_down", "w_o", "w_a_out", "w_b_out")
    rows_p, (conv_p, pool_p, ssm_p), casted = _mix(
        x_prompt, jnp.zeros((bp, CONV_W - 1, QKV_W), F32), jnp.zeros((bp, POOL_HIST, WIDTH_B), F32),
        jnp.zeros((bp, N_HEADS, HEAD_DIM, HEAD_DIM), F32), 0, prm,
        front_blk=(1, 512), delta_blk=(4, 256), chunk=GROUP, act_dtype=BF16, rows_first=False,
        cast_to_bf16=((w_up[0].astype(F32), 1), (w_down[0].astype(F32), 0), (w_o[0].astype(F32), 0),
                      (w_a_out[0].astype(F32), 0), (w_b_out[0].astype(F32), 0)))
    prm.update(zip(converted, casted, strict=True))
    conv_p, pool_p = conv_p[None], pool_p[None]
    dec_len = x_sample.shape[1]
    rows_major = lambda s: jnp.transpose(s[0].astype(F32), (1, 0, 2))
    rows_s, (conv_s, pool_s, ssm_s), _ = _mix(
        x_sample, rows_major(state_conv), rows_major(state_pool), state_ssm[0].astype(F32), PAST_LEN, prm,
        front_blk=(256 // dec_len, dec_len), delta_blk=(2 * GROUP // dec_len, dec_len),
        chunk=dec_len, act_dtype=F32, rows_first=True)
    conv_s, pool_s = jnp.transpose(conv_s, (1, 0, 2))[None], jnp.transpose(pool_s, (1, 0, 2))[None]
    y_p = _merge_mlp(*rows_p, prm, tm=512).reshape(x_prompt.shape)
    y_s = _merge_mlp(*rows_s, prm, tm=256).reshape(x_sample.shape)
    return (y_p, y_s, conv_p.astype(state_conv.dtype), pool_p.astype(state_pool.dtype),
            ssm_p.astype(state_ssm.dtype), conv_s.astype(state_conv.dtype), pool_s.astype(state_pool.dtype),
            ssm_s.astype(state_ssm.dtype))
```

```python
import functools
import math

import jax
import jax.numpy as jnp
from jax import lax
from jax.experimental import pallas as pl
from jax.experimental.pallas import tpu as pltpu

D_MODEL = 1024
N_HEADS = 4
HEAD_DIM = 128
QK_W = N_HEADS * HEAD_DIM
QKV_W = 3 * QK_W
CONV_W = 4
POOL_WINDOWS = (2, 4, 8, 16)
POOL_GROUP = 128
WIDTH_B = len(POOL_WINDOWS) * POOL_GROUP
POOL_HIST = 15
D_FF = 4 * D_MODEL
EPS = 1e-6
PAST_LEN = 16384
LANES = 128
SUBLANES = 8
MXU_COLS = 256
N_GATE_ROWS = 4

QKVZ_W = QKV_W + QK_W
GATE_SCALARS = 2 * N_HEADS
REST_W = WIDTH_B + 2 * D_MODEL
W_GATE0 = QKVZ_W
WT_COLS = QKVZ_W + REST_W
WT_ROWS = D_MODEL + 2 * SUBLANES
WT_CHUNK = 512
REST_Z0 = 2 * D_MODEL
REST_P0 = REST_Z0 + QK_W
REST_OUT_W = REST_P0 + WIDTH_B

GROUP = 128
SERIES_BLOCK = 64
CONV_PAD = 8
POOL_LOOKBACK = 16
POOL_PAD = 24
FF_BLOCK = 1024

VMEM_LIMIT = 58 * 1024 * 1024
X_SLOTS = 3

BF16 = jnp.bfloat16
F32 = jnp.float32


def _dot(a, b):
    return jnp.dot(a.astype(BF16), b.astype(BF16), preferred_element_type=F32)


def _sigmoid(x):
    return 1.0 / (1.0 + jnp.exp(-x))


def _silu_of_twice(half):
    return half * jnp.tanh(half) + half


def _silu(x):
    return _silu_of_twice(0.5 * x)


def _rms_scale(x):
    return lax.rsqrt(jnp.mean(x * x, axis=-1, keepdims=True) + EPS)


def _const_spec(shape):
    zeros = (0,) * len(shape)
    return pl.BlockSpec(shape, lambda *_: zeros, pipeline_mode=pl.Buffered(1))


def _seq_spec(bb, tl, width):
    return pl.BlockSpec((bb, tl, width), lambda i, j: (i, j, 0))


def _state_spec(bb, n_rows, width, rows_first):
    if rows_first:
        return pl.BlockSpec((n_rows, bb, width), lambda i, j: (0, i, 0))
    return pl.BlockSpec((bb, n_rows, width), lambda i, j: (i, 0, 0))


def _state_shape(bsz, n_rows, width, rows_first):
    return jax.ShapeDtypeStruct((n_rows, bsz, width) if rows_first else (bsz, n_rows, width), F32)


def _history_to_scratch(scratch_ref, first_row, state_ref, rows_first):
    if rows_first:
        for j in range(state_ref.shape[0]):
            scratch_ref[:, first_row + j, :] = state_ref[j]
    else:
        scratch_ref[:, first_row:first_row + state_ref.shape[1], :] = state_ref[...]


def _history_from_scratch(state_ref, scratch_ref, first_row, rows_first):
    if rows_first:
        for j in range(state_ref.shape[0]):
            state_ref[j] = scratch_ref[:, first_row + j, :]
    else:
        state_ref[...] = scratch_ref[:, first_row:first_row + state_ref.shape[1], :]


def _gate_rows_spec(bb, tl, n_tiles):
    return pl.BlockSpec((N_GATE_ROWS, SUBLANES, bb * tl), lambda i, j: (0, 0, i * n_tiles + j))


def _lane_prefix_sum(x, chunk):
    lane = lax.broadcasted_iota(jnp.int32, x.shape, 1)
    shift = 1
    while shift < chunk:
        x = x + jnp.where(lane % chunk >= shift, pltpu.roll(x, shift, axis=1), 0.0)
        shift *= 2
    return x


def _lane_suffix_sum(x, chunk):
    lane = lax.broadcasted_iota(jnp.int32, x.shape, 1)
    shift = 1
    while shift < chunk:
        x = x + jnp.where(lane % chunk + shift < chunk, pltpu.roll(x, x.shape[1] - shift, axis=1), 0.0)
        shift *= 2
    return x


def _front_kernel(x_ref, g_ref, w_ref, wconv_ref, decay_ref, keep_ref, convprev_ref,
                  qkv_ref, rest_ref, gsc_ref, convnew_ref,
                  *tail, bb, tl, n_tiles, chunk, rows_first, convert_weights):
    tile = pl.program_id(1)
    rows = bb * tl
    hist = slice(CONV_PAD - (CONV_W - 1), CONV_PAD)
    ext_ref, xbuf_ref, xsem_ref = tail[-3:]

    n_steps = pl.num_programs(0) * n_tiles
    step = pl.program_id(0) * n_tiles + tile

    def x_copy(s):
        slot = s % X_SLOTS
        block = x_ref.at[pl.ds((s // n_tiles) * bb, bb), pl.ds((s % n_tiles) * tl, tl)]
        return pltpu.make_async_copy(block, xbuf_ref.at[slot], xsem_ref.at[slot])

    @pl.when(step == 0)
    def _():
        for s in range(X_SLOTS - 1):
            x_copy(s).start()

    @pl.when(step + (X_SLOTS - 1) < n_steps)
    def _():
        x_copy(step + (X_SLOTS - 1)).start()

    if convert_weights:
        w_src_ref, w_ref = w_ref, tail[0]
        src_rest0 = W_GATE0 + GATE_SCALARS

        @pl.when((pl.program_id(0) == 0) & (tile == 0))
        def _():
            for src0, dst0, n in ((0, 0, QKVZ_W), (src_rest0, QKVZ_W, REST_W)):
                for c in range(0, n, WT_CHUNK):
                    w_ref[0:D_MODEL, dst0 + c:dst0 + c + WT_CHUNK] = (
                        w_src_ref[src0 + c:src0 + c + WT_CHUNK, :].T.astype(BF16))
            gate_rows = jnp.concatenate([w_src_ref[W_GATE0:src_rest0, :],
                                         jnp.zeros((WT_ROWS - D_MODEL - GATE_SCALARS, D_MODEL), F32)], axis=0)
            w_ref[D_MODEL:WT_ROWS, 0:D_MODEL] = gate_rows.astype(BF16)
            w_ref[D_MODEL:WT_ROWS, D_MODEL:WT_COLS] = jnp.zeros((WT_ROWS - D_MODEL, WT_COLS - D_MODEL), BF16)

    @pl.when(tile == 0)
    def _():
        _history_to_scratch(ext_ref, hist.start, convprev_ref, rows_first)
        ext_ref[:, 0:CONV_PAD - (CONV_W - 1), :] = jnp.zeros((bb, CONV_PAD - (CONV_W - 1), QKV_W), F32)

    x_copy(step).wait()
    x = xbuf_ref[step % X_SLOTS].reshape(rows, D_MODEL)
    normed = (x * _rms_scale(x) * g_ref[...]).astype(BF16)
    proj = lambda c0, n: jnp.dot(normed, w_ref[0:D_MODEL, c0:c0 + n], preferred_element_type=F32)
    ext_ref[:, CONV_PAD:CONV_PAD + tl, :] = proj(0, QKV_W).reshape(bb, tl, QKV_W)
    _history_from_scratch(convnew_ref, ext_ref, CONV_PAD + tl - (CONV_W - 1), rows_first)

    x8 = lax.dot_general(w_ref[D_MODEL:WT_ROWS, 0:D_MODEL], normed, (((1,), (1,)), ((), ())),
                         preferred_element_type=F32)[0:SUBLANES, :]
    xs = x8 + decay_ref[:, 1:2]
    softplus = jnp.maximum(xs, 0.0) + jnp.log1p(jnp.exp(-jnp.abs(xs)))
    graw8 = -jnp.exp(decay_ref[:, 0:1]) * softplus
    g8 = _lane_prefix_sum(graw8, chunk)
    gsc_ref[0] = _sigmoid(x8)
    gsc_ref[1] = g8
    gsc_ref[2] = jnp.exp(g8)
    gsc_ref[3] = jnp.exp(_lane_suffix_sum(graw8, chunk) - graw8)

    keep = keep_ref[...] != 0

    def conv_block(c0, dep):
        cols = slice(c0, c0 + LANES)
        xe = ext_ref[:, :, cols].reshape(bb * (CONV_PAD + tl), LANES)
        x1 = pltpu.roll(xe, 1, axis=0)
        tap = lambda j: 0.5 * wconv_ref[j:j + 1, cols]
        acc = (tap(3) * xe + tap(2) * x1) + pltpu.roll(tap(1) * xe + tap(0) * x1, 2, axis=0)
        val = _silu_of_twice(acc.reshape(bb, CONV_PAD + tl, LANES)[:, CONV_PAD:, :])
        if c0 < 2 * QK_W:
            scale = HEAD_DIM ** -0.5 if c0 < QK_W else 1.0
            val = val * (lax.rsqrt(jnp.sum(val * val, axis=-1, keepdims=True) + EPS) * scale)
        qkv_ref[:, :, cols] = jnp.where(keep, val, dep[:, 0:LANES].reshape(bb, tl, LANES)).astype(qkv_ref.dtype)

    def proj_block(out0, w0, c0):
        val = proj(w0 + c0, MXU_COLS)
        rest_ref[:, :, out0 + c0:out0 + c0 + MXU_COLS] = val.reshape(bb, tl, MXU_COLS)
        return val

    mxu_work = ([functools.partial(proj_block, REST_Z0, QKV_W, c0) for c0 in range(0, QK_W, MXU_COLS)]
                + [functools.partial(proj_block, REST_P0, QKVZ_W, c0) for c0 in range(0, WIDTH_B, MXU_COLS)]
                + [functools.partial(proj_block, 0, QKVZ_W + WIDTH_B, c0)
                   for c0 in range(0, 2 * D_MODEL, MXU_COLS)])
    for i, c0 in enumerate(range(0, QKV_W, LANES)):
        conv_block(c0, mxu_work[i]())

    if n_tiles > 1:
        _history_to_scratch(ext_ref, hist.start, convnew_ref, rows_first)


def _front(x, g_attn, w_proj, w_conv, a_log, dt_bias, conv_prev, *, bb, tl, chunk, act_dtype, rows_first):
    convert_weights = w_proj.dtype != BF16
    assert w_proj.shape == ((QKVZ_W + GATE_SCALARS + REST_W, D_MODEL) if convert_weights else (WT_ROWS, WT_COLS))
    bsz, l, _ = x.shape
    assert bsz % bb == 0 and l % tl == 0 and (bb * tl) % chunk == 0, "blocks must tile the batch, length and chunks"
    assert bb == 1 or tl == l, "a block of several sequences must hold them whole (gate rows are token-major)"
    assert CONV_W == 4, "the conv is written out as two pairs of taps"
    n_tiles = l // tl
    assert (bsz // bb) * n_tiles >= X_SLOTS - 1, "the first step starts the copies of the first X_SLOTS - 1 steps"
    kern = functools.partial(_front_kernel, bb=bb, tl=tl, n_tiles=n_tiles, chunk=chunk, rows_first=rows_first,
                             convert_weights=convert_weights)
    sds = lambda *shape: jax.ShapeDtypeStruct(shape, F32)
    w_out_specs = [_const_spec((WT_ROWS, WT_COLS))] if convert_weights else []
    w_out_shapes = [jax.ShapeDtypeStruct((WT_ROWS, WT_COLS), BF16)] if convert_weights else []
    decay_prm = jnp.pad(jnp.stack([a_log, dt_bias], axis=1).astype(F32), ((N_HEADS, 0), (0, 0)))
    return pl.pallas_call(
        kern,
        grid=(bsz // bb, n_tiles),
        in_specs=[pl.BlockSpec(memory_space=pl.ANY), _const_spec((1, D_MODEL)), _const_spec(w_proj.shape),
                  _const_spec((CONV_W, QKV_W)), _const_spec((SUBLANES, 2)), _const_spec((1, LANES)),
                  _state_spec(bb, CONV_W - 1, QKV_W, rows_first)],
        out_specs=[_seq_spec(bb, tl, QKV_W), _seq_spec(bb, tl, REST_OUT_W), _gate_rows_spec(bb, tl, n_tiles),
                   _state_spec(bb, CONV_W - 1, QKV_W, rows_first)] + w_out_specs,
        out_shape=[jax.ShapeDtypeStruct((bsz, l, QKV_W), act_dtype), sds(bsz, l, REST_OUT_W),
                   sds(N_GATE_ROWS, SUBLANES, bsz * l),
                   _state_shape(bsz, CONV_W - 1, QKV_W, rows_first)] + w_out_shapes,
        scratch_shapes=[pltpu.VMEM((bb, CONV_PAD + tl, QKV_W), F32), pltpu.VMEM((X_SLOTS, bb, tl, D_MODEL), F32),
                        pltpu.SemaphoreType.DMA((X_SLOTS,))],
        compiler_params=pltpu.CompilerParams(dimension_semantics=("arbitrary", "arbitrary"),
                                             vmem_limit_bytes=VMEM_LIMIT),
        name="front",
    )(x, g_attn, w_proj, w_conv, decay_prm, jnp.ones((1, LANES), jnp.int32), conv_prev)


def _unit_lower_inverses(ms, chunk, side_jobs=()):
    nb = min(chunk, SERIES_BLOCK)
    n_blocks = GROUP // nb
    n_factors = int(math.log2(nb))
    assert chunk == nb or (chunk == 2 * nb and n_blocks == 2), "chunks are one or two series blocks"
    lane = lax.broadcasted_iota(jnp.int32, (nb, GROUP), 1)
    lane_block = lane // nb

    def packed(m):
        out = m[0:nb]
        for b in range(1, n_blocks):
            out = jnp.where(lane_block == b, m[b * nb:(b + 1) * nb], out)
        return out

    def block_diag(p):
        return jnp.concatenate([jnp.where(lane_block == b, p, 0.0) for b in range(n_blocks)], axis=0)

    side_jobs = list(side_jobs)
    per_step = -(-len(side_jobs) // max(n_factors - 1, 1))

    def issue_side_jobs(step_values):
        for k in range(min(per_step, len(side_jobs))):
            side_jobs.pop(0)(step_values[k % len(step_values)])

    nps = [packed(m) for m in ms]
    eye_p = jnp.where(lane % nb == lax.broadcasted_iota(jnp.int32, (nb, GROUP), 0), 1.0, 0.0).astype(F32)
    invs = [eye_p - n for n in nps]
    if n_factors > 1:
        pws = [_dot(n, block_diag(n)) for n in nps]
        issue_side_jobs(pws)
        for _ in range(n_factors - 2):
            boths = [_dot(jnp.concatenate([inv, pw], axis=0), block_diag(pw)) for inv, pw in zip(invs, pws)]
            invs = [inv + both[0:nb] for inv, both in zip(invs, boths)]
            pws = [both[nb:] for both in boths]
            issue_side_jobs(pws)
        invs = [inv + _dot(inv, block_diag(pw)) for inv, pw in zip(invs, pws)]
    while side_jobs:
        side_jobs.pop(0)(None)
    if chunk == nb:
        return [block_diag(inv) for inv in invs]
    zeros = jnp.zeros((nb, GROUP), F32)
    a_invs = [jnp.where(lane < nb, inv, 0.0) for inv in invs]
    lows = [jnp.where(lane < nb, m[nb:], 0.0) for m in ms]
    xs = [_dot(low, jnp.concatenate([a_inv, zeros], axis=0)) for low, a_inv in zip(lows, a_invs)]
    ys = [_dot(inv, jnp.concatenate([zeros, x], axis=0)) for inv, x in zip(invs, xs)]
    return [jnp.concatenate([a_inv, jnp.where(lane >= nb, inv, 0.0) - y], axis=0)
            for a_inv, y, inv in zip(a_invs, ys, invs)]


def _delta_kernel(*refs, bb, tl, chunk, pos0, n_tiles, n_gate_refs, n_casts, rows_first):
    gsc_refs, refs = refs[:n_gate_refs], refs[n_gate_refs:]
    (qkv_ref, zp_ref, poolprev_ref, s0_ref, wonorm_ref, wmix_ref, pscale_ref, keep_ref) = refs[0:8]
    cast_src, refs = refs[8:8 + n_casts], refs[8 + n_casts:]
    oab_ref, poolnew_ref, snew_ref = refs[0:3]
    cast_dst, pext_ref = refs[3:3 + n_casts], refs[3 + n_casts]
    tile = pl.program_id(1)

    for src, dst in zip(cast_src, cast_dst):
        dst[...] = src[...].astype(dst.dtype)

    rows_b = min(tl, GROUP)
    seqs_g = GROUP // rows_b
    groups_b = tl // rows_b
    n_groups = bb * tl // GROUP
    chained = chunk == GROUP
    assert chained or (chunk == tl and tl < GROUP), "chunk must be a whole group or a whole short sequence"
    pool_hist = slice(POOL_PAD - POOL_HIST, POOL_PAD)

    @pl.when(tile == 0)
    def _():
        snew_ref[...] = s0_ref[...]
        _history_to_scratch(pext_ref, pool_hist.start, poolprev_ref, rows_first)
        pext_ref[:, POOL_PAD - POOL_LOOKBACK:POOL_PAD - POOL_HIST, :] = jnp.zeros((bb, 1, WIDTH_B), F32)

    pext_ref[:, POOL_PAD:POOL_PAD + tl, :] = zp_ref[:, :, QK_W:QK_W + WIDTH_B]
    _history_from_scratch(poolnew_ref, pext_ref, POOL_PAD + tl - POOL_HIST, rows_first)

    row = lax.broadcasted_iota(jnp.int32, (GROUP, GROUP), 0)
    col = lax.broadcasted_iota(jnp.int32, (GROUP, GROUP), 1)
    causal = row >= col
    strict = row > col
    if not chained:
        same = (row // chunk) == (col // chunk)
        causal = causal & same
        strict = strict & same

    def origin(g):
        if tl >= GROUP:
            return g // groups_b, (g % groups_b) * GROUP
        return g * seqs_g, 0

    def load(c0, g):
        b0, t0 = origin(g)
        return qkv_ref[b0:b0 + seqs_g, t0:t0 + rows_b, c0:c0 + HEAD_DIM].astype(F32).reshape(GROUP, HEAD_DIM)

    def gate_rows(g):
        if n_gate_refs > 1:
            b0, t0 = origin(g)
            return tuple(gsc_refs[b0][q, :, t0:t0 + GROUP] for q in range(N_GATE_ROWS))
        return tuple(gsc_refs[0][q, :, g * GROUP:(g + 1) * GROUP] for q in range(N_GATE_ROWS))

    gates = [gate_rows(g) for g in range(n_groups)]

    probs = [(g, h) for g in range(n_groups) for h in range(N_HEADS)]
    st = []
    for g, h in probs:
        beta8, g8, eg8, kds8 = gates[g]
        d = {"q": load(h * HEAD_DIM, g), "k": load(QK_W + h * HEAD_DIM, g), "v": load(2 * QK_W + h * HEAD_DIM, g)}
        d["beta_row"] = beta8[h:h + 1, :]
        d["eg_row"] = eg8[N_HEADS + h:N_HEADS + h + 1, :]
        d["g_row"] = g8[N_HEADS + h:N_HEADS + h + 1, :]
        d["kb_row"] = kds8[N_HEADS + h:N_HEADS + h + 1, :] * d["beta_row"]
        d["g_col"] = jnp.broadcast_to(d["g_row"], (GROUP, GROUP)).T
        d["kt"] = d["k"].T
        st.append(d)
    for d in st:
        both = _dot(jnp.concatenate([d["k"], d["q"]], axis=0), d["kt"])
        d["kk"], d["qk"] = both[0:GROUP], both[GROUP:]
    for d in st:
        decay = jnp.exp(jnp.where(causal, d["g_col"] - d["g_row"], -jnp.inf))
        decay_beta = decay * d["beta_row"]
        d["m"] = jnp.where(strict, d.pop("kk") * decay_beta, 0.0)
        d["qkm"] = d.pop("qk") * decay_beta
        d["q_dec"] = d.pop("q") * jnp.exp(d["g_col"])
        d["kt_dec"] = d.pop("kt") * d["kb_row"]

    keep = keep_ref[...] != 0

    def pool_block(g, gi, issued_with):
        b0, t0 = origin(g)
        win = POOL_WINDOWS[gi]
        cols = slice(gi * POOL_GROUP, (gi + 1) * POOL_GROUP)
        pos = pos0 + tile * tl + t0 + row % rows_b
        slab = pext_ref[b0:b0 + seqs_g, POOL_PAD - POOL_LOOKBACK + t0:POOL_PAD + t0 + rows_b, cols]
        acc = slab.reshape(seqs_g * (POOL_LOOKBACK + rows_b), POOL_GROUP)
        shift = 1
        while shift < win:
            acc = acc + pltpu.roll(acc, shift, axis=0)
            shift *= 2
        acc = acc.reshape(seqs_g, POOL_LOOKBACK + rows_b, POOL_GROUP)[:, POOL_LOOKBACK:, :]
        cur = slab[:, POOL_LOOKBACK:, :]
        pooled = (acc / jnp.minimum(pos + 1, win).astype(F32).reshape(seqs_g, rows_b, POOL_GROUP) - cur)
        mixed = _dot(pooled.reshape(GROUP, POOL_GROUP), wmix_ref[gi]) * pscale_ref[:, cols]
        if issued_with is not None:
            reps = GROUP // issued_with.shape[0]
            mixed = jnp.where(keep, mixed, jnp.concatenate([issued_with] * reps, axis=0))
        oab_ref[b0:b0 + seqs_g, t0:t0 + rows_b, QK_W + gi * POOL_GROUP:QK_W + (gi + 1) * POOL_GROUP] = (
            mixed.reshape(seqs_g, rows_b, POOL_GROUP).astype(oab_ref.dtype))

    pool_jobs = [functools.partial(pool_block, g, gi) for g in range(n_groups) for gi in range(len(POOL_WINDOWS))]
    invs = _unit_lower_inverses([d.pop("m") for d in st], chunk, pool_jobs)
    for d, inv in zip(st, invs):
        d["uy"] = _dot(inv, d.pop("v"))
        d["wy"] = _dot(inv * d["eg_row"], d.pop("k"))

    def group_slab(ref, g, cols, row_off=0):
        b0, t0 = origin(g)
        return ref.at[b0:b0 + seqs_g, row_off + t0:row_off + t0 + rows_b, cols]

    def head_out(g, h, o):
        cols = slice(h * HEAD_DIM, (h + 1) * HEAD_DIM)
        z = group_slab(zp_ref, g, cols)[...].reshape(GROUP, HEAD_DIM)
        group_slab(oab_ref, g, cols)[...] = (o * _rms_scale(o) * wonorm_ref[...] * _silu(z)).reshape(
            seqs_g, rows_b, HEAD_DIM).astype(oab_ref.dtype)

    if chained:
        for j in range(groups_b):
            wave = [(i, g, h) for i, (g, h) in enumerate(probs) if g % groups_b == j]
            s_old = {i: snew_ref[origin(g)[0], h] for i, g, h in wave}
            ws = {i: _dot(jnp.concatenate([st[i]["wy"], st[i]["q_dec"]], axis=0), s_old[i]) for i, g, h in wave}
            ys = {i: st[i]["uy"] - ws[i][0:GROUP] for i, g, h in wave}
            outs = {i: ws[i][GROUP:] + _dot(st[i]["qkm"], ys[i]) for i, g, h in wave}
            for i, g, h in wave:
                last = jnp.exp(st[i]["g_col"][GROUP - 1:GROUP, :])
                snew_ref[origin(g)[0], h] = s_old[i] * last + _dot(st[i]["kt_dec"], ys[i])
            for i, g, h in wave:
                head_out(g, h, outs[i])
    else:
        outs = {}
        for i, (g, h) in enumerate(probs):
            d, b0 = st[i], origin(g)[0]
            ws_w, ws_q = [], []
            for s_i in range(seqs_g):
                r = slice(s_i * rows_b, (s_i + 1) * rows_b)
                ws = _dot(jnp.concatenate([d["wy"][r], d["q_dec"][r]], axis=0), snew_ref[b0 + s_i, h])
                ws_w.append(ws[0:rows_b])
                ws_q.append(ws[rows_b:])
            d["y"] = d["uy"] - jnp.concatenate(ws_w, axis=0)
            outs[i] = jnp.concatenate(ws_q, axis=0) + _dot(d["qkm"], d["y"])
        for i, (g, h) in enumerate(probs):
            d, b0 = st[i], origin(g)[0]
            for s_i in range(seqs_g):
                last = jnp.exp(d["g_col"][(s_i + 1) * rows_b - 1:(s_i + 1) * rows_b, :])
                upd = _dot(jnp.where(col // rows_b == s_i, d["kt_dec"], 0.0), d["y"])
                snew_ref[b0 + s_i, h] = snew_ref[b0 + s_i, h] * last + upd
        for i, (g, h) in enumerate(probs):
            head_out(g, h, outs[i])

    if n_tiles > 1:
        _history_to_scratch(pext_ref, pool_hist.start, poolnew_ref, rows_first)


def _delta(qkv, rest, gsc, pool_prev, s0, w_onorm, w_mix, pool_scale, *, bb, tl, chunk, pos0, act_dtype,
           rows_first, cast_to_bf16=()):
    bsz, l = qkv.shape[0], qkv.shape[1]
    assert bsz % bb == 0 and l % tl == 0 and (bb * tl) % GROUP == 0, "blocks must tile the batch, length and groups"
    n_tiles = l // tl
    n_steps = (bsz // bb) * n_tiles
    cast_specs = []
    for arr, axis in cast_to_bf16:
        tile = LANES if axis == 1 else 2 * SUBLANES
        assert arr.shape[axis] % (n_steps * tile) == 0, "each step converts a tile-aligned slice"
        blk = tuple(d // n_steps if a == axis else d for a, d in enumerate(arr.shape))
        cast_specs.append(pl.BlockSpec(blk, lambda i, j, axis=axis: tuple(
            i * n_tiles + j if a == axis else 0 for a in range(2))))
    state_s = pl.BlockSpec((bb, N_HEADS, HEAD_DIM, HEAD_DIM), lambda i, j: (i, 0, 0, 0))
    if n_tiles == 1:
        gate_specs = [_gate_rows_spec(bb, tl, 1)]
    else:
        gate_specs = [pl.BlockSpec((N_GATE_ROWS, SUBLANES, tl), lambda i, j, k=k: (0, 0, (i * bb + k) * n_tiles + j))
                      for k in range(bb)]
    kern = functools.partial(_delta_kernel, bb=bb, tl=tl, chunk=chunk, pos0=pos0, n_tiles=n_tiles,
                             n_gate_refs=len(gate_specs), n_casts=len(cast_specs), rows_first=rows_first)
    sds = lambda *shape: jax.ShapeDtypeStruct(shape, F32)
    return pl.pallas_call(
        kern,
        grid=(bsz // bb, n_tiles),
        in_specs=gate_specs + [
                  _seq_spec(bb, tl, QKV_W),
                  pl.BlockSpec((bb, tl, QK_W + WIDTH_B), lambda i, j: (i, j, REST_Z0 // (QK_W + WIDTH_B))),
                  _state_spec(bb, POOL_HIST, WIDTH_B, rows_first), state_s,
                  _const_spec((1, HEAD_DIM)), _const_spec((len(POOL_WINDOWS), POOL_GROUP, POOL_GROUP)),
                  _const_spec((1, WIDTH_B)), _const_spec((1, LANES))] + cast_specs,
        out_specs=[_seq_spec(bb, tl, QK_W + WIDTH_B),
                   _state_spec(bb, POOL_HIST, WIDTH_B, rows_first), state_s] + cast_specs,
        out_shape=[jax.ShapeDtypeStruct((bsz, l, QK_W + WIDTH_B), act_dtype),
                   _state_shape(bsz, POOL_HIST, WIDTH_B, rows_first),
                   sds(bsz, N_HEADS, HEAD_DIM, HEAD_DIM)]
                  + [jax.ShapeDtypeStruct(arr.shape, BF16) for arr, _ in cast_to_bf16],
        scratch_shapes=[pltpu.VMEM((bb, POOL_PAD + tl, WIDTH_B), F32)],
        compiler_params=pltpu.CompilerParams(dimension_semantics=("arbitrary", "arbitrary"),
                                             vmem_limit_bytes=VMEM_LIMIT),
        name="delta",
    )(*([gsc] * len(gate_specs)), qkv, rest, pool_prev, s0, w_onorm, w_mix, pool_scale,
      jnp.ones((1, LANES), jnp.int32), *[arr for arr, _ in cast_to_bf16])


def _mlp_kernel(x_ref, oab_ref, gate_ref, wa_ref, wb_ref, wo_ref, gmlp_ref, wup_hbm, wdown_hbm, gfin_ref, y_ref,
                wup_ref, wdown_ref, wsem_ref):
    copies = [pltpu.make_async_copy(wup_hbm, wup_ref, wsem_ref.at[0]),
              pltpu.make_async_copy(wdown_hbm, wdown_ref, wsem_ref.at[1])]

    @pl.when(pl.program_id(0) == 0)
    def _():
        for copy in copies:
            copy.start()

    ma = _dot(oab_ref[:, 0:QK_W], wa_ref[...])
    mb = _dot(oab_ref[:, QK_W:QK_W + WIDTH_B], wb_ref[...])
    merged = _sigmoid(gate_ref[:, 0:D_MODEL]) * ma + _sigmoid(gate_ref[:, D_MODEL:]) * mb
    x1 = x_ref[...] + _dot(merged, wo_ref[...])
    h2 = (x1 * _rms_scale(x1) * gmlp_ref[...]).astype(BF16)
    acc = x1

    @pl.when(pl.program_id(0) == 0)
    def _():
        for copy in copies:
            copy.wait()

    for c0 in range(0, D_FF, FF_BLOCK):
        up = jnp.dot(h2, wup_ref[:, c0:c0 + FF_BLOCK], preferred_element_type=F32)
        act = jnp.square(jnp.maximum(up, 0.0))
        acc = acc + _dot(act, wdown_ref[c0:c0 + FF_BLOCK, :])
    y_ref[...] = acc * _rms_scale(acc) * gfin_ref[...]


def _merge_mlp(x2d, oab, rest, prm, tm):
    t = x2d.shape[0]
    assert t % tm == 0, "row tiles must cover the tokens exactly"
    row = lambda w: pl.BlockSpec((tm, w), lambda i: (i, 0))
    return pl.pallas_call(
        _mlp_kernel,
        grid=(t // tm,),
        in_specs=[row(D_MODEL), row(QK_W + WIDTH_B), row(2 * D_MODEL),
                  _const_spec((QK_W, D_MODEL)), _const_spec((WIDTH_B, D_MODEL)), _const_spec((D_MODEL, D_MODEL)),
                  _const_spec((1, D_MODEL)), pl.BlockSpec(memory_space=pl.ANY), pl.BlockSpec(memory_space=pl.ANY),
                  _const_spec((1, D_MODEL))],
        out_specs=row(D_MODEL),
        out_shape=jax.ShapeDtypeStruct((t, D_MODEL), F32),
        scratch_shapes=[pltpu.VMEM((D_MODEL, D_FF), BF16), pltpu.VMEM((D_FF, D_MODEL), BF16),
                        pltpu.SemaphoreType.DMA((2,))],
        compiler_params=pltpu.CompilerParams(dimension_semantics=("arbitrary",),
                                             vmem_limit_bytes=VMEM_LIMIT),
        name="merge_mlp",
    )(x2d, oab, rest, prm["w_a_out"], prm["w_b_out"], prm["w_o"], prm["g_mlp"], prm["w_up"], prm["w_down"],
      prm["g_final"])


def _mix(x, conv_prev, pool_prev, s_prev, pos0, prm, *, front_blk, delta_blk, chunk, act_dtype, rows_first,
         cast_to_bf16=()):
    bsz, l, _ = x.shape
    t = bsz * l
    qkv, rest, gsc, conv_new, *w_proj_bf16 = _front(
        x, prm["g_attn"], prm["w_proj"], prm["w_conv"], prm["a_log"], prm["dt_bias"],
        conv_prev, bb=front_blk[0], tl=front_blk[1], chunk=chunk, act_dtype=act_dtype, rows_first=rows_first)
    oab, pool_new, s_new, *casted = _delta(
        qkv, rest, gsc, pool_prev, s_prev, prm["w_onorm"], prm["w_mix"], prm["pool_scale"], bb=delta_blk[0],
        tl=delta_blk[1], chunk=chunk, pos0=pos0, act_dtype=act_dtype, rows_first=rows_first,
        cast_to_bf16=cast_to_bf16)
    rows = (x.reshape(t, D_MODEL), oab.reshape(t, QK_W + WIDTH_B), rest.reshape(t, REST_OUT_W))
    return rows, (conv_new, pool_new, s_new[None]), w_proj_bf16 + casted


def kernel(x_prompt, x_sample, state_conv, state_pool, state_ssm, w_in, w_conv, a_log, dt_bias, w_onorm,
           w_pool_mix, pool_scale, w_a_out, w_b_out, w_o, g_attn, g_mlp, w_up, w_down, g_final):
    assert w_in.shape[0] == 1, "single-layer decoder"
    prm = {
        "w_proj": jnp.transpose(w_in[0]).astype(F32),
        "g_attn": g_attn[0][None, :], "g_mlp": g_mlp[0][None, :], "g_final": g_final[None, :],
        "w_conv": w_conv[0].astype(F32),
        "a_log": a_log[0], "dt_bias": dt_bias[0],
        "w_onorm": w_onorm[0][None, :].astype(F32),
        "w_mix": w_pool_mix[0].astype(BF16), "pool_scale": pool_scale[0][None, :].astype(F32),
    }
    bp = x_prompt.shape[0]
    converted = ("w_proj", "w_up", "w_down", "w_o", "w_a_out", "w_b_out")
    rows_p, (conv_p, pool_p, ssm_p), casted = _mix(
        x_prompt, jnp.zeros((bp, CONV_W - 1, QKV_W), F32), jnp.zeros((bp, POOL_HIST, WIDTH_B), F32),
        jnp.zeros((bp, N_HEADS, HEAD_DIM, HEAD_DIM), F32), 0, prm,
        front_blk=(1, 512), delta_blk=(4, 256), chunk=GROUP, act_dtype=BF16, rows_first=False,
        cast_to_bf16=((w_up[0].astype(F32), 1), (w_down[0].astype(F32), 0), (w_o[0].astype(F32), 0),
                      (w_a_out[0].astype(F32), 0), (w_b_out[0].astype(F32), 0)))
    prm.update(zip(converted, casted, strict=True))
    conv_p, pool_p = conv_p[None], pool_p[None]
    dec_len = x_sample.shape[1]
    rows_major = lambda s: jnp.transpose(s[0].astype(F32), (1, 0, 2))
    rows_s, (conv_s, pool_s, ssm_s), _ = _mix(
        x_sample, rows_major(state_conv), rows_major(state_pool), state_ssm[0].astype(F32), PAST_LEN, prm,
        front_blk=(256 // dec_len, dec_len), delta_blk=(2 * GROUP // dec_len, dec_len),
        chunk=dec_len, act_dtype=F32, rows_first=True)
    conv_s, pool_s = jnp.transpose(conv_s, (1, 0, 2))[None], jnp.transpose(pool_s, (1, 0, 2))[None]
    y_p = _merge_mlp(*rows_p, prm, tm=512).reshape(x_prompt.shape)
    y_s = _merge_mlp(*rows_s, prm, tm=256).reshape(x_sample.shape)
    return (y_p, y_s, conv_p.astype(state_conv.dtype), pool_p.astype(state_pool.dtype),
            ssm_p.astype(state_ssm.dtype), conv_s.astype(state_conv.dtype), pool_s.astype(state_pool.dtype),
            ssm_s.astype(state_ssm.dtype))
```

```python
import functools
import math

import jax
import jax.numpy as jnp
from jax import lax
from jax.experimental import pallas as pl
from jax.experimental.pallas import tpu as pltpu

D_MODEL = 1024
N_HEADS = 4
HEAD_DIM = 128
QK_W = N_HEADS * HEAD_DIM
QKV_W = 3 * QK_W
CONV_W = 4
POOL_WINDOWS = (2, 4, 8, 16)
POOL_GROUP = 128
WIDTH_B = len(POOL_WINDOWS) * POOL_GROUP
POOL_HIST = 15
D_FF = 4 * D_MODEL
EPS = 1e-6
PAST_LEN = 16384
LANES = 128
SUBLANES = 8
MXU_COLS = 256
N_GATE_ROWS = 4

QKVZ_W = QKV_W + QK_W
GATE_SCALARS = 2 * N_HEADS
REST_W = WIDTH_B + 2 * D_MODEL
W_GATE0 = QKVZ_W
WT_COLS = QKVZ_W + REST_W
WT_ROWS = D_MODEL + 2 * SUBLANES
WT_CHUNK = 512
REST_Z0 = 2 * D_MODEL
REST_P0 = REST_Z0 + QK_W
REST_OUT_W = REST_P0 + WIDTH_B

GROUP = 128
SERIES_BLOCK = 64
CONV_PAD = 8
POOL_LOOKBACK = 16
POOL_PAD = 24
FF_BLOCK = 1024

VMEM_LIMIT = 58 * 1024 * 1024
X_SLOTS = 3

BF16 = jnp.bfloat16
F32 = jnp.float32


def _dot(a, b):
    return jnp.dot(a.astype(BF16), b.astype(BF16), preferred_element_type=F32)


def _sigmoid(x):
    return 1.0 / (1.0 + jnp.exp(-x))


def _silu_of_twice(half):
    return half * jnp.tanh(half) + half


def _silu(x):
    return _silu_of_twice(0.5 * x)


def _rms_scale(x):
    return lax.rsqrt(jnp.mean(x * x, axis=-1, keepdims=True) + EPS)


def _const_spec(shape):
    zeros = (0,) * len(shape)
    return pl.BlockSpec(shape, lambda *_: zeros, pipeline_mode=pl.Buffered(1))


def _seq_spec(bb, tl, width):
    return pl.BlockSpec((bb, tl, width), lambda i, j: (i, j, 0))


def _state_spec(bb, n_rows, width, rows_first):
    if rows_first:
        return pl.BlockSpec((n_rows, bb, width), lambda i, j: (0, i, 0))
    return pl.BlockSpec((bb, n_rows, width), lambda i, j: (i, 0, 0))


def _state_shape(bsz, n_rows, width, rows_first):
    return jax.ShapeDtypeStruct((n_rows, bsz, width) if rows_first else (bsz, n_rows, width), F32)


def _history_to_scratch(scratch_ref, first_row, state_ref, rows_first):
    if rows_first:
        for j in range(state_ref.shape[0]):
            scratch_ref[:, first_row + j, :] = state_ref[j]
    else:
        scratch_ref[:, first_row:first_row + state_ref.shape[1], :] = state_ref[...]


def _history_from_scratch(state_ref, scratch_ref, first_row, rows_first):
    if rows_first:
        for j in range(state_ref.shape[0]):
            state_ref[j] = scratch_ref[:, first_row + j, :]
    else:
        state_ref[...] = scratch_ref[:, first_row:first_row + state_ref.shape[1], :]


def _gate_rows_spec(bb, tl, n_tiles):
    return pl.BlockSpec((N_GATE_ROWS, SUBLANES, bb * tl), lambda i, j: (0, 0, i * n_tiles + j))


def _lane_prefix_sum(x, chunk):
    lane = lax.broadcasted_iota(jnp.int32, x.shape, 1)
    shift = 1
    while shift < chunk:
        x = x + jnp.where(lane % chunk >= shift, pltpu.roll(x, shift, axis=1), 0.0)
        shift *= 2
    return x


def _lane_suffix_sum(x, chunk):
    lane = lax.broadcasted_iota(jnp.int32, x.shape, 1)
    shift = 1
    while shift < chunk:
        x = x + jnp.where(lane % chunk + shift < chunk, pltpu.roll(x, x.shape[1] - shift, axis=1), 0.0)
        shift *= 2
    return x


def _front_kernel(x_ref, g_ref, w_ref, wconv_ref, decay_ref, keep_ref, convprev_ref,
                  qkv_ref, rest_ref, gsc_ref, convnew_ref,
                  *tail, bb, tl, n_tiles, chunk, rows_first, convert_weights):
    tile = pl.program_id(1)
    rows = bb * tl
    hist = slice(CONV_PAD - (CONV_W - 1), CONV_PAD)
    ext_ref, xbuf_ref, xsem_ref = tail[-3:]

    n_steps = pl.num_programs(0) * n_tiles
    step = pl.program_id(0) * n_tiles + tile

    def x_copy(s):
        slot = s % X_SLOTS
        block = x_ref.at[pl.ds((s // n_tiles) * bb, bb), pl.ds((s % n_tiles) * tl, tl)]
        return pltpu.make_async_copy(block, xbuf_ref.at[slot], xsem_ref.at[slot])

    @pl.when(step == 0)
    def _():
        for s in range(X_SLOTS - 1):
            x_copy(s).start()

    @pl.when(step + (X_SLOTS - 1) < n_steps)
    def _():
        x_copy(step + (X_SLOTS - 1)).start()

    if convert_weights:
        w_src_ref, w_ref = w_ref, tail[0]
        src_rest0 = W_GATE0 + GATE_SCALARS

        @pl.when((pl.program_id(0) == 0) & (tile == 0))
        def _():
            for src0, dst0, n in ((0, 0, QKVZ_W), (src_rest0, QKVZ_W, REST_W)):
                for c in range(0, n, WT_CHUNK):
                    w_ref[0:D_MODEL, dst0 + c:dst0 + c + WT_CHUNK] = (
                        w_src_ref[src0 + c:src0 + c + WT_CHUNK, :].T.astype(BF16))
            gate_rows = jnp.concatenate([w_src_ref[W_GATE0:src_rest0, :],
                                         jnp.zeros((WT_ROWS - D_MODEL - GATE_SCALARS, D_MODEL), F32)], axis=0)
            w_ref[D_MODEL:WT_ROWS, 0:D_MODEL] = gate_rows.astype(BF16)
            w_ref[D_MODEL:WT_ROWS, D_MODEL:WT_COLS] = jnp.zeros((WT_ROWS - D_MODEL, WT_COLS - D_MODEL), BF16)

    @pl.when(tile == 0)
    def _():
        _history_to_scratch(ext_ref, hist.start, convprev_ref, rows_first)
        ext_ref[:, 0:CONV_PAD - (CONV_W - 1), :] = jnp.zeros((bb, CONV_PAD - (CONV_W - 1), QKV_W), F32)

    x_copy(step).wait()
    x = xbuf_ref[step % X_SLOTS].reshape(rows, D_MODEL)
    normed = (x * _rms_scale(x) * g_ref[...]).astype(BF16)
    proj = lambda c0, n: jnp.dot(normed, w_ref[0:D_MODEL, c0:c0 + n], preferred_element_type=F32)
    ext_ref[:, CONV_PAD:CONV_PAD + tl, :] = proj(0, QKV_W).reshape(bb, tl, QKV_W)
    _history_from_scratch(convnew_ref, ext_ref, CONV_PAD + tl - (CONV_W - 1), rows_first)

    x8 = lax.dot_general(w_ref[D_MODEL:WT_ROWS, 0:D_MODEL], normed, (((1,), (1,)), ((), ())),
                         preferred_element_type=F32)[0:SUBLANES, :]
    xs = x8 + decay_ref[:, 1:2]
    softplus = jnp.maximum(xs, 0.0) + jnp.log1p(jnp.exp(-jnp.abs(xs)))
    graw8 = -jnp.exp(decay_ref[:, 0:1]) * softplus
    g8 = _lane_prefix_sum(graw8, chunk)
    gsc_ref[0] = _sigmoid(x8)
    gsc_ref[1] = g8
    gsc_ref[2] = jnp.exp(g8)
    gsc_ref[3] = jnp.exp(_lane_suffix_sum(graw8, chunk) - graw8)

    keep = keep_ref[...] != 0

    def conv_block(c0, dep):
        cols = slice(c0, c0 + LANES)
        xe = ext_ref[:, :, cols].reshape(bb * (CONV_PAD + tl), LANES)
        x1 = pltpu.roll(xe, 1, axis=0)
        tap = lambda j: 0.5 * wconv_ref[j:j + 1, cols]
        acc = (tap(3) * xe + tap(2) * x1) + pltpu.roll(tap(1) * xe + tap(0) * x1, 2, axis=0)
        val = _silu_of_twice(acc.reshape(bb, CONV_PAD + tl, LANES)[:, CONV_PAD:, :])
        if c0 < 2 * QK_W:
            scale = HEAD_DIM ** -0.5 if c0 < QK_W else 1.0
            val = val * (lax.rsqrt(jnp.sum(val * val, axis=-1, keepdims=True) + EPS) * scale)
        qkv_ref[:, :, cols] = jnp.where(keep, val, dep[:, 0:LANES].reshape(bb, tl, LANES)).astype(qkv_ref.dtype)

    def proj_block(out0, w0, c0):
        val = proj(w0 + c0, MXU_COLS)
        rest_ref[:, :, out0 + c0:out0 + c0 + MXU_COLS] = val.reshape(bb, tl, MXU_COLS)
        return val

    mxu_work = ([functools.partial(proj_block, REST_Z0, QKV_W, c0) for c0 in range(0, QK_W, MXU_COLS)]
                + [functools.partial(proj_block, REST_P0, QKVZ_W, c0) for c0 in range(0, WIDTH_B, MXU_COLS)]
                + [functools.partial(proj_block, 0, QKVZ_W + WIDTH_B, c0)
                   for c0 in range(0, 2 * D_MODEL, MXU_COLS)])
    for i, c0 in enumerate(range(0, QKV_W, LANES)):
        conv_block(c0, mxu_work[i]())

    if n_tiles > 1:
        _history_to_scratch(ext_ref, hist.start, convnew_ref, rows_first)


def _front(x, g_attn, w_proj, w_conv, a_log, dt_bias, conv_prev, *, bb, tl, chunk, act_dtype, rows_first):
    convert_weights = w_proj.dtype != BF16
    assert w_proj.shape == ((QKVZ_W + GATE_SCALARS + REST_W, D_MODEL) if convert_weights else (WT_ROWS, WT_COLS))
    bsz, l, _ = x.shape
    assert bsz % bb == 0 and l % tl == 0 and (bb * tl) % chunk == 0, "blocks must tile the batch, length and chunks"
    assert bb == 1 or tl == l, "a block of several sequences must hold them whole (gate rows are token-major)"
    assert CONV_W == 4, "the conv is written out as two pairs of taps"
    n_tiles = l // tl
    assert (bsz // bb) * n_tiles >= X_SLOTS - 1, "the first step starts the copies of the first X_SLOTS - 1 steps"
    kern = functools.partial(_front_kernel, bb=bb, tl=tl, n_tiles=n_tiles, chunk=chunk, rows_first=rows_first,
                             convert_weights=convert_weights)
    sds = lambda *shape: jax.ShapeDtypeStruct(shape, F32)
    w_out_specs = [_const_spec((WT_ROWS, WT_COLS))] if convert_weights else []
    w_out_shapes = [jax.ShapeDtypeStruct((WT_ROWS, WT_COLS), BF16)] if convert_weights else []
    decay_prm = jnp.pad(jnp.stack([a_log, dt_bias], axis=1).astype(F32), ((N_HEADS, 0), (0, 0)))
    return pl.pallas_call(
        kern,
        grid=(bsz // bb, n_tiles),
        in_specs=[pl.BlockSpec(memory_space=pl.ANY), _const_spec((1, D_MODEL)), _const_spec(w_proj.shape),
                  _const_spec((CONV_W, QKV_W)), _const_spec((SUBLANES, 2)), _const_spec((1, LANES)),
                  _state_spec(bb, CONV_W - 1, QKV_W, rows_first)],
        out_specs=[_seq_spec(bb, tl, QKV_W), _seq_spec(bb, tl, REST_OUT_W), _gate_rows_spec(bb, tl, n_tiles),
                   _state_spec(bb, CONV_W - 1, QKV_W, rows_first)] + w_out_specs,
        out_shape=[jax.ShapeDtypeStruct((bsz, l, QKV_W), act_dtype), sds(bsz, l, REST_OUT_W),
                   sds(N_GATE_ROWS, SUBLANES, bsz * l),
                   _state_shape(bsz, CONV_W - 1, QKV_W, rows_first)] + w_out_shapes,
        scratch_shapes=[pltpu.VMEM((bb, CONV_PAD + tl, QKV_W), F32), pltpu.VMEM((X_SLOTS, bb, tl, D_MODEL), F32),
                        pltpu.SemaphoreType.DMA((X_SLOTS,))],
        compiler_params=pltpu.CompilerParams(dimension_semantics=("arbitrary", "arbitrary"),
                                             vmem_limit_bytes=VMEM_LIMIT),
        name="front",
    )(x, g_attn, w_proj, w_conv, decay_prm, jnp.ones((1, LANES), jnp.int32), conv_prev)


def _unit_lower_inverses(ms, chunk, side_jobs=()):
    nb = min(chunk, SERIES_BLOCK)
    n_blocks = GROUP // nb
    n_factors = int(math.log2(nb))
    assert chunk == nb or (chunk == 2 * nb and n_blocks == 2), "chunks are one or two series blocks"
    lane = lax.broadcasted_iota(jnp.int32, (nb, GROUP), 1)
    lane_block = lane // nb

    def packed(m):
        out = m[0:nb]
        for b in range(1, n_blocks):
            out = jnp.where(lane_block == b, m[b * nb:(b + 1) * nb], out)
        return out

    def block_diag(p):
        return jnp.concatenate([jnp.where(lane_block == b, p, 0.0) for b in range(n_blocks)], axis=0)

    side_jobs = list(side_jobs)
    per_step = -(-len(side_jobs) // max(n_factors - 1, 1))

    def issue_side_jobs(step_values):
        for k in range(min(per_step, len(side_jobs))):
            side_jobs.pop(0)(step_values[k % len(step_values)])

    nps = [packed(m) for m in ms]
    eye_p = jnp.where(lane % nb == lax.broadcasted_iota(jnp.int32, (nb, GROUP), 0), 1.0, 0.0).astype(F32)
    invs = [eye_p - n for n in nps]
    if n_factors > 1:
        pws = [_dot(n, block_diag(n)) for n in nps]
        issue_side_jobs(pws)
        for _ in range(n_factors - 2):
            boths = [_dot(jnp.concatenate([inv, pw], axis=0), block_diag(pw)) for inv, pw in zip(invs, pws)]
            invs = [inv + both[0:nb] for inv, both in zip(invs, boths)]
            pws = [both[nb:] for both in boths]
            issue_side_jobs(pws)
        invs = [inv + _dot(inv, block_diag(pw)) for inv, pw in zip(invs, pws)]
    while side_jobs:
        side_jobs.pop(0)(None)
    if chunk == nb:
        return [block_diag(inv) for inv in invs]
    zeros = jnp.zeros((nb, GROUP), F32)
    a_invs = [jnp.where(lane < nb, inv, 0.0) for inv in invs]
    lows = [jnp.where(lane < nb, m[nb:], 0.0) for m in ms]
    xs = [_dot(low, jnp.concatenate([a_inv, zeros], axis=0)) for low, a_inv in zip(lows, a_invs)]
    ys = [_dot(inv, jnp.concatenate([zeros, x], axis=0)) for inv, x in zip(invs, xs)]
    return [jnp.concatenate([a_inv, jnp.where(lane >= nb, inv, 0.0) - y], axis=0)
            for a_inv, y, inv in zip(a_invs, ys, invs)]


def _delta_kernel(*refs, bb, tl, chunk, pos0, n_tiles, n_gate_refs, n_casts, rows_first):
    gsc_refs, refs = refs[:n_gate_refs], refs[n_gate_refs:]
    (qkv_ref, zp_ref, poolprev_ref, s0_ref, wonorm_ref, wmix_ref, pscale_ref, keep_ref) = refs[0:8]
    cast_src, refs = refs[8:8 + n_casts], refs[8 + n_casts:]
    oab_ref, poolnew_ref, snew_ref = refs[0:3]
    cast_dst, pext_ref = refs[3:3 + n_casts], refs[3 + n_casts]
    tile = pl.program_id(1)

    for src, dst in zip(cast_src, cast_dst):
        dst[...] = src[...].astype(dst.dtype)

    rows_b = min(tl, GROUP)
    seqs_g = GROUP // rows_b
    groups_b = tl // rows_b
    n_groups = bb * tl // GROUP
    chained = chunk == GROUP
    assert chained or (chunk == tl and tl < GROUP), "chunk must be a whole group or a whole short sequence"
    pool_hist = slice(POOL_PAD - POOL_HIST, POOL_PAD)

    @pl.when(tile == 0)
    def _():
        snew_ref[...] = s0_ref[...]
        _history_to_scratch(pext_ref, pool_hist.start, poolprev_ref, rows_first)
        pext_ref[:, POOL_PAD - POOL_LOOKBACK:POOL_PAD - POOL_HIST, :] = jnp.zeros((bb, 1, WIDTH_B), F32)

    pext_ref[:, POOL_PAD:POOL_PAD + tl, :] = zp_ref[:, :, QK_W:QK_W + WIDTH_B]
    _history_from_scratch(poolnew_ref, pext_ref, POOL_PAD + tl - POOL_HIST, rows_first)

    row = lax.broadcasted_iota(jnp.int32, (GROUP, GROUP), 0)
    col = lax.broadcasted_iota(jnp.int32, (GROUP, GROUP), 1)
    causal = row >= col
    strict = row > col
    if not chained:
        same = (row // chunk) == (col // chunk)
        causal = causal & same
        strict = strict & same

    def origin(g):
        if tl >= GROUP:
            return g // groups_b, (g % groups_b) * GROUP
        return g * seqs_g, 0

    def load(c0, g):
        b0, t0 = origin(g)
        return qkv_ref[b0:b0 + seqs_g, t0:t0 + rows_b, c0:c0 + HEAD_DIM].astype(F32).reshape(GROUP, HEAD_DIM)

    def gate_rows(g):
        if n_gate_refs > 1:
            b0, t0 = origin(g)
            return tuple(gsc_refs[b0][q, :, t0:t0 + GROUP] for q in range(N_GATE_ROWS))
        return tuple(gsc_refs[0][q, :, g * GROUP:(g + 1) * GROUP] for q in range(N_GATE_ROWS))

    gates = [gate_rows(g) for g in range(n_groups)]

    probs = [(g, h) for g in range(n_groups) for h in range(N_HEADS)]
    st = []
    for g, h in probs:
        beta8, g8, eg8, kds8 = gates[g]
        d = {"q": load(h * HEAD_DIM, g), "k": load(QK_W + h * HEAD_DIM, g), "v": load(2 * QK_W + h * HEAD_DIM, g)}
        d["beta_row"] = beta8[h:h + 1, :]
        d["eg_row"] = eg8[N_HEADS + h:N_HEADS + h + 1, :]
        d["g_row"] = g8[N_HEADS + h:N_HEADS + h + 1, :]
        d["kb_row"] = kds8[N_HEADS + h:N_HEADS + h + 1, :] * d["beta_row"]
        d["g_col"] = jnp.broadcast_to(d["g_row"], (GROUP, GROUP)).T
        d["kt"] = d["k"].T
        st.append(d)
    for d in st:
        both = _dot(jnp.concatenate([d["k"], d["q"]], axis=0), d["kt"])
        d["kk"], d["qk"] = both[0:GROUP], both[GROUP:]
    for d in st:
        decay = jnp.exp(jnp.where(causal, d["g_col"] - d["g_row"], -jnp.inf))
        decay_beta = decay * d["beta_row"]
        d["m"] = jnp.where(strict, d.pop("kk") * decay_beta, 0.0)
        d["qkm"] = d.pop("qk") * decay_beta
        d["q_dec"] = d.pop("q") * jnp.exp(d["g_col"])
        d["kt_dec"] = d.pop("kt") * d["kb_row"]

    keep = keep_ref[...] != 0

    def pool_block(g, gi, issued_with):
        b0, t0 = origin(g)
        win = POOL_WINDOWS[gi]
        cols = slice(gi * POOL_GROUP, (gi + 1) * POOL_GROUP)
        pos = pos0 + tile * tl + t0 + row % rows_b
        slab = pext_ref[b0:b0 + seqs_g, POOL_PAD - POOL_LOOKBACK + t0:POOL_PAD + t0 + rows_b, cols]
        acc = slab.reshape(seqs_g * (POOL_LOOKBACK + rows_b), POOL_GROUP)
        shift = 1
        while shift < win:
            acc = acc + pltpu.roll(acc, shift, axis=0)
            shift *= 2
        acc = acc.reshape(seqs_g, POOL_LOOKBACK + rows_b, POOL_GROUP)[:, POOL_LOOKBACK:, :]
        cur = slab[:, POOL_LOOKBACK:, :]
        pooled = (acc / jnp.minimum(pos + 1, win).astype(F32).reshape(seqs_g, rows_b, POOL_GROUP) - cur)
        mixed = _dot(pooled.reshape(GROUP, POOL_GROUP), wmix_ref[gi]) * pscale_ref[:, cols]
        if issued_with is not None:
            reps = GROUP // issued_with.shape[0]
            mixed = jnp.where(keep, mixed, jnp.concatenate([issued_with] * reps, axis=0))
        oab_ref[b0:b0 + seqs_g, t0:t0 + rows_b, QK_W + gi * POOL_GROUP:QK_W + (gi + 1) * POOL_GROUP] = (
            mixed.reshape(seqs_g, rows_b, POOL_GROUP).astype(oab_ref.dtype))

    pool_jobs = [functools.partial(pool_block, g, gi) for g in range(n_groups) for gi in range(len(POOL_WINDOWS))]
    invs = _unit_lower_inverses([d.pop("m") for d in st], chunk, pool_jobs)
    for d, inv in zip(st, invs):
        d["uy"] = _dot(inv, d.pop("v"))
        d["wy"] = _dot(inv * d["eg_row"], d.pop("k"))

    def group_slab(ref, g, cols, row_off=0):
        b0, t0 = origin(g)
        return ref.at[b0:b0 + seqs_g, row_off + t0:row_off + t0 + rows_b, cols]

    def head_out(g, h, o):
        cols = slice(h * HEAD_DIM, (h + 1) * HEAD_DIM)
        z = group_slab(zp_ref, g, cols)[...].reshape(GROUP, HEAD_DIM)
        group_slab(oab_ref, g, cols)[...] = (o * _rms_scale(o) * wonorm_ref[...] * _silu(z)).reshape(
            seqs_g, rows_b, HEAD_DIM).astype(oab_ref.dtype)

    if chained:
        for j in range(groups_b):
            wave = [(i, g, h) for i, (g, h) in enumerate(probs) if g % groups_b == j]
            s_old = {i: snew_ref[origin(g)[0], h] for i, g, h in wave}
            ws = {i: _dot(jnp.concatenate([st[i]["wy"], st[i]["q_dec"]], axis=0), s_old[i]) for i, g, h in wave}
            ys = {i: st[i]["uy"] - ws[i][0:GROUP] for i, g, h in wave}
            outs = {i: ws[i][GROUP:] + _dot(st[i]["qkm"], ys[i]) for i, g, h in wave}
            for i, g, h in wave:
                last = jnp.exp(st[i]["g_col"][GROUP - 1:GROUP, :])
                snew_ref[origin(g)[0], h] = s_old[i] * last + _dot(st[i]["kt_dec"], ys[i])
            for i, g, h in wave:
                head_out(g, h, outs[i])
    else:
        outs = {}
        for i, (g, h) in enumerate(probs):
            d, b0 = st[i], origin(g)[0]
            ws_w, ws_q = [], []
            for s_i in range(seqs_g):
                r = slice(s_i * rows_b, (s_i + 1) * rows_b)
                ws = _dot(jnp.concatenate([d["wy"][r], d["q_dec"][r]], axis=0), snew_ref[b0 + s_i, h])
                ws_w.append(ws[0:rows_b])
                ws_q.append(ws[rows_b:])
            d["y"] = d["uy"] - jnp.concatenate(ws_w, axis=0)
            outs[i] = jnp.concatenate(ws_q, axis=0) + _dot(d["qkm"], d["y"])
        for i, (g, h) in enumerate(probs):
            d, b0 = st[i], origin(g)[0]
            for s_i in range(seqs_g):
                last = jnp.exp(d["g_col"][(s_i + 1) * rows_b - 1:(s_i + 1) * rows_b, :])
                upd = _dot(jnp.where(col // rows_b == s_i, d["kt_dec"], 0.0), d["y"])
                snew_ref[b0 + s_i, h] = snew_ref[b0 + s_i, h] * last + upd
        for i, (g, h) in enumerate(probs):
            head_out(g, h, outs[i])

    if n_tiles > 1:
        _history_to_scratch(pext_ref, pool_hist.start, poolnew_ref, rows_first)


def _delta(qkv, rest, gsc, pool_prev, s0, w_onorm, w_mix, pool_scale, *, bb, tl, chunk, pos0, act_dtype,
           rows_first, cast_to_bf16=()):
    bsz, l = qkv.shape[0], qkv.shape[1]
    assert bsz % bb == 0 and l % tl == 0 and (bb * tl) % GROUP == 0, "blocks must tile the batch, length and groups"
    n_tiles = l // tl
    n_steps = (bsz // bb) * n_tiles
    cast_specs = []
    for arr, axis in cast_to_bf16:
        tile = LANES if axis == 1 else 2 * SUBLANES
        assert arr.shape[axis] % (n_steps * tile) == 0, "each step converts a tile-aligned slice"
        blk = tuple(d // n_steps if a == axis else d for a, d in enumerate(arr.shape))
        cast_specs.append(pl.BlockSpec(blk, lambda i, j, axis=axis: tuple(
            i * n_tiles + j if a == axis else 0 for a in range(2))))
    state_s = pl.BlockSpec((bb, N_HEADS, HEAD_DIM, HEAD_DIM), lambda i, j: (i, 0, 0, 0))
    if n_tiles == 1:
        gate_specs = [_gate_rows_spec(bb, tl, 1)]
    else:
        gate_specs = [pl.BlockSpec((N_GATE_ROWS, SUBLANES, tl), lambda i, j, k=k: (0, 0, (i * bb + k) * n_tiles + j))
                      for k in range(bb)]
    kern = functools.partial(_delta_kernel, bb=bb, tl=tl, chunk=chunk, pos0=pos0, n_tiles=n_tiles,
                             n_gate_refs=len(gate_specs), n_casts=len(cast_specs), rows_first=rows_first)
    sds = lambda *shape: jax.ShapeDtypeStruct(shape, F32)
    return pl.pallas_call(
        kern,
        grid=(bsz // bb, n_tiles),
        in_specs=gate_specs + [
                  _seq_spec(bb, tl, QKV_W),
                  pl.BlockSpec((bb, tl, QK_W + WIDTH_B), lambda i, j: (i, j, REST_Z0 // (QK_W + WIDTH_B))),
                  _state_spec(bb, POOL_HIST, WIDTH_B, rows_first), state_s,
                  _const_spec((1, HEAD_DIM)), _const_spec((len(POOL_WINDOWS), POOL_GROUP, POOL_GROUP)),
                  _const_spec((1, WIDTH_B)), _const_spec((1, LANES))] + cast_specs,
        out_specs=[_seq_spec(bb, tl, QK_W + WIDTH_B),
                   _state_spec(bb, POOL_HIST, WIDTH_B, rows_first), state_s] + cast_specs,
        out_shape=[jax.ShapeDtypeStruct((bsz, l, QK_W + WIDTH_B), act_dtype),
                   _state_shape(bsz, POOL_HIST, WIDTH_B, rows_first),
                   sds(bsz, N_HEADS, HEAD_DIM, HEAD_DIM)]
                  + [jax.ShapeDtypeStruct(arr.shape, BF16) for arr, _ in cast_to_bf16],
        scratch_shapes=[pltpu.VMEM((bb, POOL_PAD + tl, WIDTH_B), F32)],
        compiler_params=pltpu.CompilerParams(dimension_semantics=("arbitrary", "arbitrary"),
                                             vmem_limit_bytes=VMEM_LIMIT),
        name="delta",
    )(*([gsc] * len(gate_specs)), qkv, rest, pool_prev, s0, w_onorm, w_mix, pool_scale,
      jnp.ones((1, LANES), jnp.int32), *[arr for arr, _ in cast_to_bf16])


def _mlp_kernel(x_ref, oab_ref, gate_ref, wa_ref, wb_ref, wo_ref, gmlp_ref, wup_ref, wdown_ref, gfin_ref, y_ref):
    ma = _dot(oab_ref[:, 0:QK_W], wa_ref[...])
    mb = _dot(oab_ref[:, QK_W:QK_W + WIDTH_B], wb_ref[...])
    merged = _sigmoid(gate_ref[:, 0:D_MODEL]) * ma + _sigmoid(gate_ref[:, D_MODEL:]) * mb
    x1 = x_ref[...] + _dot(merged, wo_ref[...])
    h2 = (x1 * _rms_scale(x1) * gmlp_ref[...]).astype(BF16)
    acts = []
    for c0 in range(0, D_FF, FF_BLOCK):
        up = jnp.dot(h2, wup_ref[:, c0:c0 + FF_BLOCK], preferred_element_type=F32)
        acts.append(jnp.square(jnp.maximum(up, 0.0)).astype(BF16))
    acc = x1 + jnp.dot(jnp.concatenate(acts, axis=1), wdown_ref[...], preferred_element_type=F32)
    y_ref[...] = acc * _rms_scale(acc) * gfin_ref[...]


def _merge_mlp(x2d, oab, rest, prm, tm):
    t = x2d.shape[0]
    assert t % tm == 0, "row tiles must cover the tokens exactly"
    row = lambda w: pl.BlockSpec((tm, w), lambda i: (i, 0))
    return pl.pallas_call(
        _mlp_kernel,
        grid=(t // tm,),
        in_specs=[row(D_MODEL), row(QK_W + WIDTH_B), row(2 * D_MODEL),
                  _const_spec((QK_W, D_MODEL)), _const_spec((WIDTH_B, D_MODEL)), _const_spec((D_MODEL, D_MODEL)),
                  _const_spec((1, D_MODEL)), _const_spec((D_MODEL, D_FF)), _const_spec((D_FF, D_MODEL)),
                  _const_spec((1, D_MODEL))],
        out_specs=row(D_MODEL),
        out_shape=jax.ShapeDtypeStruct((t, D_MODEL), F32),
        compiler_params=pltpu.CompilerParams(dimension_semantics=("arbitrary",),
                                             vmem_limit_bytes=VMEM_LIMIT),
        name="merge_mlp",
    )(x2d, oab, rest, prm["w_a_out"], prm["w_b_out"], prm["w_o"], prm["g_mlp"], prm["w_up"], prm["w_down"],
      prm["g_final"])


def _mix(x, conv_prev, pool_prev, s_prev, pos0, prm, *, front_blk, delta_blk, chunk, act_dtype, rows_first,
         cast_to_bf16=()):
    bsz, l, _ = x.shape
    t = bsz * l
    qkv, rest, gsc, conv_new, *w_proj_bf16 = _front(
        x, prm["g_attn"], prm["w_proj"], prm["w_conv"], prm["a_log"], prm["dt_bias"],
        conv_prev, bb=front_blk[0], tl=front_blk[1], chunk=chunk, act_dtype=act_dtype, rows_first=rows_first)
    oab, pool_new, s_new, *casted = _delta(
        qkv, rest, gsc, pool_prev, s_prev, prm["w_onorm"], prm["w_mix"], prm["pool_scale"], bb=delta_blk[0],
        tl=delta_blk[1], chunk=chunk, pos0=pos0, act_dtype=act_dtype, rows_first=rows_first,
        cast_to_bf16=cast_to_bf16)
    rows = (x.reshape(t, D_MODEL), oab.reshape(t, QK_W + WIDTH_B), rest.reshape(t, REST_OUT_W))
    return rows, (conv_new, pool_new, s_new[None]), w_proj_bf16 + casted


def kernel(x_prompt, x_sample, state_conv, state_pool, state_ssm, w_in, w_conv, a_log, dt_bias, w_onorm,
           w_pool_mix, pool_scale, w_a_out, w_b_out, w_o, g_attn, g_mlp, w_up, w_down, g_final):
    assert w_in.shape[0] == 1, "single-layer decoder"
    prm = {
        "w_proj": jnp.transpose(w_in[0]).astype(F32),
        "g_attn": g_attn[0][None, :], "g_mlp": g_mlp[0][None, :], "g_final": g_final[None, :],
        "w_conv": w_conv[0].astype(F32),
        "a_log": a_log[0], "dt_bias": dt_bias[0],
        "w_onorm": w_onorm[0][None, :].astype(F32),
        "w_mix": w_pool_mix[0].astype(BF16), "pool_scale": pool_scale[0][None, :].astype(F32),
    }
    bp = x_prompt.shape[0]
    converted = ("w_proj", "w_up", "w_down", "w_o", "w_a_out", "w_b_out")
    rows_p, (conv_p, pool_p, ssm_p), casted = _mix(
        x_prompt, jnp.zeros((bp, CONV_W - 1, QKV_W), F32), jnp.zeros((bp, POOL_HIST, WIDTH_B), F32),
        jnp.zeros((bp, N_HEADS, HEAD_DIM, HEAD_DIM), F32), 0, prm,
        front_blk=(1, 512), delta_blk=(4, 256), chunk=GROUP, act_dtype=BF16, rows_first=False,
        cast_to_bf16=((w_up[0].astype(F32), 1), (w_down[0].astype(F32), 0), (w_o[0].astype(F32), 0),
                      (w_a_out[0].astype(F32), 0), (w_b_out[0].astype(F32), 0)))
    prm.update(zip(converted, casted, strict=True))
    conv_p, pool_p = conv_p[None], pool_p[None]
    dec_len = x_sample.shape[1]
    rows_major = lambda s: jnp.transpose(s[0].astype(F32), (1, 0, 2))
    rows_s, (conv_s, pool_s, ssm_s), _ = _mix(
        x_sample, rows_major(state_conv), rows_major(state_pool), state_ssm[0].astype(F32), PAST_LEN, prm,
        front_blk=(256 // dec_len, dec_len), delta_blk=(2 * GROUP // dec_len, dec_len),
        chunk=dec_len, act_dtype=F32, rows_first=True)
    conv_s, pool_s = jnp.transpose(conv_s, (1, 0, 2))[None], jnp.transpose(pool_s, (1, 0, 2))[None]
    y_p = _merge_mlp(*rows_p, prm, tm=512).reshape(x_prompt.shape)
    y_s = _merge_mlp(*rows_s, prm, tm=256).reshape(x_sample.shape)
    return (y_p, y_s, conv_p.astype(state_conv.dtype), pool_p.astype(state_pool.dtype),
            ssm_p.astype(state_ssm.dtype), conv_s.astype(state_conv.dtype), pool_s.astype(state_pool.dtype),
            ssm_s.astype(state_ssm.dtype))
```
